```python
import jax, jax.numpy as jnp
from jax import lax
import numpy as np

D_MODEL = 1024
BATCH = 8
SEQ = 4096
DEPTH = 1

N_HEADS = 8
QK_NOPE_DIM = 128
QK_ROPE_DIM = 64
V_HEAD_DIM = 128
Q_LORA_RANK = 3 * D_MODEL // 8
KV_LORA_RANK = D_MODEL // 4
ROPE_THETA = 10000.0
Q_BLOCK = 128
CONV_CHANNELS = D_MODEL
CONV_WIDTH = 31
D_FF = 4 * D_MODEL
N_BRANCHES = 2
EPS = 1e-6
IN_SIZES = (Q_LORA_RANK, KV_LORA_RANK, QK_ROPE_DIM, 2 * CONV_CHANNELS, N_BRANCHES * D_MODEL)
IN_OFFSETS = tuple(int(o) for o in np.cumsum(IN_SIZES)[:-1])
D_IN = int(sum(IN_SIZES))

kernel_name = "hybrid_mla_conformer_conv_gated_block"


def rms_norm(x, g):
    x32 = x.astype(jnp.float32)
    y = x32 * lax.rsqrt(jnp.mean(x32 * x32, axis=-1, keepdims=True) + EPS)
    return y.astype(x.dtype) * g


def layer_norm(x, g, b):
    x32 = x.astype(jnp.float32)
    mu = jnp.mean(x32, axis=-1, keepdims=True)
    xc = x32 - mu
    var = jnp.mean(xc * xc, axis=-1, keepdims=True)
    y = xc * lax.rsqrt(var + EPS)
    return y.astype(x.dtype) * g + b


def rope_tables(positions):
    inv_freq = ROPE_THETA ** (-jnp.arange(0, QK_ROPE_DIM, 2, dtype=jnp.float32) / QK_ROPE_DIM)
    ang = positions.astype(jnp.float32)[..., None] * inv_freq
    return jnp.cos(ang), jnp.sin(ang)


def apply_rope(x, cos, sin):
    half = x.shape[-1] // 2
    x1, x2 = x[..., :half], x[..., half:]
    cos = cos.astype(x.dtype)
    sin = sin.astype(x.dtype)
    return jnp.concatenate([x1 * cos - x2 * sin, x2 * cos + x1 * sin], axis=-1)


def causal_block_attention(q_nope, q_rope, k_nope, k_rope, v):
    B, H, S, Dn = q_nope.shape
    Dr = q_rope.shape[-1]
    Dv = v.shape[-1]
    n_blk = S // Q_BLOCK
    qn = q_nope.reshape(B, H, n_blk, Q_BLOCK, Dn).transpose(2, 0, 1, 3, 4)
    qr = q_rope.reshape(B, H, n_blk, Q_BLOCK, Dr).transpose(2, 0, 1, 3, 4)
    scale = (Dn + Dr) ** -0.5
    key_idx = jnp.arange(S)

    def one_block(args):
        qn_b, qr_b, blk = args
        s = jnp.einsum('bhqd,bhkd->bhqk', qn_b, k_nope) + jnp.einsum('bhqd,bkd->bhqk', qr_b, k_rope)
        s = s.astype(jnp.float32) * scale
        q_idx = blk * Q_BLOCK + jnp.arange(Q_BLOCK)
        mask = key_idx[None, :] <= q_idx[:, None]
        s = jnp.where(mask, s, -jnp.inf)
        p = jax.nn.softmax(s, axis=-1).astype(v.dtype)
        return jnp.einsum('bhqk,bhkd->bhqd', p, v)

    out = lax.map(one_block, (qn, qr, jnp.arange(n_blk)))
    return out.transpose(1, 0, 3, 2, 4).reshape(B, S, H * Dv)


def hybrid_mixer(h, cos, sin, w_in, q_norm, w_uq, kv_norm, w_uk, w_uv, w_o_attn,
                 conv_w, conv_b, conv_ln_g, conv_ln_b, w_pw2, b_pw2, w_out):
    B, S, _ = h.shape
    z = h @ w_in
    c_q, c_kv, k_r, u_glu, gate_logits = jnp.split(z, IN_OFFSETS, axis=-1)

    q = (rms_norm(c_q, q_norm) @ w_uq).reshape(B, S, N_HEADS, QK_NOPE_DIM + QK_ROPE_DIM)
    q_nope = q[..., :QK_NOPE_DIM]
    q_rope = apply_rope(q[..., QK_NOPE_DIM:], cos[:, :, None, :], sin[:, :, None, :])
    c_kv_n = rms_norm(c_kv, kv_norm)
    k_nope = jnp.einsum('bsc,chd->bhsd', c_kv_n, w_uk)
    v = jnp.einsum('bsc,chd->bhsd', c_kv_n, w_uv)
    k_rope = apply_rope(k_r, cos, sin)
    attn = causal_block_attention(q_nope.transpose(0, 2, 1, 3), q_rope.transpose(0, 2, 1, 3),
                                  k_nope, k_rope, v)
    y_attn = attn @ w_o_attn

    a, b = jnp.split(u_glu, 2, axis=-1)
    u = a * jax.nn.sigmoid(b)
    u = lax.conv_general_dilated(u, conv_w, window_strides=(1,),
                                 padding=[(CONV_WIDTH - 1, 0)],
                                 dimension_numbers=('NWC', 'WIO', 'NWC'),
                                 feature_group_count=CONV_CHANNELS) + conv_b
    u = jax.nn.silu(layer_norm(u, conv_ln_g, conv_ln_b))
    y_conv = u @ w_pw2 + b_pw2

    g_attn, g_conv = jnp.split(gate_logits, N_BRANCHES, axis=-1)
    merged = jax.nn.sigmoid(g_attn) * y_attn + jax.nn.sigmoid(g_conv) * y_conv
    return merged @ w_out


def squared_relu_mlp(h, w_ff1, w_ff2):
    return jnp.square(jax.nn.relu(h @ w_ff1)) @ w_ff2


def _fwd_setup_inputs(seed: int = 0) -> dict:
    key = jax.random.key(seed)
    ks = jax.random.split(key, 24)
    f32 = jnp.float32

    def nrm(k, shape, scale):
        return jax.random.normal(k, shape, f32) * scale

    def gain(k, shape):
        return 1.0 + 0.01 * jax.random.normal(k, shape, f32)

    L = DEPTH
    x = jax.random.normal(ks[0], (BATCH, SEQ, D_MODEL), f32)
    offset = jax.random.randint(ks[1], (BATCH, 1), 0, 1024, dtype=jnp.int32)
    positions = (offset + jnp.arange(SEQ, dtype=jnp.int32)[None, :]).astype(jnp.int32)
    return {
        "x": x,
        "positions": positions,
        "norm_mix_pre": gain(ks[2], (L, D_MODEL)),
        "w_in": nrm(ks[3], (L, D_MODEL, D_IN), D_MODEL ** -0.5),
        "q_norm": gain(ks[4], (L, Q_LORA_RANK)),
        "w_uq": nrm(ks[5], (L, Q_LORA_RANK, N_HEADS * (QK_NOPE_DIM + QK_ROPE_DIM)), Q_LORA_RANK ** -0.5),
        "kv_norm": gain(ks[6], (L, KV_LORA_RANK)),
        "w_uk": nrm(ks[7], (L, KV_LORA_RANK, N_HEADS, QK_NOPE_DIM), KV_LORA_RANK ** -0.5),
        "w_uv": nrm(ks[8], (L, KV_LORA_RANK, N_HEADS, V_HEAD_DIM), KV_LORA_RANK ** -0.5),
        "w_o_attn": nrm(ks[9], (L, N_HEADS * V_HEAD_DIM, D_MODEL), (N_HEADS * V_HEAD_DIM) ** -0.5),
        "conv_w": nrm(ks[10], (L, CONV_WIDTH, 1, CONV_CHANNELS), CONV_WIDTH ** -0.5),
        "conv_b": nrm(ks[11], (L, CONV_CHANNELS), 0.01),
        "conv_ln_g": gain(ks[12], (L, CONV_CHANNELS)),
        "conv_ln_b": nrm(ks[13], (L, CONV_CHANNELS), 0.01),
        "w_pw2": nrm(ks[14], (L, CONV_CHANNELS, D_MODEL), CONV_CHANNELS ** -0.5),
        "b_pw2": nrm(ks[15], (L, D_MODEL), 0.01),
        "w_out": nrm(ks[16], (L, D_MODEL, D_MODEL), D_MODEL ** -0.5),
        "norm_mix_post": gain(ks[17], (L, D_MODEL)),
        "norm_mlp_pre": gain(ks[18], (L, D_MODEL)),
        "w_ff1": nrm(ks[19], (L, D_MODEL, D_FF), D_MODEL ** -0.5),
        "w_ff2": nrm(ks[20], (L, D_FF, D_MODEL), D_FF ** -0.5),
        "norm_mlp_post": gain(ks[21], (L, D_MODEL)),
    }


def _fwd_reference(x, positions, norm_mix_pre, w_in, q_norm, w_uq, kv_norm, w_uk, w_uv, w_o_attn,
              conv_w, conv_b, conv_ln_g, conv_ln_b, w_pw2, b_pw2, w_out, norm_mix_post,
              norm_mlp_pre, w_ff1, w_ff2, norm_mlp_post):
    cos, sin = rope_tables(positions)
    for l in range(DEPTH):
        h = rms_norm(x, norm_mix_pre[l])
        m = hybrid_mixer(h, cos, sin, w_in[l], q_norm[l], w_uq[l], kv_norm[l], w_uk[l], w_uv[l],
                         w_o_attn[l], conv_w[l], conv_b[l], conv_ln_g[l], conv_ln_b[l],
                         w_pw2[l], b_pw2[l], w_out[l])
        x = x + rms_norm(m, norm_mix_post[l])
        f = squared_relu_mlp(rms_norm(x, norm_mlp_pre[l]), w_ff1[l], w_ff2[l])
        x = x + rms_norm(f, norm_mlp_post[l])
    return x


import jax as _jax
import jax.numpy as _jnp

TWIN_FORMAT = 'train_step'
FWD_PARAMS = ['x', 'positions', 'norm_mix_pre', 'w_in', 'q_norm', 'w_uq', 'kv_norm', 'w_uk', 'w_uv', 'w_o_attn', 'conv_w', 'conv_b', 'conv_ln_g', 'conv_ln_b', 'w_pw2', 'b_pw2', 'w_out', 'norm_mix_post', 'norm_mlp_pre', 'w_ff1', 'w_ff2', 'norm_mlp_post']
TWIN_WEIGHTS = ['norm_mix_pre', 'w_in', 'q_norm', 'w_uq', 'kv_norm', 'w_uk', 'w_uv', 'w_o_attn', 'conv_w', 'conv_b', 'conv_ln_g', 'conv_ln_b', 'w_pw2', 'b_pw2', 'w_out', 'norm_mix_post', 'norm_mlp_pre', 'w_ff1', 'w_ff2', 'norm_mlp_post']
TWIN_DIFF_INPUT = 'x'
TWIN_INPUTS = ['x', 'positions', 'norm_mix_pre', 'w_in', 'q_norm', 'w_uq', 'kv_norm', 'w_uk', 'w_uv', 'w_o_attn', 'conv_w', 'conv_b', 'conv_ln_g', 'conv_ln_b', 'w_pw2', 'b_pw2', 'w_out', 'norm_mix_post', 'norm_mlp_pre', 'w_ff1', 'w_ff2', 'norm_mlp_post', 'loss_target', 'm_norm_mix_pre', 'm_w_in', 'm_q_norm', 'm_w_uq', 'm_kv_norm', 'm_w_uk', 'm_w_uv', 'm_w_o_attn', 'm_conv_w', 'm_conv_b', 'm_conv_ln_g', 'm_conv_ln_b', 'm_w_pw2', 'm_b_pw2', 'm_w_out', 'm_norm_mix_post', 'm_norm_mlp_pre', 'm_w_ff1', 'm_w_ff2', 'm_norm_mlp_post', 'v_norm_mix_pre', 'v_w_in', 'v_q_norm', 'v_w_uq', 'v_kv_norm', 'v_w_uk', 'v_w_uv', 'v_w_o_attn', 'v_conv_w', 'v_conv_b', 'v_conv_ln_g', 'v_conv_ln_b', 'v_w_pw2', 'v_b_pw2', 'v_w_out', 'v_norm_mix_post', 'v_norm_mlp_pre', 'v_w_ff1', 'v_w_ff2', 'v_norm_mlp_post']
TWIN_OUTPUTS = ['loss', 'grad_x', 'grad_norm_mix_pre', 'grad_w_in', 'grad_q_norm', 'grad_w_uq', 'grad_kv_norm', 'grad_w_uk', 'grad_w_uv', 'grad_w_o_attn', 'grad_conv_w', 'grad_conv_b', 'grad_conv_ln_g', 'grad_conv_ln_b', 'grad_w_pw2', 'grad_b_pw2', 'grad_w_out', 'grad_norm_mix_post', 'grad_norm_mlp_pre', 'grad_w_ff1', 'grad_w_ff2', 'grad_norm_mlp_post', 'delta_norm_mix_pre', 'delta_w_in', 'delta_q_norm', 'delta_w_uq', 'delta_kv_norm', 'delta_w_uk', 'delta_w_uv', 'delta_w_o_attn', 'delta_conv_w', 'delta_conv_b', 'delta_conv_ln_g', 'delta_conv_ln_b', 'delta_w_pw2', 'delta_b_pw2', 'delta_w_out', 'delta_norm_mix_post', 'delta_norm_mlp_pre', 'delta_w_ff1', 'delta_w_ff2', 'delta_norm_mlp_post', 'new_m_norm_mix_pre', 'new_m_w_in', 'new_m_q_norm', 'new_m_w_uq', 'new_m_kv_norm', 'new_m_w_uk', 'new_m_w_uv', 'new_m_w_o_attn', 'new_m_conv_w', 'new_m_conv_b', 'new_m_conv_ln_g', 'new_m_conv_ln_b', 'new_m_w_pw2', 'new_m_b_pw2', 'new_m_w_out', 'new_m_norm_mix_post', 'new_m_norm_mlp_pre', 'new_m_w_ff1', 'new_m_w_ff2', 'new_m_norm_mlp_post', 'new_v_norm_mix_pre', 'new_v_w_in', 'new_v_q_norm', 'new_v_w_uq', 'new_v_kv_norm', 'new_v_w_uk', 'new_v_w_uv', 'new_v_w_o_attn', 'new_v_conv_w', 'new_v_conv_b', 'new_v_conv_ln_g', 'new_v_conv_ln_b', 'new_v_w_pw2', 'new_v_b_pw2', 'new_v_w_out', 'new_v_norm_mix_post', 'new_v_norm_mlp_pre', 'new_v_w_ff1', 'new_v_w_ff2', 'new_v_norm_mlp_post']
TWIN_LEAF_KINDS = {'loss': 'loss', 'grad_x': 'grad_x', 'grad_norm_mix_pre': 'grad_w', 'grad_w_in': 'grad_w', 'grad_q_norm': 'grad_w', 'grad_w_uq': 'grad_w', 'grad_kv_norm': 'grad_w', 'grad_w_uk': 'grad_w', 'grad_w_uv': 'grad_w', 'grad_w_o_attn': 'grad_w', 'grad_conv_w': 'grad_w', 'grad_conv_b': 'grad_w', 'grad_conv_ln_g': 'grad_w', 'grad_conv_ln_b': 'grad_w', 'grad_w_pw2': 'grad_w', 'grad_b_pw2': 'grad_w', 'grad_w_out': 'grad_w', 'grad_norm_mix_post': 'grad_w', 'grad_norm_mlp_pre': 'grad_w', 'grad_w_ff1': 'grad_w', 'grad_w_ff2': 'grad_w', 'grad_norm_mlp_post': 'grad_w', 'delta_norm_mix_pre': 'delta_w', 'delta_w_in': 'delta_w', 'delta_q_norm': 'delta_w', 'delta_w_uq': 'delta_w', 'delta_kv_norm': 'delta_w', 'delta_w_uk': 'delta_w', 'delta_w_uv': 'delta_w', 'delta_w_o_attn': 'delta_w', 'delta_conv_w': 'delta_w', 'delta_conv_b': 'delta_w', 'delta_conv_ln_g': 'delta_w', 'delta_conv_ln_b': 'delta_w', 'delta_w_pw2': 'delta_w', 'delta_b_pw2': 'delta_w', 'delta_w_out': 'delta_w', 'delta_norm_mix_post': 'delta_w', 'delta_norm_mlp_pre': 'delta_w', 'delta_w_ff1': 'delta_w', 'delta_w_ff2': 'delta_w', 'delta_norm_mlp_post': 'delta_w', 'new_m_norm_mix_pre': 'new_m', 'new_m_w_in': 'new_m', 'new_m_q_norm': 'new_m', 'new_m_w_uq': 'new_m', 'new_m_kv_norm': 'new_m', 'new_m_w_uk': 'new_m', 'new_m_w_uv': 'new_m', 'new_m_w_o_attn': 'new_m', 'new_m_conv_w': 'new_m', 'new_m_conv_b': 'new_m', 'new_m_conv_ln_g': 'new_m', 'new_m_conv_ln_b': 'new_m', 'new_m_w_pw2': 'new_m', 'new_m_b_pw2': 'new_m', 'new_m_w_out': 'new_m', 'new_m_norm_mix_post': 'new_m', 'new_m_norm_mlp_pre': 'new_m', 'new_m_w_ff1': 'new_m', 'new_m_w_ff2': 'new_m', 'new_m_norm_mlp_post': 'new_m', 'new_v_norm_mix_pre': 'new_v', 'new_v_w_in': 'new_v', 'new_v_q_norm': 'new_v', 'new_v_w_uq': 'new_v', 'new_v_kv_norm': 'new_v', 'new_v_w_uk': 'new_v', 'new_v_w_uv': 'new_v', 'new_v_w_o_attn': 'new_v', 'new_v_conv_w': 'new_v', 'new_v_conv_b': 'new_v', 'new_v_conv_ln_g': 'new_v', 'new_v_conv_ln_b': 'new_v', 'new_v_w_pw2': 'new_v', 'new_v_b_pw2': 'new_v', 'new_v_w_out': 'new_v', 'new_v_norm_mix_post': 'new_v', 'new_v_norm_mlp_pre': 'new_v', 'new_v_w_ff1': 'new_v', 'new_v_w_ff2': 'new_v', 'new_v_norm_mlp_post': 'new_v'}


def _forward(args):
    return _fwd_reference(*[args[k] for k in FWD_PARAMS])


def _output_shape():
    def fwd():
        inp = _fwd_setup_inputs(0)
        return _fwd_reference(*[inp[k] for k in FWD_PARAMS])
    out = _jax.eval_shape(fwd)
    return out.shape, out.dtype

N_MICROBATCH = 1
ADAM_LR = 0.001
ADAM_B1 = 0.9
ADAM_B2 = 0.999
ADAM_EPS = 1e-08
ADAM_WD = 0.01
ADAM_STEP = 10
PER_EXAMPLE_BATCH_AXIS = {'x': 0, 'positions': 0, 'loss_target': 0}
SHARED_INPUTS = []
_WEIGHT_DTYPES = {'norm_mix_pre': _jnp.float32, 'w_in': _jnp.float32, 'q_norm': _jnp.float32, 'w_uq': _jnp.float32, 'kv_norm': _jnp.float32, 'w_uk': _jnp.float32, 'w_uv': _jnp.float32, 'w_o_attn': _jnp.float32, 'conv_w': _jnp.float32, 'conv_b': _jnp.float32, 'conv_ln_g': _jnp.float32, 'conv_ln_b': _jnp.float32, 'w_pw2': _jnp.float32, 'b_pw2': _jnp.float32, 'w_out': _jnp.float32, 'norm_mix_post': _jnp.float32, 'norm_mlp_pre': _jnp.float32, 'w_ff1': _jnp.float32, 'w_ff2': _jnp.float32, 'norm_mlp_post': _jnp.float32}
MOMENT_SCALE = {'norm_mix_pre': 6.019235e-01, 'w_in': 2.631750e-01, 'q_norm': 1.911216e-01, 'w_uq': 9.135913e-02, 'kv_norm': 4.761463e-01, 'w_uk': 9.456724e-02, 'w_uv': 2.066215e-01, 'w_o_attn': 2.074199e-01, 'conv_w': 1.233259e+00, 'conv_b': 2.038837e+01, 'conv_ln_g': 7.358912e+00, 'conv_ln_b': 1.041775e+01, 'w_pw2': 4.709660e+00, 'b_pw2': 2.139310e+01, 'w_out': 4.994501e+00, 'norm_mix_post': 3.295454e+01, 'norm_mlp_pre': 1.976831e+00, 'w_ff1': 9.942265e-01, 'w_ff2': 4.827363e+00, 'norm_mlp_post': 3.359551e+01}


def _to_microbatches(a, axis):
    t = _jnp.moveaxis(a, axis, 0)
    t = t.reshape((N_MICROBATCH, t.shape[0] // N_MICROBATCH) + t.shape[1:])
    return _jnp.moveaxis(t, 1, axis + 1)


def setup_inputs(seed: int = 0) -> dict:
    inp = _fwd_setup_inputs(seed)
    key = _jax.random.fold_in(_jax.random.key(seed), 7919)
    shape, _ = _output_shape()
    out = dict(inp)
    out["loss_target"] = _jax.random.normal(_jax.random.fold_in(key, 0), shape, _jnp.float32)
    for i, name in enumerate(TWIN_WEIGHTS):
        w = inp[name].astype(_jnp.float32)
        if MOMENT_SCALE is None:
            s = _jnp.sqrt(_jnp.mean(_jnp.square(w)) + 1e-30)
        else:
            s = MOMENT_SCALE[name]
        km, kv = _jax.random.split(_jax.random.fold_in(key, i + 1))
        out[name] = w
        out["m_" + name] = s * _jax.random.normal(km, w.shape, _jnp.float32)
        out["v_" + name] = (s * s) * _jax.random.uniform(kv, w.shape, _jnp.float32, 0.5, 1.5)
    if N_MICROBATCH > 1:
        for name, axis in PER_EXAMPLE_BATCH_AXIS.items():
            out[name] = _to_microbatches(out[name], axis)
    return {'x': out['x'], 'positions': out['positions'], 'norm_mix_pre': out['norm_mix_pre'], 'w_in': out['w_in'], 'q_norm': out['q_norm'], 'w_uq': out['w_uq'], 'kv_norm': out['kv_norm'], 'w_uk': out['w_uk'], 'w_uv': out['w_uv'], 'w_o_attn': out['w_o_attn'], 'conv_w': out['conv_w'], 'conv_b': out['conv_b'], 'conv_ln_g': out['conv_ln_g'], 'conv_ln_b': out['conv_ln_b'], 'w_pw2': out['w_pw2'], 'b_pw2': out['b_pw2'], 'w_out': out['w_out'], 'norm_mix_post': out['norm_mix_post'], 'norm_mlp_pre': out['norm_mlp_pre'], 'w_ff1': out['w_ff1'], 'w_ff2': out['w_ff2'], 'norm_mlp_post': out['norm_mlp_post'], 'loss_target': out['loss_target'], 'm_norm_mix_pre': out['m_norm_mix_pre'], 'm_w_in': out['m_w_in'], 'm_q_norm': out['m_q_norm'], 'm_w_uq': out['m_w_uq'], 'm_kv_norm': out['m_kv_norm'], 'm_w_uk': out['m_w_uk'], 'm_w_uv': out['m_w_uv'], 'm_w_o_attn': out['m_w_o_attn'], 'm_conv_w': out['m_conv_w'], 'm_conv_b': out['m_conv_b'], 'm_conv_ln_g': out['m_conv_ln_g'], 'm_conv_ln_b': out['m_conv_ln_b'], 'm_w_pw2': out['m_w_pw2'], 'm_b_pw2': out['m_b_pw2'], 'm_w_out': out['m_w_out'], 'm_norm_mix_post': out['m_norm_mix_post'], 'm_norm_mlp_pre': out['m_norm_mlp_pre'], 'm_w_ff1': out['m_w_ff1'], 'm_w_ff2': out['m_w_ff2'], 'm_norm_mlp_post': out['m_norm_mlp_post'], 'v_norm_mix_pre': out['v_norm_mix_pre'], 'v_w_in': out['v_w_in'], 'v_q_norm': out['v_q_norm'], 'v_w_uq': out['v_w_uq'], 'v_kv_norm': out['v_kv_norm'], 'v_w_uk': out['v_w_uk'], 'v_w_uv': out['v_w_uv'], 'v_w_o_attn': out['v_w_o_attn'], 'v_conv_w': out['v_conv_w'], 'v_conv_b': out['v_conv_b'], 'v_conv_ln_g': out['v_conv_ln_g'], 'v_conv_ln_b': out['v_conv_ln_b'], 'v_w_pw2': out['v_w_pw2'], 'v_b_pw2': out['v_b_pw2'], 'v_w_out': out['v_w_out'], 'v_norm_mix_post': out['v_norm_mix_post'], 'v_norm_mlp_pre': out['v_norm_mlp_pre'], 'v_w_ff1': out['v_w_ff1'], 'v_w_ff2': out['v_w_ff2'], 'v_norm_mlp_post': out['v_norm_mlp_post']}


def _loss(weights, diff, rest, loss_target):
    with _jax.named_scope("forward"):
        args = {**rest, TWIN_DIFF_INPUT: diff, **{k: w.astype(_WEIGHT_DTYPES[k]) for k, w in weights.items()}}
        y = _forward(args)
    with _jax.named_scope("loss_head"):
        err = _jnp.square(y.astype(_jnp.float32) - loss_target)
        return 0.5 * _jnp.sum(_jnp.mean(err, axis=-1)) if err.ndim else 0.5 * err


def _adamw(w, g, m, v):
    m = ADAM_B1 * m + (1.0 - ADAM_B1) * g
    v = ADAM_B2 * v + (1.0 - ADAM_B2) * _jnp.square(g)
    m_hat = m / (1.0 - ADAM_B1 ** ADAM_STEP)
    v_hat = v / (1.0 - ADAM_B2 ** ADAM_STEP)
    delta = -ADAM_LR * (m_hat / (_jnp.sqrt(v_hat) + ADAM_EPS) + ADAM_WD * w)
    return delta, m, v


def reference(x, positions, norm_mix_pre, w_in, q_norm, w_uq, kv_norm, w_uk, w_uv, w_o_attn, conv_w, conv_b, conv_ln_g, conv_ln_b, w_pw2, b_pw2, w_out, norm_mix_post, norm_mlp_pre, w_ff1, w_ff2, norm_mlp_post, loss_target, m_norm_mix_pre, m_w_in, m_q_norm, m_w_uq, m_kv_norm, m_w_uk, m_w_uv, m_w_o_attn, m_conv_w, m_conv_b, m_conv_ln_g, m_conv_ln_b, m_w_pw2, m_b_pw2, m_w_out, m_norm_mix_post, m_norm_mlp_pre, m_w_ff1, m_w_ff2, m_norm_mlp_post, v_norm_mix_pre, v_w_in, v_q_norm, v_w_uq, v_kv_norm, v_w_uk, v_w_uv, v_w_o_attn, v_conv_w, v_conv_b, v_conv_ln_g, v_conv_ln_b, v_w_pw2, v_b_pw2, v_w_out, v_norm_mix_post, v_norm_mlp_pre, v_w_ff1, v_w_ff2, v_norm_mlp_post):
    given = dict(x=x, positions=positions, norm_mix_pre=norm_mix_pre, w_in=w_in, q_norm=q_norm, w_uq=w_uq, kv_norm=kv_norm, w_uk=w_uk, w_uv=w_uv, w_o_attn=w_o_attn, conv_w=conv_w, conv_b=conv_b, conv_ln_g=conv_ln_g, conv_ln_b=conv_ln_b, w_pw2=w_pw2, b_pw2=b_pw2, w_out=w_out, norm_mix_post=norm_mix_post, norm_mlp_pre=norm_mlp_pre, w_ff1=w_ff1, w_ff2=w_ff2, norm_mlp_post=norm_mlp_post, loss_target=loss_target, m_norm_mix_pre=m_norm_mix_pre, m_w_in=m_w_in, m_q_norm=m_q_norm, m_w_uq=m_w_uq, m_kv_norm=m_kv_norm, m_w_uk=m_w_uk, m_w_uv=m_w_uv, m_w_o_attn=m_w_o_attn, m_conv_w=m_conv_w, m_conv_b=m_conv_b, m_conv_ln_g=m_conv_ln_g, m_conv_ln_b=m_conv_ln_b, m_w_pw2=m_w_pw2, m_b_pw2=m_b_pw2, m_w_out=m_w_out, m_norm_mix_post=m_norm_mix_post, m_norm_mlp_pre=m_norm_mlp_pre, m_w_ff1=m_w_ff1, m_w_ff2=m_w_ff2, m_norm_mlp_post=m_norm_mlp_post, v_norm_mix_pre=v_norm_mix_pre, v_w_in=v_w_in, v_q_norm=v_q_norm, v_w_uq=v_w_uq, v_kv_norm=v_kv_norm, v_w_uk=v_w_uk, v_w_uv=v_w_uv, v_w_o_attn=v_w_o_attn, v_conv_w=v_conv_w, v_conv_b=v_conv_b, v_conv_ln_g=v_conv_ln_g, v_conv_ln_b=v_conv_ln_b, v_w_pw2=v_w_pw2, v_b_pw2=v_b_pw2, v_w_out=v_w_out, v_norm_mix_post=v_norm_mix_post, v_norm_mlp_pre=v_norm_mlp_pre, v_w_ff1=v_w_ff1, v_w_ff2=v_w_ff2, v_norm_mlp_post=v_norm_mlp_post)
    weights = {n: given[n] for n in TWIN_WEIGHTS}
    shared = {n: given[n] for n in SHARED_INPUTS}
    per_example = {n: given[n] for n in ['x', 'positions']}
    grad_fn = _jax.value_and_grad(_loss, argnums=(0, 1))

    def one_microbatch(ex, loss_target):
        ex = dict(ex)
        diff = ex.pop(TWIN_DIFF_INPUT)
        return grad_fn(weights, diff, {**shared, **ex}, loss_target)

    if N_MICROBATCH == 1:
        loss, (grad_w, grad_x) = one_microbatch(per_example, given["loss_target"])
    else:
        def body(carry, xs):
            loss_sum, grad_sum = carry
            l_k, (gw_k, gx_k) = one_microbatch(xs[0], xs[1])
            with _jax.named_scope("update"):
                return (loss_sum + l_k, _jax.tree.map(_jnp.add, grad_sum, gw_k)), gx_k

        init = (_jnp.zeros((), _jnp.float32), _jax.tree.map(_jnp.zeros_like, weights))
        (loss, grad_w), grad_x = _jax.lax.scan(body, init, (per_example, given["loss_target"]))
    with _jax.named_scope("update"):
        delta_w, new_m, new_v = {}, {}, {}
        for n in TWIN_WEIGHTS:
            delta_w[n], new_m[n], new_v[n] = _adamw(weights[n], grad_w[n], given["m_" + n], given["v_" + n])
    return (loss, grad_x, *[grad_w[n] for n in TWIN_WEIGHTS], *[delta_w[n] for n in TWIN_WEIGHTS],
            *[new_m[n] for n in TWIN_WEIGHTS], *[new_v[n] for n in TWIN_WEIGHTS])
```

```python
import functools

import numpy as np
import jax
import jax.numpy as jnp
from jax import lax
from jax.experimental import pallas as pl
from jax.experimental.pallas import tpu as pltpu

F32 = jnp.float32
BF16 = jnp.bfloat16

D_MODEL = 1024
N_HEADS = 8
NOPE = 128
ROPE = 64
HALF = ROPE // 2
Q_RANK = 384
KV_RANK = 256
CONV_W = 31
D_FF = 4096
EPS = 1e-6
ROPE_THETA = 10000.0
HEAD_PAD = 256
QK_SCALE = (NOPE + ROPE) ** -0.5
N_DEV = 8

ZS = Q_RANK + KV_RANK + 128
OFF_A = ZS
OFF_B = OFF_A + D_MODEL
OFF_GA = OFF_B + D_MODEL
OFF_GC = OFF_GA + D_MODEL
D_IN_PAD = OFF_GC + D_MODEL

ADAM_LR = 0.001
ADAM_B1 = 0.9
ADAM_B2 = 0.999
ADAM_EPS = 1e-08
ADAM_WD = 0.01
ADAM_STEP = 10

VMEM_LIMIT = 56 * 1024 * 1024

_PACK = (("w_in", 600), ("w_uq", 72), ("w_uk", 32), ("w_uv", 32), ("w_o_attn", 128), ("w_pw2", 128),
         ("w_out", 128), ("w_ff1", 512), ("w_ff2", 512), ("conv_hi", 4), ("conv_lo", 4))
_PACK_OFF = {}
_o = 0
for _n, _r in _PACK:
    _PACK_OFF[_n] = (_o, _r)
    _o += _r
PACK_ROWS = -(-_o // 16) * 16
CONV_SHARD = CONV_W * (D_MODEL // N_DEV)

_SMALL = ("norm_mix_pre", "q_norm", "kv_norm", "conv_b", "conv_ln_g", "conv_ln_b", "b_pw2", "norm_mix_post",
          "norm_mlp_pre", "norm_mlp_post")
SMALL_ROWS = 16
LOSS_ROW = len(_SMALL)

_BIG = ("w_in", "w_uq", "w_uk", "w_uv", "w_o_attn", "conv_w", "w_pw2", "w_out", "w_ff1", "w_ff2")
_WEIGHTS = ("norm_mix_pre", "w_in", "q_norm", "w_uq", "kv_norm", "w_uk", "w_uv", "w_o_attn", "conv_w", "conv_b",
            "conv_ln_g", "conv_ln_b", "w_pw2", "b_pw2", "w_out", "norm_mix_post", "norm_mlp_pre", "w_ff1", "w_ff2",
            "norm_mlp_post")


def _dot(a, b):
    return jnp.dot(a, b, preferred_element_type=F32)


def _dot_nt(a, b):
    return lax.dot_general(a, b, (((1,), (1,)), ((), ())), preferred_element_type=F32)


def _dot_tn(a, b):
    return lax.dot_general(a, b, (((0,), (0,)), ((), ())), preferred_element_type=F32)


def _sigmoid(x):
    return 1.0 / (1.0 + jnp.exp(-x))


def _rms_fwd(x, g):
    r = lax.rsqrt(jnp.mean(x * x, axis=-1, keepdims=True) + EPS)
    return x * r * g


def _rms_bwd(dy, x, g):
    r = lax.rsqrt(jnp.mean(x * x, axis=-1, keepdims=True) + EPS)
    xh = x * r
    gy = dy * g
    dx = r * (gy - xh * jnp.mean(gy * xh, axis=-1, keepdims=True))
    return dx, jnp.sum(dy * xh, axis=0, keepdims=True)


def _rope(q, c, sa, sb):
    n = q.shape[-1]
    return q * c + pltpu.roll(q, n - HALF, 1) * sa + pltpu.roll(q, HALF, 1) * sb


def _rope_bwd(d, c, sa, sb):
    n = d.shape[-1]
    return d * c - pltpu.roll(d, n - HALF, 1) * sa - pltpu.roll(d, HALF, 1) * sb


def _params(n_axes=1):
    return pltpu.CompilerParams(dimension_semantics=("arbitrary",) * n_axes, vmem_limit_bytes=VMEM_LIMIT)


def _row_call(body, name, tb, row_ins, full_ins, row_outs, acc_outs):
    t = row_ins[0].shape[0]
    in_specs = [pl.BlockSpec((tb, a.shape[1]), lambda i: (i, 0)) for a in row_ins]
    in_specs += [pl.BlockSpec(a.shape, lambda i, nd=a.ndim: (0,) * nd) for a in full_ins]
    out_specs = [pl.BlockSpec((tb, c), lambda i: (i, 0)) for c, _ in row_outs]
    out_specs += [pl.BlockSpec(s, lambda i, nd=len(s): (0,) * nd) for s, _ in acc_outs]
    out_shape = [jax.ShapeDtypeStruct((t, c), dt) for c, dt in row_outs]
    out_shape += [jax.ShapeDtypeStruct(s, dt) for s, dt in acc_outs]
    return pl.pallas_call(
        functools.partial(body), name=name, grid=(t // tb,), in_specs=in_specs, out_specs=out_specs,
        out_shape=out_shape, compiler_params=_params(1),
    )(*row_ins, *full_ins)


def _acc(ref, val):
    @pl.when(pl.program_id(0) == 0)
    def _():
        ref[...] = jnp.zeros_like(ref)
    ref[...] += val


def _fwd_in_proj(x, g_pre, w_in_p, tb=256):
    def body(x_ref, g_ref, w_ref, h_ref, zs_ref, a_ref, sb_ref, u_ref, sa_ref, sc_ref):
        hb = _rms_fwd(x_ref[...], g_ref[...]).astype(BF16)
        h_ref[...] = hb
        zs_ref[...] = _dot(hb, w_ref[:, 0:ZS])
        a = _dot(hb, w_ref[:, OFF_A:OFF_B])
        sb = _sigmoid(_dot(hb, w_ref[:, OFF_B:OFF_GA]))
        a_ref[...] = a
        sb_ref[...] = sb
        u_ref[...] = a * sb
        sa_ref[...] = _sigmoid(_dot(hb, w_ref[:, OFF_GA:OFF_GC]))
        sc_ref[...] = _sigmoid(_dot(hb, w_ref[:, OFF_GC:D_IN_PAD]))

    d = D_MODEL
    return _row_call(body, "fwd_in_proj", tb, [x], [g_pre, w_in_p],
                     [(d, BF16), (ZS, F32), (d, F32), (d, F32), (d, F32), (d, F32), (d, F32)], [])


def _fwd_qkv(zs, tc, tsa, tsb, q_norm, kv_norm, w_uq_p, w_uk, w_uv, tb=256):
    def body(zs_ref, c_ref, sa_ref, sb_ref, qg_ref, kg_ref, wq_ref, wk_ref, wv_ref,
             cqn_ref, ckvn_ref, q_ref, k_ref, v_ref):
        zs_ = zs_ref[...]
        c, sa, sb = c_ref[...], sa_ref[...], sb_ref[...]
        cqn = _rms_fwd(zs_[:, 0:Q_RANK], qg_ref[...]).astype(BF16)
        cqn_ref[...] = cqn
        q = _dot(cqn, wq_ref[...])
        q = _rope(q, jnp.tile(c, (1, N_HEADS)), jnp.tile(sa, (1, N_HEADS)), jnp.tile(sb, (1, N_HEADS)))
        q_ref[...] = q.astype(BF16)
        kr = zs_[:, Q_RANK + KV_RANK:ZS]
        kr = _rope(kr, c[:, NOPE:], sa[:, NOPE:], sb[:, NOPE:]).astype(BF16)
        ckvn = _rms_fwd(zs_[:, Q_RANK:Q_RANK + KV_RANK], kg_ref[...]).astype(BF16)
        ckvn_ref[...] = ckvn
        kn = _dot(ckvn, wk_ref[...]).astype(BF16)
        v_ref[...] = _dot(ckvn, wv_ref[...]).astype(BF16)
        for h in range(N_HEADS):
            k_ref[:, h * HEAD_PAD:h * HEAD_PAD + NOPE] = kn[:, h * NOPE:(h + 1) * NOPE]
            k_ref[:, h * HEAD_PAD + NOPE:(h + 1) * HEAD_PAD] = kr

    hp = N_HEADS * HEAD_PAD
    return _row_call(body, "fwd_qkv", tb, [zs, tc, tsa, tsb], [q_norm, kv_norm, w_uq_p, w_uk, w_uv],
                     [(Q_RANK, BF16), (KV_RANK, BF16), (hp, BF16), (hp, BF16), (D_MODEL, BF16)], [])


def _attn_fwd(q, k, v, tq=512):
    t = q.shape[0]
    nq = t // tq

    def body(q_ref, k_ref, v_ref, o_ref, lse_ref, m_sc, l_sc, acc_sc):
        i, j = pl.program_id(1), pl.program_id(2)

        @pl.when(j == 0)
        def _():
            m_sc[...] = jnp.full_like(m_sc, -1e30)
            l_sc[...] = jnp.zeros_like(l_sc)
            acc_sc[...] = jnp.zeros_like(acc_sc)

        def step(masked):
            s = _dot_nt(q_ref[...], k_ref[...]) * QK_SCALE
            if masked:
                row = lax.broadcasted_iota(jnp.int32, (tq, tq), 0)
                col = lax.broadcasted_iota(jnp.int32, (tq, tq), 1)
                s = jnp.where(col <= row, s, -1e30)
            m_prev = m_sc[...]
            m_new = jnp.maximum(m_prev, jnp.max(s, axis=1, keepdims=True))
            alpha = jnp.exp(m_prev - m_new)
            p = jnp.exp(s - m_new)
            l_sc[...] = alpha * l_sc[...] + jnp.sum(p, axis=1, keepdims=True)
            acc_sc[...] = alpha * acc_sc[...] + _dot(p.astype(BF16), v_ref[...])
            m_sc[...] = m_new

        @pl.when(j < i)
        def _():
            step(False)

        @pl.when(j == i)
        def _():
            step(True)
            l = l_sc[...]
            o_ref[...] = (acc_sc[...] / l).astype(BF16)
            lse_ref[...] = jnp.broadcast_to(m_sc[...] + jnp.log(l), (tq, NOPE))

    return pl.pallas_call(
        functools.partial(body), name="attn_fwd", grid=(N_HEADS, nq, nq),
        in_specs=[pl.BlockSpec((tq, HEAD_PAD), lambda h, i, j: (i, h)),
                  pl.BlockSpec((tq, HEAD_PAD), lambda h, i, j: (jnp.minimum(i, j), h)),
                  pl.BlockSpec((tq, NOPE), lambda h, i, j: (jnp.minimum(i, j), h))],
        out_specs=[pl.BlockSpec((tq, NOPE), lambda h, i, j: (i, h)),
                   pl.BlockSpec((tq, NOPE), lambda h, i, j: (i, h))],
        out_shape=[jax.ShapeDtypeStruct((t, D_MODEL), BF16), jax.ShapeDtypeStruct((t, D_MODEL), F32)],
        scratch_shapes=[pltpu.VMEM((tq, 1), F32), pltpu.VMEM((tq, 1), F32), pltpu.VMEM((tq, NOPE), F32)],
        compiler_params=_params(3),
    )(q, k, v)


def _conv_fwd(u, conv_w, conv_b, ln_g, ln_b, tb=256):
    t, c = u.shape
    halo = 32

    def body(u_ref, up_ref, w_ref, b_ref, g_ref, be_ref, co_ref, act_ref, buf):
        i = pl.program_id(0)
        buf[0:halo, :] = jnp.where(i == 0, 0.0, up_ref[...])
        buf[halo:halo + tb, :] = u_ref[...]
        acc = jnp.zeros((tb, c), F32)
        for k in range(CONV_W):
            acc = acc + w_ref[k:k + 1, :] * buf[pl.ds(halo - (CONV_W - 1) + k, tb), :]
        co = acc + b_ref[...]
        co_ref[...] = co
        mu = jnp.mean(co, axis=-1, keepdims=True)
        xc = co - mu
        r = lax.rsqrt(jnp.mean(xc * xc, axis=-1, keepdims=True) + EPS)
        y = xc * r * g_ref[...] + be_ref[...]
        act_ref[...] = (y * _sigmoid(y)).astype(BF16)

    ratio = tb // halo
    return pl.pallas_call(
        functools.partial(body), name="conv_fwd", grid=(t // tb,),
        in_specs=[pl.BlockSpec((tb, c), lambda i: (i, 0)),
                  pl.BlockSpec((halo, c), lambda i: (jnp.maximum(i * ratio - 1, 0), 0)),
                  pl.BlockSpec(conv_w.shape, lambda i: (0, 0)),
                  pl.BlockSpec((1, c), lambda i: (0, 0)), pl.BlockSpec((1, c), lambda i: (0, 0)),
                  pl.BlockSpec((1, c), lambda i: (0, 0))],
        out_specs=[pl.BlockSpec((tb, c), lambda i: (i, 0)), pl.BlockSpec((tb, c), lambda i: (i, 0))],
        out_shape=[jax.ShapeDtypeStruct((t, c), F32), jax.ShapeDtypeStruct((t, c), BF16)],
        scratch_shapes=[pltpu.VMEM((tb + halo, c), F32)],
        compiler_params=_params(1),
    )(u, u, conv_w, conv_b, ln_g, ln_b)


def _fwd_merge(attn, uact, sa, sc, x, w_o, w_pw2, b_pw2, w_out, g_post, tb=256):
    def body(at_ref, ua_ref, sa_ref, sc_ref, x_ref, wo_ref, wp_ref, bp_ref, wout_ref, g_ref,
             ya_ref, yc_ref, mb_ref, m_ref, x1_ref):
        ya = _dot(at_ref[...], wo_ref[...])
        yc = _dot(ua_ref[...], wp_ref[...]) + bp_ref[...]
        ya_ref[...] = ya
        yc_ref[...] = yc
        mb = (sa_ref[...] * ya + sc_ref[...] * yc).astype(BF16)
        mb_ref[...] = mb
        m = _dot(mb, wout_ref[...])
        m_ref[...] = m
        x1_ref[...] = x_ref[...] + _rms_fwd(m, g_ref[...])

    d = D_MODEL
    return _row_call(body, "fwd_merge", tb, [attn, uact, sa, sc, x], [w_o, w_pw2, b_pw2, w_out, g_post],
                     [(d, F32), (d, F32), (d, BF16), (d, F32), (d, F32)], [])


def _fwd_ff1(x1, g, w_ff1, tb=256):
    def body(x1_ref, g_ref, w_ref, h2_ref, r1_ref, act_ref):
        h2 = _rms_fwd(x1_ref[...], g_ref[...]).astype(BF16)
        h2_ref[...] = h2
        r1 = jnp.maximum(_dot(h2, w_ref[...]), 0.0)
        r1_ref[...] = r1.astype(BF16)
        act_ref[...] = (r1 * r1).astype(BF16)

    return _row_call(body, "fwd_ff1", tb, [x1], [g, w_ff1], [(D_MODEL, BF16), (D_FF, BF16), (D_FF, BF16)], [])


def _fwd_ff2_loss(act, x1, target, w_ff2, g, tb=256):
    def body(act_ref, x1_ref, tg_ref, w_ref, g_ref, f_ref, dy_ref, loss_ref):
        f = _dot(act_ref[...], w_ref[...])
        f_ref[...] = f
        e = x1_ref[...] + _rms_fwd(f, g_ref[...]) - tg_ref[...]
        dy_ref[...] = e * (1.0 / D_MODEL)
        _acc(loss_ref, jnp.sum(e * e))

    return _row_call(body, "fwd_ff2_loss", tb, [act, x1, target], [w_ff2, g],
                     [(D_MODEL, F32), (D_MODEL, F32)], [((8, 128), F32)])


def _bwd_ff2(dy, f, r1, w_ff2, g, tb=256):
    def body(dy_ref, f_ref, r1_ref, w_ref, g_ref, df_ref, df1_ref, dg_ref):
        df, dg = _rms_bwd(dy_ref[...], f_ref[...], g_ref[...])
        _acc(dg_ref, dg)
        dfb = df.astype(BF16)
        df_ref[...] = dfb
        dact = _dot_nt(dfb, w_ref[...])
        df1_ref[...] = (dact * (2.0 * r1_ref[...].astype(F32))).astype(BF16)

    return _row_call(body, "bwd_ff2", tb, [dy, f, r1], [w_ff2, g], [(D_MODEL, BF16), (D_FF, BF16)],
                     [((1, D_MODEL), F32)])


def _bwd_ff1(df1, x1, dy, w_ff1, g, tb=256):
    def body(df1_ref, x1_ref, dy_ref, w_ref, g_ref, dx1_ref, dg_ref):
        dh2 = _dot_nt(df1_ref[...], w_ref[...])
        dxn, dg = _rms_bwd(dh2, x1_ref[...], g_ref[...])
        _acc(dg_ref, dg)
        dx1_ref[...] = dy_ref[...] + dxn

    return _row_call(body, "bwd_ff1", tb, [df1, x1, dy], [w_ff1, g], [(D_MODEL, F32)], [((1, D_MODEL), F32)])


def _bwd_merge(dx1, m, sa, sc, ya, yc, attn, w_out, w_o, w_pw2, g_post, tb=256):
    def body(dx1_ref, m_ref, sa_ref, sc_ref, ya_ref, yc_ref, at_ref, wout_ref, wo_ref, wp_ref, g_ref,
             dm_ref, dya_ref, dyc_ref, dga_ref, dgc_ref, dat_ref, delta_ref, dua_ref, dg_ref, dbp_ref):
        dm, dg = _rms_bwd(dx1_ref[...], m_ref[...], g_ref[...])
        _acc(dg_ref, dg)
        dmb = dm.astype(BF16)
        dm_ref[...] = dmb
        dmerged = _dot_nt(dmb, wout_ref[...])
        sa, sc = sa_ref[...], sc_ref[...]
        dya = dmerged * sa
        dyc = dmerged * sc
        _acc(dbp_ref, jnp.sum(dyc, axis=0, keepdims=True))
        dyab = dya.astype(BF16)
        dycb = dyc.astype(BF16)
        dya_ref[...] = dyab
        dyc_ref[...] = dycb
        dga_ref[...] = (dmerged * ya_ref[...] * sa * (1.0 - sa)).astype(BF16)
        dgc_ref[...] = (dmerged * yc_ref[...] * sc * (1.0 - sc)).astype(BF16)
        dat = _dot_nt(dyab, wo_ref[...])
        dat_ref[...] = dat.astype(BF16)
        prod = dat * at_ref[...].astype(F32)
        for h in range(N_HEADS):
            dl = jnp.sum(prod[:, h * NOPE:(h + 1) * NOPE], axis=1, keepdims=True)
            delta_ref[:, h * NOPE:(h + 1) * NOPE] = jnp.broadcast_to(dl, (tb, NOPE))
        dua_ref[...] = _dot_nt(dycb, wp_ref[...])

    d = D_MODEL
    return _row_call(body, "bwd_merge", tb, [dx1, m, sa, sc, ya, yc, attn], [w_out, w_o, w_pw2, g_post],
                     [(d, BF16), (d, BF16), (d, BF16), (d, BF16), (d, BF16), (d, BF16), (d, F32), (d, F32)],
                     [((1, d), F32), ((1, d), F32)])


def _bwd_ln(dua, co, ln_g, ln_b, tb=256):
    def body(dua_ref, co_ref, g_ref, be_ref, dco_ref, dg_ref, db_ref, dcb_ref):
        co = co_ref[...]
        g = g_ref[...]
        mu = jnp.mean(co, axis=-1, keepdims=True)
        xc = co - mu
        r = lax.rsqrt(jnp.mean(xc * xc, axis=-1, keepdims=True) + EPS)
        xh = xc * r
        y = xh * g + be_ref[...]
        s = _sigmoid(y)
        dy = dua_ref[...] * (s + y * s * (1.0 - s))
        _acc(db_ref, jnp.sum(dy, axis=0, keepdims=True))
        _acc(dg_ref, jnp.sum(dy * xh, axis=0, keepdims=True))
        gy = dy * g
        dco = r * (gy - jnp.mean(gy, axis=-1, keepdims=True) - xh * jnp.mean(gy * xh, axis=-1, keepdims=True))
        dco_ref[...] = dco
        _acc(dcb_ref, jnp.sum(dco, axis=0, keepdims=True))

    d = D_MODEL
    return _row_call(body, "bwd_ln", tb, [dua, co], [ln_g, ln_b], [(d, F32)],
                     [((1, d), F32), ((1, d), F32), ((1, d), F32)])


def _conv_bwd(dco, u, a, sb, conv_w, tb=256):
    t, c = u.shape
    halo = 32
    ratio = tb // halo
    nblk = t // tb

    def body(d_ref, dn_ref, u_ref, up_ref, a_ref, sb_ref, w_ref, da_ref, db_ref, dw_ref, bufd, bufu):
        i = pl.program_id(0)

        @pl.when(i == 0)
        def _():
            dw_ref[...] = jnp.zeros_like(dw_ref)

        dco = d_ref[...]
        bufd[0:tb, :] = dco
        bufd[tb:tb + halo, :] = jnp.where(i == nblk - 1, 0.0, dn_ref[...])
        bufu[0:halo, :] = jnp.where(i == 0, 0.0, up_ref[...])
        bufu[halo:halo + tb, :] = u_ref[...]
        du = jnp.zeros((tb, c), F32)
        for k in range(CONV_W):
            du = du + w_ref[k:k + 1, :] * bufd[pl.ds(CONV_W - 1 - k, tb), :]
            dw_ref[k:k + 1, :] += jnp.sum(dco * bufu[pl.ds(halo - (CONV_W - 1) + k, tb), :], axis=0, keepdims=True)
        sb_ = sb_ref[...]
        da_ref[...] = (du * sb_).astype(BF16)
        db_ref[...] = (du * a_ref[...] * sb_ * (1.0 - sb_)).astype(BF16)

    return pl.pallas_call(
        functools.partial(body), name="conv_bwd", grid=(nblk,),
        in_specs=[pl.BlockSpec((tb, c), lambda i: (i, 0)),
                  pl.BlockSpec((halo, c), lambda i: (jnp.minimum((i + 1) * ratio, t // halo - 1), 0)),
                  pl.BlockSpec((tb, c), lambda i: (i, 0)),
                  pl.BlockSpec((halo, c), lambda i: (jnp.maximum(i * ratio - 1, 0), 0)),
                  pl.BlockSpec((tb, c), lambda i: (i, 0)), pl.BlockSpec((tb, c), lambda i: (i, 0)),
                  pl.BlockSpec(conv_w.shape, lambda i: (0, 0))],
        out_specs=[pl.BlockSpec((tb, c), lambda i: (i, 0)), pl.BlockSpec((tb, c), lambda i: (i, 0)),
                   pl.BlockSpec((32, c), lambda i: (0, 0))],
        out_shape=[jax.ShapeDtypeStruct((t, c), BF16), jax.ShapeDtypeStruct((t, c), BF16),
                   jax.ShapeDtypeStruct((32, c), F32)],
        scratch_shapes=[pltpu.VMEM((tb + halo, c), F32), pltpu.VMEM((tb + halo, c), F32)],
        compiler_params=_params(1),
    )(dco, dco, u, u, a, sb, conv_w)


def _attn_bwd(q, k, v, do, lse, delta, tq=512):
    t = q.shape[0]
    nq = t // tq

    def body(q_ref, k_ref, v_ref, do_ref, lse_ref, dl_ref, dq_ref, dk_ref, dv_ref, dk_sc, dv_sc):
        j, i = pl.program_id(1), pl.program_id(2)

        @pl.when(i == 0)
        def _():
            dk_sc[...] = jnp.zeros_like(dk_sc)
            dv_sc[...] = jnp.zeros_like(dv_sc)

        rows = pl.ds(pl.multiple_of(i * tq, tq), tq)

        @pl.when(j == 0)
        def _():
            dq_ref[rows, :] = jnp.zeros((tq, HEAD_PAD), F32)

        def step(masked):
            qb, kb, dob = q_ref[...], k_ref[...], do_ref[...]
            s = _dot_nt(qb, kb) * QK_SCALE
            p = jnp.exp(s - lse_ref[:, 0:1])
            if masked:
                row = lax.broadcasted_iota(jnp.int32, (tq, tq), 0)
                col = lax.broadcasted_iota(jnp.int32, (tq, tq), 1)
                p = jnp.where(col <= row, p, 0.0)
            dv_sc[...] += _dot_tn(p.astype(BF16), dob)
            dp = _dot_nt(dob, v_ref[...])
            ds = (p * (dp - dl_ref[:, 0:1]) * QK_SCALE).astype(BF16)
            dk_sc[...] += _dot_tn(ds, qb)
            dq_ref[rows, :] += _dot(ds, kb)

        @pl.when(i > j)
        def _():
            step(False)

        @pl.when(i == j)
        def _():
            step(True)

        @pl.when(i == nq - 1)
        def _():
            dk_ref[...] = dk_sc[...]
            dv_ref[...] = dv_sc[...].astype(BF16)

    hp = N_HEADS * HEAD_PAD
    return pl.pallas_call(
        functools.partial(body), name="attn_bwd", grid=(N_HEADS, nq, nq),
        in_specs=[pl.BlockSpec((tq, HEAD_PAD), lambda h, j, i: (jnp.maximum(i, j), h)),
                  pl.BlockSpec((tq, HEAD_PAD), lambda h, j, i: (j, h)),
                  pl.BlockSpec((tq, NOPE), lambda h, j, i: (j, h)),
                  pl.BlockSpec((tq, NOPE), lambda h, j, i: (jnp.maximum(i, j), h)),
                  pl.BlockSpec((tq, NOPE), lambda h, j, i: (jnp.maximum(i, j), h)),
                  pl.BlockSpec((tq, NOPE), lambda h, j, i: (jnp.maximum(i, j), h))],
        out_specs=[pl.BlockSpec((t, HEAD_PAD), lambda h, j, i: (0, h)),
                   pl.BlockSpec((tq, HEAD_PAD), lambda h, j, i: (j, h)),
                   pl.BlockSpec((tq, NOPE), lambda h, j, i: (j, h))],
        out_shape=[jax.ShapeDtypeStruct((t, hp), F32), jax.ShapeDtypeStruct((t, hp), F32),
                   jax.ShapeDtypeStruct((t, D_MODEL), BF16)],
        scratch_shapes=[pltpu.VMEM((tq, HEAD_PAD), F32), pltpu.VMEM((tq, NOPE), F32)],
        compiler_params=_params(3),
    )(q, k, v, do, lse, delta)


def _bwd_qkv(dq, dk, dv, zs, tc, tsa, tsb, q_norm, kv_norm, w_uq_p, w_uk, w_uv, tb=256):
    def body(dq_ref, dk_ref, dv_ref, zs_ref, c_ref, sa_ref, sb_ref, qg_ref, kg_ref, wq_ref, wk_ref, wv_ref,
             dqp_ref, dkn_ref, dzs_ref, dqg_ref, dkg_ref):
        c, sa, sb = c_ref[...], sa_ref[...], sb_ref[...]
        zs_ = zs_ref[...]
        dqp = _rope_bwd(dq_ref[...], jnp.tile(c, (1, N_HEADS)), jnp.tile(sa, (1, N_HEADS)),
                        jnp.tile(sb, (1, N_HEADS))).astype(BF16)
        dqp_ref[...] = dqp
        dcq, dqg = _rms_bwd(_dot_nt(dqp, wq_ref[...]), zs_[:, 0:Q_RANK], qg_ref[...])
        _acc(dqg_ref, dqg)
        dzs_ref[:, 0:Q_RANK] = dcq.astype(BF16)
        dkr = jnp.zeros((tb, NOPE), F32)
        for h in range(N_HEADS):
            dkn_ref[:, h * NOPE:(h + 1) * NOPE] = dk_ref[:, h * HEAD_PAD:h * HEAD_PAD + NOPE].astype(BF16)
            dkr = dkr + dk_ref[:, h * HEAD_PAD + NOPE:(h + 1) * HEAD_PAD]
        dzs_ref[:, Q_RANK + KV_RANK:ZS] = _rope_bwd(dkr, c[:, NOPE:], sa[:, NOPE:], sb[:, NOPE:]).astype(BF16)
        dckvn = _dot_nt(dkn_ref[...], wk_ref[...]) + _dot_nt(dv_ref[...], wv_ref[...])
        dckv, dkg = _rms_bwd(dckvn, zs_[:, Q_RANK:Q_RANK + KV_RANK], kg_ref[...])
        _acc(dkg_ref, dkg)
        dzs_ref[:, Q_RANK:Q_RANK + KV_RANK] = dckv.astype(BF16)

    hp = N_HEADS * HEAD_PAD
    return _row_call(body, "bwd_qkv", tb, [dq, dk, dv, zs, tc, tsa, tsb], [q_norm, kv_norm, w_uq_p, w_uk, w_uv],
                     [(hp, BF16), (D_MODEL, BF16), (ZS, BF16)], [((1, Q_RANK), F32), ((1, KV_RANK), F32)])


def _bwd_in_proj(dzs, da, db, dga, dgc, x, dx1, w_in_p, g_pre, tb=256):
    def body(dzs_ref, da_ref, db_ref, dga_ref, dgc_ref, x_ref, dx1_ref, w_ref, g_ref, gx_ref, dg_ref):
        dh = _dot_nt(dzs_ref[...], w_ref[:, 0:ZS])
        dh = dh + _dot_nt(da_ref[...], w_ref[:, OFF_A:OFF_B])
        dh = dh + _dot_nt(db_ref[...], w_ref[:, OFF_B:OFF_GA])
        dh = dh + _dot_nt(dga_ref[...], w_ref[:, OFF_GA:OFF_GC])
        dh = dh + _dot_nt(dgc_ref[...], w_ref[:, OFF_GC:D_IN_PAD])
        dxn, dg = _rms_bwd(dh, x_ref[...], g_ref[...])
        _acc(dg_ref, dg)
        gx_ref[...] = dx1_ref[...] + dxn

    return _row_call(body, "bwd_in_proj", tb, [dzs, da, db, dga, dgc, x, dx1], [w_in_p, g_pre],
                     [(D_MODEL, F32)], [((1, D_MODEL), F32)])


def _mm_tn(a, b, name, tt=512):
    t, m = a.shape
    n = b.shape[1]
    tm, tn = min(m, 1024), min(n, 1024)

    def body(a_ref, b_ref, o_ref):
        @pl.when(pl.program_id(2) == 0)
        def _():
            o_ref[...] = jnp.zeros_like(o_ref)
        o_ref[...] += _dot_tn(a_ref[...], b_ref[...])

    return pl.pallas_call(
        functools.partial(body), name=name, grid=(m // tm, n // tn, t // tt),
        in_specs=[pl.BlockSpec((tt, tm), lambda i, j, k: (k, i)), pl.BlockSpec((tt, tn), lambda i, j, k: (k, j))],
        out_specs=pl.BlockSpec((tm, tn), lambda i, j, k: (i, j)),
        out_shape=jax.ShapeDtypeStruct((m, n), F32),
        compiler_params=_params(3),
    )(a, b)


_ANY = pl.BlockSpec(memory_space=pl.ANY)
_MESH = pl.DeviceIdType.MESH


def _all_gather(shard, name):
    m_per, n = shard.shape

    def body(x_ref, out_ref, send_sems, recv_sems, local_sem):
        x, y, c = lax.axis_index("x"), lax.axis_index("y"), lax.axis_index("c")
        me, sibling = (x, y, c), (x, y, 1 - c)
        chips = [(1 - x, y), (x, 1 - y), (1 - x, 1 - y)]

        def rows(px, py, pc):
            return out_ref.at[pl.ds((4 * px + 2 * py + pc) * m_per, m_per), :]

        def copy(k, block, to, src=None):
            return pltpu.make_async_remote_copy(
                src_ref=rows(*block) if src is None else src, dst_ref=rows(*block),
                send_sem=send_sems.at[k], recv_sem=recv_sems.at[k], device_id=to, device_id_type=_MESH)

        mine = pltpu.make_async_copy(x_ref, rows(*me), local_sem)
        mine.start()
        first = [copy(0, me, sibling, src=x_ref)]
        first += [copy(1 + j, me, (*chip, c), src=x_ref) for j, chip in enumerate(chips)]
        for cp in first:
            cp.start()
        passed = [copy(4 + j, (*chip, c), sibling) for j, chip in enumerate(chips)]
        for j, chip in enumerate(chips):
            copy(1 + j, (*chip, c), me).wait_recv()
            passed[j].start()
        copy(0, sibling, me).wait_recv()
        for j, chip in enumerate(chips):
            copy(4 + j, (*chip, 1 - c), me).wait_recv()
        for cp in first + passed:
            cp.wait_send()
        mine.wait()

    return pl.pallas_call(
        functools.partial(body), name=name, out_shape=jax.ShapeDtypeStruct((N_DEV * m_per, n), shard.dtype),
        in_specs=[_ANY], out_specs=_ANY,
        scratch_shapes=[pltpu.SemaphoreType.DMA((7,)), pltpu.SemaphoreType.DMA((7,)), pltpu.SemaphoreType.DMA],
    )(shard)


def _pair_exchange(g):
    _, r, n = g.shape

    def body(g_ref, l_ref, send_sems, recv_sems):
        x, y, c = lax.axis_index("x"), lax.axis_index("y"), lax.axis_index("c")
        copies = [pltpu.make_async_remote_copy(
            src_ref=g_ref.at[2 * k + (1 - c)], dst_ref=l_ref.at[k], send_sem=send_sems.at[k],
            recv_sem=recv_sems.at[k], device_id=(x, y, 1 - c), device_id_type=_MESH) for k in range(4)]
        for cp in copies:
            cp.start()
        for cp in copies:
            cp.wait()

    return pl.pallas_call(
        functools.partial(body), name="rs_pair_exchange", out_shape=jax.ShapeDtypeStruct((4, r, n), g.dtype),
        in_specs=[_ANY], out_specs=_ANY,
        scratch_shapes=[pltpu.SemaphoreType.DMA((4,)), pltpu.SemaphoreType.DMA((4,))],
    )(g)


def _pair_sum(g, l, c_idx, tr=720):
    _, r, n = g.shape

    def body(c_ref, g_ref, l_ref, p_ref):
        p_ref[...] = (g_ref[...].astype(F32) + l_ref[...].astype(F32)).astype(BF16)

    return pl.pallas_call(
        functools.partial(body), name="rs_pair_sum",
        grid_spec=pltpu.PrefetchScalarGridSpec(
            num_scalar_prefetch=1, grid=(4, r // tr),
            in_specs=[pl.BlockSpec((1, tr, n), lambda k, i, c: (2 * k + c[0], i, 0)),
                      pl.BlockSpec((1, tr, n), lambda k, i, c: (k, i, 0))],
            out_specs=pl.BlockSpec((1, tr, n), lambda k, i, c: (k, i, 0))),
        out_shape=jax.ShapeDtypeStruct((4, r, n), BF16), compiler_params=_params(2),
    )(c_idx, g, l)


def _chip_exchange(p):
    _, r, n = p.shape

    def body(p_ref, q_ref, send_sems, recv_sems):
        x, y, c = lax.axis_index("x"), lax.axis_index("y"), lax.axis_index("c")
        chips = [(1 - x, y), (x, 1 - y), (1 - x, 1 - y)]
        copies = [pltpu.make_async_remote_copy(
            src_ref=p_ref.at[2 * px + py], dst_ref=q_ref.at[s], send_sem=send_sems.at[s],
            recv_sem=recv_sems.at[s], device_id=(px, py, c), device_id_type=_MESH)
            for s, (px, py) in enumerate(chips)]
        for cp in copies:
            cp.start()
        for cp in copies:
            cp.wait()

    return pl.pallas_call(
        functools.partial(body), name="rs_chip_exchange", out_shape=jax.ShapeDtypeStruct((3, r, n), p.dtype),
        in_specs=[_ANY], out_specs=_ANY,
        scratch_shapes=[pltpu.SemaphoreType.DMA((3,)), pltpu.SemaphoreType.DMA((3,))],
    )(p)


def _adamw(w, g, m, v):
    m2 = ADAM_B1 * m + (1.0 - ADAM_B1) * g
    v2 = ADAM_B2 * v + (1.0 - ADAM_B2) * (g * g)
    m_hat = m2 / (1.0 - ADAM_B1 ** ADAM_STEP)
    v_hat = v2 / (1.0 - ADAM_B2 ** ADAM_STEP)
    delta = -ADAM_LR * (m_hat / (jnp.sqrt(v_hat) + ADAM_EPS) + ADAM_WD * w)
    return delta, m2, v2


def _update_big(g, l, q, w, m, v, idx, tr=240):
    _, r, n = g.shape

    def body(idx_ref, g_ref, l_ref, q_ref, w_ref, m_ref, v_ref, go_ref, d_ref, mo_ref, vo_ref):
        gr = g_ref[0].astype(F32) + l_ref[0].astype(F32)
        gr = gr + q_ref[0].astype(F32)
        gr = gr + q_ref[1].astype(F32)
        gr = gr + q_ref[2].astype(F32)
        go_ref[...] = gr
        d_ref[...], mo_ref[...], vo_ref[...] = _adamw(w_ref[...], gr, m_ref[...], v_ref[...])

    row = pl.BlockSpec((tr, n), lambda i, s: (i, 0))
    return pl.pallas_call(
        functools.partial(body), name="update_big",
        grid_spec=pltpu.PrefetchScalarGridSpec(
            num_scalar_prefetch=1, grid=(r // tr,),
            in_specs=[pl.BlockSpec((1, tr, n), lambda i, s: (s[0], i, 0)),
                      pl.BlockSpec((1, tr, n), lambda i, s: (s[1], i, 0)),
                      pl.BlockSpec((3, tr, n), lambda i, s: (0, i, 0)), row, row, row],
            out_specs=[row, row, row, row]),
        out_shape=[jax.ShapeDtypeStruct((r, n), F32)] * 4, compiler_params=_params(1),
    )(idx, g, l, q, w, m, v)


def _update_small(gathered, w, m, v):
    r, n = w.shape

    def body(ga_ref, w_ref, m_ref, v_ref, go_ref, d_ref, mo_ref, vo_ref):
        gr = ga_ref[0:r, :]
        for dev in range(1, N_DEV):
            gr = gr + ga_ref[dev * r:(dev + 1) * r, :]
        go_ref[...] = gr
        d_ref[...], mo_ref[...], vo_ref[...] = _adamw(w_ref[...], gr, m_ref[...], v_ref[...])

    return pl.pallas_call(
        functools.partial(body), name="update_small", out_shape=[jax.ShapeDtypeStruct((r, n), F32)] * 4,
    )(gathered, w, m, v)


def _rows1024(a):
    return a.reshape(-1, 1024)


def _pad_rows(a, rows):
    return jnp.pad(a, ((0, rows - a.shape[0]), (0, 0)))


def _conv_rows(conv_shard):
    return jnp.pad(conv_shard.reshape(-1), (0, 4096 - CONV_SHARD)).reshape(4, 1024)


def _pack_local(shards, conv_hi, conv_lo):
    parts = [_rows1024(shards[n]) for n in ("w_in", "w_uq", "w_uk", "w_uv", "w_o_attn", "w_pw2", "w_out", "w_ff1",
                                            "w_ff2")]
    parts += [conv_hi, conv_lo]
    return _pad_rows(jnp.concatenate(parts, axis=0), PACK_ROWS)


def _unpack_local(p, like):
    out = {}
    for n in _BIG:
        key = "conv_hi" if n == "conv_w" else n
        off, r = _PACK_OFF[key]
        flat = p[off:off + r].reshape(-1)
        out[n] = flat[:like[n].size].reshape(like[n].shape)
    return out


def _unpack_gathered(wg):
    def blk(name):
        off, r = _PACK_OFF[name]
        return wg[:, off:off + r]

    w_in = blk("w_in").reshape(N_DEV, D_MODEL, 600).transpose(1, 0, 2).reshape(D_MODEL, 4800)
    kr_off = Q_RANK + KV_RANK
    w_in_p = jnp.concatenate([w_in[:, :kr_off + ROPE], jnp.zeros((D_MODEL, 128 - ROPE), BF16),
                              w_in[:, kr_off + ROPE:]], axis=1)
    w_uq = blk("w_uq").reshape(N_DEV, Q_RANK, NOPE + ROPE).transpose(1, 0, 2)
    w_uq_p = jnp.pad(w_uq, ((0, 0), (0, 0), (0, HEAD_PAD - NOPE - ROPE))).reshape(Q_RANK, N_HEADS * HEAD_PAD)
    w_uk = blk("w_uk").reshape(KV_RANK, N_HEADS * NOPE)
    w_uv = blk("w_uv").reshape(KV_RANK, N_HEADS * NOPE)
    w_o = blk("w_o_attn").reshape(D_MODEL, D_MODEL)
    w_pw2 = blk("w_pw2").reshape(D_MODEL, D_MODEL)
    w_out = blk("w_out").reshape(D_MODEL, D_MODEL)
    w_ff1 = blk("w_ff1").reshape(N_DEV, D_MODEL, D_FF // N_DEV).transpose(1, 0, 2).reshape(D_MODEL, D_FF)
    w_ff2 = blk("w_ff2").reshape(D_FF, D_MODEL)

    def conv(name):
        c = blk(name).reshape(N_DEV, 4096)[:, :CONV_SHARD].reshape(N_DEV, CONV_W, D_MODEL // N_DEV)
        return c.transpose(1, 0, 2).reshape(CONV_W, D_MODEL).astype(F32)

    conv_w = jnp.pad(conv("conv_hi") + conv("conv_lo"), ((0, 1), (0, 0)))
    return w_in_p, w_uq_p, w_uk, w_uv, w_o, w_pw2, w_out, w_ff1, w_ff2, conv_w


def _pack_grads(dw_in, dw_uq, dw_uk, dw_uv, dw_o, dw_pw2, dw_out, dw_ff1, dw_ff2, dconv):
    def cols(a, per):
        return a.reshape(a.shape[0], N_DEV, per).transpose(1, 0, 2).reshape(N_DEV, -1, 1024)

    conv = dconv.reshape(CONV_W, N_DEV, D_MODEL // N_DEV).transpose(1, 0, 2).reshape(N_DEV, CONV_SHARD)
    conv = jnp.pad(conv, ((0, 0), (0, 4096 - CONV_SHARD))).reshape(N_DEV, 4, 1024)
    parts = [cols(dw_in, 600), cols(dw_uq, NOPE + ROPE), dw_uk.reshape(N_DEV, -1, 1024),
             dw_uv.reshape(N_DEV, -1, 1024), dw_o.reshape(N_DEV, -1, 1024), dw_pw2.reshape(N_DEV, -1, 1024),
             dw_out.reshape(N_DEV, -1, 1024), cols(dw_ff1, D_FF // N_DEV), dw_ff2.reshape(N_DEV, -1, 1024),
             conv, jnp.zeros_like(conv)]
    g = jnp.concatenate(parts, axis=1)
    g = jnp.pad(g, ((0, 0), (0, PACK_ROWS - g.shape[1]), (0, 0)))
    return g.astype(BF16)


def _small_pack(vals):
    rows = [jnp.pad(v.reshape(1, -1), ((0, 0), (0, 1024 - v.size))) for v in vals]
    return _pad_rows(jnp.concatenate(rows, axis=0), SMALL_ROWS)


def _rope_tables(positions):
    inv_freq = ROPE_THETA ** (-jnp.arange(0, ROPE, 2, dtype=F32) / ROPE)
    ang = positions.reshape(-1).astype(F32)[:, None] * inv_freq
    cos, sin = jnp.cos(ang), jnp.sin(ang)
    t = cos.shape[0]
    z32, z64 = jnp.zeros((t, HALF), F32), jnp.zeros((t, HEAD_PAD - NOPE - ROPE), F32)
    z128 = jnp.zeros((t, NOPE), F32)
    tc = jnp.concatenate([jnp.ones((t, NOPE), F32), cos, cos, z64], axis=1)
    tsa = jnp.concatenate([z128, -sin, z32, z64], axis=1)
    tsb = jnp.concatenate([z128, z32, sin, z64], axis=1)
    return tc, tsa, tsb


def _local_step(x, positions, target, small, weights):
    w_in_p, w_uq_p, w_uk, w_uv, w_o, w_pw2, w_out, w_ff1, w_ff2, conv_w = weights
    tc, tsa, tsb = _rope_tables(positions)

    h, zs, a, sb, u, sa, sc = _fwd_in_proj(x, small["norm_mix_pre"], w_in_p)
    cqn, ckvn, q, k, v = _fwd_qkv(zs, tc, tsa, tsb, small["q_norm"], small["kv_norm"], w_uq_p, w_uk, w_uv)
    attn, lse = _attn_fwd(q, k, v)
    co, uact = _conv_fwd(u, conv_w, small["conv_b"], small["conv_ln_g"], small["conv_ln_b"])
    ya, yc, mb, m, x1 = _fwd_merge(attn, uact, sa, sc, x, w_o, w_pw2, small["b_pw2"], w_out, small["norm_mix_post"])
    h2, r1, act = _fwd_ff1(x1, small["norm_mlp_pre"], w_ff1)
    f, dy, loss_blk = _fwd_ff2_loss(act, x1, target, w_ff2, small["norm_mlp_post"])

    df, df1, dg_mlp_post = _bwd_ff2(dy, f, r1, w_ff2, small["norm_mlp_post"])
    dx1, dg_mlp_pre = _bwd_ff1(df1, x1, dy, w_ff1, small["norm_mlp_pre"])
    dw_ff2 = _mm_tn(act, df, "dw_ff2")
    dw_ff1 = _mm_tn(h2, df1, "dw_ff1")
    dmb, dya, dyc, dga, dgc, dat, delta, dua, dg_mix_post, db_pw2 = _bwd_merge(
        dx1, m, sa, sc, ya, yc, attn, w_out, w_o, w_pw2, small["norm_mix_post"])
    dw_out = _mm_tn(mb, dmb, "dw_out")
    dw_o = _mm_tn(attn, dya, "dw_o")
    dw_pw2 = _mm_tn(uact, dyc, "dw_pw2")
    dco, dln_g, dln_b, dconv_b = _bwd_ln(dua, co, small["conv_ln_g"], small["conv_ln_b"])
    da, db, dconv = _conv_bwd(dco, u, a, sb, conv_w)
    dq, dk, dv = _attn_bwd(q, k, v, dat, lse, delta)
    dqp, dkn, dzs, dq_norm, dkv_norm = _bwd_qkv(dq, dk, dv, zs, tc, tsa, tsb, small["q_norm"], small["kv_norm"],
                                                w_uq_p, w_uk, w_uv)
    dw_uq_p = _mm_tn(cqn, dqp, "dw_uq")
    dw_uk = _mm_tn(ckvn, dkn, "dw_uk")
    dw_uv = _mm_tn(ckvn, dv, "dw_uv")
    grad_x, dg_pre = _bwd_in_proj(dzs, da, db, dga, dgc, x, dx1, w_in_p, small["norm_mix_pre"])
    dw_zs = _mm_tn(h, dzs, "dw_in_zs")
    dw_a = _mm_tn(h, da, "dw_in_a")
    dw_b = _mm_tn(h, db, "dw_in_b")
    dw_ga = _mm_tn(h, dga, "dw_in_ga")
    dw_gc = _mm_tn(h, dgc, "dw_in_gc")

    dw_in = jnp.concatenate([dw_zs[:, :Q_RANK + KV_RANK + ROPE], dw_a, dw_b, dw_ga, dw_gc], axis=1)
    dw_uq = dw_uq_p.reshape(Q_RANK, N_HEADS, HEAD_PAD)[:, :, :NOPE + ROPE].reshape(Q_RANK, -1)
    big = (dw_in, dw_uq, dw_uk, dw_uv, dw_o, dw_pw2, dw_out, dw_ff1, dw_ff2, dconv[:CONV_W])
    small_grads = (dg_pre, dq_norm, dkv_norm, dconv_b, dln_g, dln_b, db_pw2, dg_mix_post, dg_mlp_pre, dg_mlp_post)
    return loss_blk, grad_x, small_grads, big


def kernel(x, positions, norm_mix_pre, w_in, q_norm, w_uq, kv_norm, w_uk, w_uv, w_o_attn, conv_w, conv_b, conv_ln_g, conv_ln_b, w_pw2, b_pw2, w_out, norm_mix_post, norm_mlp_pre, w_ff1, w_ff2, norm_mlp_post, loss_target, m_norm_mix_pre, m_w_in, m_q_norm, m_w_uq, m_kv_norm, m_w_uk, m_w_uv, m_w_o_attn, m_conv_w, m_conv_b, m_conv_ln_g, m_conv_ln_b, m_w_pw2, m_b_pw2, m_w_out, m_norm_mix_post, m_norm_mlp_pre, m_w_ff1, m_w_ff2, m_norm_mlp_post, v_norm_mix_pre, v_w_in, v_q_norm, v_w_uq, v_kv_norm, v_w_uk, v_w_uv, v_w_o_attn, v_conv_w, v_conv_b, v_conv_ln_g, v_conv_ln_b, v_w_pw2, v_b_pw2, v_w_out, v_norm_mix_post, v_norm_mlp_pre, v_w_ff1, v_w_ff2, v_norm_mlp_post):
    wts = dict(norm_mix_pre=norm_mix_pre, w_in=w_in, q_norm=q_norm, w_uq=w_uq, kv_norm=kv_norm, w_uk=w_uk, w_uv=w_uv,
               w_o_attn=w_o_attn, conv_w=conv_w, conv_b=conv_b, conv_ln_g=conv_ln_g, conv_ln_b=conv_ln_b,
               w_pw2=w_pw2, b_pw2=b_pw2, w_out=w_out, norm_mix_post=norm_mix_post, norm_mlp_pre=norm_mlp_pre,
               w_ff1=w_ff1, w_ff2=w_ff2, norm_mlp_post=norm_mlp_post)
    mom_m = dict(norm_mix_pre=m_norm_mix_pre, w_in=m_w_in, q_norm=m_q_norm, w_uq=m_w_uq, kv_norm=m_kv_norm,
                 w_uk=m_w_uk, w_uv=m_w_uv, w_o_attn=m_w_o_attn, conv_w=m_conv_w, conv_b=m_conv_b,
                 conv_ln_g=m_conv_ln_g, conv_ln_b=m_conv_ln_b, w_pw2=m_w_pw2, b_pw2=m_b_pw2, w_out=m_w_out,
                 norm_mix_post=m_norm_mix_post, norm_mlp_pre=m_norm_mlp_pre, w_ff1=m_w_ff1, w_ff2=m_w_ff2,
                 norm_mlp_post=m_norm_mlp_post)
    mom_v = dict(norm_mix_pre=v_norm_mix_pre, w_in=v_w_in, q_norm=v_q_norm, w_uq=v_w_uq, kv_norm=v_kv_norm,
                 w_uk=v_w_uk, w_uv=v_w_uv, w_o_attn=v_w_o_attn, conv_w=v_conv_w, conv_b=v_conv_b,
                 conv_ln_g=v_conv_ln_g, conv_ln_b=v_conv_ln_b, w_pw2=v_w_pw2, b_pw2=v_b_pw2, w_out=v_w_out,
                 norm_mix_post=v_norm_mix_post, norm_mlp_pre=v_norm_mlp_pre, w_ff1=v_w_ff1, w_ff2=v_w_ff2,
                 norm_mlp_post=v_norm_mlp_post)
    cx, cy, cc = lax.axis_index("x"), lax.axis_index("y"), lax.axis_index("c")

    big_local = {n: wts[n] for n in _BIG}
    conv_rows = _conv_rows(conv_w)
    conv_hi = conv_rows.astype(BF16)
    conv_lo = (conv_rows - conv_hi.astype(F32)).astype(BF16)
    send = _pack_local({n: big_local[n].astype(BF16) for n in _BIG}, conv_hi, conv_lo)
    gathered = _all_gather(send, "ag_weights").reshape(N_DEV, PACK_ROWS, 1024)
    weights = _unpack_gathered(gathered)

    small = {n: wts[n].reshape(1, -1) for n in _SMALL}
    loss_blk, grad_x, small_grads, big_grads = _local_step(x[0], positions, loss_target[0], small, weights)

    g_all = _pack_grads(*big_grads)
    l_sib = _pair_exchange(g_all)
    p_chip = _pair_sum(g_all, l_sib, cc.reshape(1).astype(jnp.int32))
    q_in = _chip_exchange(p_chip)
    idx = jnp.stack([4 * cx + 2 * cy + cc, 2 * cx + cy]).astype(jnp.int32)
    zero_rows = jnp.zeros((4, 1024), F32)
    w_pk = _pack_local(big_local, conv_rows, zero_rows)
    m_pk = _pack_local(mom_m, _conv_rows(m_conv_w), zero_rows)
    v_pk = _pack_local(mom_v, _conv_rows(v_conv_w), zero_rows)
    g_pk, d_pk, m_new_pk, v_new_pk = _update_big(g_all, l_sib, q_in, w_pk, m_pk, v_pk, idx)
    out_g, out_d = _unpack_local(g_pk, big_local), _unpack_local(d_pk, big_local)
    out_m, out_v = _unpack_local(m_new_pk, big_local), _unpack_local(v_new_pk, big_local)

    loss_row = jnp.broadcast_to(loss_blk[0:1, 0:1], (1, 1024))
    sv = _small_pack(list(small_grads) + [loss_row])
    sv_all = _all_gather(sv, "ag_small")
    sw = _small_pack([wts[n] for n in _SMALL])
    sm = _small_pack([mom_m[n] for n in _SMALL])
    s_v = _small_pack([mom_v[n] for n in _SMALL])
    sg, sd, sm_new, sv_new = _update_small(sv_all, sw, sm, s_v)
    for i, n in enumerate(_SMALL):
        size = wts[n].size
        out_g[n] = sg[i, :size].reshape(wts[n].shape)
        out_d[n] = sd[i, :size].reshape(wts[n].shape)
        out_m[n] = sm_new[i, :size].reshape(wts[n].shape)
        out_v[n] = sv_new[i, :size].reshape(wts[n].shape)
    loss = sg[LOSS_ROW, 0] * (0.5 / D_MODEL)

    return (loss, grad_x[None], *[out_g[n] for n in _WEIGHTS], *[out_d[n] for n in _WEIGHTS],
            *[out_m[n] for n in _WEIGHTS], *[out_v[n] for n in _WEIGHTS])
```

```python
import functools

import numpy as np
import jax
import jax.numpy as jnp
from jax import lax
from jax.experimental import pallas as pl
from jax.experimental.pallas import tpu as pltpu

F32 = jnp.float32
BF16 = jnp.bfloat16

D_MODEL = 1024
N_HEADS = 8
NOPE = 128
ROPE = 64
HALF = ROPE // 2
Q_RANK = 384
KV_RANK = 256
CONV_W = 31
D_FF = 4096
EPS = 1e-6
ROPE_THETA = 10000.0
HEAD_PAD = 256
QK_SCALE = (NOPE + ROPE) ** -0.5
LOG2E = 1.4426950408889634
N_DEV = 8
FF_SHARD = D_FF // N_DEV
IN_SHARD = 4800 // N_DEV

ZS = Q_RANK + KV_RANK + 128
OFF_A = ZS
OFF_B = OFF_A + D_MODEL
OFF_GA = OFF_B + D_MODEL
OFF_GC = OFF_GA + D_MODEL
D_IN_PAD = OFF_GC + D_MODEL

ADAM_LR = 0.001
ADAM_B1 = 0.9
ADAM_B2 = 0.999
ADAM_EPS = 1e-08
ADAM_WD = 0.01
ADAM_STEP = 10

VMEM_LIMIT = 56 * 1024 * 1024

_SMALL = ("norm_mix_pre", "q_norm", "kv_norm", "conv_b", "conv_ln_g", "conv_ln_b", "b_pw2", "norm_mix_post",
          "norm_mlp_pre", "norm_mlp_post")
SMALL_ROWS = 16
LOSS_ROW = len(_SMALL)

_BIG = ("w_in", "w_uq", "w_uk", "w_uv", "w_o_attn", "conv_w", "w_pw2", "w_out", "w_ff1", "w_ff2")
_WEIGHTS = ("norm_mix_pre", "w_in", "q_norm", "w_uq", "kv_norm", "w_uk", "w_uv", "w_o_attn", "conv_w", "conv_b",
            "conv_ln_g", "conv_ln_b", "w_pw2", "b_pw2", "w_out", "norm_mix_post", "norm_mlp_pre", "w_ff1", "w_ff2",
            "norm_mlp_post")


def _dot(a, b):
    return jnp.dot(a, b, preferred_element_type=F32)


def _dot_nt(a, b):
    return lax.dot_general(a, b, (((1,), (1,)), ((), ())), preferred_element_type=F32)


def _dot_tn(a, b):
    return lax.dot_general(a, b, (((0,), (0,)), ((), ())), preferred_element_type=F32)


def _sigmoid(x):
    return 1.0 / (1.0 + jnp.exp(-x))


def _rms_fwd(x, g):
    r = lax.rsqrt(jnp.mean(x * x, axis=-1, keepdims=True) + EPS)
    return x * r * g


def _rms_bwd(dy, x, g):
    r = lax.rsqrt(jnp.mean(x * x, axis=-1, keepdims=True) + EPS)
    xh = x * r
    gy = dy * g
    dx = r * (gy - xh * jnp.mean(gy * xh, axis=-1, keepdims=True))
    return dx, jnp.sum(dy * xh, axis=0, keepdims=True)


def _rope(q, c, sa, sb):
    n = q.shape[-1]
    return q * c + pltpu.roll(q, n - HALF, 1) * sa + pltpu.roll(q, HALF, 1) * sb


def _rope_bwd(d, c, sa, sb):
    n = d.shape[-1]
    return d * c - pltpu.roll(d, n - HALF, 1) * sa - pltpu.roll(d, HALF, 1) * sb


def _params(n_axes=1):
    return pltpu.CompilerParams(dimension_semantics=("arbitrary",) * n_axes, vmem_limit_bytes=VMEM_LIMIT)


def _row_call(body, name, tb, row_ins, full_ins, row_outs, acc_outs, lane_outs=()):
    t = row_ins[0].shape[0]
    in_specs = [pl.BlockSpec((tb, a.shape[1]), lambda i: (i, 0)) for a in row_ins]
    in_specs += [pl.BlockSpec(a.shape, lambda i, nd=a.ndim: (0,) * nd) for a in full_ins]
    out_specs = [pl.BlockSpec((tb, c), lambda i: (i, 0)) for c, _ in row_outs]
    out_specs += [pl.BlockSpec(s, lambda i, nd=len(s): (0,) * nd) for s, _ in acc_outs]
    out_specs += [pl.BlockSpec((n, None, 1, tb), lambda i: (0, i, 0, 0)) for n in lane_outs]
    out_shape = [jax.ShapeDtypeStruct((t, c), dt) for c, dt in row_outs]
    out_shape += [jax.ShapeDtypeStruct(s, dt) for s, dt in acc_outs]
    out_shape += [jax.ShapeDtypeStruct((n, t // tb, 1, tb), F32) for n in lane_outs]
    return pl.pallas_call(
        functools.partial(body), name=name, grid=(t // tb,), in_specs=in_specs, out_specs=out_specs,
        out_shape=out_shape, compiler_params=_params(1),
    )(*row_ins, *full_ins)


def _acc(ref, val):
    @pl.when(pl.program_id(0) == 0)
    def _():
        ref[...] = jnp.zeros_like(ref)
    ref[...] += val


def _fwd_in_proj(x, g_pre, w_in_p, tb=256):
    def body(x_ref, g_ref, w_ref, h_ref, zs_ref, a_ref, sb_ref, u_ref, sa_ref, sc_ref):
        hb = _rms_fwd(x_ref[...], g_ref[...]).astype(BF16)
        h_ref[...] = hb
        zs_ref[...] = _dot(hb, w_ref[:, 0:ZS])
        a = _dot(hb, w_ref[:, OFF_A:OFF_B])
        sb = _sigmoid(_dot(hb, w_ref[:, OFF_B:OFF_GA]))
        a_ref[...] = a
        sb_ref[...] = sb
        u_ref[...] = a * sb
        sa_ref[...] = _sigmoid(_dot(hb, w_ref[:, OFF_GA:OFF_GC]))
        sc_ref[...] = _sigmoid(_dot(hb, w_ref[:, OFF_GC:D_IN_PAD]))

    d = D_MODEL
    return _row_call(body, "fwd_in_proj", tb, [x], [g_pre, w_in_p],
                     [(d, BF16), (ZS, F32), (d, F32), (d, F32), (d, F32), (d, F32), (d, F32)], [])


def _fwd_qkv(zs, tc, tsa, tsb, q_norm, kv_norm, w_uq_p, w_uk, w_uv, tb=256):
    def body(zs_ref, c_ref, sa_ref, sb_ref, qg_ref, kg_ref, wq_ref, wk_ref, wv_ref,
             cqn_ref, ckvn_ref, q_ref, k_ref, v_ref):
        zs_ = zs_ref[...]
        c, sa, sb = c_ref[...], sa_ref[...], sb_ref[...]
        cqn = _rms_fwd(zs_[:, 0:Q_RANK], qg_ref[...]).astype(BF16)
        cqn_ref[...] = cqn
        q = jnp.concatenate([_dot(cqn, wq_ref[h]) for h in range(N_HEADS)], axis=1)
        q = _rope(q, jnp.tile(c, (1, N_HEADS)), jnp.tile(sa, (1, N_HEADS)), jnp.tile(sb, (1, N_HEADS)))
        q_ref[...] = q.astype(BF16)
        kr = zs_[:, Q_RANK + KV_RANK:ZS]
        kr = _rope(kr, c[:, NOPE:], sa[:, NOPE:], sb[:, NOPE:]).astype(BF16)
        ckvn = _rms_fwd(zs_[:, Q_RANK:Q_RANK + KV_RANK], kg_ref[...]).astype(BF16)
        ckvn_ref[...] = ckvn
        kn = _dot(ckvn, wk_ref[...]).astype(BF16)
        v_ref[...] = _dot(ckvn, wv_ref[...]).astype(BF16)
        for h in range(N_HEADS):
            k_ref[:, h * HEAD_PAD:h * HEAD_PAD + NOPE] = kn[:, h * NOPE:(h + 1) * NOPE]
            k_ref[:, h * HEAD_PAD + NOPE:(h + 1) * HEAD_PAD] = kr

    hp = N_HEADS * HEAD_PAD
    return _row_call(body, "fwd_qkv", tb, [zs, tc, tsa, tsb], [q_norm, kv_norm, w_uq_p, w_uk, w_uv],
                     [(Q_RANK, BF16), (KV_RANK, BF16), (hp, BF16), (hp, BF16), (D_MODEL, BF16)], [])


def _attn_fwd(q, k, v, tq=512):
    t = q.shape[0]
    nq = t // tq
    c2 = QK_SCALE * LOG2E

    def body(q_ref, k_ref, v_ref, o_ref, lse_ref, m_sc, l_sc, acc_sc):
        i = pl.program_id(1)
        m_sc[...] = jnp.full_like(m_sc, -1e30)
        l_sc[...] = jnp.zeros_like(l_sc)
        acc_sc[...] = jnp.zeros_like(acc_sc)
        qb = q_ref[...]

        def step(j, masked):
            rows = pl.ds(pl.multiple_of(j * tq, tq), tq)
            st = _dot_nt(k_ref[rows, :], qb)
            if masked:
                key = lax.broadcasted_iota(jnp.int32, (tq, tq), 0)
                qry = lax.broadcasted_iota(jnp.int32, (tq, tq), 1)
                st = jnp.where(key <= qry, st, -1e30)
            m_prev = m_sc[...]
            m_new = jnp.maximum(m_prev, jnp.max(st, axis=0, keepdims=True))
            alpha = jnp.exp2((m_prev - m_new) * c2)
            pt = jnp.exp2((st - m_new) * c2)
            l_sc[...] = alpha * l_sc[...] + jnp.sum(pt, axis=0, keepdims=True)
            acc_sc[...] = alpha * acc_sc[...] + _dot_tn(v_ref[rows, :], pt.astype(BF16))
            m_sc[...] = m_new

        def unmasked(j, carry):
            step(j, False)
            return carry

        lax.fori_loop(0, i, unmasked, 0)
        step(i, True)
        l = l_sc[...]
        o_ref[...] = (acc_sc[...] / l).T.astype(BF16)
        lse_ref[...] = m_sc[...] * c2 + jnp.log2(l)

    return pl.pallas_call(
        functools.partial(body), name="attn_fwd", grid=(N_HEADS, nq),
        in_specs=[pl.BlockSpec((tq, HEAD_PAD), lambda h, i: (i, h)),
                  pl.BlockSpec((t, HEAD_PAD), lambda h, i: (0, h)),
                  pl.BlockSpec((t, NOPE), lambda h, i: (0, h))],
        out_specs=[pl.BlockSpec((tq, NOPE), lambda h, i: (i, h)),
                   pl.BlockSpec((None, None, 1, tq), lambda h, i: (h, i, 0, 0))],
        out_shape=[jax.ShapeDtypeStruct((t, D_MODEL), BF16), jax.ShapeDtypeStruct((N_HEADS, nq, 1, tq), F32)],
        scratch_shapes=[pltpu.VMEM((1, tq), F32), pltpu.VMEM((1, tq), F32), pltpu.VMEM((NOPE, tq), F32)],
        compiler_params=_params(2),
    )(q, k, v)


def _conv_fwd(u, conv_w, conv_b, ln_g, ln_b, tb=256):
    t, c = u.shape
    halo = 32

    def body(u_ref, up_ref, w_ref, b_ref, g_ref, be_ref, co_ref, act_ref, buf):
        i = pl.program_id(0)
        buf[0:halo, :] = jnp.where(i == 0, 0.0, up_ref[...])
        buf[halo:halo + tb, :] = u_ref[...]
        acc = jnp.zeros((tb, c), F32)
        for k in range(CONV_W):
            acc = acc + w_ref[k:k + 1, :] * buf[pl.ds(halo - (CONV_W - 1) + k, tb), :]
        co = acc + b_ref[...]
        co_ref[...] = co
        mu = jnp.mean(co, axis=-1, keepdims=True)
        xc = co - mu
        r = lax.rsqrt(jnp.mean(xc * xc, axis=-1, keepdims=True) + EPS)
        y = xc * r * g_ref[...] + be_ref[...]
        act_ref[...] = (y * _sigmoid(y)).astype(BF16)

    ratio = tb // halo
    return pl.pallas_call(
        functools.partial(body), name="conv_fwd", grid=(t // tb,),
        in_specs=[pl.BlockSpec((tb, c), lambda i: (i, 0)),
                  pl.BlockSpec((halo, c), lambda i: (jnp.maximum(i * ratio - 1, 0), 0)),
                  pl.BlockSpec(conv_w.shape, lambda i: (0, 0)),
                  pl.BlockSpec((1, c), lambda i: (0, 0)), pl.BlockSpec((1, c), lambda i: (0, 0)),
                  pl.BlockSpec((1, c), lambda i: (0, 0))],
        out_specs=[pl.BlockSpec((tb, c), lambda i: (i, 0)), pl.BlockSpec((tb, c), lambda i: (i, 0))],
        out_shape=[jax.ShapeDtypeStruct((t, c), F32), jax.ShapeDtypeStruct((t, c), BF16)],
        scratch_shapes=[pltpu.VMEM((tb + halo, c), F32)],
        compiler_params=_params(1),
    )(u, u, conv_w, conv_b, ln_g, ln_b)


def _fwd_merge(attn, uact, sa, sc, x, w_o, w_pw2, b_pw2, w_out, g_post, tb=256):
    def body(at_ref, ua_ref, sa_ref, sc_ref, x_ref, wo_ref, wp_ref, bp_ref, wout_ref, g_ref,
             ya_ref, yc_ref, mb_ref, m_ref, x1_ref):
        ya = _dot(at_ref[...], wo_ref[...])
        yc = _dot(ua_ref[...], wp_ref[...]) + bp_ref[...]
        ya_ref[...] = ya
        yc_ref[...] = yc
        mb = (sa_ref[...] * ya + sc_ref[...] * yc).astype(BF16)
        mb_ref[...] = mb
        m = _dot(mb, wout_ref[...])
        m_ref[...] = m
        x1_ref[...] = x_ref[...] + _rms_fwd(m, g_ref[...])

    d = D_MODEL
    return _row_call(body, "fwd_merge", tb, [attn, uact, sa, sc, x], [w_o, w_pw2, b_pw2, w_out, g_post],
                     [(d, F32), (d, F32), (d, BF16), (d, F32), (d, F32)], [])


def _fwd_ff1(x1, g, w_ff1, tb=256):
    def body(x1_ref, g_ref, w_ref, h2_ref, r1_ref, act_ref):
        h2 = _rms_fwd(x1_ref[...], g_ref[...]).astype(BF16)
        h2_ref[...] = h2
        for j in range(N_DEV):
            cols = slice(j * FF_SHARD, (j + 1) * FF_SHARD)
            r1 = jnp.maximum(_dot(h2, w_ref[j]), 0.0)
            r1_ref[:, cols] = r1.astype(BF16)
            act_ref[:, cols] = (r1 * r1).astype(BF16)

    return _row_call(body, "fwd_ff1", tb, [x1], [g, w_ff1], [(D_MODEL, BF16), (D_FF, BF16), (D_FF, BF16)], [])


def _fwd_ff2_loss(act, x1, target, w_ff2, g, tb=256):
    def body(act_ref, x1_ref, tg_ref, w_ref, g_ref, f_ref, dy_ref, loss_ref):
        f = _dot(act_ref[...], w_ref[...])
        f_ref[...] = f
        e = x1_ref[...] + _rms_fwd(f, g_ref[...]) - tg_ref[...]
        dy_ref[...] = e * (1.0 / D_MODEL)
        _acc(loss_ref, jnp.sum(e * e))

    return _row_call(body, "fwd_ff2_loss", tb, [act, x1, target], [w_ff2, g],
                     [(D_MODEL, F32), (D_MODEL, F32)], [((8, 128), F32)])


def _bwd_ff2(dy, f, r1, w_ff2, g, tb=256):
    def body(dy_ref, f_ref, r1_ref, w_ref, g_ref, df_ref, df1_ref, dg_ref):
        df, dg = _rms_bwd(dy_ref[...], f_ref[...], g_ref[...])
        _acc(dg_ref, dg)
        dfb = df.astype(BF16)
        df_ref[...] = dfb
        dact = _dot_nt(dfb, w_ref[...])
        df1_ref[...] = (dact * (2.0 * r1_ref[...].astype(F32))).astype(BF16)

    return _row_call(body, "bwd_ff2", tb, [dy, f, r1], [w_ff2, g], [(D_MODEL, BF16), (D_FF, BF16)],
                     [((1, D_MODEL), F32)])


def _bwd_ff1(df1, x1, dy, w_ff1, g, tb=256):
    def body(df1_ref, x1_ref, dy_ref, w_ref, g_ref, dx1_ref, dg_ref):
        dh2 = _dot_nt(df1_ref[:, 0:FF_SHARD], w_ref[0])
        for j in range(1, N_DEV):
            dh2 = dh2 + _dot_nt(df1_ref[:, j * FF_SHARD:(j + 1) * FF_SHARD], w_ref[j])
        dxn, dg = _rms_bwd(dh2, x1_ref[...], g_ref[...])
        _acc(dg_ref, dg)
        dx1_ref[...] = dy_ref[...] + dxn

    return _row_call(body, "bwd_ff1", tb, [df1, x1, dy], [w_ff1, g], [(D_MODEL, F32)], [((1, D_MODEL), F32)])


def _bwd_merge(dx1, m, sa, sc, ya, yc, attn, w_out, w_o, w_pw2, g_post, tb=256):
    def body(dx1_ref, m_ref, sa_ref, sc_ref, ya_ref, yc_ref, at_ref, wout_ref, wo_ref, wp_ref, g_ref,
             dm_ref, dya_ref, dyc_ref, dga_ref, dgc_ref, dat_ref, dua_ref, dg_ref, dbp_ref, delta_ref):
        dm, dg = _rms_bwd(dx1_ref[...], m_ref[...], g_ref[...])
        _acc(dg_ref, dg)
        dmb = dm.astype(BF16)
        dm_ref[...] = dmb
        dmerged = _dot_nt(dmb, wout_ref[...])
        sa, sc = sa_ref[...], sc_ref[...]
        dya = dmerged * sa
        dyc = dmerged * sc
        _acc(dbp_ref, jnp.sum(dyc, axis=0, keepdims=True))
        dyab = dya.astype(BF16)
        dycb = dyc.astype(BF16)
        dya_ref[...] = dyab
        dyc_ref[...] = dycb
        dga_ref[...] = (dmerged * ya_ref[...] * sa * (1.0 - sa)).astype(BF16)
        dgc_ref[...] = (dmerged * yc_ref[...] * sc * (1.0 - sc)).astype(BF16)
        dat = _dot_nt(dyab, wo_ref[...])
        dat_ref[...] = dat.astype(BF16)
        prod = dat * at_ref[...].astype(F32)
        lane = lax.broadcasted_iota(jnp.int32, (tb, NOPE), 1)
        dl = jnp.zeros((tb, NOPE), F32)
        for h in range(N_HEADS):
            dl = dl + jnp.where(lane == h, jnp.sum(prod[:, h * NOPE:(h + 1) * NOPE], axis=1, keepdims=True), 0.0)
        dlt = dl.T
        for h in range(N_HEADS):
            delta_ref[h] = dlt[h:h + 1, :]
        dua_ref[...] = _dot_nt(dycb, wp_ref[...])

    d = D_MODEL
    return _row_call(body, "bwd_merge", tb, [dx1, m, sa, sc, ya, yc, attn], [w_out, w_o, w_pw2, g_post],
                     [(d, BF16), (d, BF16), (d, BF16), (d, BF16), (d, BF16), (d, BF16), (d, F32)],
                     [((1, d), F32), ((1, d), F32)], lane_outs=(N_HEADS,))


def _bwd_ln(dua, co, ln_g, ln_b, tb=256):
    def body(dua_ref, co_ref, g_ref, be_ref, dco_ref, dg_ref, db_ref, dcb_ref):
        co = co_ref[...]
        g = g_ref[...]
        mu = jnp.mean(co, axis=-1, keepdims=True)
        xc = co - mu
        r = lax.rsqrt(jnp.mean(xc * xc, axis=-1, keepdims=True) + EPS)
        xh = xc * r
        y = xh * g + be_ref[...]
        s = _sigmoid(y)
        dy = dua_ref[...] * (s + y * s * (1.0 - s))
        _acc(db_ref, jnp.sum(dy, axis=0, keepdims=True))
        _acc(dg_ref, jnp.sum(dy * xh, axis=0, keepdims=True))
        gy = dy * g
        dco = r * (gy - jnp.mean(gy, axis=-1, keepdims=True) - xh * jnp.mean(gy * xh, axis=-1, keepdims=True))
        dco_ref[...] = dco
        _acc(dcb_ref, jnp.sum(dco, axis=0, keepdims=True))

    d = D_MODEL
    return _row_call(body, "bwd_ln", tb, [dua, co], [ln_g, ln_b], [(d, F32)],
                     [((1, d), F32), ((1, d), F32), ((1, d), F32)])


def _conv_bwd(dco, u, a, sb, conv_w, tb=256):
    t, c = u.shape
    halo = 32
    ratio = tb // halo
    nblk = t // tb

    def body(d_ref, dn_ref, u_ref, up_ref, a_ref, sb_ref, w_ref, da_ref, db_ref, dw_ref, bufd, bufu):
        i = pl.program_id(0)

        @pl.when(i == 0)
        def _():
            dw_ref[...] = jnp.zeros_like(dw_ref)

        dco = d_ref[...]
        bufd[0:tb, :] = dco
        bufd[tb:tb + halo, :] = jnp.where(i == nblk - 1, 0.0, dn_ref[...])
        bufu[0:halo, :] = jnp.where(i == 0, 0.0, up_ref[...])
        bufu[halo:halo + tb, :] = u_ref[...]
        du = jnp.zeros((tb, c), F32)
        for k in range(CONV_W):
            du = du + w_ref[k:k + 1, :] * bufd[pl.ds(CONV_W - 1 - k, tb), :]
            dw_ref[k:k + 1, :] += jnp.sum(dco * bufu[pl.ds(halo - (CONV_W - 1) + k, tb), :], axis=0, keepdims=True)
        sb_ = sb_ref[...]
        da_ref[...] = (du * sb_).astype(BF16)
        db_ref[...] = (du * a_ref[...] * sb_ * (1.0 - sb_)).astype(BF16)

    return pl.pallas_call(
        functools.partial(body), name="conv_bwd", grid=(nblk,),
        in_specs=[pl.BlockSpec((tb, c), lambda i: (i, 0)),
                  pl.BlockSpec((halo, c), lambda i: (jnp.minimum((i + 1) * ratio, t // halo - 1), 0)),
                  pl.BlockSpec((tb, c), lambda i: (i, 0)),
                  pl.BlockSpec((halo, c), lambda i: (jnp.maximum(i * ratio - 1, 0), 0)),
                  pl.BlockSpec((tb, c), lambda i: (i, 0)), pl.BlockSpec((tb, c), lambda i: (i, 0)),
                  pl.BlockSpec(conv_w.shape, lambda i: (0, 0))],
        out_specs=[pl.BlockSpec((tb, c), lambda i: (i, 0)), pl.BlockSpec((tb, c), lambda i: (i, 0)),
                   pl.BlockSpec((32, c), lambda i: (0, 0))],
        out_shape=[jax.ShapeDtypeStruct((t, c), BF16), jax.ShapeDtypeStruct((t, c), BF16),
                   jax.ShapeDtypeStruct((32, c), F32)],
        scratch_shapes=[pltpu.VMEM((tb + halo, c), F32), pltpu.VMEM((tb + halo, c), F32)],
        compiler_params=_params(1),
    )(dco, dco, u, u, a, sb, conv_w)


def _attn_bwd(q, k, v, do, lse2, delta, tq=512):
    t = q.shape[0]
    nq = t // tq
    td = delta.shape[-1]
    per = tq // td
    c2 = QK_SCALE * LOG2E

    def body(q_ref, k_ref, v_ref, do_ref, lse_ref, dl_ref, dq_ref, dk_ref, dv_ref, dk_sc, dv_sc):
        j = pl.program_id(1)

        @pl.when(j == 0)
        def _():
            dq_ref[...] = jnp.zeros_like(dq_ref)

        dk_sc[...] = jnp.zeros_like(dk_sc)
        dv_sc[...] = jnp.zeros_like(dv_sc)
        kb, vb = k_ref[...], v_ref[...]

        def step(i, masked):
            rows = pl.ds(pl.multiple_of(i * tq, tq), tq)
            qb, dob = q_ref[rows, :], do_ref[rows, :]
            pt = jnp.exp2(_dot_nt(kb, qb) * c2 - lse_ref[i])
            if masked:
                key = lax.broadcasted_iota(jnp.int32, (tq, tq), 0)
                qry = lax.broadcasted_iota(jnp.int32, (tq, tq), 1)
                pt = jnp.where(key <= qry, pt, 0.0)
            dv_sc[...] += _dot(pt.astype(BF16), dob)
            dl = jnp.concatenate([dl_ref[per * i + r] for r in range(per)], axis=-1)
            dst = (pt * (_dot_nt(vb, dob) - dl) * QK_SCALE).astype(BF16)
            dk_sc[...] += _dot(dst, qb)
            dq_ref[rows, :] += _dot_tn(dst, kb)

        def unmasked(i, carry):
            step(i, False)
            return carry

        step(j, True)
        lax.fori_loop(j + 1, nq, unmasked, 0)
        dk_ref[...] = dk_sc[...]
        dv_ref[...] = dv_sc[...].astype(BF16)

    hp = N_HEADS * HEAD_PAD
    return pl.pallas_call(
        functools.partial(body), name="attn_bwd", grid=(N_HEADS, nq),
        in_specs=[pl.BlockSpec((t, HEAD_PAD), lambda h, j: (0, h)),
                  pl.BlockSpec((tq, HEAD_PAD), lambda h, j: (j, h)),
                  pl.BlockSpec((tq, NOPE), lambda h, j: (j, h)),
                  pl.BlockSpec((t, NOPE), lambda h, j: (0, h)),
                  pl.BlockSpec((None, nq, 1, tq), lambda h, j: (h, 0, 0, 0)),
                  pl.BlockSpec((None, t // td, 1, td), lambda h, j: (h, 0, 0, 0))],
        out_specs=[pl.BlockSpec((t, HEAD_PAD), lambda h, j: (0, h)),
                   pl.BlockSpec((tq, HEAD_PAD), lambda h, j: (j, h)),
                   pl.BlockSpec((tq, NOPE), lambda h, j: (j, h))],
        out_shape=[jax.ShapeDtypeStruct((t, hp), F32), jax.ShapeDtypeStruct((t, hp), F32),
                   jax.ShapeDtypeStruct((t, D_MODEL), BF16)],
        scratch_shapes=[pltpu.VMEM((tq, HEAD_PAD), F32), pltpu.VMEM((tq, NOPE), F32)],
        compiler_params=_params(2),
    )(q, k, v, do, lse2, delta)


def _bwd_qkv(dq, dk, dv, zs, tc, tsa, tsb, q_norm, kv_norm, w_uq_p, w_uk, w_uv, tb=256):
    def body(dq_ref, dk_ref, dv_ref, zs_ref, c_ref, sa_ref, sb_ref, qg_ref, kg_ref, wq_ref, wk_ref, wv_ref,
             dqp_ref, dkn_ref, dzs_ref, dqg_ref, dkg_ref):
        c, sa, sb = c_ref[...], sa_ref[...], sb_ref[...]
        zs_ = zs_ref[...]
        dqp = _rope_bwd(dq_ref[...], jnp.tile(c, (1, N_HEADS)), jnp.tile(sa, (1, N_HEADS)),
                        jnp.tile(sb, (1, N_HEADS))).astype(BF16)
        dqp_ref[...] = dqp
        dcqn = _dot_nt(dqp[:, 0:HEAD_PAD], wq_ref[0])
        for h in range(1, N_HEADS):
            dcqn = dcqn + _dot_nt(dqp[:, h * HEAD_PAD:(h + 1) * HEAD_PAD], wq_ref[h])
        dcq, dqg = _rms_bwd(dcqn, zs_[:, 0:Q_RANK], qg_ref[...])
        _acc(dqg_ref, dqg)
        dzs_ref[:, 0:Q_RANK] = dcq.astype(BF16)
        dkr = jnp.zeros((tb, NOPE), F32)
        for h in range(N_HEADS):
            dkn_ref[:, h * NOPE:(h + 1) * NOPE] = dk_ref[:, h * HEAD_PAD:h * HEAD_PAD + NOPE].astype(BF16)
            dkr = dkr + dk_ref[:, h * HEAD_PAD + NOPE:(h + 1) * HEAD_PAD]
        dzs_ref[:, Q_RANK + KV_RANK:ZS] = _rope_bwd(dkr, c[:, NOPE:], sa[:, NOPE:], sb[:, NOPE:]).astype(BF16)
        dckvn = _dot_nt(dkn_ref[...], wk_ref[...]) + _dot_nt(dv_ref[...], wv_ref[...])
        dckv, dkg = _rms_bwd(dckvn, zs_[:, Q_RANK:Q_RANK + KV_RANK], kg_ref[...])
        _acc(dkg_ref, dkg)
        dzs_ref[:, Q_RANK:Q_RANK + KV_RANK] = dckv.astype(BF16)

    hp = N_HEADS * HEAD_PAD
    return _row_call(body, "bwd_qkv", tb, [dq, dk, dv, zs, tc, tsa, tsb], [q_norm, kv_norm, w_uq_p, w_uk, w_uv],
                     [(hp, BF16), (D_MODEL, BF16), (ZS, BF16)], [((1, Q_RANK), F32), ((1, KV_RANK), F32)])


def _bwd_in_proj(dzs, da, db, dga, dgc, x, dx1, w_in_p, g_pre, tb=256):
    def body(dzs_ref, da_ref, db_ref, dga_ref, dgc_ref, x_ref, dx1_ref, w_ref, g_ref, gx_ref, dg_ref):
        dh = _dot_nt(dzs_ref[...], w_ref[:, 0:ZS])
        dh = dh + _dot_nt(da_ref[...], w_ref[:, OFF_A:OFF_B])
        dh = dh + _dot_nt(db_ref[...], w_ref[:, OFF_B:OFF_GA])
        dh = dh + _dot_nt(dga_ref[...], w_ref[:, OFF_GA:OFF_GC])
        dh = dh + _dot_nt(dgc_ref[...], w_ref[:, OFF_GC:D_IN_PAD])
        dxn, dg = _rms_bwd(dh, x_ref[...], g_ref[...])
        _acc(dg_ref, dg)
        gx_ref[...] = dx1_ref[...] + dxn

    return _row_call(body, "bwd_in_proj", tb, [dzs, da, db, dga, dgc, x, dx1], [w_in_p, g_pre],
                     [(D_MODEL, F32)], [((1, D_MODEL), F32)])


def _mm_tn(a, b, name, shard_cols=None, tt=512):
    t, m = a.shape
    n = b.shape[1]
    tm = min(m, 1024)
    tn = shard_cols if shard_cols else min(n, 1024)
    nt = t // tt

    def body(a_ref, b_ref, o_ref, acc):
        k = pl.program_id(2)

        @pl.when(k == 0)
        def _():
            acc[...] = jnp.zeros_like(acc)

        acc[...] += _dot_tn(a_ref[...], b_ref[...])

        @pl.when(k == nt - 1)
        def _():
            o_ref[...] = acc[...].astype(BF16)

    if shard_cols:
        out_spec = pl.BlockSpec((None, tm, tn), lambda i, j, k: (j, i, 0))
        out_shape = jax.ShapeDtypeStruct((n // tn, m, tn), BF16)
    else:
        out_spec = pl.BlockSpec((tm, tn), lambda i, j, k: (i, j))
        out_shape = jax.ShapeDtypeStruct((m, n), BF16)
    return pl.pallas_call(
        functools.partial(body), name=name, grid=(m // tm, n // tn, nt),
        in_specs=[pl.BlockSpec((tt, tm), lambda i, j, k: (k, i)), pl.BlockSpec((tt, tn), lambda i, j, k: (k, j))],
        out_specs=out_spec, out_shape=out_shape, scratch_shapes=[pltpu.VMEM((tm, tn), F32)],
        compiler_params=_params(3),
    )(a, b)


_ANY = pl.BlockSpec(memory_space=pl.ANY)
_MESH = pl.DeviceIdType.MESH


def _all_gather(shards, name):
    n = len(shards)

    def body(*refs):
        ins, outs = refs[:n], refs[n:2 * n]
        send_sems, recv_sems, local_sems = refs[2 * n:]
        x, y, c = lax.axis_index("x"), lax.axis_index("y"), lax.axis_index("c")
        me, sibling = (x, y, c), (x, y, 1 - c)
        chips = [(1 - x, y), (x, 1 - y), (1 - x, 1 - y)]

        def copy(w, k, block, to, src=None):
            dst = outs[w].at[4 * block[0] + 2 * block[1] + block[2]]
            return pltpu.make_async_remote_copy(
                src_ref=dst if src is None else src, dst_ref=dst, send_sem=send_sems.at[7 * w + k],
                recv_sem=recv_sems.at[7 * w + k], device_id=to, device_id_type=_MESH)

        mine = [pltpu.make_async_copy(ins[w], outs[w].at[4 * x + 2 * y + c], local_sems.at[w]) for w in range(n)]
        for cp in mine:
            cp.start()
        first = []
        for w in range(n):
            first.append(copy(w, 0, me, sibling, src=ins[w]))
            first += [copy(w, 1 + j, me, (*chip, c), src=ins[w]) for j, chip in enumerate(chips)]
        for cp in first:
            cp.start()
        passed = []
        for j, chip in enumerate(chips):
            for w in range(n):
                copy(w, 1 + j, (*chip, c), me).wait_recv()
                passed.append(copy(w, 4 + j, (*chip, c), sibling))
                passed[-1].start()
        for w in range(n):
            copy(w, 0, sibling, me).wait_recv()
        for j, chip in enumerate(chips):
            for w in range(n):
                copy(w, 4 + j, (*chip, 1 - c), me).wait_recv()
        for cp in first + passed:
            cp.wait_send()
        for cp in mine:
            cp.wait()

    return pl.pallas_call(
        functools.partial(body), name=name,
        out_shape=[jax.ShapeDtypeStruct((N_DEV,) + s.shape, s.dtype) for s in shards],
        in_specs=[_ANY] * n, out_specs=[_ANY] * n,
        scratch_shapes=[pltpu.SemaphoreType.DMA((7 * n,)), pltpu.SemaphoreType.DMA((7 * n,)),
                        pltpu.SemaphoreType.DMA((n,))],
    )(*shards)


def _pair_exchange(gs):
    n = len(gs)

    def body(*refs):
        ins, outs = refs[:n], refs[n:2 * n]
        send_sems, recv_sems = refs[2 * n:]
        x, y, c = lax.axis_index("x"), lax.axis_index("y"), lax.axis_index("c")
        copies = [pltpu.make_async_remote_copy(
            src_ref=ins[w].at[:, 1 - c], dst_ref=outs[w], send_sem=send_sems.at[w], recv_sem=recv_sems.at[w],
            device_id=(x, y, 1 - c), device_id_type=_MESH) for w in range(n)]
        for cp in copies:
            cp.start()
        for cp in copies:
            cp.wait()

    return pl.pallas_call(
        functools.partial(body), name="rs_pair_exchange",
        out_shape=[jax.ShapeDtypeStruct((4,) + g.shape[2:], g.dtype) for g in gs],
        in_specs=[_ANY] * n, out_specs=[_ANY] * n,
        scratch_shapes=[pltpu.SemaphoreType.DMA((n,)), pltpu.SemaphoreType.DMA((n,))],
    )(*gs)


def _pair_sum(g, l, c_idx, name):
    _, _, r, n = g.shape
    tr = min(r, 512)

    def body(c_ref, g_ref, l_ref, p_ref):
        p_ref[...] = (g_ref[...].astype(F32) + l_ref[...].astype(F32)).astype(BF16)

    return pl.pallas_call(
        functools.partial(body), name=name,
        grid_spec=pltpu.PrefetchScalarGridSpec(
            num_scalar_prefetch=1, grid=(4, r // tr),
            in_specs=[pl.BlockSpec((None, None, tr, n), lambda k, i, c: (k, c[0], i, 0)),
                      pl.BlockSpec((None, tr, n), lambda k, i, c: (k, i, 0))],
            out_specs=pl.BlockSpec((None, tr, n), lambda k, i, c: (k, i, 0))),
        out_shape=jax.ShapeDtypeStruct((4, r, n), BF16), compiler_params=_params(2),
    )(c_idx, g, l)


def _chip_exchange(ps):
    n = len(ps)

    def body(*refs):
        ins, outs = refs[:n], refs[n:2 * n]
        send_sems, recv_sems = refs[2 * n:]
        x, y, c = lax.axis_index("x"), lax.axis_index("y"), lax.axis_index("c")
        chips = [(1 - x, y), (x, 1 - y), (1 - x, 1 - y)]
        copies = [pltpu.make_async_remote_copy(
            src_ref=ins[w].at[2 * px + py], dst_ref=outs[w].at[s], send_sem=send_sems.at[3 * w + s],
            recv_sem=recv_sems.at[3 * w + s], device_id=(px, py, c), device_id_type=_MESH)
            for w in range(n) for s, (px, py) in enumerate(chips)]
        for cp in copies:
            cp.start()
        for cp in copies:
            cp.wait()

    return pl.pallas_call(
        functools.partial(body), name="rs_chip_exchange",
        out_shape=[jax.ShapeDtypeStruct((3,) + p.shape[1:], p.dtype) for p in ps],
        in_specs=[_ANY] * n, out_specs=[_ANY] * n,
        scratch_shapes=[pltpu.SemaphoreType.DMA((3 * n,)), pltpu.SemaphoreType.DMA((3 * n,))],
    )(*ps)


def _adamw(w, g, m, v):
    m2 = ADAM_B1 * m + (1.0 - ADAM_B1) * g
    v2 = ADAM_B2 * v + (1.0 - ADAM_B2) * (g * g)
    m_hat = m2 / (1.0 - ADAM_B1 ** ADAM_STEP)
    v_hat = v2 / (1.0 - ADAM_B2 ** ADAM_STEP)
    delta = -ADAM_LR * (m_hat / (jnp.sqrt(v_hat) + ADAM_EPS) + ADAM_WD * w)
    return delta, m2, v2


def _update(g, l, q, w, m, v, idx, name):
    _, _, r, n = g.shape
    tr = min(r, 256)
    while r % tr:
        tr -= 16

    def body(idx_ref, g_ref, l_ref, q_ref, w_ref, m_ref, v_ref, go_ref, d_ref, mo_ref, vo_ref):
        gr = g_ref[...].astype(F32) + l_ref[...].astype(F32)
        gr = gr + q_ref[0].astype(F32)
        gr = gr + q_ref[1].astype(F32)
        gr = gr + q_ref[2].astype(F32)
        go_ref[...] = gr
        d_ref[...], mo_ref[...], vo_ref[...] = _adamw(w_ref[...], gr, m_ref[...], v_ref[...])

    row = pl.BlockSpec((tr, n), lambda i, s: (i, 0))
    return pl.pallas_call(
        functools.partial(body), name=name,
        grid_spec=pltpu.PrefetchScalarGridSpec(
            num_scalar_prefetch=1, grid=(r // tr,),
            in_specs=[pl.BlockSpec((None, None, tr, n), lambda i, s: (s[0], s[1], i, 0)),
                      pl.BlockSpec((None, tr, n), lambda i, s: (s[0], i, 0)),
                      pl.BlockSpec((3, tr, n), lambda i, s: (0, i, 0)), row, row, row],
            out_specs=[row, row, row, row]),
        out_shape=[jax.ShapeDtypeStruct((r, n), F32)] * 4, compiler_params=_params(1),
    )(idx, g, l, q, w, m, v)


def _update_small(gathered, w, m, v):
    r, n = w.shape

    def body(ga_ref, w_ref, m_ref, v_ref, go_ref, d_ref, mo_ref, vo_ref):
        gr = ga_ref[0:r, :]
        for dev in range(1, N_DEV):
            gr = gr + ga_ref[dev * r:(dev + 1) * r, :]
        go_ref[...] = gr
        d_ref[...], mo_ref[...], vo_ref[...] = _adamw(w_ref[...], gr, m_ref[...], v_ref[...])

    return pl.pallas_call(
        functools.partial(body), name="update_small", out_shape=[jax.ShapeDtypeStruct((r, n), F32)] * 4,
    )(gathered, w, m, v)


def _to_exchange(shards):
    conv = jnp.pad(shards["conv_w"].reshape(CONV_W, D_MODEL // N_DEV), ((0, 1), (0, 0)))
    return [shards["w_in"][0], jnp.pad(shards["w_uq"][0], ((0, 0), (0, HEAD_PAD - NOPE - ROPE))),
            shards["w_uk"].reshape(KV_RANK // N_DEV, N_HEADS * NOPE),
            shards["w_uv"].reshape(KV_RANK // N_DEV, N_HEADS * NOPE), shards["w_o_attn"][0], conv,
            shards["w_pw2"][0], shards["w_out"][0], shards["w_ff1"][0], shards["w_ff2"][0]]


def _from_exchange(arrs, like):
    out = {}
    for n, a in zip(_BIG, arrs):
        if n == "w_uq":
            a = a[:, :NOPE + ROPE]
        elif n == "conv_w":
            a = a[:CONV_W]
        out[n] = a.reshape(like[n].shape)
    return out


def _gathered_weights(g):
    g_in, g_uq, g_uk, g_uv, g_o, g_conv, g_pw2, g_out, g_ff1, g_ff2 = g
    w_in = g_in.transpose(1, 0, 2).reshape(D_MODEL, N_DEV * IN_SHARD)
    kr_end = Q_RANK + KV_RANK + ROPE
    w_in_p = jnp.concatenate([w_in[:, :kr_end], jnp.zeros((D_MODEL, 128 - ROPE), BF16), w_in[:, kr_end:]], axis=1)
    conv = (g_conv[:, 0].astype(F32) + g_conv[:, 1].astype(F32)).transpose(1, 0, 2).reshape(32, D_MODEL)
    return (w_in_p, g_uq, g_uk.reshape(KV_RANK, -1), g_uv.reshape(KV_RANK, -1), g_o.reshape(D_MODEL, D_MODEL),
            g_pw2.reshape(D_MODEL, D_MODEL), g_out.reshape(D_MODEL, D_MODEL), g_ff1, g_ff2.reshape(D_FF, D_MODEL),
            conv)


def _pad_rows(a, rows):
    return jnp.pad(a, ((0, rows - a.shape[0]), (0, 0)))


def _small_pack(vals):
    rows = [jnp.pad(v.reshape(1, -1), ((0, 0), (0, 1024 - v.size))) for v in vals]
    return _pad_rows(jnp.concatenate(rows, axis=0), SMALL_ROWS)


def _rope_tables(positions):
    inv_freq = ROPE_THETA ** (-jnp.arange(0, ROPE, 2, dtype=F32) / ROPE)
    ang = positions.reshape(-1).astype(F32)[:, None] * inv_freq
    cos, sin = jnp.cos(ang), jnp.sin(ang)
    t = cos.shape[0]
    z32, z64 = jnp.zeros((t, HALF), F32), jnp.zeros((t, HEAD_PAD - NOPE - ROPE), F32)
    z128 = jnp.zeros((t, NOPE), F32)
    tc = jnp.concatenate([jnp.ones((t, NOPE), F32), cos, cos, z64], axis=1)
    tsa = jnp.concatenate([z128, -sin, z32, z64], axis=1)
    tsb = jnp.concatenate([z128, z32, sin, z64], axis=1)
    return tc, tsa, tsb


def _local_step(x, positions, target, small, weights):
    w_in_p, w_uq_p, w_uk, w_uv, w_o, w_pw2, w_out, w_ff1, w_ff2, conv_w = weights
    tc, tsa, tsb = _rope_tables(positions)

    h, zs, a, sb, u, sa, sc = _fwd_in_proj(x, small["norm_mix_pre"], w_in_p)
    cqn, ckvn, q, k, v = _fwd_qkv(zs, tc, tsa, tsb, small["q_norm"], small["kv_norm"], w_uq_p, w_uk, w_uv)
    attn, lse = _attn_fwd(q, k, v)
    co, uact = _conv_fwd(u, conv_w, small["conv_b"], small["conv_ln_g"], small["conv_ln_b"])
    ya, yc, mb, m, x1 = _fwd_merge(attn, uact, sa, sc, x, w_o, w_pw2, small["b_pw2"], w_out, small["norm_mix_post"])
    h2, r1, act = _fwd_ff1(x1, small["norm_mlp_pre"], w_ff1)
    f, dy, loss_blk = _fwd_ff2_loss(act, x1, target, w_ff2, small["norm_mlp_post"])

    df, df1, dg_mlp_post = _bwd_ff2(dy, f, r1, w_ff2, small["norm_mlp_post"])
    dx1, dg_mlp_pre = _bwd_ff1(df1, x1, dy, w_ff1, small["norm_mlp_pre"])
    dw_ff2 = _mm_tn(act, df, "dw_ff2")
    dw_ff1 = _mm_tn(h2, df1, "dw_ff1", shard_cols=FF_SHARD)
    dmb, dya, dyc, dga, dgc, dat, dua, dg_mix_post, db_pw2, delta = _bwd_merge(
        dx1, m, sa, sc, ya, yc, attn, w_out, w_o, w_pw2, small["norm_mix_post"])
    dw_out = _mm_tn(mb, dmb, "dw_out")
    dw_o = _mm_tn(attn, dya, "dw_o")
    dw_pw2 = _mm_tn(uact, dyc, "dw_pw2")
    dco, dln_g, dln_b, dconv_b = _bwd_ln(dua, co, small["conv_ln_g"], small["conv_ln_b"])
    da, db, dconv = _conv_bwd(dco, u, a, sb, conv_w)
    dq, dk, dv = _attn_bwd(q, k, v, dat, lse, delta)
    dqp, dkn, dzs, dq_norm, dkv_norm = _bwd_qkv(dq, dk, dv, zs, tc, tsa, tsb, small["q_norm"], small["kv_norm"],
                                                w_uq_p, w_uk, w_uv)
    dw_uq = _mm_tn(cqn, dqp, "dw_uq", shard_cols=HEAD_PAD)
    dw_uk = _mm_tn(ckvn, dkn, "dw_uk")
    dw_uv = _mm_tn(ckvn, dv, "dw_uv")
    grad_x, dg_pre = _bwd_in_proj(dzs, da, db, dga, dgc, x, dx1, w_in_p, small["norm_mix_pre"])
    dw_zs = _mm_tn(h, dzs, "dw_in_zs")
    dw_a = _mm_tn(h, da, "dw_in_a")
    dw_b = _mm_tn(h, db, "dw_in_b")
    dw_ga = _mm_tn(h, dga, "dw_in_ga")
    dw_gc = _mm_tn(h, dgc, "dw_in_gc")

    dw_in = jnp.concatenate([dw_zs[:, :Q_RANK + KV_RANK + ROPE], dw_a, dw_b, dw_ga, dw_gc], axis=1)
    dw_in = dw_in.reshape(D_MODEL, N_DEV, IN_SHARD).transpose(1, 0, 2)
    dconv = dconv.reshape(32, N_DEV, D_MODEL // N_DEV).transpose(1, 0, 2).astype(BF16)
    rows = lambda a: a.reshape(N_DEV, a.shape[0] // N_DEV, a.shape[1])
    big = (dw_in, dw_uq, rows(dw_uk), rows(dw_uv), rows(dw_o), dconv, rows(dw_pw2), rows(dw_out), dw_ff1,
           rows(dw_ff2))
    small_grads = (dg_pre, dq_norm, dkv_norm, dconv_b, dln_g, dln_b, db_pw2, dg_mix_post, dg_mlp_pre, dg_mlp_post)
    return loss_blk, grad_x, small_grads, big


def kernel(x, positions, norm_mix_pre, w_in, q_norm, w_uq, kv_norm, w_uk, w_uv, w_o_attn, conv_w, conv_b, conv_ln_g, conv_ln_b, w_pw2, b_pw2, w_out, norm_mix_post, norm_mlp_pre, w_ff1, w_ff2, norm_mlp_post, loss_target, m_norm_mix_pre, m_w_in, m_q_norm, m_w_uq, m_kv_norm, m_w_uk, m_w_uv, m_w_o_attn, m_conv_w, m_conv_b, m_conv_ln_g, m_conv_ln_b, m_w_pw2, m_b_pw2, m_w_out, m_norm_mix_post, m_norm_mlp_pre, m_w_ff1, m_w_ff2, m_norm_mlp_post, v_norm_mix_pre, v_w_in, v_q_norm, v_w_uq, v_kv_norm, v_w_uk, v_w_uv, v_w_o_attn, v_conv_w, v_conv_b, v_conv_ln_g, v_conv_ln_b, v_w_pw2, v_b_pw2, v_w_out, v_norm_mix_post, v_norm_mlp_pre, v_w_ff1, v_w_ff2, v_norm_mlp_post):
    wts = dict(norm_mix_pre=norm_mix_pre, w_in=w_in, q_norm=q_norm, w_uq=w_uq, kv_norm=kv_norm, w_uk=w_uk, w_uv=w_uv,
               w_o_attn=w_o_attn, conv_w=conv_w, conv_b=conv_b, conv_ln_g=conv_ln_g, conv_ln_b=conv_ln_b,
               w_pw2=w_pw2, b_pw2=b_pw2, w_out=w_out, norm_mix_post=norm_mix_post, norm_mlp_pre=norm_mlp_pre,
               w_ff1=w_ff1, w_ff2=w_ff2, norm_mlp_post=norm_mlp_post)
    mom_m = dict(norm_mix_pre=m_norm_mix_pre, w_in=m_w_in, q_norm=m_q_norm, w_uq=m_w_uq, kv_norm=m_kv_norm,
                 w_uk=m_w_uk, w_uv=m_w_uv, w_o_attn=m_w_o_attn, conv_w=m_conv_w, conv_b=m_conv_b,
                 conv_ln_g=m_conv_ln_g, conv_ln_b=m_conv_ln_b, w_pw2=m_w_pw2, b_pw2=m_b_pw2, w_out=m_w_out,
                 norm_mix_post=m_norm_mix_post, norm_mlp_pre=m_norm_mlp_pre, w_ff1=m_w_ff1, w_ff2=m_w_ff2,
                 norm_mlp_post=m_norm_mlp_post)
    mom_v = dict(norm_mix_pre=v_norm_mix_pre, w_in=v_w_in, q_norm=v_q_norm, w_uq=v_w_uq, kv_norm=v_kv_norm,
                 w_uk=v_w_uk, w_uv=v_w_uv, w_o_attn=v_w_o_attn, conv_w=v_conv_w, conv_b=v_conv_b,
                 conv_ln_g=v_conv_ln_g, conv_ln_b=v_conv_ln_b, w_pw2=v_w_pw2, b_pw2=v_b_pw2, w_out=v_w_out,
                 norm_mix_post=v_norm_mix_post, norm_mlp_pre=v_norm_mlp_pre, w_ff1=v_w_ff1, w_ff2=v_w_ff2,
                 norm_mlp_post=v_norm_mlp_post)
    cx, cy, cc = lax.axis_index("x"), lax.axis_index("y"), lax.axis_index("c")

    big_local = {n: wts[n] for n in _BIG}
    w_ex = _to_exchange(big_local)
    send = [a.astype(BF16) for a in w_ex]
    conv_i = _BIG.index("conv_w")
    conv_lo = (w_ex[conv_i] - send[conv_i].astype(F32)).astype(BF16)
    send[conv_i] = jnp.stack([send[conv_i], conv_lo])
    weights = _gathered_weights(_all_gather(send, "ag_weights"))

    small = {n: wts[n].reshape(1, -1) for n in _SMALL}
    loss_blk, grad_x, small_grads, big_grads = _local_step(x[0], positions, loss_target[0], small, weights)

    g4 = [g.reshape((4, 2) + g.shape[1:]) for g in big_grads]
    l_sib = _pair_exchange(g4)
    c_idx = cc.reshape(1).astype(jnp.int32)
    p_chip = [_pair_sum(g, l, c_idx, "rs_pair_sum_" + n) for n, g, l in zip(_BIG, g4, l_sib)]
    q_in = _chip_exchange(p_chip)
    idx = jnp.stack([2 * cx + cy, cc]).astype(jnp.int32)
    m_ex, v_ex = _to_exchange({n: mom_m[n] for n in _BIG}), _to_exchange({n: mom_v[n] for n in _BIG})
    upd = [_update(g4[i], l_sib[i], q_in[i], w_ex[i], m_ex[i], v_ex[i], idx, "update_" + n)
           for i, n in enumerate(_BIG)]
    out_g, out_d, out_m, out_v = (_from_exchange([u[j] for u in upd], big_local) for j in range(4))

    loss_row = jnp.broadcast_to(loss_blk[0:1, 0:1], (1, 1024))
    sv = _small_pack(list(small_grads) + [loss_row])
    sv_all = _all_gather([sv], "ag_small")[0].reshape(N_DEV * SMALL_ROWS, 1024)
    sw = _small_pack([wts[n] for n in _SMALL])
    sm = _small_pack([mom_m[n] for n in _SMALL])
    s_v = _small_pack([mom_v[n] for n in _SMALL])
    sg, sd, sm_new, sv_new = _update_small(sv_all, sw, sm, s_v)
    for i, n in enumerate(_SMALL):
        size = wts[n].size
        out_g[n] = sg[i, :size].reshape(wts[n].shape)
        out_d[n] = sd[i, :size].reshape(wts[n].shape)
        out_m[n] = sm_new[i, :size].reshape(wts[n].shape)
        out_v[n] = sv_new[i, :size].reshape(wts[n].shape)
    loss = sg[LOSS_ROW, 0] * (0.5 / D_MODEL)

    return (loss, grad_x[None], *[out_g[n] for n in _WEIGHTS], *[out_d[n] for n in _WEIGHTS],
            *[out_m[n] for n in _WEIGHTS], *[out_v[n] for n in _WEIGHTS])
```

```python
import collections
import functools

import jax
import jax.numpy as jnp
from jax import lax
from jax.experimental import pallas as pl
from jax.experimental.pallas import tpu as pltpu

F32 = jnp.float32
BF16 = jnp.bfloat16

D_MODEL = 1024
N_HEADS = 8
NOPE = 128
ROPE = 64
HALF = ROPE // 2
Q_RANK = 384
KV_RANK = 256
CONV_W = 31
D_FF = 4096
EPS = 1e-6
ROPE_THETA = 10000.0
HEAD_PAD = 256
QK_SCALE = (NOPE + ROPE) ** -0.5
LOG2E = 1.4426950408889634
SUBLANES = 8
N_DEV = 8
FF_SHARD = D_FF // N_DEV
IN_SHARD = 4800 // N_DEV

ZS = Q_RANK + KV_RANK + 128
OFF_A = ZS
OFF_B = OFF_A + D_MODEL
OFF_GA = OFF_B + D_MODEL
OFF_GC = OFF_GA + D_MODEL
D_IN_PAD = OFF_GC + D_MODEL

ADAM_LR = 0.001
ADAM_B1 = 0.9
ADAM_B2 = 0.999
ADAM_EPS = 1e-08
ADAM_WD = 0.01
ADAM_STEP = 10

VMEM_LIMIT = 56 * 1024 * 1024

_SMALL = ("norm_mix_pre", "q_norm", "kv_norm", "conv_b", "conv_ln_g", "conv_ln_b", "b_pw2", "norm_mix_post",
          "norm_mlp_pre", "norm_mlp_post")
SMALL_ROWS = 16
LOSS_ROW = len(_SMALL)

_BIG = ("w_in", "w_uq", "w_uk", "w_uv", "w_o_attn", "conv_w", "w_pw2", "w_out", "w_ff1", "w_ff2")
_WEIGHTS = ("norm_mix_pre", "w_in", "q_norm", "w_uq", "kv_norm", "w_uk", "w_uv", "w_o_attn", "conv_w", "conv_b",
            "conv_ln_g", "conv_ln_b", "w_pw2", "b_pw2", "w_out", "norm_mix_post", "norm_mlp_pre", "w_ff1", "w_ff2",
            "norm_mlp_post")


def _dot(a, b):
    return jnp.dot(a, b, preferred_element_type=F32)


def _dot_nt(a, b):
    return lax.dot_general(a, b, (((1,), (1,)), ((), ())), preferred_element_type=F32)


def _dot_tn(a, b):
    return lax.dot_general(a, b, (((0,), (0,)), ((), ())), preferred_element_type=F32)


def _sigmoid(x):
    return 1.0 / (1.0 + jnp.exp(-x))


def _rms_fwd(x, g):
    r = lax.rsqrt(jnp.mean(x * x, axis=-1, keepdims=True) + EPS)
    return x * r * g


def _rms_bwd(dy, x, g):
    r = lax.rsqrt(jnp.mean(x * x, axis=-1, keepdims=True) + EPS)
    xh = x * r
    gy = dy * g
    dx = r * (gy - xh * jnp.mean(gy * xh, axis=-1, keepdims=True))
    return dx, jnp.sum(dy * xh, axis=0, keepdims=True)


def _rope(q, c, sa, sb):
    n = q.shape[-1]
    return q * c + pltpu.roll(q, n - HALF, 1) * sa + pltpu.roll(q, HALF, 1) * sb


def _rope_bwd(d, c, sa, sb):
    n = d.shape[-1]
    return d * c - pltpu.roll(d, n - HALF, 1) * sa - pltpu.roll(d, HALF, 1) * sb


def _shifted_copies(buf, shifted, tb):
    n = shifted.shape[1]
    for b in range(1, SUBLANES):
        shifted[b - 1] = buf[pl.ds(b, n), :]


def _rows_at(buf, shifted, start, tb):
    a, b = divmod(start, SUBLANES)
    src = buf if b == 0 else shifted.at[b - 1]
    return src[pl.ds(SUBLANES * a, tb), :]


def _params(n_axes=1):
    return pltpu.CompilerParams(dimension_semantics=("arbitrary",) * n_axes, vmem_limit_bytes=VMEM_LIMIT)


def _row_call(body, name, tb, row_ins, full_ins, row_outs, acc_outs, lane_outs=(), exchange=None):
    t = row_ins[0].shape[0]
    in_specs = [pl.BlockSpec((tb, a.shape[1]), lambda i: (i, 0)) for a in row_ins]
    in_specs += [pl.BlockSpec(a.shape, lambda i, nd=a.ndim: (0,) * nd) for a in full_ins]
    out_specs = [pl.BlockSpec((tb, c), lambda i: (i, 0)) for c, _ in row_outs]
    out_specs += [pl.BlockSpec(s, lambda i, nd=len(s): (0,) * nd) for s, _ in acc_outs]
    out_specs += [pl.BlockSpec((n, None, 1, tb), lambda i: (0, i, 0, 0)) for n in lane_outs]
    out_shape = [jax.ShapeDtypeStruct((t, c), dt) for c, dt in row_outs]
    out_shape += [jax.ShapeDtypeStruct(s, dt) for s, dt in acc_outs]
    out_shape += [jax.ShapeDtypeStruct((n, t // tb, 1, tb), F32) for n in lane_outs]
    if exchange is None:
        return pl.pallas_call(
            functools.partial(body), name=name, grid=(t // tb,), in_specs=in_specs, out_specs=out_specs,
            out_shape=out_shape, compiler_params=_params(1),
        )(*row_ins, *full_ins)
    steps = _phase_steps(len(exchange.phases), t // tb)
    fn = _hosted(body, len(in_specs), len(out_specs), exchange, lambda k: pl.program_id(0) == steps[k])
    res = pl.pallas_call(
        fn, name=name, grid=(t // tb,), in_specs=in_specs + [_ANY] * len(exchange.ins),
        out_specs=out_specs + [_ANY] * len(exchange.out_shapes), out_shape=out_shape + exchange.out_shapes,
        scratch_shapes=exchange.scratch, compiler_params=_params(1),
    )(*row_ins, *full_ins, *exchange.ins)
    return res[:len(out_specs)], res[len(out_specs):]


def _acc(ref, val):
    @pl.when(pl.program_id(0) == 0)
    def _():
        ref[...] = jnp.zeros_like(ref)
    ref[...] += val


def _fwd_in_proj(x, g_pre, w_in_p, exchange, tb=256):
    def body(x_ref, g_ref, w_ref, h_ref, zs_ref, a_ref, sb_ref, u_ref, sa_ref, sc_ref):
        hb = _rms_fwd(x_ref[...], g_ref[...]).astype(BF16)
        h_ref[...] = hb
        zs_ref[...] = _dot(hb, w_ref[:, 0:ZS])
        a = _dot(hb, w_ref[:, OFF_A:OFF_B])
        sb = _sigmoid(_dot(hb, w_ref[:, OFF_B:OFF_GA]))
        a_ref[...] = a
        sb_ref[...] = sb
        u_ref[...] = a * sb
        sa_ref[...] = _sigmoid(_dot(hb, w_ref[:, OFF_GA:OFF_GC]))
        sc_ref[...] = _sigmoid(_dot(hb, w_ref[:, OFF_GC:D_IN_PAD]))

    d = D_MODEL
    return _row_call(body, "fwd_in_proj", tb, [x], [g_pre, w_in_p],
                     [(d, BF16), (ZS, F32), (d, F32), (d, F32), (d, F32), (d, F32), (d, F32)], [], exchange=exchange)


def _fwd_qkv(zs, tc, tsa, tsb, q_norm, kv_norm, w_uq_p, w_uk, w_uv, tb=256):
    def body(zs_ref, c_ref, sa_ref, sb_ref, qg_ref, kg_ref, wq_ref, wk_ref, wv_ref,
             cqn_ref, ckvn_ref, q_ref, k_ref, v_ref):
        zs_ = zs_ref[...]
        c, sa, sb = c_ref[...], sa_ref[...], sb_ref[...]
        cqn = _rms_fwd(zs_[:, 0:Q_RANK], qg_ref[...]).astype(BF16)
        cqn_ref[...] = cqn
        q = jnp.concatenate([_dot(cqn, wq_ref[h]) for h in range(N_HEADS)], axis=1)
        q = _rope(q, jnp.tile(c, (1, N_HEADS)), jnp.tile(sa, (1, N_HEADS)), jnp.tile(sb, (1, N_HEADS)))
        q_ref[...] = q.astype(BF16)
        kr = zs_[:, Q_RANK + KV_RANK:ZS]
        kr = _rope(kr, c[:, NOPE:], sa[:, NOPE:], sb[:, NOPE:]).astype(BF16)
        ckvn = _rms_fwd(zs_[:, Q_RANK:Q_RANK + KV_RANK], kg_ref[...]).astype(BF16)
        ckvn_ref[...] = ckvn
        kn = _dot(ckvn, wk_ref[...]).astype(BF16)
        v_ref[...] = _dot(ckvn, wv_ref[...]).astype(BF16)
        for h in range(N_HEADS):
            k_ref[:, h * HEAD_PAD:h * HEAD_PAD + NOPE] = kn[:, h * NOPE:(h + 1) * NOPE]
            k_ref[:, h * HEAD_PAD + NOPE:(h + 1) * HEAD_PAD] = kr

    hp = N_HEADS * HEAD_PAD
    return _row_call(body, "fwd_qkv", tb, [zs, tc, tsa, tsb], [q_norm, kv_norm, w_uq_p, w_uk, w_uv],
                     [(Q_RANK, BF16), (KV_RANK, BF16), (hp, BF16), (hp, BF16), (D_MODEL, BF16)], [])


def _attn_fwd(q, k, v, exchange, tq=512):
    t = q.shape[0]
    nq = t // tq
    c2 = QK_SCALE * LOG2E

    def body(q_ref, k_ref, v_ref, o_ref, lse_ref, m_sc, l_sc, acc_sc):
        i = pl.program_id(1)
        m_sc[...] = jnp.full_like(m_sc, -1e30)
        l_sc[...] = jnp.zeros_like(l_sc)
        acc_sc[...] = jnp.zeros_like(acc_sc)
        qb = q_ref[...]

        def step(j, masked):
            rows = pl.ds(pl.multiple_of(j * tq, tq), tq)
            st = _dot_nt(k_ref[rows, :], qb)
            if masked:
                key = lax.broadcasted_iota(jnp.int32, (tq, tq), 0)
                qry = lax.broadcasted_iota(jnp.int32, (tq, tq), 1)
                st = jnp.where(key <= qry, st, -1e30)
            m_prev = m_sc[...]
            m_new = jnp.maximum(m_prev, jnp.max(st, axis=0, keepdims=True))
            alpha = jnp.exp2((m_prev - m_new) * c2)
            pt = jnp.exp2((st - m_new) * c2)
            l_sc[...] = alpha * l_sc[...] + jnp.sum(pt, axis=0, keepdims=True)
            acc_sc[...] = alpha * acc_sc[...] + _dot_tn(v_ref[rows, :], pt.astype(BF16))
            m_sc[...] = m_new

        def unmasked(j, carry):
            step(j, False)
            return carry

        lax.fori_loop(0, i, unmasked, 0)
        step(i, True)
        l = l_sc[...]
        o_ref[...] = (acc_sc[...] / l).T.astype(BF16)
        lse_ref[...] = m_sc[...] * c2 + jnp.log2(l)

    steps = _phase_steps(len(exchange.phases), N_HEADS * nq)
    fn = _hosted(body, 3, 2, exchange, lambda p: pl.program_id(0) * nq + pl.program_id(1) == steps[p])
    res = pl.pallas_call(
        fn, name="attn_fwd", grid=(N_HEADS, nq),
        in_specs=[pl.BlockSpec((tq, HEAD_PAD), lambda h, i: (i, h)),
                  pl.BlockSpec((t, HEAD_PAD), lambda h, i: (0, h)),
                  pl.BlockSpec((t, NOPE), lambda h, i: (0, h))] + [_ANY] * len(exchange.ins),
        out_specs=[pl.BlockSpec((tq, NOPE), lambda h, i: (i, h)),
                   pl.BlockSpec((None, None, 1, tq), lambda h, i: (h, i, 0, 0))] + [_ANY] * len(exchange.out_shapes),
        out_shape=[jax.ShapeDtypeStruct((t, D_MODEL), BF16),
                   jax.ShapeDtypeStruct((N_HEADS, nq, 1, tq), F32)] + exchange.out_shapes,
        scratch_shapes=[pltpu.VMEM((1, tq), F32), pltpu.VMEM((1, tq), F32),
                        pltpu.VMEM((NOPE, tq), F32)] + exchange.scratch,
        compiler_params=_params(2),
    )(q, k, v, *exchange.ins)
    return res[:2], res[2:]


def _conv_fwd(u, conv_w, conv_b, ln_g, ln_b, tb=256):
    t, c = u.shape
    halo = 32

    def body(u_ref, up_ref, w_ref, b_ref, g_ref, be_ref, co_ref, act_ref, buf, shifted):
        i = pl.program_id(0)
        buf[0:halo, :] = jnp.where(i == 0, 0.0, up_ref[...])
        buf[halo:halo + tb, :] = u_ref[...]
        _shifted_copies(buf, shifted, tb)
        acc = jnp.zeros((tb, c), F32)
        for k in range(CONV_W):
            acc = acc + w_ref[k:k + 1, :] * _rows_at(buf, shifted, halo - (CONV_W - 1) + k, tb)
        co = acc + b_ref[...]
        co_ref[...] = co
        mu = jnp.mean(co, axis=-1, keepdims=True)
        xc = co - mu
        r = lax.rsqrt(jnp.mean(xc * xc, axis=-1, keepdims=True) + EPS)
        y = xc * r * g_ref[...] + be_ref[...]
        act_ref[...] = (y * _sigmoid(y)).astype(BF16)

    ratio = tb // halo
    return pl.pallas_call(
        functools.partial(body), name="conv_fwd", grid=(t // tb,),
        in_specs=[pl.BlockSpec((tb, c), lambda i: (i, 0)),
                  pl.BlockSpec((halo, c), lambda i: (jnp.maximum(i * ratio - 1, 0), 0)),
                  pl.BlockSpec(conv_w.shape, lambda i: (0, 0)),
                  pl.BlockSpec((1, c), lambda i: (0, 0)), pl.BlockSpec((1, c), lambda i: (0, 0)),
                  pl.BlockSpec((1, c), lambda i: (0, 0))],
        out_specs=[pl.BlockSpec((tb, c), lambda i: (i, 0)), pl.BlockSpec((tb, c), lambda i: (i, 0))],
        out_shape=[jax.ShapeDtypeStruct((t, c), F32), jax.ShapeDtypeStruct((t, c), BF16)],
        scratch_shapes=[pltpu.VMEM((tb + halo, c), F32), pltpu.VMEM((SUBLANES - 1, tb + halo - SUBLANES, c), F32)],
        compiler_params=_params(1),
    )(u, u, conv_w, conv_b, ln_g, ln_b)


def _fwd_merge(attn, uact, sa, sc, x, w_o, w_pw2, b_pw2, w_out, g_post, tb=256):
    def body(at_ref, ua_ref, sa_ref, sc_ref, x_ref, wo_ref, wp_ref, bp_ref, wout_ref, g_ref,
             ya_ref, yc_ref, mb_ref, m_ref, x1_ref):
        ya = _dot(at_ref[...], wo_ref[...])
        yc = _dot(ua_ref[...], wp_ref[...]) + bp_ref[...]
        ya_ref[...] = ya
        yc_ref[...] = yc
        mb = (sa_ref[...] * ya + sc_ref[...] * yc).astype(BF16)
        mb_ref[...] = mb
        m = _dot(mb, wout_ref[...])
        m_ref[...] = m
        x1_ref[...] = x_ref[...] + _rms_fwd(m, g_ref[...])

    d = D_MODEL
    return _row_call(body, "fwd_merge", tb, [attn, uact, sa, sc, x], [w_o, w_pw2, b_pw2, w_out, g_post],
                     [(d, F32), (d, F32), (d, BF16), (d, F32), (d, F32)], [])


def _fwd_ff1(x1, g, w_ff1, tb=256):
    def body(x1_ref, g_ref, w_ref, h2_ref, r1_ref, act_ref):
        h2 = _rms_fwd(x1_ref[...], g_ref[...]).astype(BF16)
        h2_ref[...] = h2
        for j in range(N_DEV):
            cols = slice(j * FF_SHARD, (j + 1) * FF_SHARD)
            r1 = jnp.maximum(_dot(h2, w_ref[j]), 0.0)
            r1_ref[:, cols] = r1.astype(BF16)
            act_ref[:, cols] = (r1 * r1).astype(BF16)

    return _row_call(body, "fwd_ff1", tb, [x1], [g, w_ff1], [(D_MODEL, BF16), (D_FF, BF16), (D_FF, BF16)], [])


def _fwd_ff2_loss(act, x1, target, w_ff2, g, tb=256):
    def body(act_ref, x1_ref, tg_ref, w_ref, g_ref, f_ref, dy_ref, loss_ref):
        f = _dot(act_ref[...], w_ref[...])
        f_ref[...] = f
        e = x1_ref[...] + _rms_fwd(f, g_ref[...]) - tg_ref[...]
        dy_ref[...] = e * (1.0 / D_MODEL)
        _acc(loss_ref, jnp.sum(e * e))

    return _row_call(body, "fwd_ff2_loss", tb, [act, x1, target], [w_ff2, g],
                     [(D_MODEL, F32), (D_MODEL, F32)], [((8, 128), F32)])


def _bwd_ff2(dy, f, r1, w_ff2, g, tb=256):
    def body(dy_ref, f_ref, r1_ref, w_ref, g_ref, df_ref, df1_ref, dg_ref):
        df, dg = _rms_bwd(dy_ref[...], f_ref[...], g_ref[...])
        _acc(dg_ref, dg)
        dfb = df.astype(BF16)
        df_ref[...] = dfb
        dact = _dot_nt(dfb, w_ref[...])
        df1_ref[...] = (dact * (2.0 * r1_ref[...].astype(F32))).astype(BF16)

    return _row_call(body, "bwd_ff2", tb, [dy, f, r1], [w_ff2, g], [(D_MODEL, BF16), (D_FF, BF16)],
                     [((1, D_MODEL), F32)])


def _bwd_ff1(df1, x1, dy, w_ff1, g, tb=256):
    def body(df1_ref, x1_ref, dy_ref, w_ref, g_ref, dx1_ref, dg_ref):
        dh2 = _dot_nt(df1_ref[:, 0:FF_SHARD], w_ref[0])
        for j in range(1, N_DEV):
            dh2 = dh2 + _dot_nt(df1_ref[:, j * FF_SHARD:(j + 1) * FF_SHARD], w_ref[j])
        dxn, dg = _rms_bwd(dh2, x1_ref[...], g_ref[...])
        _acc(dg_ref, dg)
        dx1_ref[...] = dy_ref[...] + dxn

    return _row_call(body, "bwd_ff1", tb, [df1, x1, dy], [w_ff1, g], [(D_MODEL, F32)], [((1, D_MODEL), F32)])


def _bwd_merge(dx1, m, sa, sc, ya, yc, attn, w_out, w_o, w_pw2, g_post, exchange, tb=256):
    def body(dx1_ref, m_ref, sa_ref, sc_ref, ya_ref, yc_ref, at_ref, wout_ref, wo_ref, wp_ref, g_ref,
             dm_ref, dya_ref, dyc_ref, dga_ref, dgc_ref, dat_ref, dua_ref, dg_ref, dbp_ref, delta_ref):
        dm, dg = _rms_bwd(dx1_ref[...], m_ref[...], g_ref[...])
        _acc(dg_ref, dg)
        dmb = dm.astype(BF16)
        dm_ref[...] = dmb
        dmerged = _dot_nt(dmb, wout_ref[...])
        sa, sc = sa_ref[...], sc_ref[...]
        dya = dmerged * sa
        dyc = dmerged * sc
        _acc(dbp_ref, jnp.sum(dyc, axis=0, keepdims=True))
        dyab = dya.astype(BF16)
        dycb = dyc.astype(BF16)
        dya_ref[...] = dyab
        dyc_ref[...] = dycb
        dga_ref[...] = (dmerged * ya_ref[...] * sa * (1.0 - sa)).astype(BF16)
        dgc_ref[...] = (dmerged * yc_ref[...] * sc * (1.0 - sc)).astype(BF16)
        dat = _dot_nt(dyab, wo_ref[...])
        dat_ref[...] = dat.astype(BF16)
        prod = dat * at_ref[...].astype(F32)
        lane = lax.broadcasted_iota(jnp.int32, (tb, NOPE), 1)
        dl = jnp.zeros((tb, NOPE), F32)
        for h in range(N_HEADS):
            dl = dl + jnp.where(lane == h, jnp.sum(prod[:, h * NOPE:(h + 1) * NOPE], axis=1, keepdims=True), 0.0)
        dlt = dl.T
        for h in range(N_HEADS):
            delta_ref[h] = dlt[h:h + 1, :]
        dua_ref[...] = _dot_nt(dycb, wp_ref[...])

    d = D_MODEL
    return _row_call(body, "bwd_merge", tb, [dx1, m, sa, sc, ya, yc, attn], [w_out, w_o, w_pw2, g_post],
                     [(d, BF16), (d, BF16), (d, BF16), (d, BF16), (d, BF16), (d, BF16), (d, F32)],
                     [((1, d), F32), ((1, d), F32)], lane_outs=(N_HEADS,), exchange=exchange)


def _bwd_ln(dua, co, ln_g, ln_b, tb=256):
    def body(dua_ref, co_ref, g_ref, be_ref, dco_ref, dg_ref, db_ref, dcb_ref):
        co = co_ref[...]
        g = g_ref[...]
        mu = jnp.mean(co, axis=-1, keepdims=True)
        xc = co - mu
        r = lax.rsqrt(jnp.mean(xc * xc, axis=-1, keepdims=True) + EPS)
        xh = xc * r
        y = xh * g + be_ref[...]
        s = _sigmoid(y)
        dy = dua_ref[...] * (s + y * s * (1.0 - s))
        _acc(db_ref, jnp.sum(dy, axis=0, keepdims=True))
        _acc(dg_ref, jnp.sum(dy * xh, axis=0, keepdims=True))
        gy = dy * g
        dco = r * (gy - jnp.mean(gy, axis=-1, keepdims=True) - xh * jnp.mean(gy * xh, axis=-1, keepdims=True))
        dco_ref[...] = dco
        _acc(dcb_ref, jnp.sum(dco, axis=0, keepdims=True))

    d = D_MODEL
    return _row_call(body, "bwd_ln", tb, [dua, co], [ln_g, ln_b], [(d, F32)],
                     [((1, d), F32), ((1, d), F32), ((1, d), F32)])


def _conv_bwd(dco, u, a, sb, conv_w, exchange, tb=256):
    t, c = u.shape
    halo = 32
    ratio = tb // halo
    nblk = t // tb

    def body(d_ref, dn_ref, u_ref, up_ref, a_ref, sb_ref, w_ref, da_ref, db_ref, dw_ref, bufd, bufu, shd, shu):
        i = pl.program_id(0)

        @pl.when(i == 0)
        def _():
            dw_ref[...] = jnp.zeros_like(dw_ref)

        dco = d_ref[...]
        bufd[0:tb, :] = dco
        bufd[tb:tb + halo, :] = jnp.where(i == nblk - 1, 0.0, dn_ref[...])
        bufu[0:halo, :] = jnp.where(i == 0, 0.0, up_ref[...])
        bufu[halo:halo + tb, :] = u_ref[...]
        _shifted_copies(bufd, shd, tb)
        _shifted_copies(bufu, shu, tb)
        du = jnp.zeros((tb, c), F32)
        for k in range(CONV_W):
            du = du + w_ref[k:k + 1, :] * _rows_at(bufd, shd, CONV_W - 1 - k, tb)
            dw_ref[k:k + 1, :] += jnp.sum(dco * _rows_at(bufu, shu, halo - (CONV_W - 1) + k, tb), axis=0,
                                          keepdims=True)
        sb_ = sb_ref[...]
        da_ref[...] = (du * sb_).astype(BF16)
        db_ref[...] = (du * a_ref[...] * sb_ * (1.0 - sb_)).astype(BF16)

    steps = _phase_steps(len(exchange.phases), nblk)
    fn = _hosted(body, 7, 3, exchange, lambda p: pl.program_id(0) == steps[p])
    res = pl.pallas_call(
        fn, name="conv_bwd", grid=(nblk,),
        in_specs=[pl.BlockSpec((tb, c), lambda i: (i, 0)),
                  pl.BlockSpec((halo, c), lambda i: (jnp.minimum((i + 1) * ratio, t // halo - 1), 0)),
                  pl.BlockSpec((tb, c), lambda i: (i, 0)),
                  pl.BlockSpec((halo, c), lambda i: (jnp.maximum(i * ratio - 1, 0), 0)),
                  pl.BlockSpec((tb, c), lambda i: (i, 0)), pl.BlockSpec((tb, c), lambda i: (i, 0)),
                  pl.BlockSpec(conv_w.shape, lambda i: (0, 0))] + [_ANY] * len(exchange.ins),
        out_specs=[pl.BlockSpec((tb, c), lambda i: (i, 0)), pl.BlockSpec((tb, c), lambda i: (i, 0)),
                   pl.BlockSpec((32, c), lambda i: (0, 0))] + [_ANY] * len(exchange.out_shapes),
        out_shape=[jax.ShapeDtypeStruct((t, c), BF16), jax.ShapeDtypeStruct((t, c), BF16),
                   jax.ShapeDtypeStruct((32, c), F32)] + exchange.out_shapes,
        scratch_shapes=[pltpu.VMEM((tb + halo, c), F32), pltpu.VMEM((tb + halo, c), F32),
                        pltpu.VMEM((SUBLANES - 1, tb + halo - SUBLANES, c), F32),
                        pltpu.VMEM((SUBLANES - 1, tb + halo - SUBLANES, c), F32)] + exchange.scratch,
        compiler_params=_params(1),
    )(dco, dco, u, u, a, sb, conv_w, *exchange.ins)
    return res[:3], res[3:]


def _attn_bwd(q, k, v, do, lse2, delta, exchange, tq=512):
    t = q.shape[0]
    nq = t // tq
    td = delta.shape[-1]
    per = tq // td
    c2 = QK_SCALE * LOG2E

    def body(q_ref, k_ref, v_ref, do_ref, lse_ref, dl_ref, dq_ref, dk_ref, dv_ref, dk_sc, dv_sc):
        j = pl.program_id(1)

        @pl.when(j == 0)
        def _():
            dq_ref[...] = jnp.zeros_like(dq_ref)

        dk_sc[...] = jnp.zeros_like(dk_sc)
        dv_sc[...] = jnp.zeros_like(dv_sc)
        kb, vb = k_ref[...], v_ref[...]

        def step(i, masked):
            rows = pl.ds(pl.multiple_of(i * tq, tq), tq)
            qb, dob = q_ref[rows, :], do_ref[rows, :]
            pt = jnp.exp2(_dot_nt(kb, qb) * c2 - lse_ref[i])
            if masked:
                key = lax.broadcasted_iota(jnp.int32, (tq, tq), 0)
                qry = lax.broadcasted_iota(jnp.int32, (tq, tq), 1)
                pt = jnp.where(key <= qry, pt, 0.0)
            dv_sc[...] += _dot(pt.astype(BF16), dob)
            dl = jnp.concatenate([dl_ref[per * i + r] for r in range(per)], axis=-1)
            dst = (pt * (_dot_nt(vb, dob) - dl) * QK_SCALE).astype(BF16)
            dk_sc[...] += _dot(dst, qb)
            dq_ref[rows, :] += _dot_tn(dst, kb)

        def unmasked(i, carry):
            step(i, False)
            return carry

        step(j, True)
        lax.fori_loop(j + 1, nq, unmasked, 0)
        dk_ref[...] = dk_sc[...]
        dv_ref[...] = dv_sc[...].astype(BF16)

    hp = N_HEADS * HEAD_PAD
    steps = _phase_steps(len(exchange.phases), N_HEADS * nq)
    fn = _hosted(body, 6, 3, exchange, lambda p: pl.program_id(0) * nq + pl.program_id(1) == steps[p])
    res = pl.pallas_call(
        fn, name="attn_bwd", grid=(N_HEADS, nq),
        in_specs=[pl.BlockSpec((t, HEAD_PAD), lambda h, j: (0, h)),
                  pl.BlockSpec((tq, HEAD_PAD), lambda h, j: (j, h)),
                  pl.BlockSpec((tq, NOPE), lambda h, j: (j, h)),
                  pl.BlockSpec((t, NOPE), lambda h, j: (0, h)),
                  pl.BlockSpec((None, nq, 1, tq), lambda h, j: (h, 0, 0, 0)),
                  pl.BlockSpec((None, t // td, 1, td), lambda h, j: (h, 0, 0, 0))] + [_ANY] * len(exchange.ins),
        out_specs=[pl.BlockSpec((t, HEAD_PAD), lambda h, j: (0, h)),
                   pl.BlockSpec((tq, HEAD_PAD), lambda h, j: (j, h)),
                   pl.BlockSpec((tq, NOPE), lambda h, j: (j, h))] + [_ANY] * len(exchange.out_shapes),
        out_shape=[jax.ShapeDtypeStruct((t, hp), F32), jax.ShapeDtypeStruct((t, hp), F32),
                   jax.ShapeDtypeStruct((t, D_MODEL), BF16)] + exchange.out_shapes,
        scratch_shapes=[pltpu.VMEM((tq, HEAD_PAD), F32), pltpu.VMEM((tq, NOPE), F32)] + exchange.scratch,
        compiler_params=_params(2),
    )(q, k, v, do, lse2, delta, *exchange.ins)
    return res[:3], res[3:]


def _bwd_qkv(dq, dk, dv, zs, tc, tsa, tsb, q_norm, kv_norm, w_uq_p, w_uk, w_uv, tb=256):
    def body(dq_ref, dk_ref, dv_ref, zs_ref, c_ref, sa_ref, sb_ref, qg_ref, kg_ref, wq_ref, wk_ref, wv_ref,
             dqp_ref, dkn_ref, dzs_ref, dqg_ref, dkg_ref):
        c, sa, sb = c_ref[...], sa_ref[...], sb_ref[...]
        zs_ = zs_ref[...]
        dqp = _rope_bwd(dq_ref[...], jnp.tile(c, (1, N_HEADS)), jnp.tile(sa, (1, N_HEADS)),
                        jnp.tile(sb, (1, N_HEADS))).astype(BF16)
        dqp_ref[...] = dqp
        dcqn = _dot_nt(dqp[:, 0:HEAD_PAD], wq_ref[0])
        for h in range(1, N_HEADS):
            dcqn = dcqn + _dot_nt(dqp[:, h * HEAD_PAD:(h + 1) * HEAD_PAD], wq_ref[h])
        dcq, dqg = _rms_bwd(dcqn, zs_[:, 0:Q_RANK], qg_ref[...])
        _acc(dqg_ref, dqg)
        dzs_ref[:, 0:Q_RANK] = dcq.astype(BF16)
        dkr = jnp.zeros((tb, NOPE), F32)
        for h in range(N_HEADS):
            dkn_ref[:, h * NOPE:(h + 1) * NOPE] = dk_ref[:, h * HEAD_PAD:h * HEAD_PAD + NOPE].astype(BF16)
            dkr = dkr + dk_ref[:, h * HEAD_PAD + NOPE:(h + 1) * HEAD_PAD]
        dzs_ref[:, Q_RANK + KV_RANK:ZS] = _rope_bwd(dkr, c[:, NOPE:], sa[:, NOPE:], sb[:, NOPE:]).astype(BF16)
        dckvn = _dot_nt(dkn_ref[...], wk_ref[...]) + _dot_nt(dv_ref[...], wv_ref[...])
        dckv, dkg = _rms_bwd(dckvn, zs_[:, Q_RANK:Q_RANK + KV_RANK], kg_ref[...])
        _acc(dkg_ref, dkg)
        dzs_ref[:, Q_RANK:Q_RANK + KV_RANK] = dckv.astype(BF16)

    hp = N_HEADS * HEAD_PAD
    return _row_call(body, "bwd_qkv", tb, [dq, dk, dv, zs, tc, tsa, tsb], [q_norm, kv_norm, w_uq_p, w_uk, w_uv],
                     [(hp, BF16), (D_MODEL, BF16), (ZS, BF16)], [((1, Q_RANK), F32), ((1, KV_RANK), F32)])


def _bwd_in_proj(dzs, da, db, dga, dgc, x, dx1, w_in_p, g_pre, exchange, tb=256):
    def body(dzs_ref, da_ref, db_ref, dga_ref, dgc_ref, x_ref, dx1_ref, w_ref, g_ref, gx_ref, dg_ref):
        dh = _dot_nt(dzs_ref[...], w_ref[:, 0:ZS])
        dh = dh + _dot_nt(da_ref[...], w_ref[:, OFF_A:OFF_B])
        dh = dh + _dot_nt(db_ref[...], w_ref[:, OFF_B:OFF_GA])
        dh = dh + _dot_nt(dga_ref[...], w_ref[:, OFF_GA:OFF_GC])
        dh = dh + _dot_nt(dgc_ref[...], w_ref[:, OFF_GC:D_IN_PAD])
        dxn, dg = _rms_bwd(dh, x_ref[...], g_ref[...])
        _acc(dg_ref, dg)
        gx_ref[...] = dx1_ref[...] + dxn

    return _row_call(body, "bwd_in_proj", tb, [dzs, da, db, dga, dgc, x, dx1], [w_in_p, g_pre],
                     [(D_MODEL, F32)], [((1, D_MODEL), F32)], exchange=exchange)


def _mm_tn(a, b, name, shard_cols=None, tt=512):
    t, m = a.shape
    n = b.shape[1]
    tm = min(m, 1024)
    tn = min(n, 1024)
    nt = t // tt
    per = tn // shard_cols if shard_cols else 1

    def body(a_ref, b_ref, o_ref, acc):
        k = pl.program_id(2)

        @pl.when(k == 0)
        def _():
            acc[...] = jnp.zeros_like(acc)

        acc[...] += _dot_tn(a_ref[...], b_ref[...])

        @pl.when(k == nt - 1)
        def _():
            if shard_cols:
                for s in range(per):
                    o_ref[s] = acc[:, s * shard_cols:(s + 1) * shard_cols].astype(BF16)
            else:
                o_ref[...] = acc[...].astype(BF16)

    if shard_cols:
        out_spec = pl.BlockSpec((per, tm, shard_cols), lambda i, j, k: (j, i, 0))
        out_shape = jax.ShapeDtypeStruct((n // shard_cols, m, shard_cols), BF16)
    else:
        out_spec = pl.BlockSpec((tm, tn), lambda i, j, k: (i, j))
        out_shape = jax.ShapeDtypeStruct((m, n), BF16)
    return pl.pallas_call(
        functools.partial(body), name=name, grid=(m // tm, n // tn, nt),
        in_specs=[pl.BlockSpec((tt, tm), lambda i, j, k: (k, i)), pl.BlockSpec((tt, tn), lambda i, j, k: (k, j))],
        out_specs=out_spec, out_shape=out_shape, scratch_shapes=[pltpu.VMEM((tm, tn), F32)],
        compiler_params=_params(3),
    )(a, b)


_ANY = pl.BlockSpec(memory_space=pl.ANY)
_MESH = pl.DeviceIdType.MESH

_Exchange = collections.namedtuple("_Exchange", "ins out_shapes scratch phases")


def _ag_exchange(shards):
    n = len(shards)

    def parts(ins, outs, sems):
        send_sems, recv_sems, _ = sems
        x, y, c = lax.axis_index("x"), lax.axis_index("y"), lax.axis_index("c")
        chips = [(1 - x, y), (x, 1 - y), (1 - x, 1 - y)]

        def copy(w, k, block, to, src=None):
            dst = outs[w].at[4 * block[0] + 2 * block[1] + block[2]]
            return pltpu.make_async_remote_copy(
                src_ref=dst if src is None else src, dst_ref=dst, send_sem=send_sems.at[7 * w + k],
                recv_sem=recv_sems.at[7 * w + k], device_id=to, device_id_type=_MESH)

        def first(w):
            return [copy(w, 0, (x, y, c), (x, y, 1 - c), src=ins[w])] + [
                copy(w, 1 + j, (x, y, c), (*chip, c), src=ins[w]) for j, chip in enumerate(chips)]

        def mine(w):
            return pltpu.make_async_copy(ins[w], outs[w].at[4 * x + 2 * y + c], sems[2].at[w])

        return (x, y, c), chips, copy, first, mine

    def start(ins, outs, sems):
        _, _, _, first, mine = parts(ins, outs, sems)
        for w in range(n):
            mine(w).start()
            for cp in first(w):
                cp.start()

    def forward(ins, outs, sems):
        (x, y, c), chips, copy, _, _ = parts(ins, outs, sems)
        for j, chip in enumerate(chips):
            for w in range(n):
                copy(w, 1 + j, (*chip, c), (x, y, c)).wait_recv()
                copy(w, 4 + j, (*chip, c), (x, y, 1 - c)).start()

    def finish(ins, outs, sems):
        (x, y, c), chips, copy, first, mine = parts(ins, outs, sems)
        for w in range(n):
            copy(w, 0, (x, y, 1 - c), (x, y, c)).wait_recv()
        for j, chip in enumerate(chips):
            for w in range(n):
                copy(w, 4 + j, (*chip, 1 - c), (x, y, c)).wait_recv()
        for w in range(n):
            for cp in first(w) + [copy(w, 4 + j, (*chip, c), (x, y, 1 - c)) for j, chip in enumerate(chips)]:
                cp.wait_send()
            mine(w).wait()

    return _Exchange(
        ins=list(shards), out_shapes=[jax.ShapeDtypeStruct((N_DEV,) + s.shape, s.dtype) for s in shards],
        scratch=[pltpu.SemaphoreType.DMA((7 * n,)), pltpu.SemaphoreType.DMA((7 * n,)), pltpu.SemaphoreType.DMA((n,))],
        phases=[start, forward, finish])


def _pair_exchange(gs):
    n = len(gs)

    def copies(ins, outs, sems):
        x, y, c = lax.axis_index("x"), lax.axis_index("y"), lax.axis_index("c")
        return [pltpu.make_async_remote_copy(
            src_ref=ins[w].at[:, 1 - c], dst_ref=outs[w], send_sem=sems[0].at[w], recv_sem=sems[1].at[w],
            device_id=(x, y, 1 - c), device_id_type=_MESH) for w in range(n)]

    return _start_then_wait(gs, [jax.ShapeDtypeStruct((4,) + g.shape[2:], g.dtype) for g in gs], n, copies)


def _start_then_wait(ins, out_shapes, n_copies, copies):
    def start(ins_, outs, sems):
        for cp in copies(ins_, outs, sems):
            cp.start()

    def finish(ins_, outs, sems):
        for cp in copies(ins_, outs, sems):
            cp.wait()

    return _Exchange(ins=list(ins), out_shapes=out_shapes,
                     scratch=[pltpu.SemaphoreType.DMA((n_copies,)), pltpu.SemaphoreType.DMA((n_copies,))],
                     phases=[start, finish])


def _run_exchange(ex, name):
    ni, no = len(ex.ins), len(ex.out_shapes)

    def body(*refs):
        for phase in ex.phases:
            phase(refs[:ni], refs[ni:ni + no], refs[ni + no:])

    return pl.pallas_call(functools.partial(body), name=name, out_shape=ex.out_shapes, in_specs=[_ANY] * ni,
                          out_specs=[_ANY] * no, scratch_shapes=ex.scratch)(*ex.ins)


def _hosted(body, n_in, n_out, ex, when):
    ni, no, ns = len(ex.ins), len(ex.out_shapes), len(ex.scratch)

    def fn(*refs):
        ins, ex_ins = refs[:n_in], refs[n_in:n_in + ni]
        outs = refs[n_in + ni:n_in + ni + n_out]
        ex_outs = refs[n_in + ni + n_out:n_in + ni + n_out + no]
        scratch, sems = refs[n_in + ni + n_out + no:len(refs) - ns], refs[len(refs) - ns:]
        last = len(ex.phases) - 1
        for k in range(last):
            pl.when(when(k))(functools.partial(ex.phases[k], ex_ins, ex_outs, sems))
        body(*ins, *outs, *scratch)
        pl.when(when(last))(functools.partial(ex.phases[last], ex_ins, ex_outs, sems))

    return fn


def _phase_steps(n_phases, n_steps):
    return [0, n_steps - 1] if n_phases == 2 else [0, 2 * n_steps // 3, n_steps - 1]


def _pair_sum(g, l, c_idx, name):
    _, _, r, n = g.shape
    tr = min(r, 512)

    def body(c_ref, g_ref, l_ref, p_ref):
        p_ref[...] = (g_ref[...].astype(F32) + l_ref[...].astype(F32)).astype(BF16)

    return pl.pallas_call(
        functools.partial(body), name=name,
        grid_spec=pltpu.PrefetchScalarGridSpec(
            num_scalar_prefetch=1, grid=(4, r // tr),
            in_specs=[pl.BlockSpec((None, None, tr, n), lambda k, i, c: (k, c[0], i, 0)),
                      pl.BlockSpec((None, tr, n), lambda k, i, c: (k, i, 0))],
            out_specs=pl.BlockSpec((None, tr, n), lambda k, i, c: (k, i, 0))),
        out_shape=jax.ShapeDtypeStruct((4, r, n), BF16), compiler_params=_params(2),
    )(c_idx, g, l)


def _chip_exchange(ps):
    n = len(ps)

    def copies(ins, outs, sems):
        x, y, c = lax.axis_index("x"), lax.axis_index("y"), lax.axis_index("c")
        chips = [(1 - x, y), (x, 1 - y), (1 - x, 1 - y)]
        return [pltpu.make_async_remote_copy(
            src_ref=ins[w].at[2 * px + py], dst_ref=outs[w].at[s], send_sem=sems[0].at[3 * w + s],
            recv_sem=sems[1].at[3 * w + s], device_id=(px, py, c), device_id_type=_MESH)
            for w in range(n) for s, (px, py) in enumerate(chips)]

    return _start_then_wait(ps, [jax.ShapeDtypeStruct((3,) + p.shape[1:], p.dtype) for p in ps], 3 * n, copies)


def _adamw(w, g, m, v):
    m2 = ADAM_B1 * m + (1.0 - ADAM_B1) * g
    v2 = ADAM_B2 * v + (1.0 - ADAM_B2) * (g * g)
    m_hat = m2 / (1.0 - ADAM_B1 ** ADAM_STEP)
    v_hat = v2 / (1.0 - ADAM_B2 ** ADAM_STEP)
    delta = -ADAM_LR * (m_hat / (jnp.sqrt(v_hat) + ADAM_EPS) + ADAM_WD * w)
    return delta, m2, v2


def _update(g, l, q, w, m, v, idx, name):
    _, _, r, n = g.shape
    tr = min(r, 256)
    while r % tr:
        tr -= 16

    def body(idx_ref, g_ref, l_ref, q_ref, w_ref, m_ref, v_ref, go_ref, d_ref, mo_ref, vo_ref):
        gr = g_ref[...].astype(F32) + l_ref[...].astype(F32)
        gr = gr + q_ref[0].astype(F32)
        gr = gr + q_ref[1].astype(F32)
        gr = gr + q_ref[2].astype(F32)
        go_ref[...] = gr
        d_ref[...], mo_ref[...], vo_ref[...] = _adamw(w_ref[...], gr, m_ref[...], v_ref[...])

    row = pl.BlockSpec((tr, n), lambda i, s: (i, 0))
    return pl.pallas_call(
        functools.partial(body), name=name,
        grid_spec=pltpu.PrefetchScalarGridSpec(
            num_scalar_prefetch=1, grid=(r // tr,),
            in_specs=[pl.BlockSpec((None, None, tr, n), lambda i, s: (s[0], s[1], i, 0)),
                      pl.BlockSpec((None, tr, n), lambda i, s: (s[0], i, 0)),
                      pl.BlockSpec((3, tr, n), lambda i, s: (0, i, 0)), row, row, row],
            out_specs=[row, row, row, row]),
        out_shape=[jax.ShapeDtypeStruct((r, n), F32)] * 4, compiler_params=_params(1),
    )(idx, g, l, q, w, m, v)


def _update_small(gathered, w, m, v):
    r, n = w.shape

    def body(ga_ref, w_ref, m_ref, v_ref, go_ref, d_ref, mo_ref, vo_ref):
        gr = ga_ref[0:r, :]
        for dev in range(1, N_DEV):
            gr = gr + ga_ref[dev * r:(dev + 1) * r, :]
        go_ref[...] = gr
        d_ref[...], mo_ref[...], vo_ref[...] = _adamw(w_ref[...], gr, m_ref[...], v_ref[...])

    return pl.pallas_call(
        functools.partial(body), name="update_small", out_shape=[jax.ShapeDtypeStruct((r, n), F32)] * 4,
    )(gathered, w, m, v)


def _to_exchange(shards):
    conv = jnp.pad(shards["conv_w"].reshape(CONV_W, D_MODEL // N_DEV), ((0, 1), (0, 0)))
    return [shards["w_in"][0], jnp.pad(shards["w_uq"][0], ((0, 0), (0, HEAD_PAD - NOPE - ROPE))),
            shards["w_uk"].reshape(KV_RANK // N_DEV, N_HEADS * NOPE),
            shards["w_uv"].reshape(KV_RANK // N_DEV, N_HEADS * NOPE), shards["w_o_attn"][0], conv,
            shards["w_pw2"][0], shards["w_out"][0], shards["w_ff1"][0], shards["w_ff2"][0]]


def _from_exchange(arrs, like):
    out = {}
    for n, a in zip(_BIG, arrs):
        if n == "w_uq":
            a = a[:, :NOPE + ROPE]
        elif n == "conv_w":
            a = a[:CONV_W]
        out[n] = a.reshape(like[n].shape)
    return out


def _padded_w_in(g_in):
    w_in = g_in.transpose(1, 0, 2).reshape(D_MODEL, N_DEV * IN_SHARD)
    kr_end = Q_RANK + KV_RANK + ROPE
    return jnp.concatenate([w_in[:, :kr_end], jnp.zeros((D_MODEL, 128 - ROPE), BF16), w_in[:, kr_end:]], axis=1)


def _full_conv_w(g_conv):
    return (g_conv[:, 0].astype(F32) + g_conv[:, 1].astype(F32)).transpose(1, 0, 2).reshape(32, D_MODEL)


def _pad_rows(a, rows):
    return jnp.pad(a, ((0, rows - a.shape[0]), (0, 0)))


def _small_pack(vals):
    rows = [jnp.pad(v.reshape(1, -1), ((0, 0), (0, 1024 - v.size))) for v in vals]
    return _pad_rows(jnp.concatenate(rows, axis=0), SMALL_ROWS)


def _rope_tables(positions):
    inv_freq = ROPE_THETA ** (-jnp.arange(0, ROPE, 2, dtype=F32) / ROPE)
    ang = positions.reshape(-1).astype(F32)[:, None] * inv_freq
    cos, sin = jnp.cos(ang), jnp.sin(ang)
    t = cos.shape[0]
    z32, z64 = jnp.zeros((t, HALF), F32), jnp.zeros((t, HEAD_PAD - NOPE - ROPE), F32)
    z128 = jnp.zeros((t, NOPE), F32)
    tc = jnp.concatenate([jnp.ones((t, NOPE), F32), cos, cos, z64], axis=1)
    tsa = jnp.concatenate([z128, -sin, z32, z64], axis=1)
    tsb = jnp.concatenate([z128, z32, sin, z64], axis=1)
    return tc, tsa, tsb


def _blocks(dw):
    if dw.ndim == 2:
        dw = dw.reshape(N_DEV, dw.shape[0] // N_DEV, dw.shape[1])
    return dw.reshape((4, 2) + dw.shape[1:])


def _step(x, positions, target, small, send, c_idx):
    s_in, s_uq, s_uk, s_uv, s_o, s_conv, s_pw2, s_out, s_ff1, s_ff2 = send
    tc, tsa, tsb = _rope_tables(positions)

    w_in_p = _padded_w_in(_run_exchange(_ag_exchange([s_in]), "ag_w_in")[0])
    (h, zs, a, sb, u, sa, sc), (w_uq, w_uk, w_uv) = _fwd_in_proj(
        x, small["norm_mix_pre"], w_in_p, _ag_exchange([s_uq, s_uk, s_uv]))
    w_uk, w_uv = w_uk.reshape(KV_RANK, -1), w_uv.reshape(KV_RANK, -1)
    cqn, ckvn, q, k, v = _fwd_qkv(zs, tc, tsa, tsb, small["q_norm"], small["kv_norm"], w_uq, w_uk, w_uv)
    (attn, lse), (w_o, g_conv, w_pw2, w_out, w_ff1, w_ff2) = _attn_fwd(
        q, k, v, _ag_exchange([s_o, s_conv, s_pw2, s_out, s_ff1, s_ff2]))
    w_o, w_pw2, w_out = (w.reshape(D_MODEL, D_MODEL) for w in (w_o, w_pw2, w_out))
    w_ff2, conv_w = w_ff2.reshape(D_FF, D_MODEL), _full_conv_w(g_conv)
    co, uact = _conv_fwd(u, conv_w, small["conv_b"], small["conv_ln_g"], small["conv_ln_b"])
    ya, yc, mb, m, x1 = _fwd_merge(attn, uact, sa, sc, x, w_o, w_pw2, small["b_pw2"], w_out, small["norm_mix_post"])
    h2, r1, act = _fwd_ff1(x1, small["norm_mlp_pre"], w_ff1)
    f, dy, loss_blk = _fwd_ff2_loss(act, x1, target, w_ff2, small["norm_mlp_post"])

    def pair_sums(names, gs, ls):
        return [_pair_sum(g, l, c_idx, "rs_pair_sum_" + n) for n, g, l in zip(names, gs, ls)]

    df, df1, dg_mlp_post = _bwd_ff2(dy, f, r1, w_ff2, small["norm_mlp_post"])
    dx1, dg_mlp_pre = _bwd_ff1(df1, x1, dy, w_ff1, small["norm_mlp_pre"])
    g_ff = [_blocks(_mm_tn(h2, df1, "dw_ff1", shard_cols=FF_SHARD)), _blocks(_mm_tn(act, df, "dw_ff2"))]
    (dmb, dya, dyc, dga, dgc, dat, dua, dg_mix_post, db_pw2, delta), l_ff = _bwd_merge(
        dx1, m, sa, sc, ya, yc, attn, w_out, w_o, w_pw2, small["norm_mix_post"], _pair_exchange(g_ff))
    p_ff = pair_sums(("w_ff1", "w_ff2"), g_ff, l_ff)
    g_mix = [_blocks(_mm_tn(attn, dya, "dw_o")), _blocks(_mm_tn(uact, dyc, "dw_pw2")),
             _blocks(_mm_tn(mb, dmb, "dw_out"))]
    dco, dln_g, dln_b, dconv_b = _bwd_ln(dua, co, small["conv_ln_g"], small["conv_ln_b"])
    (da, db, dconv), l_mix = _conv_bwd(dco, u, a, sb, conv_w, _pair_exchange(g_mix))
    p_mix = pair_sums(("w_o_attn", "w_pw2", "w_out"), g_mix, l_mix)
    (dq, dk, dv), q_early = _attn_bwd(q, k, v, dat, lse, delta, _chip_exchange(p_ff + p_mix))
    dqp, dkn, dzs, dq_norm, dkv_norm = _bwd_qkv(dq, dk, dv, zs, tc, tsa, tsb, small["q_norm"], small["kv_norm"],
                                                w_uq, w_uk, w_uv)
    dw_in = jnp.concatenate([_mm_tn(h, dzs, "dw_in_zs")[:, :Q_RANK + KV_RANK + ROPE], _mm_tn(h, da, "dw_in_a"),
                             _mm_tn(h, db, "dw_in_b"), _mm_tn(h, dga, "dw_in_ga"), _mm_tn(h, dgc, "dw_in_gc")],
                            axis=1)
    g_late = [_blocks(dw_in.reshape(D_MODEL, N_DEV, IN_SHARD).transpose(1, 0, 2)),
              _blocks(_mm_tn(cqn, dqp, "dw_uq", shard_cols=HEAD_PAD)), _blocks(_mm_tn(ckvn, dkn, "dw_uk")),
              _blocks(_mm_tn(ckvn, dv, "dw_uv")),
              _blocks(dconv.reshape(32, N_DEV, D_MODEL // N_DEV).transpose(1, 0, 2).astype(BF16))]
    l_late = _run_exchange(_pair_exchange(g_late), "rs_pair_exchange_late")
    p_late = pair_sums(("w_in", "w_uq", "w_uk", "w_uv", "conv_w"), g_late, l_late)
    (grad_x, dg_pre), q_late = _bwd_in_proj(dzs, da, db, dga, dgc, x, dx1, w_in_p, small["norm_mix_pre"],
                                            _chip_exchange(p_late))

    order = lambda late, mix, ff: list(late[:4]) + [mix[0], late[4], mix[1], mix[2]] + list(ff)
    exchanged = [order(g_late, g_mix, g_ff), order(l_late, l_mix, l_ff),
                 order(q_late, q_early[2:], q_early[:2])]
    small_grads = (dg_pre, dq_norm, dkv_norm, dconv_b, dln_g, dln_b, db_pw2, dg_mix_post, dg_mlp_pre, dg_mlp_post)
    return loss_blk, grad_x, small_grads, exchanged


def kernel(x, positions, norm_mix_pre, w_in, q_norm, w_uq, kv_norm, w_uk, w_uv, w_o_attn, conv_w, conv_b, conv_ln_g, conv_ln_b, w_pw2, b_pw2, w_out, norm_mix_post, norm_mlp_pre, w_ff1, w_ff2, norm_mlp_post, loss_target, m_norm_mix_pre, m_w_in, m_q_norm, m_w_uq, m_kv_norm, m_w_uk, m_w_uv, m_w_o_attn, m_conv_w, m_conv_b, m_conv_ln_g, m_conv_ln_b, m_w_pw2, m_b_pw2, m_w_out, m_norm_mix_post, m_norm_mlp_pre, m_w_ff1, m_w_ff2, m_norm_mlp_post, v_norm_mix_pre, v_w_in, v_q_norm, v_w_uq, v_kv_norm, v_w_uk, v_w_uv, v_w_o_attn, v_conv_w, v_conv_b, v_conv_ln_g, v_conv_ln_b, v_w_pw2, v_b_pw2, v_w_out, v_norm_mix_post, v_norm_mlp_pre, v_w_ff1, v_w_ff2, v_norm_mlp_post):
    wts = dict(norm_mix_pre=norm_mix_pre, w_in=w_in, q_norm=q_norm, w_uq=w_uq, kv_norm=kv_norm, w_uk=w_uk, w_uv=w_uv,
               w_o_attn=w_o_attn, conv_w=conv_w, conv_b=conv_b, conv_ln_g=conv_ln_g, conv_ln_b=conv_ln_b,
               w_pw2=w_pw2, b_pw2=b_pw2, w_out=w_out, norm_mix_post=norm_mix_post, norm_mlp_pre=norm_mlp_pre,
               w_ff1=w_ff1, w_ff2=w_ff2, norm_mlp_post=norm_mlp_post)
    mom_m = dict(norm_mix_pre=m_norm_mix_pre, w_in=m_w_in, q_norm=m_q_norm, w_uq=m_w_uq, kv_norm=m_kv_norm,
                 w_uk=m_w_uk, w_uv=m_w_uv, w_o_attn=m_w_o_attn, conv_w=m_conv_w, conv_b=m_conv_b,
                 conv_ln_g=m_conv_ln_g, conv_ln_b=m_conv_ln_b, w_pw2=m_w_pw2, b_pw2=m_b_pw2, w_out=m_w_out,
                 norm_mix_post=m_norm_mix_post, norm_mlp_pre=m_norm_mlp_pre, w_ff1=m_w_ff1, w_ff2=m_w_ff2,
                 norm_mlp_post=m_norm_mlp_post)
    mom_v = dict(norm_mix_pre=v_norm_mix_pre, w_in=v_w_in, q_norm=v_q_norm, w_uq=v_w_uq, kv_norm=v_kv_norm,
                 w_uk=v_w_uk, w_uv=v_w_uv, w_o_attn=v_w_o_attn, conv_w=v_conv_w, conv_b=v_conv_b,
                 conv_ln_g=v_conv_ln_g, conv_ln_b=v_conv_ln_b, w_pw2=v_w_pw2, b_pw2=v_b_pw2, w_out=v_w_out,
                 norm_mix_post=v_norm_mix_post, norm_mlp_pre=v_norm_mlp_pre, w_ff1=v_w_ff1, w_ff2=v_w_ff2,
                 norm_mlp_post=v_norm_mlp_post)
    cx, cy, cc = lax.axis_index("x"), lax.axis_index("y"), lax.axis_index("c")

    big_local = {n: wts[n] for n in _BIG}
    w_ex = _to_exchange(big_local)
    send = [a.astype(BF16) for a in w_ex]
    conv_i = _BIG.index("conv_w")
    conv_lo = (w_ex[conv_i] - send[conv_i].astype(F32)).astype(BF16)
    send[conv_i] = jnp.stack([send[conv_i], conv_lo])

    small = {n: wts[n].reshape(1, -1) for n in _SMALL}
    c_idx = cc.reshape(1).astype(jnp.int32)
    loss_blk, grad_x, small_grads, (g4, l_sib, q_in) = _step(x[0], positions, loss_target[0], small, send, c_idx)

    idx = jnp.stack([2 * cx + cy, cc]).astype(jnp.int32)
    m_ex, v_ex = _to_exchange({n: mom_m[n] for n in _BIG}), _to_exchange({n: mom_v[n] for n in _BIG})
    upd = [_update(g4[i], l_sib[i], q_in[i], w_ex[i], m_ex[i], v_ex[i], idx, "update_" + n)
           for i, n in enumerate(_BIG)]
    out_g, out_d, out_m, out_v = (_from_exchange([u[j] for u in upd], big_local) for j in range(4))

    loss_row = jnp.broadcast_to(loss_blk[0:1, 0:1], (1, 1024))
    sv = _small_pack(list(small_grads) + [loss_row])
    sv_all = _run_exchange(_ag_exchange([sv]), "ag_small")[0].reshape(N_DEV * SMALL_ROWS, 1024)
    sw = _small_pack([wts[n] for n in _SMALL])
    sm = _small_pack([mom_m[n] for n in _SMALL])
    s_v = _small_pack([mom_v[n] for n in _SMALL])
    sg, sd, sm_new, sv_new = _update_small(sv_all, sw, sm, s_v)
    for i, n in enumerate(_SMALL):
        size = wts[n].size
        out_g[n] = sg[i, :size].reshape(wts[n].shape)
        out_d[n] = sd[i, :size].reshape(wts[n].shape)
        out_m[n] = sm_new[i, :size].reshape(wts[n].shape)
        out_v[n] = sv_new[i, :size].reshape(wts[n].shape)
    loss = sg[LOSS_ROW, 0] * (0.5 / D_MODEL)

    return (loss, grad_x[None], *[out_g[n] for n in _WEIGHTS], *[out_d[n] for n in _WEIGHTS],
            *[out_m[n] for n in _WEIGHTS], *[out_v[n] for n in _WEIGHTS])
```

```python
import collections
import functools

import jax
import jax.numpy as jnp
from jax import lax
from jax.experimental import pallas as pl
from jax.experimental.pallas import tpu as pltpu

F32 = jnp.float32
BF16 = jnp.bfloat16

D_MODEL = 1024
N_HEADS = 8
NOPE = 128
ROPE = 64
HALF = ROPE // 2
Q_RANK = 384
KV_RANK = 256
CONV_W = 31
D_FF = 4096
EPS = 1e-6
ROPE_THETA = 10000.0
HEAD_PAD = 256
QK_SCALE = (NOPE + ROPE) ** -0.5
LOG2E = 1.4426950408889634
SUBLANES = 8
N_DEV = 8
FF_SHARD = D_FF // N_DEV
IN_SHARD = 4800 // N_DEV

ZS = Q_RANK + KV_RANK + 128
OFF_A = ZS
OFF_B = OFF_A + D_MODEL
OFF_GA = OFF_B + D_MODEL
OFF_GC = OFF_GA + D_MODEL
D_IN_PAD = OFF_GC + D_MODEL

ADAM_LR = 0.001
ADAM_B1 = 0.9
ADAM_B2 = 0.999
ADAM_EPS = 1e-08
ADAM_WD = 0.01
ADAM_STEP = 10

VMEM_LIMIT = 56 * 1024 * 1024

_SMALL = ("norm_mix_pre", "q_norm", "kv_norm", "conv_b", "conv_ln_g", "conv_ln_b", "b_pw2", "norm_mix_post",
          "norm_mlp_pre", "norm_mlp_post")
SMALL_ROWS = 16
LOSS_ROW = len(_SMALL)

_BIG = ("w_in", "w_uq", "w_uk", "w_uv", "w_o_attn", "conv_w", "w_pw2", "w_out", "w_ff1", "w_ff2")
_WEIGHTS = ("norm_mix_pre", "w_in", "q_norm", "w_uq", "kv_norm", "w_uk", "w_uv", "w_o_attn", "conv_w", "conv_b",
            "conv_ln_g", "conv_ln_b", "w_pw2", "b_pw2", "w_out", "norm_mix_post", "norm_mlp_pre", "w_ff1", "w_ff2",
            "norm_mlp_post")


def _dot(a, b):
    return jnp.dot(a, b, preferred_element_type=F32)


def _dot_nt(a, b):
    return lax.dot_general(a, b, (((1,), (1,)), ((), ())), preferred_element_type=F32)


def _dot_tn(a, b):
    return lax.dot_general(a, b, (((0,), (0,)), ((), ())), preferred_element_type=F32)


def _sigmoid(x):
    return 1.0 / (1.0 + jnp.exp(-x))


def _rms_fwd(x, g):
    r = lax.rsqrt(jnp.mean(x * x, axis=-1, keepdims=True) + EPS)
    return x * r * g


def _rms_bwd(dy, x, g):
    r = lax.rsqrt(jnp.mean(x * x, axis=-1, keepdims=True) + EPS)
    xh = x * r
    gy = dy * g
    dx = r * (gy - xh * jnp.mean(gy * xh, axis=-1, keepdims=True))
    return dx, jnp.sum(dy * xh, axis=0, keepdims=True)


def _rope(q, c, sa, sb):
    n = q.shape[-1]
    return q * c + pltpu.roll(q, n - HALF, 1) * sa + pltpu.roll(q, HALF, 1) * sb


def _rope_bwd(d, c, sa, sb):
    n = d.shape[-1]
    return d * c - pltpu.roll(d, n - HALF, 1) * sa - pltpu.roll(d, HALF, 1) * sb


def _shifted_copies(buf, shifted, tb):
    n = shifted.shape[1]
    for b in range(1, SUBLANES):
        shifted[b - 1] = buf[pl.ds(b, n), :]


def _rows_at(buf, shifted, start, tb):
    a, b = divmod(start, SUBLANES)
    src = buf if b == 0 else shifted.at[b - 1]
    return src[pl.ds(SUBLANES * a, tb), :]


def _params(n_axes=1):
    return pltpu.CompilerParams(dimension_semantics=("arbitrary",) * n_axes, vmem_limit_bytes=VMEM_LIMIT)


def _row_call(body, name, tb, row_ins, full_ins, row_outs, acc_outs, lane_outs=(), exchange=None):
    t = row_ins[0].shape[0]
    in_specs = [pl.BlockSpec((tb, a.shape[1]), lambda i: (i, 0)) for a in row_ins]
    in_specs += [pl.BlockSpec(a.shape, lambda i, nd=a.ndim: (0,) * nd) for a in full_ins]
    out_specs = [pl.BlockSpec((tb, c), lambda i: (i, 0)) for c, _ in row_outs]
    out_specs += [pl.BlockSpec(s, lambda i, nd=len(s): (0,) * nd) for s, _ in acc_outs]
    out_specs += [pl.BlockSpec((n, None, 1, tb), lambda i: (0, i, 0, 0)) for n in lane_outs]
    out_shape = [jax.ShapeDtypeStruct((t, c), dt) for c, dt in row_outs]
    out_shape += [jax.ShapeDtypeStruct(s, dt) for s, dt in acc_outs]
    out_shape += [jax.ShapeDtypeStruct((n, t // tb, 1, tb), F32) for n in lane_outs]
    if exchange is None:
        return pl.pallas_call(
            functools.partial(body), name=name, grid=(t // tb,), in_specs=in_specs, out_specs=out_specs,
            out_shape=out_shape, compiler_params=_params(1),
        )(*row_ins, *full_ins)
    steps = _phase_steps(len(exchange.phases), t // tb)
    fn = _hosted(body, len(in_specs), len(out_specs), exchange, lambda k: pl.program_id(0) == steps[k])
    res = pl.pallas_call(
        fn, name=name, grid=(t // tb,), in_specs=in_specs + [_ANY] * len(exchange.ins),
        out_specs=out_specs + [_ANY] * len(exchange.out_shapes), out_shape=out_shape + exchange.out_shapes,
        scratch_shapes=exchange.scratch, compiler_params=_params(1),
    )(*row_ins, *full_ins, *exchange.ins)
    return res[:len(out_specs)], res[len(out_specs):]


def _acc(ref, val):
    @pl.when(pl.program_id(0) == 0)
    def _():
        ref[...] = jnp.zeros_like(ref)
    ref[...] += val


def _fwd_in_proj(x, g_pre, w_in_p, exchange, tb=256):
    def body(x_ref, g_ref, w_ref, h_ref, zs_ref, a_ref, sb_ref, u_ref, sa_ref, sc_ref):
        hb = _rms_fwd(x_ref[...], g_ref[...]).astype(BF16)
        h_ref[...] = hb
        zs_ref[...] = _dot(hb, w_ref[:, 0:ZS])
        a = _dot(hb, w_ref[:, OFF_A:OFF_B])
        sb = _sigmoid(_dot(hb, w_ref[:, OFF_B:OFF_GA]))
        a_ref[...] = a
        sb_ref[...] = sb
        u_ref[...] = a * sb
        sa_ref[...] = _sigmoid(_dot(hb, w_ref[:, OFF_GA:OFF_GC]))
        sc_ref[...] = _sigmoid(_dot(hb, w_ref[:, OFF_GC:D_IN_PAD]))

    d = D_MODEL
    return _row_call(body, "fwd_in_proj", tb, [x], [g_pre, w_in_p],
                     [(d, BF16), (ZS, F32), (d, F32), (d, F32), (d, F32), (d, F32), (d, F32)], [], exchange=exchange)


def _fwd_qkv(zs, tc, tsa, tsb, q_norm, kv_norm, w_uq_p, w_uk, w_uv, tb=256):
    def body(zs_ref, c_ref, sa_ref, sb_ref, qg_ref, kg_ref, wq_ref, wk_ref, wv_ref,
             cqn_ref, ckvn_ref, q_ref, k_ref, v_ref):
        zs_ = zs_ref[...]
        c, sa, sb = c_ref[...], sa_ref[...], sb_ref[...]
        cqn = _rms_fwd(zs_[:, 0:Q_RANK], qg_ref[...]).astype(BF16)
        cqn_ref[...] = cqn
        q = jnp.concatenate([_dot(cqn, wq_ref[h]) for h in range(N_HEADS)], axis=1)
        q = _rope(q, jnp.tile(c, (1, N_HEADS)), jnp.tile(sa, (1, N_HEADS)), jnp.tile(sb, (1, N_HEADS)))
        q_ref[...] = (q * (QK_SCALE * LOG2E)).astype(BF16)
        kr = zs_[:, Q_RANK + KV_RANK:ZS]
        kr = _rope(kr, c[:, NOPE:], sa[:, NOPE:], sb[:, NOPE:]).astype(BF16)
        ckvn = _rms_fwd(zs_[:, Q_RANK:Q_RANK + KV_RANK], kg_ref[...]).astype(BF16)
        ckvn_ref[...] = ckvn
        kn = _dot(ckvn, wk_ref[...]).astype(BF16)
        v_ref[...] = _dot(ckvn, wv_ref[...]).astype(BF16)
        for h in range(N_HEADS):
            k_ref[:, h * HEAD_PAD:h * HEAD_PAD + NOPE] = kn[:, h * NOPE:(h + 1) * NOPE]
            k_ref[:, h * HEAD_PAD + NOPE:(h + 1) * HEAD_PAD] = kr

    hp = N_HEADS * HEAD_PAD
    return _row_call(body, "fwd_qkv", tb, [zs, tc, tsa, tsb], [q_norm, kv_norm, w_uq_p, w_uk, w_uv],
                     [(Q_RANK, BF16), (KV_RANK, BF16), (hp, BF16), (hp, BF16), (D_MODEL, BF16)], [])


def _attn_fwd(q, k, v, exchange, tq=512):
    t = q.shape[0]
    nq = t // tq

    def body(q_ref, k_ref, v_ref, o_ref, lse_ref, m_sc, l_sc, acc_sc):
        i = pl.program_id(1)
        m_sc[...] = jnp.full_like(m_sc, -1e30)
        l_sc[...] = jnp.zeros_like(l_sc)
        acc_sc[...] = jnp.zeros_like(acc_sc)
        qb = q_ref[...]

        def step(j, masked):
            rows = pl.ds(pl.multiple_of(j * tq, tq), tq)
            st = _dot_nt(k_ref[rows, :], qb)
            if masked:
                key = lax.broadcasted_iota(jnp.int32, (tq, tq), 0)
                qry = lax.broadcasted_iota(jnp.int32, (tq, tq), 1)
                st = jnp.where(key <= qry, st, -1e30)
            m_prev = m_sc[...]
            m_new = jnp.maximum(m_prev, jnp.max(st, axis=0, keepdims=True))
            alpha = jnp.exp2(m_prev - m_new)
            pt = jnp.exp2(st - m_new)
            l_sc[...] = alpha * l_sc[...] + jnp.sum(pt, axis=0, keepdims=True)
            acc_sc[...] = alpha * acc_sc[...] + _dot_tn(v_ref[rows, :], pt.astype(BF16))
            m_sc[...] = m_new

        def unmasked(j, carry):
            step(j, False)
            return carry

        lax.fori_loop(0, i, unmasked, 0)
        step(i, True)
        l = l_sc[...]
        o_ref[...] = (acc_sc[...] / l).T.astype(BF16)
        lse_ref[...] = m_sc[...] + jnp.log2(l)

    steps = _phase_steps(len(exchange.phases), N_HEADS * nq)
    fn = _hosted(body, 3, 2, exchange, lambda p: pl.program_id(0) * nq + pl.program_id(1) == steps[p])
    res = pl.pallas_call(
        fn, name="attn_fwd", grid=(N_HEADS, nq),
        in_specs=[pl.BlockSpec((tq, HEAD_PAD), lambda h, i: (i, h)),
                  pl.BlockSpec((t, HEAD_PAD), lambda h, i: (0, h)),
                  pl.BlockSpec((t, NOPE), lambda h, i: (0, h))] + [_ANY] * len(exchange.ins),
        out_specs=[pl.BlockSpec((tq, NOPE), lambda h, i: (i, h)),
                   pl.BlockSpec((None, None, 1, tq), lambda h, i: (h, i, 0, 0))] + [_ANY] * len(exchange.out_shapes),
        out_shape=[jax.ShapeDtypeStruct((t, D_MODEL), BF16),
                   jax.ShapeDtypeStruct((N_HEADS, nq, 1, tq), F32)] + exchange.out_shapes,
        scratch_shapes=[pltpu.VMEM((1, tq), F32), pltpu.VMEM((1, tq), F32),
                        pltpu.VMEM((NOPE, tq), F32)] + exchange.scratch,
        compiler_params=_params(2),
    )(q, k, v, *exchange.ins)
    return res[:2], res[2:]


def _conv_fwd(u, conv_w, conv_b, ln_g, ln_b, tb=256):
    t, c = u.shape
    halo = 32

    def body(u_ref, up_ref, w_ref, b_ref, g_ref, be_ref, co_ref, act_ref, buf, shifted):
        i = pl.program_id(0)
        buf[0:halo, :] = jnp.where(i == 0, 0.0, up_ref[...])
        buf[halo:halo + tb, :] = u_ref[...]
        _shifted_copies(buf, shifted, tb)
        acc = jnp.zeros((tb, c), F32)
        for k in range(CONV_W):
            acc = acc + w_ref[k:k + 1, :] * _rows_at(buf, shifted, halo - (CONV_W - 1) + k, tb)
        co = acc + b_ref[...]
        co_ref[...] = co
        mu = jnp.mean(co, axis=-1, keepdims=True)
        xc = co - mu
        r = lax.rsqrt(jnp.mean(xc * xc, axis=-1, keepdims=True) + EPS)
        y = xc * r * g_ref[...] + be_ref[...]
        act_ref[...] = (y * _sigmoid(y)).astype(BF16)

    ratio = tb // halo
    return pl.pallas_call(
        functools.partial(body), name="conv_fwd", grid=(t // tb,),
        in_specs=[pl.BlockSpec((tb, c), lambda i: (i, 0)),
                  pl.BlockSpec((halo, c), lambda i: (jnp.maximum(i * ratio - 1, 0), 0)),
                  pl.BlockSpec(conv_w.shape, lambda i: (0, 0)),
                  pl.BlockSpec((1, c), lambda i: (0, 0)), pl.BlockSpec((1, c), lambda i: (0, 0)),
                  pl.BlockSpec((1, c), lambda i: (0, 0))],
        out_specs=[pl.BlockSpec((tb, c), lambda i: (i, 0)), pl.BlockSpec((tb, c), lambda i: (i, 0))],
        out_shape=[jax.ShapeDtypeStruct((t, c), F32), jax.ShapeDtypeStruct((t, c), BF16)],
        scratch_shapes=[pltpu.VMEM((tb + halo, c), F32), pltpu.VMEM((SUBLANES - 1, tb + halo - SUBLANES, c), F32)],
        compiler_params=_params(1),
    )(u, u, conv_w, conv_b, ln_g, ln_b)


def _fwd_merge(attn, uact, sa, sc, x, w_o, w_pw2, b_pw2, w_out, g_post, tb=256):
    def body(at_ref, ua_ref, sa_ref, sc_ref, x_ref, wo_ref, wp_ref, bp_ref, wout_ref, g_ref,
             ya_ref, yc_ref, mb_ref, m_ref, x1_ref):
        ya = _dot(at_ref[...], wo_ref[...])
        yc = _dot(ua_ref[...], wp_ref[...]) + bp_ref[...]
        ya_ref[...] = ya
        yc_ref[...] = yc
        mb = (sa_ref[...] * ya + sc_ref[...] * yc).astype(BF16)
        mb_ref[...] = mb
        m = _dot(mb, wout_ref[...])
        m_ref[...] = m
        x1_ref[...] = x_ref[...] + _rms_fwd(m, g_ref[...])

    d = D_MODEL
    return _row_call(body, "fwd_merge", tb, [attn, uact, sa, sc, x], [w_o, w_pw2, b_pw2, w_out, g_post],
                     [(d, F32), (d, F32), (d, BF16), (d, F32), (d, F32)], [])


def _fwd_ff1(x1, g, w_ff1, tb=256):
    def body(x1_ref, g_ref, w_ref, h2_ref, r1_ref, act_ref):
        h2 = _rms_fwd(x1_ref[...], g_ref[...]).astype(BF16)
        h2_ref[...] = h2
        for j in range(N_DEV):
            cols = slice(j * FF_SHARD, (j + 1) * FF_SHARD)
            r1 = jnp.maximum(_dot(h2, w_ref[j]), 0.0)
            r1_ref[:, cols] = r1.astype(BF16)
            act_ref[:, cols] = (r1 * r1).astype(BF16)

    return _row_call(body, "fwd_ff1", tb, [x1], [g, w_ff1], [(D_MODEL, BF16), (D_FF, BF16), (D_FF, BF16)], [])


def _fwd_ff2_loss(act, x1, target, w_ff2, g, tb=256):
    def body(act_ref, x1_ref, tg_ref, w_ref, g_ref, f_ref, dy_ref, loss_ref):
        f = _dot(act_ref[...], w_ref[...])
        f_ref[...] = f
        e = x1_ref[...] + _rms_fwd(f, g_ref[...]) - tg_ref[...]
        dy_ref[...] = e * (1.0 / D_MODEL)
        _acc(loss_ref, jnp.sum(e * e))

    return _row_call(body, "fwd_ff2_loss", tb, [act, x1, target], [w_ff2, g],
                     [(D_MODEL, F32), (D_MODEL, F32)], [((8, 128), F32)])


def _bwd_ff2(dy, f, r1, w_ff2, g, tb=256):
    def body(dy_ref, f_ref, r1_ref, w_ref, g_ref, df_ref, df1_ref, dg_ref):
        df, dg = _rms_bwd(dy_ref[...], f_ref[...], g_ref[...])
        _acc(dg_ref, dg)
        dfb = df.astype(BF16)
        df_ref[...] = dfb
        dact = _dot_nt(dfb, w_ref[...])
        df1_ref[...] = (dact * (2.0 * r1_ref[...].astype(F32))).astype(BF16)

    return _row_call(body, "bwd_ff2", tb, [dy, f, r1], [w_ff2, g], [(D_MODEL, BF16), (D_FF, BF16)],
                     [((1, D_MODEL), F32)])


def _bwd_ff1(df1, x1, dy, w_ff1, g, tb=256):
    def body(df1_ref, x1_ref, dy_ref, w_ref, g_ref, dx1_ref, dg_ref):
        dh2 = _dot_nt(df1_ref[:, 0:FF_SHARD], w_ref[0])
        for j in range(1, N_DEV):
            dh2 = dh2 + _dot_nt(df1_ref[:, j * FF_SHARD:(j + 1) * FF_SHARD], w_ref[j])
        dxn, dg = _rms_bwd(dh2, x1_ref[...], g_ref[...])
        _acc(dg_ref, dg)
        dx1_ref[...] = dy_ref[...] + dxn

    return _row_call(body, "bwd_ff1", tb, [df1, x1, dy], [w_ff1, g], [(D_MODEL, F32)], [((1, D_MODEL), F32)])


def _bwd_merge(dx1, m, sa, sc, ya, yc, attn, w_out, w_o, w_pw2, g_post, exchange, tb=256):
    def body(dx1_ref, m_ref, sa_ref, sc_ref, ya_ref, yc_ref, at_ref, wout_ref, wo_ref, wp_ref, g_ref,
             dm_ref, dya_ref, dyc_ref, dga_ref, dgc_ref, dat_ref, dua_ref, dg_ref, dbp_ref, delta_ref):
        dm, dg = _rms_bwd(dx1_ref[...], m_ref[...], g_ref[...])
        _acc(dg_ref, dg)
        dmb = dm.astype(BF16)
        dm_ref[...] = dmb
        dmerged = _dot_nt(dmb, wout_ref[...])
        sa, sc = sa_ref[...], sc_ref[...]
        dya = dmerged * sa
        dyc = dmerged * sc
        _acc(dbp_ref, jnp.sum(dyc, axis=0, keepdims=True))
        dyab = dya.astype(BF16)
        dycb = dyc.astype(BF16)
        dya_ref[...] = dyab
        dyc_ref[...] = dycb
        dga_ref[...] = (dmerged * ya_ref[...] * sa * (1.0 - sa)).astype(BF16)
        dgc_ref[...] = (dmerged * yc_ref[...] * sc * (1.0 - sc)).astype(BF16)
        dat = _dot_nt(dyab, wo_ref[...])
        dat_ref[...] = dat.astype(BF16)
        prod = dat * at_ref[...].astype(F32)
        lane = lax.broadcasted_iota(jnp.int32, (tb, NOPE), 1)
        dl = jnp.zeros((tb, NOPE), F32)
        for h in range(N_HEADS):
            dl = dl + jnp.where(lane == h, jnp.sum(prod[:, h * NOPE:(h + 1) * NOPE], axis=1, keepdims=True), 0.0)
        dlt = dl.T
        for h in range(N_HEADS):
            delta_ref[h] = dlt[h:h + 1, :]
        dua_ref[...] = _dot_nt(dycb, wp_ref[...])

    d = D_MODEL
    return _row_call(body, "bwd_merge", tb, [dx1, m, sa, sc, ya, yc, attn], [w_out, w_o, w_pw2, g_post],
                     [(d, BF16), (d, BF16), (d, BF16), (d, BF16), (d, BF16), (d, BF16), (d, F32)],
                     [((1, d), F32), ((1, d), F32)], lane_outs=(N_HEADS,), exchange=exchange)


def _bwd_ln(dua, co, ln_g, ln_b, tb=256):
    def body(dua_ref, co_ref, g_ref, be_ref, dco_ref, dg_ref, db_ref, dcb_ref):
        co = co_ref[...]
        g = g_ref[...]
        mu = jnp.mean(co, axis=-1, keepdims=True)
        xc = co - mu
        r = lax.rsqrt(jnp.mean(xc * xc, axis=-1, keepdims=True) + EPS)
        xh = xc * r
        y = xh * g + be_ref[...]
        s = _sigmoid(y)
        dy = dua_ref[...] * (s + y * s * (1.0 - s))
        _acc(db_ref, jnp.sum(dy, axis=0, keepdims=True))
        _acc(dg_ref, jnp.sum(dy * xh, axis=0, keepdims=True))
        gy = dy * g
        dco = r * (gy - jnp.mean(gy, axis=-1, keepdims=True) - xh * jnp.mean(gy * xh, axis=-1, keepdims=True))
        dco_ref[...] = dco
        _acc(dcb_ref, jnp.sum(dco, axis=0, keepdims=True))

    d = D_MODEL
    return _row_call(body, "bwd_ln", tb, [dua, co], [ln_g, ln_b], [(d, F32)],
                     [((1, d), F32), ((1, d), F32), ((1, d), F32)])


def _conv_bwd(dco, u, a, sb, conv_w, exchange, tb=256):
    t, c = u.shape
    halo = 32
    ratio = tb // halo
    nblk = t // tb

    def body(d_ref, dn_ref, u_ref, up_ref, a_ref, sb_ref, w_ref, da_ref, db_ref, dw_ref, bufd, bufu, shd, shu):
        i = pl.program_id(0)

        @pl.when(i == 0)
        def _():
            dw_ref[...] = jnp.zeros_like(dw_ref)

        dco = d_ref[...]
        bufd[0:tb, :] = dco
        bufd[tb:tb + halo, :] = jnp.where(i == nblk - 1, 0.0, dn_ref[...])
        bufu[0:halo, :] = jnp.where(i == 0, 0.0, up_ref[...])
        bufu[halo:halo + tb, :] = u_ref[...]
        _shifted_copies(bufd, shd, tb)
        _shifted_copies(bufu, shu, tb)
        du = jnp.zeros((tb, c), F32)
        for k in range(CONV_W):
            du = du + w_ref[k:k + 1, :] * _rows_at(bufd, shd, CONV_W - 1 - k, tb)
            dw_ref[k:k + 1, :] += jnp.sum(dco * _rows_at(bufu, shu, halo - (CONV_W - 1) + k, tb), axis=0,
                                          keepdims=True)
        sb_ = sb_ref[...]
        da_ref[...] = (du * sb_).astype(BF16)
        db_ref[...] = (du * a_ref[...] * sb_ * (1.0 - sb_)).astype(BF16)

    steps = _phase_steps(len(exchange.phases), nblk)
    fn = _hosted(body, 7, 3, exchange, lambda p: pl.program_id(0) == steps[p])
    res = pl.pallas_call(
        fn, name="conv_bwd", grid=(nblk,),
        in_specs=[pl.BlockSpec((tb, c), lambda i: (i, 0)),
                  pl.BlockSpec((halo, c), lambda i: (jnp.minimum((i + 1) * ratio, t // halo - 1), 0)),
                  pl.BlockSpec((tb, c), lambda i: (i, 0)),
                  pl.BlockSpec((halo, c), lambda i: (jnp.maximum(i * ratio - 1, 0), 0)),
                  pl.BlockSpec((tb, c), lambda i: (i, 0)), pl.BlockSpec((tb, c), lambda i: (i, 0)),
                  pl.BlockSpec(conv_w.shape, lambda i: (0, 0))] + [_ANY] * len(exchange.ins),
        out_specs=[pl.BlockSpec((tb, c), lambda i: (i, 0)), pl.BlockSpec((tb, c), lambda i: (i, 0)),
                   pl.BlockSpec((32, c), lambda i: (0, 0))] + [_ANY] * len(exchange.out_shapes),
        out_shape=[jax.ShapeDtypeStruct((t, c), BF16), jax.ShapeDtypeStruct((t, c), BF16),
                   jax.ShapeDtypeStruct((32, c), F32)] + exchange.out_shapes,
        scratch_shapes=[pltpu.VMEM((tb + halo, c), F32), pltpu.VMEM((tb + halo, c), F32),
                        pltpu.VMEM((SUBLANES - 1, tb + halo - SUBLANES, c), F32),
                        pltpu.VMEM((SUBLANES - 1, tb + halo - SUBLANES, c), F32)] + exchange.scratch,
        compiler_params=_params(1),
    )(dco, dco, u, u, a, sb, conv_w, *exchange.ins)
    return res[:3], res[3:]


def _attn_bwd(q, k, v, do, lse2, delta, exchange, tq=512):
    t = q.shape[0]
    nq = t // tq
    td = delta.shape[-1]
    per = tq // td

    def body(q_ref, k_ref, v_ref, do_ref, lse_ref, dl_ref, dq_ref, dk_ref, dv_ref, dk_sc, dv_sc):
        j = pl.program_id(1)

        @pl.when(j == 0)
        def _():
            dq_ref[...] = jnp.zeros_like(dq_ref)

        dk_sc[...] = jnp.zeros_like(dk_sc)
        dv_sc[...] = jnp.zeros_like(dv_sc)
        kb, vb = k_ref[...], v_ref[...]

        def step(i, masked):
            rows = pl.ds(pl.multiple_of(i * tq, tq), tq)
            qb, dob = q_ref[rows, :], do_ref[rows, :]
            pt = jnp.exp2(_dot_nt(kb, qb) - lse_ref[i])
            if masked:
                key = lax.broadcasted_iota(jnp.int32, (tq, tq), 0)
                qry = lax.broadcasted_iota(jnp.int32, (tq, tq), 1)
                pt = jnp.where(key <= qry, pt, 0.0)
            dv_sc[...] += _dot(pt.astype(BF16), dob)
            dl = jnp.concatenate([dl_ref[per * i + r] for r in range(per)], axis=-1)
            dst = (pt * (_dot_nt(vb, dob) - dl)).astype(BF16)
            dk_sc[...] += _dot(dst, qb)
            dq_ref[rows, :] += _dot_tn(dst, kb)

        def unmasked(i, carry):
            step(i, False)
            return carry

        step(j, True)
        lax.fori_loop(j + 1, nq, unmasked, 0)
        dk_ref[...] = dk_sc[...] * (1.0 / LOG2E)
        dv_ref[...] = dv_sc[...].astype(BF16)

    hp = N_HEADS * HEAD_PAD
    steps = _phase_steps(len(exchange.phases), N_HEADS * nq)
    fn = _hosted(body, 6, 3, exchange, lambda p: pl.program_id(0) * nq + pl.program_id(1) == steps[p])
    res = pl.pallas_call(
        fn, name="attn_bwd", grid=(N_HEADS, nq),
        in_specs=[pl.BlockSpec((t, HEAD_PAD), lambda h, j: (0, h)),
                  pl.BlockSpec((tq, HEAD_PAD), lambda h, j: (j, h)),
                  pl.BlockSpec((tq, NOPE), lambda h, j: (j, h)),
                  pl.BlockSpec((t, NOPE), lambda h, j: (0, h)),
                  pl.BlockSpec((None, nq, 1, tq), lambda h, j: (h, 0, 0, 0)),
                  pl.BlockSpec((None, t // td, 1, td), lambda h, j: (h, 0, 0, 0))] + [_ANY] * len(exchange.ins),
        out_specs=[pl.BlockSpec((t, HEAD_PAD), lambda h, j: (0, h)),
                   pl.BlockSpec((tq, HEAD_PAD), lambda h, j: (j, h)),
                   pl.BlockSpec((tq, NOPE), lambda h, j: (j, h))] + [_ANY] * len(exchange.out_shapes),
        out_shape=[jax.ShapeDtypeStruct((t, hp), F32), jax.ShapeDtypeStruct((t, hp), F32),
                   jax.ShapeDtypeStruct((t, D_MODEL), BF16)] + exchange.out_shapes,
        scratch_shapes=[pltpu.VMEM((tq, HEAD_PAD), F32), pltpu.VMEM((tq, NOPE), F32)] + exchange.scratch,
        compiler_params=_params(2),
    )(q, k, v, do, lse2, delta, *exchange.ins)
    return res[:3], res[3:]


def _bwd_qkv(dq, dk, dv, zs, tc, tsa, tsb, q_norm, kv_norm, w_uq_p, w_uk, w_uv, tb=256):
    def body(dq_ref, dk_ref, dv_ref, zs_ref, c_ref, sa_ref, sb_ref, qg_ref, kg_ref, wq_ref, wk_ref, wv_ref,
             dqp_ref, dkn_ref, dzs_ref, dqg_ref, dkg_ref):
        c, sa, sb = c_ref[...], sa_ref[...], sb_ref[...]
        zs_ = zs_ref[...]
        dqp = (_rope_bwd(dq_ref[...], jnp.tile(c, (1, N_HEADS)), jnp.tile(sa, (1, N_HEADS)),
                         jnp.tile(sb, (1, N_HEADS))) * QK_SCALE).astype(BF16)
        dqp_ref[...] = dqp
        dcqn = _dot_nt(dqp[:, 0:HEAD_PAD], wq_ref[0])
        for h in range(1, N_HEADS):
            dcqn = dcqn + _dot_nt(dqp[:, h * HEAD_PAD:(h + 1) * HEAD_PAD], wq_ref[h])
        dcq, dqg = _rms_bwd(dcqn, zs_[:, 0:Q_RANK], qg_ref[...])
        _acc(dqg_ref, dqg)
        dzs_ref[:, 0:Q_RANK] = dcq.astype(BF16)
        dkr = jnp.zeros((tb, NOPE), F32)
        for h in range(N_HEADS):
            dkn_ref[:, h * NOPE:(h + 1) * NOPE] = dk_ref[:, h * HEAD_PAD:h * HEAD_PAD + NOPE].astype(BF16)
            dkr = dkr + dk_ref[:, h * HEAD_PAD + NOPE:(h + 1) * HEAD_PAD]
        dzs_ref[:, Q_RANK + KV_RANK:ZS] = _rope_bwd(dkr, c[:, NOPE:], sa[:, NOPE:], sb[:, NOPE:]).astype(BF16)
        dckvn = _dot_nt(dkn_ref[...], wk_ref[...]) + _dot_nt(dv_ref[...], wv_ref[...])
        dckv, dkg = _rms_bwd(dckvn, zs_[:, Q_RANK:Q_RANK + KV_RANK], kg_ref[...])
        _acc(dkg_ref, dkg)
        dzs_ref[:, Q_RANK:Q_RANK + KV_RANK] = dckv.astype(BF16)

    hp = N_HEADS * HEAD_PAD
    return _row_call(body, "bwd_qkv", tb, [dq, dk, dv, zs, tc, tsa, tsb], [q_norm, kv_norm, w_uq_p, w_uk, w_uv],
                     [(hp, BF16), (D_MODEL, BF16), (ZS, BF16)], [((1, Q_RANK), F32), ((1, KV_RANK), F32)])


def _bwd_in_proj(dzs, da, db, dga, dgc, x, dx1, w_in_p, g_pre, exchange, tb=256):
    def body(dzs_ref, da_ref, db_ref, dga_ref, dgc_ref, x_ref, dx1_ref, w_ref, g_ref, gx_ref, dg_ref):
        dh = _dot_nt(dzs_ref[...], w_ref[:, 0:ZS])
        dh = dh + _dot_nt(da_ref[...], w_ref[:, OFF_A:OFF_B])
        dh = dh + _dot_nt(db_ref[...], w_ref[:, OFF_B:OFF_GA])
        dh = dh + _dot_nt(dga_ref[...], w_ref[:, OFF_GA:OFF_GC])
        dh = dh + _dot_nt(dgc_ref[...], w_ref[:, OFF_GC:D_IN_PAD])
        dxn, dg = _rms_bwd(dh, x_ref[...], g_ref[...])
        _acc(dg_ref, dg)
        gx_ref[...] = dx1_ref[...] + dxn

    return _row_call(body, "bwd_in_proj", tb, [dzs, da, db, dga, dgc, x, dx1], [w_in_p, g_pre],
                     [(D_MODEL, F32)], [((1, D_MODEL), F32)], exchange=exchange)


def _mm_tn(a, b, name, shard_cols=None, tt=2048):
    t, m = a.shape
    n = b.shape[1]
    tm = min(m, 1024)
    tn = min(n, 1024)
    tt = min(t, tt)
    nt = t // tt
    per = tn // shard_cols if shard_cols else 1

    def body(a_ref, b_ref, o_ref, acc):
        k = pl.program_id(2)

        @pl.when(k == 0)
        def _():
            acc[...] = jnp.zeros_like(acc)

        acc[...] += _dot_tn(a_ref[...], b_ref[...])

        @pl.when(k == nt - 1)
        def _():
            if shard_cols:
                for s in range(per):
                    o_ref[s] = acc[:, s * shard_cols:(s + 1) * shard_cols].astype(BF16)
            else:
                o_ref[...] = acc[...].astype(BF16)

    if shard_cols:
        out_spec = pl.BlockSpec((per, tm, shard_cols), lambda i, j, k: (j, i, 0))
        out_shape = jax.ShapeDtypeStruct((n // shard_cols, m, shard_cols), BF16)
    else:
        out_spec = pl.BlockSpec((tm, tn), lambda i, j, k: (i, j))
        out_shape = jax.ShapeDtypeStruct((m, n), BF16)
    return pl.pallas_call(
        functools.partial(body), name=name, grid=(m // tm, n // tn, nt),
        in_specs=[pl.BlockSpec((tt, tm), lambda i, j, k: (k, i)), pl.BlockSpec((tt, tn), lambda i, j, k: (k, j))],
        out_specs=out_spec, out_shape=out_shape, scratch_shapes=[pltpu.VMEM((tm, tn), F32)],
        compiler_params=_params(3),
    )(a, b)


_ANY = pl.BlockSpec(memory_space=pl.ANY)
_MESH = pl.DeviceIdType.MESH

_Exchange = collections.namedtuple("_Exchange", "ins out_shapes scratch phases")


def _ag_exchange(shards):
    n = len(shards)

    def parts(ins, outs, sems):
        send_sems, recv_sems, _ = sems
        x, y, c = lax.axis_index("x"), lax.axis_index("y"), lax.axis_index("c")
        chips = [(1 - x, y), (x, 1 - y), (1 - x, 1 - y)]

        def copy(w, k, block, to, src=None):
            dst = outs[w].at[4 * block[0] + 2 * block[1] + block[2]]
            return pltpu.make_async_remote_copy(
                src_ref=dst if src is None else src, dst_ref=dst, send_sem=send_sems.at[7 * w + k],
                recv_sem=recv_sems.at[7 * w + k], device_id=to, device_id_type=_MESH)

        def first(w):
            return [copy(w, 0, (x, y, c), (x, y, 1 - c), src=ins[w])] + [
                copy(w, 1 + j, (x, y, c), (*chip, c), src=ins[w]) for j, chip in enumerate(chips)]

        def mine(w):
            return pltpu.make_async_copy(ins[w], outs[w].at[4 * x + 2 * y + c], sems[2].at[w])

        return (x, y, c), chips, copy, first, mine

    def start(ins, outs, sems):
        _, _, _, first, mine = parts(ins, outs, sems)
        for w in range(n):
            mine(w).start()
            for cp in first(w):
                cp.start()

    def forward(ins, outs, sems):
        (x, y, c), chips, copy, _, _ = parts(ins, outs, sems)
        for j, chip in enumerate(chips):
            for w in range(n):
                copy(w, 1 + j, (*chip, c), (x, y, c)).wait_recv()
                copy(w, 4 + j, (*chip, c), (x, y, 1 - c)).start()

    def finish(ins, outs, sems):
        (x, y, c), chips, copy, first, mine = parts(ins, outs, sems)
        for w in range(n):
            copy(w, 0, (x, y, 1 - c), (x, y, c)).wait_recv()
        for j, chip in enumerate(chips):
            for w in range(n):
                copy(w, 4 + j, (*chip, 1 - c), (x, y, c)).wait_recv()
        for w in range(n):
            for cp in first(w) + [copy(w, 4 + j, (*chip, c), (x, y, 1 - c)) for j, chip in enumerate(chips)]:
                cp.wait_send()
            mine(w).wait()

    return _Exchange(
        ins=list(shards), out_shapes=[jax.ShapeDtypeStruct((N_DEV,) + s.shape, s.dtype) for s in shards],
        scratch=[pltpu.SemaphoreType.DMA((7 * n,)), pltpu.SemaphoreType.DMA((7 * n,)), pltpu.SemaphoreType.DMA((n,))],
        phases=[start, forward, finish])


def _pair_exchange(gs):
    n = len(gs)

    def copies(ins, outs, sems):
        x, y, c = lax.axis_index("x"), lax.axis_index("y"), lax.axis_index("c")
        return [pltpu.make_async_remote_copy(
            src_ref=ins[w].at[:, 1 - c], dst_ref=outs[w], send_sem=sems[0].at[w], recv_sem=sems[1].at[w],
            device_id=(x, y, 1 - c), device_id_type=_MESH) for w in range(n)]

    return _start_then_wait(gs, [jax.ShapeDtypeStruct((4,) + g.shape[2:], g.dtype) for g in gs], n, copies)


def _start_then_wait(ins, out_shapes, n_copies, copies):
    def start(ins_, outs, sems):
        for cp in copies(ins_, outs, sems):
            cp.start()

    def finish(ins_, outs, sems):
        for cp in copies(ins_, outs, sems):
            cp.wait()

    return _Exchange(ins=list(ins), out_shapes=out_shapes,
                     scratch=[pltpu.SemaphoreType.DMA((n_copies,)), pltpu.SemaphoreType.DMA((n_copies,))],
                     phases=[start, finish])


def _run_exchange(ex, name):
    ni, no = len(ex.ins), len(ex.out_shapes)

    def body(*refs):
        for phase in ex.phases:
            phase(refs[:ni], refs[ni:ni + no], refs[ni + no:])

    return pl.pallas_call(functools.partial(body), name=name, out_shape=ex.out_shapes, in_specs=[_ANY] * ni,
                          out_specs=[_ANY] * no, scratch_shapes=ex.scratch)(*ex.ins)


def _hosted(body, n_in, n_out, ex, when):
    ni, no, ns = len(ex.ins), len(ex.out_shapes), len(ex.scratch)

    def fn(*refs):
        ins, ex_ins = refs[:n_in], refs[n_in:n_in + ni]
        outs = refs[n_in + ni:n_in + ni + n_out]
        ex_outs = refs[n_in + ni + n_out:n_in + ni + n_out + no]
        scratch, sems = refs[n_in + ni + n_out + no:len(refs) - ns], refs[len(refs) - ns:]
        last = len(ex.phases) - 1
        for k in range(last):
            pl.when(when(k))(functools.partial(ex.phases[k], ex_ins, ex_outs, sems))
        body(*ins, *outs, *scratch)
        pl.when(when(last))(functools.partial(ex.phases[last], ex_ins, ex_outs, sems))

    return fn


def _phase_steps(n_phases, n_steps):
    return [0, n_steps - 1] if n_phases == 2 else [0, 2 * n_steps // 3, n_steps - 1]


def _row_block(shape, steps, lead, pick):
    blk = (None,) * lead + (shape[0] // steps,) + tuple(shape[1:])
    return pl.BlockSpec(blk, lambda *a: tuple(pick(*a)) + (a[-2],) + (0,) * (len(shape) - 1))


def _pair_sum(gs, ls, c_idx, name, steps):
    n = len(gs)

    def body(c_ref, *refs):
        for w in range(n):
            refs[2 * n + w][...] = (refs[w][...].astype(F32) + refs[n + w][...].astype(F32)).astype(BF16)

    shapes = [g.shape[2:] for g in gs]
    return pl.pallas_call(
        functools.partial(body), name=name,
        grid_spec=pltpu.PrefetchScalarGridSpec(
            num_scalar_prefetch=1, grid=(4, steps),
            in_specs=[_row_block(s, steps, 2, lambda k, i, c: (k, c[0])) for s in shapes]
            + [_row_block(s, steps, 1, lambda k, i, c: (k,)) for s in shapes],
            out_specs=[_row_block(s, steps, 1, lambda k, i, c: (k,)) for s in shapes]),
        out_shape=[jax.ShapeDtypeStruct((4,) + tuple(s), BF16) for s in shapes], compiler_params=_params(2),
    )(c_idx, *gs, *ls)


def _chip_exchange(ps):
    n = len(ps)

    def copies(ins, outs, sems):
        x, y, c = lax.axis_index("x"), lax.axis_index("y"), lax.axis_index("c")
        chips = [(1 - x, y), (x, 1 - y), (1 - x, 1 - y)]
        return [pltpu.make_async_remote_copy(
            src_ref=ins[w].at[2 * px + py], dst_ref=outs[w].at[s], send_sem=sems[0].at[3 * w + s],
            recv_sem=sems[1].at[3 * w + s], device_id=(px, py, c), device_id_type=_MESH)
            for w in range(n) for s, (px, py) in enumerate(chips)]

    return _start_then_wait(ps, [jax.ShapeDtypeStruct((3,) + p.shape[1:], p.dtype) for p in ps], 3 * n, copies)


def _adamw(w, g, m, v):
    m2 = ADAM_B1 * m + (1.0 - ADAM_B1) * g
    v2 = ADAM_B2 * v + (1.0 - ADAM_B2) * (g * g)
    m_hat = m2 / (1.0 - ADAM_B1 ** ADAM_STEP)
    v_hat = v2 / (1.0 - ADAM_B2 ** ADAM_STEP)
    delta = -ADAM_LR * (m_hat / (jnp.sqrt(v_hat) + ADAM_EPS) + ADAM_WD * w)
    return delta, m2, v2


def _update(gs, ls, qs, ws, ms, vs, idx, name, steps):
    n = len(gs)

    def body(idx_ref, *refs):
        g, l, q, w, m, v = (refs[k * n:(k + 1) * n] for k in range(6))
        outs = refs[6 * n:]
        for i in range(n):
            gr = g[i][...].astype(F32) + l[i][...].astype(F32)
            gr = gr + q[i][0].astype(F32)
            gr = gr + q[i][1].astype(F32)
            gr = gr + q[i][2].astype(F32)
            outs[4 * i][...] = gr
            outs[4 * i + 1][...], outs[4 * i + 2][...], outs[4 * i + 3][...] = _adamw(w[i][...], gr, m[i][...], v[i][...])

    shapes = [w.shape for w in ws]
    own = [_row_block(s, steps, 0, lambda i, c: ()) for s in shapes]
    res = pl.pallas_call(
        functools.partial(body), name=name,
        grid_spec=pltpu.PrefetchScalarGridSpec(
            num_scalar_prefetch=1, grid=(steps,),
            in_specs=[_row_block(s, steps, 2, lambda i, c: (c[0], c[1])) for s in shapes]
            + [_row_block(s, steps, 1, lambda i, c: (c[0],)) for s in shapes]
            + [pl.BlockSpec((3, s[0] // steps) + tuple(s[1:]), lambda i, c, nd=len(s): (0, i) + (0,) * (nd - 1))
               for s in shapes] + own * 3,
            out_specs=[b for b in own for _ in range(4)]),
        out_shape=[jax.ShapeDtypeStruct(s, F32) for s in shapes for _ in range(4)], compiler_params=_params(1),
    )(idx, *gs, *ls, *qs, *ws, *ms, *vs)
    return [res[4 * i:4 * i + 4] for i in range(n)]


def _update_small(sv_all, ws, ms, vs):
    n = len(ws)

    def body(all_ref, *refs):
        w, m, v = refs[:n], refs[n:2 * n], refs[2 * n:3 * n]
        loss_ref, outs = refs[3 * n], refs[3 * n + 1:]
        total = all_ref[0]
        for dev in range(1, N_DEV):
            total = total + all_ref[dev]
        loss_ref[...] = total[LOSS_ROW:LOSS_ROW + 1, :]
        for i in range(n):
            gr = total[i:i + 1, 0:w[i].shape[1]]
            outs[4 * i][...] = gr
            outs[4 * i + 1][...], outs[4 * i + 2][...], outs[4 * i + 3][...] = _adamw(w[i][...], gr, m[i][...], v[i][...])

    res = pl.pallas_call(
        functools.partial(body), name="update_small",
        out_shape=[jax.ShapeDtypeStruct((1, 1024), F32)]
        + [jax.ShapeDtypeStruct(a.shape, F32) for a in ws for _ in range(4)],
    )(sv_all, *ws, *ms, *vs)
    return res[0], [res[1 + 4 * i:5 + 4 * i] for i in range(n)]


def _to_exchange(shards):
    return [jnp.pad(shards[n][0], ((0, 0), (0, HEAD_PAD - NOPE - ROPE))) if n == "w_uq" else shards[n][0]
            for n in _BIG]


def _from_exchange(arrs, like):
    return {n: (a[:, :NOPE + ROPE] if n == "w_uq" else a).reshape(like[n].shape) for n, a in zip(_BIG, arrs)}


def _padded_w_in(g_in):
    w_in = g_in.transpose(1, 0, 2).reshape(D_MODEL, N_DEV * IN_SHARD)
    kr_end = Q_RANK + KV_RANK + ROPE
    return jnp.concatenate([w_in[:, :kr_end], jnp.zeros((D_MODEL, 128 - ROPE), BF16), w_in[:, kr_end:]], axis=1)


def _full_conv_w(g_conv):
    taps = (g_conv[:, 0].astype(F32) + g_conv[:, 1].astype(F32)).reshape(N_DEV, CONV_W, D_MODEL // N_DEV)
    return jnp.pad(taps.transpose(1, 0, 2).reshape(CONV_W, D_MODEL), ((0, 1), (0, 0)))


def _pad_rows(a, rows):
    return jnp.pad(a, ((0, rows - a.shape[0]), (0, 0)))


def _small_pack(vals):
    rows = [jnp.pad(v.reshape(1, -1), ((0, 0), (0, 1024 - v.size))) for v in vals]
    return _pad_rows(jnp.concatenate(rows, axis=0), SMALL_ROWS)


def _rope_tables(positions):
    inv_freq = ROPE_THETA ** (-jnp.arange(0, ROPE, 2, dtype=F32) / ROPE)
    ang = positions.reshape(-1).astype(F32)[:, None] * inv_freq
    cos, sin = jnp.cos(ang), jnp.sin(ang)
    t = cos.shape[0]
    z32, z64 = jnp.zeros((t, HALF), F32), jnp.zeros((t, HEAD_PAD - NOPE - ROPE), F32)
    z128 = jnp.zeros((t, NOPE), F32)
    tc = jnp.concatenate([jnp.ones((t, NOPE), F32), cos, cos, z64], axis=1)
    tsa = jnp.concatenate([z128, -sin, z32, z64], axis=1)
    tsb = jnp.concatenate([z128, z32, sin, z64], axis=1)
    return tc, tsa, tsb


def _blocks(dw):
    if dw.ndim == 2:
        dw = dw.reshape(N_DEV, dw.shape[0] // N_DEV, dw.shape[1])
    return dw.reshape((4, 2) + dw.shape[1:])


def _step(x, positions, target, small, send, c_idx):
    s_in, s_uq, s_uk, s_uv, s_o, s_conv, s_pw2, s_out, s_ff1, s_ff2 = send
    tc, tsa, tsb = _rope_tables(positions)

    w_in_p = _padded_w_in(_run_exchange(_ag_exchange([s_in]), "ag_w_in")[0])
    (h, zs, a, sb, u, sa, sc), (w_uq, w_uk, w_uv) = _fwd_in_proj(
        x, small["norm_mix_pre"], w_in_p, _ag_exchange([s_uq, s_uk, s_uv]))
    w_uk, w_uv = w_uk.reshape(KV_RANK, -1), w_uv.reshape(KV_RANK, -1)
    cqn, ckvn, q, k, v = _fwd_qkv(zs, tc, tsa, tsb, small["q_norm"], small["kv_norm"], w_uq, w_uk, w_uv)
    (attn, lse), (w_o, g_conv, w_pw2, w_out, w_ff1, w_ff2) = _attn_fwd(
        q, k, v, _ag_exchange([s_o, s_conv, s_pw2, s_out, s_ff1, s_ff2]))
    w_o, w_pw2, w_out = (w.reshape(D_MODEL, D_MODEL) for w in (w_o, w_pw2, w_out))
    w_ff2, conv_w = w_ff2.reshape(D_FF, D_MODEL), _full_conv_w(g_conv)
    co, uact = _conv_fwd(u, conv_w, small["conv_b"], small["conv_ln_g"], small["conv_ln_b"])
    ya, yc, mb, m, x1 = _fwd_merge(attn, uact, sa, sc, x, w_o, w_pw2, small["b_pw2"], w_out, small["norm_mix_post"])
    h2, r1, act = _fwd_ff1(x1, small["norm_mlp_pre"], w_ff1)
    f, dy, loss_blk = _fwd_ff2_loss(act, x1, target, w_ff2, small["norm_mlp_post"])

    df, df1, dg_mlp_post = _bwd_ff2(dy, f, r1, w_ff2, small["norm_mlp_post"])
    dx1, dg_mlp_pre = _bwd_ff1(df1, x1, dy, w_ff1, small["norm_mlp_pre"])
    g_ff = [_blocks(_mm_tn(h2, df1, "dw_ff1", shard_cols=FF_SHARD)), _blocks(_mm_tn(act, df, "dw_ff2"))]
    (dmb, dya, dyc, dga, dgc, dat, dua, dg_mix_post, db_pw2, delta), l_ff = _bwd_merge(
        dx1, m, sa, sc, ya, yc, attn, w_out, w_o, w_pw2, small["norm_mix_post"], _pair_exchange(g_ff))
    p_ff = _pair_sum(g_ff, l_ff, c_idx, "rs_pair_sum_ff", 2)
    g_mix = [_blocks(_mm_tn(attn, dya, "dw_o")), _blocks(_mm_tn(uact, dyc, "dw_pw2")),
             _blocks(_mm_tn(mb, dmb, "dw_out"))]
    dco, dln_g, dln_b, dconv_b = _bwd_ln(dua, co, small["conv_ln_g"], small["conv_ln_b"])
    (da, db, dconv), l_mix = _conv_bwd(dco, u, a, sb, conv_w, _pair_exchange(g_mix))
    p_mix = _pair_sum(g_mix, l_mix, c_idx, "rs_pair_sum_mix", 1)
    (dq, dk, dv), q_early = _attn_bwd(q, k, v, dat, lse, delta, _chip_exchange(p_ff + p_mix))
    dqp, dkn, dzs, dq_norm, dkv_norm = _bwd_qkv(dq, dk, dv, zs, tc, tsa, tsb, small["q_norm"], small["kv_norm"],
                                                w_uq, w_uk, w_uv)
    dw_in = jnp.concatenate([_mm_tn(h, dzs, "dw_in_zs")[:, :Q_RANK + KV_RANK + ROPE], _mm_tn(h, da, "dw_in_a"),
                             _mm_tn(h, db, "dw_in_b"), _mm_tn(h, dga, "dw_in_ga"), _mm_tn(h, dgc, "dw_in_gc")],
                            axis=1)
    g_late = [_blocks(dw_in.reshape(D_MODEL, N_DEV, IN_SHARD).transpose(1, 0, 2)),
              _blocks(_mm_tn(cqn, dqp, "dw_uq", shard_cols=HEAD_PAD)),
              _blocks(_mm_tn(ckvn, dkn, "dw_uk").reshape(N_DEV, -1, N_HEADS, NOPE)),
              _blocks(_mm_tn(ckvn, dv, "dw_uv").reshape(N_DEV, -1, N_HEADS, NOPE)),
              _blocks(dconv[:CONV_W].reshape(CONV_W, N_DEV, 1, -1).transpose(1, 0, 2, 3).astype(BF16))]
    l_late = _run_exchange(_pair_exchange(g_late), "rs_pair_exchange_late")
    p_late = _pair_sum(g_late, l_late, c_idx, "rs_pair_sum_late", 1)
    (grad_x, dg_pre), q_late = _bwd_in_proj(dzs, da, db, dga, dgc, x, dx1, w_in_p, small["norm_mix_pre"],
                                            _chip_exchange(p_late))

    order = lambda late, mix, ff: list(late[:4]) + [mix[0], late[4], mix[1], mix[2]] + list(ff)
    exchanged = [order(g_late, g_mix, g_ff), order(l_late, l_mix, l_ff),
                 order(q_late, q_early[2:], q_early[:2])]
    small_grads = (dg_pre, dq_norm, dkv_norm, dconv_b, dln_g, dln_b, db_pw2, dg_mix_post, dg_mlp_pre, dg_mlp_post)
    return loss_blk, grad_x, small_grads, exchanged


def kernel(x, positions, norm_mix_pre, w_in, q_norm, w_uq, kv_norm, w_uk, w_uv, w_o_attn, conv_w, conv_b, conv_ln_g, conv_ln_b, w_pw2, b_pw2, w_out, norm_mix_post, norm_mlp_pre, w_ff1, w_ff2, norm_mlp_post, loss_target, m_norm_mix_pre, m_w_in, m_q_norm, m_w_uq, m_kv_norm, m_w_uk, m_w_uv, m_w_o_attn, m_conv_w, m_conv_b, m_conv_ln_g, m_conv_ln_b, m_w_pw2, m_b_pw2, m_w_out, m_norm_mix_post, m_norm_mlp_pre, m_w_ff1, m_w_ff2, m_norm_mlp_post, v_norm_mix_pre, v_w_in, v_q_norm, v_w_uq, v_kv_norm, v_w_uk, v_w_uv, v_w_o_attn, v_conv_w, v_conv_b, v_conv_ln_g, v_conv_ln_b, v_w_pw2, v_b_pw2, v_w_out, v_norm_mix_post, v_norm_mlp_pre, v_w_ff1, v_w_ff2, v_norm_mlp_post):
    wts = dict(norm_mix_pre=norm_mix_pre, w_in=w_in, q_norm=q_norm, w_uq=w_uq, kv_norm=kv_norm, w_uk=w_uk, w_uv=w_uv,
               w_o_attn=w_o_attn, conv_w=conv_w, conv_b=conv_b, conv_ln_g=conv_ln_g, conv_ln_b=conv_ln_b,
               w_pw2=w_pw2, b_pw2=b_pw2, w_out=w_out, norm_mix_post=norm_mix_post, norm_mlp_pre=norm_mlp_pre,
               w_ff1=w_ff1, w_ff2=w_ff2, norm_mlp_post=norm_mlp_post)
    mom_m = dict(norm_mix_pre=m_norm_mix_pre, w_in=m_w_in, q_norm=m_q_norm, w_uq=m_w_uq, kv_norm=m_kv_norm,
                 w_uk=m_w_uk, w_uv=m_w_uv, w_o_attn=m_w_o_attn, conv_w=m_conv_w, conv_b=m_conv_b,
                 conv_ln_g=m_conv_ln_g, conv_ln_b=m_conv_ln_b, w_pw2=m_w_pw2, b_pw2=m_b_pw2, w_out=m_w_out,
                 norm_mix_post=m_norm_mix_post, norm_mlp_pre=m_norm_mlp_pre, w_ff1=m_w_ff1, w_ff2=m_w_ff2,
                 norm_mlp_post=m_norm_mlp_post)
    mom_v = dict(norm_mix_pre=v_norm_mix_pre, w_in=v_w_in, q_norm=v_q_norm, w_uq=v_w_uq, kv_norm=v_kv_norm,
                 w_uk=v_w_uk, w_uv=v_w_uv, w_o_attn=v_w_o_attn, conv_w=v_conv_w, conv_b=v_conv_b,
                 conv_ln_g=v_conv_ln_g, conv_ln_b=v_conv_ln_b, w_pw2=v_w_pw2, b_pw2=v_b_pw2, w_out=v_w_out,
                 norm_mix_post=v_norm_mix_post, norm_mlp_pre=v_norm_mlp_pre, w_ff1=v_w_ff1, w_ff2=v_w_ff2,
                 norm_mlp_post=v_norm_mlp_post)
    cx, cy, cc = lax.axis_index("x"), lax.axis_index("y"), lax.axis_index("c")

    big_local = {n: wts[n] for n in _BIG}
    w_ex = _to_exchange(big_local)
    send = [a.astype(BF16) for a in w_ex]
    conv_i = _BIG.index("conv_w")
    conv_lo = (w_ex[conv_i] - send[conv_i].astype(F32)).astype(BF16)
    send[conv_i] = jnp.stack([send[conv_i], conv_lo])

    small = {n: wts[n].reshape(1, -1) for n in _SMALL}
    c_idx = cc.reshape(1).astype(jnp.int32)
    loss_blk, grad_x, small_grads, (g4, l_sib, q_in) = _step(x[0], positions, loss_target[0], small, send, c_idx)

    idx = jnp.stack([2 * cx + cy, cc]).astype(jnp.int32)
    m_ex, v_ex = _to_exchange({n: mom_m[n] for n in _BIG}), _to_exchange({n: mom_v[n] for n in _BIG})
    upd = [None] * len(_BIG)
    for group, steps in ((("w_in", "w_uq", "w_ff1", "w_ff2"), 4),
                         (("w_uk", "w_uv", "w_o_attn", "conv_w", "w_pw2", "w_out"), 1)):
        ids = [_BIG.index(n) for n in group]
        pick = lambda arrs: [arrs[i] for i in ids]
        res = _update(pick(g4), pick(l_sib), pick(q_in), pick(w_ex), pick(m_ex), pick(v_ex), idx,
                      "update_" + group[0], steps)
        for i, r in zip(ids, res):
            upd[i] = r
    out_g, out_d, out_m, out_v = (_from_exchange([u[j] for u in upd], big_local) for j in range(4))

    loss_row = jnp.broadcast_to(loss_blk[0:1, 0:1], (1, 1024))
    sv = _small_pack(list(small_grads) + [loss_row])
    sv_all = _run_exchange(_ag_exchange([sv]), "ag_small")[0]
    loss_sum, upd_small = _update_small(sv_all, [wts[n] for n in _SMALL], [mom_m[n] for n in _SMALL],
                                        [mom_v[n] for n in _SMALL])
    for n, r in zip(_SMALL, upd_small):
        out_g[n], out_d[n], out_m[n], out_v[n] = r
    loss = loss_sum[0, 0] * (0.5 / D_MODEL)

    return (loss, grad_x[None], *[out_g[n] for n in _WEIGHTS], *[out_d[n] for n in _WEIGHTS],
            *[out_m[n] for n in _WEIGHTS], *[out_v[n] for n in _WEIGHTS])
```

```python
import collections
import functools

import jax
import jax.numpy as jnp
from jax import lax
from jax.experimental import pallas as pl
from jax.experimental.pallas import tpu as pltpu

F32 = jnp.float32
BF16 = jnp.bfloat16

D_MODEL = 1024
N_HEADS = 8
NOPE = 128
ROPE = 64
HALF = ROPE // 2
Q_RANK = 384
KV_RANK = 256
CONV_W = 31
D_FF = 4096
EPS = 1e-6
ROPE_THETA = 10000.0
HEAD_PAD = 256
QK_SCALE = (NOPE + ROPE) ** -0.5
LOG2E = 1.4426950408889634
SUBLANES = 8
N_DEV = 8
FF_SHARD = D_FF // N_DEV
IN_SHARD = 4800 // N_DEV

ZS = Q_RANK + KV_RANK + 128
OFF_A = ZS
OFF_B = OFF_A + D_MODEL
OFF_GA = OFF_B + D_MODEL
OFF_GC = OFF_GA + D_MODEL
D_IN_PAD = OFF_GC + D_MODEL

ADAM_LR = 0.001
ADAM_B1 = 0.9
ADAM_B2 = 0.999
ADAM_EPS = 1e-08
ADAM_WD = 0.01
ADAM_STEP = 10

VMEM_LIMIT = 56 * 1024 * 1024

_SMALL = ("norm_mix_pre", "q_norm", "kv_norm", "conv_b", "conv_ln_g", "conv_ln_b", "b_pw2", "norm_mix_post",
          "norm_mlp_pre", "norm_mlp_post")
SMALL_ROWS = 16
LOSS_ROW = len(_SMALL)

_BIG = ("w_in", "w_uq", "w_uk", "w_uv", "w_o_attn", "conv_w", "w_pw2", "w_out", "w_ff1", "w_ff2")
_WEIGHTS = ("norm_mix_pre", "w_in", "q_norm", "w_uq", "kv_norm", "w_uk", "w_uv", "w_o_attn", "conv_w", "conv_b",
            "conv_ln_g", "conv_ln_b", "w_pw2", "b_pw2", "w_out", "norm_mix_post", "norm_mlp_pre", "w_ff1", "w_ff2",
            "norm_mlp_post")


def _dot(a, b):
    return jnp.dot(a, b, preferred_element_type=F32)


def _dot_nt(a, b):
    return lax.dot_general(a, b, (((1,), (1,)), ((), ())), preferred_element_type=F32)


def _dot_tn(a, b):
    return lax.dot_general(a, b, (((0,), (0,)), ((), ())), preferred_element_type=F32)


def _sigmoid(x):
    return 1.0 / (1.0 + jnp.exp(-x))


def _rms_fwd(x, g):
    r = lax.rsqrt(jnp.mean(x * x, axis=-1, keepdims=True) + EPS)
    return x * r * g


def _rms_bwd(dy, x, g):
    r = lax.rsqrt(jnp.mean(x * x, axis=-1, keepdims=True) + EPS)
    xh = x * r
    gy = dy * g
    dx = r * (gy - xh * jnp.mean(gy * xh, axis=-1, keepdims=True))
    return dx, jnp.sum(dy * xh, axis=0, keepdims=True)


def _rope(q, c, sa, sb):
    n = q.shape[-1]
    return q * c + pltpu.roll(q, n - HALF, 1) * sa + pltpu.roll(q, HALF, 1) * sb


def _rope_bwd(d, c, sa, sb):
    n = d.shape[-1]
    return d * c - pltpu.roll(d, n - HALF, 1) * sa - pltpu.roll(d, HALF, 1) * sb


def _shifted_copies(buf, shifted, tb):
    n = shifted.shape[1]
    for b in range(1, SUBLANES):
        shifted[b - 1] = buf[pl.ds(b, n), :]


def _rows_at(buf, shifted, start, tb):
    a, b = divmod(start, SUBLANES)
    src = buf if b == 0 else shifted.at[b - 1]
    return src[pl.ds(SUBLANES * a, tb), :]


def _params(n_axes=1):
    return pltpu.CompilerParams(dimension_semantics=("arbitrary",) * n_axes, vmem_limit_bytes=VMEM_LIMIT)


def _row_call(body, name, tb, row_ins, full_ins, row_outs, acc_outs, lane_outs=(), exchange=None):
    t = row_ins[0].shape[0]
    in_specs = [pl.BlockSpec((tb, a.shape[1]), lambda i: (i, 0)) for a in row_ins]
    in_specs += [pl.BlockSpec(a.shape, lambda i, nd=a.ndim: (0,) * nd) for a in full_ins]
    out_specs = [pl.BlockSpec((tb, c), lambda i: (i, 0)) for c, _ in row_outs]
    out_specs += [pl.BlockSpec(s, lambda i, nd=len(s): (0,) * nd) for s, _ in acc_outs]
    out_specs += [pl.BlockSpec((n, None, 1, tb), lambda i: (0, i, 0, 0)) for n in lane_outs]
    out_shape = [jax.ShapeDtypeStruct((t, c), dt) for c, dt in row_outs]
    out_shape += [jax.ShapeDtypeStruct(s, dt) for s, dt in acc_outs]
    out_shape += [jax.ShapeDtypeStruct((n, t // tb, 1, tb), F32) for n in lane_outs]
    if exchange is None:
        return pl.pallas_call(
            functools.partial(body), name=name, grid=(t // tb,), in_specs=in_specs, out_specs=out_specs,
            out_shape=out_shape, compiler_params=_params(1),
        )(*row_ins, *full_ins)
    steps = _phase_steps(len(exchange.phases), t // tb)
    fn = _hosted(body, len(in_specs), len(out_specs), exchange, lambda k: pl.program_id(0) == steps[k])
    res = pl.pallas_call(
        fn, name=name, grid=(t // tb,), in_specs=in_specs + [_ANY] * len(exchange.ins),
        out_specs=out_specs + [_ANY] * len(exchange.out_shapes), out_shape=out_shape + exchange.out_shapes,
        scratch_shapes=exchange.scratch, compiler_params=_params(1),
    )(*row_ins, *full_ins, *exchange.ins)
    return res[:len(out_specs)], res[len(out_specs):]


def _acc(ref, val):
    @pl.when(pl.program_id(0) == 0)
    def _():
        ref[...] = jnp.zeros_like(ref)
    ref[...] += val


def _fwd_in_proj(x, g_pre, w_in_p, exchange, tb=256):
    def body(x_ref, g_ref, w_ref, h_ref, zs_ref, a_ref, sb_ref, u_ref, sa_ref, sc_ref):
        hb = _rms_fwd(x_ref[...], g_ref[...]).astype(BF16)
        h_ref[...] = hb
        zs_ref[...] = _dot(hb, w_ref[:, 0:ZS])
        a = _dot(hb, w_ref[:, OFF_A:OFF_B])
        sb = _sigmoid(_dot(hb, w_ref[:, OFF_B:OFF_GA]))
        a_ref[...] = a
        sb_ref[...] = sb
        u_ref[...] = a * sb
        sa_ref[...] = _sigmoid(_dot(hb, w_ref[:, OFF_GA:OFF_GC]))
        sc_ref[...] = _sigmoid(_dot(hb, w_ref[:, OFF_GC:D_IN_PAD]))

    d = D_MODEL
    return _row_call(body, "fwd_in_proj", tb, [x], [g_pre, w_in_p],
                     [(d, BF16), (ZS, F32), (d, F32), (d, F32), (d, F32), (d, F32), (d, F32)], [], exchange=exchange)


def _fwd_qkv(zs, tc, tsa, tsb, q_norm, kv_norm, w_uq_p, w_uk, w_uv, tb=256):
    def body(zs_ref, c_ref, sa_ref, sb_ref, qg_ref, kg_ref, wq_ref, wk_ref, wv_ref,
             cqn_ref, ckvn_ref, q_ref, k_ref, v_ref):
        zs_ = zs_ref[...]
        c, sa, sb = c_ref[...], sa_ref[...], sb_ref[...]
        cqn = _rms_fwd(zs_[:, 0:Q_RANK], qg_ref[...]).astype(BF16)
        cqn_ref[...] = cqn
        q = jnp.concatenate([_dot(cqn, wq_ref[h]) for h in range(N_HEADS)], axis=1)
        q = _rope(q, jnp.tile(c, (1, N_HEADS)), jnp.tile(sa, (1, N_HEADS)), jnp.tile(sb, (1, N_HEADS)))
        q_ref[...] = (q * (QK_SCALE * LOG2E)).astype(BF16)
        kr = zs_[:, Q_RANK + KV_RANK:ZS]
        kr = _rope(kr, c[:, NOPE:], sa[:, NOPE:], sb[:, NOPE:]).astype(BF16)
        ckvn = _rms_fwd(zs_[:, Q_RANK:Q_RANK + KV_RANK], kg_ref[...]).astype(BF16)
        ckvn_ref[...] = ckvn
        kn = _dot(ckvn, wk_ref[...]).astype(BF16)
        v_ref[...] = _dot(ckvn, wv_ref[...]).astype(BF16)
        for h in range(N_HEADS):
            k_ref[:, h * HEAD_PAD:h * HEAD_PAD + NOPE] = kn[:, h * NOPE:(h + 1) * NOPE]
            k_ref[:, h * HEAD_PAD + NOPE:(h + 1) * HEAD_PAD] = kr

    hp = N_HEADS * HEAD_PAD
    return _row_call(body, "fwd_qkv", tb, [zs, tc, tsa, tsb], [q_norm, kv_norm, w_uq_p, w_uk, w_uv],
                     [(Q_RANK, BF16), (KV_RANK, BF16), (hp, BF16), (hp, BF16), (D_MODEL, BF16)], [])


def _attn_fwd(q, k, v, exchange, tq=512):
    t = q.shape[0]
    nq = t // tq

    def body(q_ref, k_ref, v_ref, o_ref, lse_ref, m_sc, l_sc, acc_sc, s_sc):
        i = pl.program_id(1)
        m_sc[...] = jnp.full_like(m_sc, -1e30)
        l_sc[...] = jnp.zeros_like(l_sc)
        acc_sc[...] = jnp.zeros_like(acc_sc)
        qb = q_ref[...]

        def rows(j):
            return pl.ds(pl.multiple_of(j * tq, tq), tq)

        def scores(j, slot):
            s_sc[slot] = _dot_nt(k_ref[rows(j), :], qb)

        def update(j, slot, masked):
            st = s_sc[slot]
            if masked:
                key = lax.broadcasted_iota(jnp.int32, (tq, tq), 0)
                qry = lax.broadcasted_iota(jnp.int32, (tq, tq), 1)
                st = jnp.where(key <= qry, st, -1e30)
            m_prev = m_sc[...]
            m_new = jnp.maximum(m_prev, jnp.max(st, axis=0, keepdims=True))
            alpha = jnp.exp2(m_prev - m_new)
            pt = jnp.exp2(st - m_new)
            l_sc[...] = alpha * l_sc[...] + jnp.sum(pt, axis=0, keepdims=True)
            acc_sc[...] = alpha * acc_sc[...] + _dot_tn(v_ref[rows(j), :], pt.astype(BF16))
            m_sc[...] = m_new

        def pair(p, carry):
            scores(2 * p + 1, 1)
            update(2 * p, 0, False)
            scores(2 * p + 2, 0)
            update(2 * p + 1, 1, False)
            return carry

        scores(0, 0)
        lax.fori_loop(0, i // 2, pair, 0)

        @pl.when(i % 2 == 1)
        def _():
            scores(i, 1)
            update(i - 1, 0, False)
            update(i, 1, True)

        @pl.when(i % 2 == 0)
        def _():
            update(i, 0, True)

        l = l_sc[...]
        o_ref[...] = (acc_sc[...] / l).T.astype(BF16)
        lse_ref[...] = m_sc[...] + jnp.log2(l)

    steps = _phase_steps(len(exchange.phases), N_HEADS * nq)
    fn = _hosted(body, 3, 2, exchange, lambda p: pl.program_id(0) * nq + pl.program_id(1) == steps[p])
    res = pl.pallas_call(
        fn, name="attn_fwd", grid=(N_HEADS, nq),
        in_specs=[pl.BlockSpec((tq, HEAD_PAD), lambda h, i: (i, h)),
                  pl.BlockSpec((t, HEAD_PAD), lambda h, i: (0, h)),
                  pl.BlockSpec((t, NOPE), lambda h, i: (0, h))] + [_ANY] * len(exchange.ins),
        out_specs=[pl.BlockSpec((tq, NOPE), lambda h, i: (i, h)),
                   pl.BlockSpec((None, None, 1, tq), lambda h, i: (h, i, 0, 0))] + [_ANY] * len(exchange.out_shapes),
        out_shape=[jax.ShapeDtypeStruct((t, D_MODEL), BF16),
                   jax.ShapeDtypeStruct((N_HEADS, nq, 1, tq), F32)] + exchange.out_shapes,
        scratch_shapes=[pltpu.VMEM((1, tq), F32), pltpu.VMEM((1, tq), F32), pltpu.VMEM((NOPE, tq), F32),
                        pltpu.VMEM((2, tq, tq), F32)] + exchange.scratch,
        compiler_params=_params(2),
    )(q, k, v, *exchange.ins)
    return res[:2], res[2:]


def _conv_fwd(u, conv_w, conv_b, ln_g, ln_b, tb=256):
    t, c = u.shape
    halo = 32

    def body(u_ref, up_ref, w_ref, b_ref, g_ref, be_ref, co_ref, act_ref, buf, shifted):
        i = pl.program_id(0)
        buf[0:halo, :] = jnp.where(i == 0, 0.0, up_ref[...])
        buf[halo:halo + tb, :] = u_ref[...]
        _shifted_copies(buf, shifted, tb)
        acc = jnp.zeros((tb, c), F32)
        for k in range(CONV_W):
            acc = acc + w_ref[k:k + 1, :] * _rows_at(buf, shifted, halo - (CONV_W - 1) + k, tb)
        co = acc + b_ref[...]
        co_ref[...] = co
        mu = jnp.mean(co, axis=-1, keepdims=True)
        xc = co - mu
        r = lax.rsqrt(jnp.mean(xc * xc, axis=-1, keepdims=True) + EPS)
        y = xc * r * g_ref[...] + be_ref[...]
        act_ref[...] = (y * _sigmoid(y)).astype(BF16)

    ratio = tb // halo
    return pl.pallas_call(
        functools.partial(body), name="conv_fwd", grid=(t // tb,),
        in_specs=[pl.BlockSpec((tb, c), lambda i: (i, 0)),
                  pl.BlockSpec((halo, c), lambda i: (jnp.maximum(i * ratio - 1, 0), 0)),
                  pl.BlockSpec(conv_w.shape, lambda i: (0, 0)),
                  pl.BlockSpec((1, c), lambda i: (0, 0)), pl.BlockSpec((1, c), lambda i: (0, 0)),
                  pl.BlockSpec((1, c), lambda i: (0, 0))],
        out_specs=[pl.BlockSpec((tb, c), lambda i: (i, 0)), pl.BlockSpec((tb, c), lambda i: (i, 0))],
        out_shape=[jax.ShapeDtypeStruct((t, c), F32), jax.ShapeDtypeStruct((t, c), BF16)],
        scratch_shapes=[pltpu.VMEM((tb + halo, c), F32), pltpu.VMEM((SUBLANES - 1, tb + halo - SUBLANES, c), F32)],
        compiler_params=_params(1),
    )(u, u, conv_w, conv_b, ln_g, ln_b)


def _fwd_merge(attn, uact, sa, sc, x, w_o, w_pw2, b_pw2, w_out, g_post, tb=256):
    def body(at_ref, ua_ref, sa_ref, sc_ref, x_ref, wo_ref, wp_ref, bp_ref, wout_ref, g_ref,
             ya_ref, yc_ref, mb_ref, m_ref, x1_ref):
        ya = _dot(at_ref[...], wo_ref[...])
        yc = _dot(ua_ref[...], wp_ref[...]) + bp_ref[...]
        ya_ref[...] = ya
        yc_ref[...] = yc
        mb = (sa_ref[...] * ya + sc_ref[...] * yc).astype(BF16)
        mb_ref[...] = mb
        m = _dot(mb, wout_ref[...])
        m_ref[...] = m
        x1_ref[...] = x_ref[...] + _rms_fwd(m, g_ref[...])

    d = D_MODEL
    return _row_call(body, "fwd_merge", tb, [attn, uact, sa, sc, x], [w_o, w_pw2, b_pw2, w_out, g_post],
                     [(d, F32), (d, F32), (d, BF16), (d, F32), (d, F32)], [])


def _fwd_ff1(x1, g, w_ff1, tb=256):
    def body(x1_ref, g_ref, w_ref, h2_ref, r1_ref, act_ref):
        h2 = _rms_fwd(x1_ref[...], g_ref[...]).astype(BF16)
        h2_ref[...] = h2
        for j in range(N_DEV):
            cols = slice(j * FF_SHARD, (j + 1) * FF_SHARD)
            r1 = jnp.maximum(_dot(h2, w_ref[j]), 0.0)
            r1_ref[:, cols] = r1.astype(BF16)
            act_ref[:, cols] = (r1 * r1).astype(BF16)

    return _row_call(body, "fwd_ff1", tb, [x1], [g, w_ff1], [(D_MODEL, BF16), (D_FF, BF16), (D_FF, BF16)], [])


def _fwd_ff2_loss(act, x1, target, w_ff2, g, tb=256):
    def body(act_ref, x1_ref, tg_ref, w_ref, g_ref, f_ref, dy_ref, loss_ref):
        f = _dot(act_ref[...], w_ref[...])
        f_ref[...] = f
        e = x1_ref[...] + _rms_fwd(f, g_ref[...]) - tg_ref[...]
        dy_ref[...] = e * (1.0 / D_MODEL)
        _acc(loss_ref, jnp.sum(e * e))

    return _row_call(body, "fwd_ff2_loss", tb, [act, x1, target], [w_ff2, g],
                     [(D_MODEL, F32), (D_MODEL, F32)], [((8, 128), F32)])


def _bwd_ff2(dy, f, r1, w_ff2, g, tb=256):
    def body(dy_ref, f_ref, r1_ref, w_ref, g_ref, df_ref, df1_ref, dg_ref):
        df, dg = _rms_bwd(dy_ref[...], f_ref[...], g_ref[...])
        _acc(dg_ref, dg)
        dfb = df.astype(BF16)
        df_ref[...] = dfb
        dact = _dot_nt(dfb, w_ref[...])
        df1_ref[...] = (dact * (2.0 * r1_ref[...].astype(F32))).astype(BF16)

    return _row_call(body, "bwd_ff2", tb, [dy, f, r1], [w_ff2, g], [(D_MODEL, BF16), (D_FF, BF16)],
                     [((1, D_MODEL), F32)])


def _bwd_ff1(df1, x1, dy, w_ff1, g, exchange, tb=256):
    def body(df1_ref, x1_ref, dy_ref, w_ref, g_ref, dx1_ref, dg_ref):
        dh2 = _dot_nt(df1_ref[:, 0:FF_SHARD], w_ref[0])
        for j in range(1, N_DEV):
            dh2 = dh2 + _dot_nt(df1_ref[:, j * FF_SHARD:(j + 1) * FF_SHARD], w_ref[j])
        dxn, dg = _rms_bwd(dh2, x1_ref[...], g_ref[...])
        _acc(dg_ref, dg)
        dx1_ref[...] = dy_ref[...] + dxn

    return _row_call(body, "bwd_ff1", tb, [df1, x1, dy], [w_ff1, g], [(D_MODEL, F32)], [((1, D_MODEL), F32)],
                     exchange=exchange)


def _bwd_merge(dx1, m, sa, sc, ya, yc, attn, w_out, w_o, w_pw2, g_post, exchange, tb=256):
    def body(dx1_ref, m_ref, sa_ref, sc_ref, ya_ref, yc_ref, at_ref, wout_ref, wo_ref, wp_ref, g_ref,
             dm_ref, dya_ref, dyc_ref, dga_ref, dgc_ref, dat_ref, dua_ref, dg_ref, dbp_ref, delta_ref):
        dm, dg = _rms_bwd(dx1_ref[...], m_ref[...], g_ref[...])
        _acc(dg_ref, dg)
        dmb = dm.astype(BF16)
        dm_ref[...] = dmb
        dmerged = _dot_nt(dmb, wout_ref[...])
        sa, sc = sa_ref[...], sc_ref[...]
        dya = dmerged * sa
        dyc = dmerged * sc
        _acc(dbp_ref, jnp.sum(dyc, axis=0, keepdims=True))
        dyab = dya.astype(BF16)
        dycb = dyc.astype(BF16)
        dya_ref[...] = dyab
        dyc_ref[...] = dycb
        dga_ref[...] = (dmerged * ya_ref[...] * sa * (1.0 - sa)).astype(BF16)
        dgc_ref[...] = (dmerged * yc_ref[...] * sc * (1.0 - sc)).astype(BF16)
        dat = _dot_nt(dyab, wo_ref[...])
        dat_ref[...] = dat.astype(BF16)
        prod = dat * at_ref[...].astype(F32)
        lane = lax.broadcasted_iota(jnp.int32, (tb, NOPE), 1)
        dl = jnp.zeros((tb, NOPE), F32)
        for h in range(N_HEADS):
            dl = dl + jnp.where(lane == h, jnp.sum(prod[:, h * NOPE:(h + 1) * NOPE], axis=1, keepdims=True), 0.0)
        dlt = dl.T
        for h in range(N_HEADS):
            delta_ref[h] = dlt[h:h + 1, :]
        dua_ref[...] = _dot_nt(dycb, wp_ref[...])

    d = D_MODEL
    return _row_call(body, "bwd_merge", tb, [dx1, m, sa, sc, ya, yc, attn], [w_out, w_o, w_pw2, g_post],
                     [(d, BF16), (d, BF16), (d, BF16), (d, BF16), (d, BF16), (d, BF16), (d, F32)],
                     [((1, d), F32), ((1, d), F32)], lane_outs=(N_HEADS,), exchange=exchange)


def _bwd_ln(dua, co, ln_g, ln_b, tb=256):
    def body(dua_ref, co_ref, g_ref, be_ref, dco_ref, dg_ref, db_ref, dcb_ref):
        co = co_ref[...]
        g = g_ref[...]
        mu = jnp.mean(co, axis=-1, keepdims=True)
        xc = co - mu
        r = lax.rsqrt(jnp.mean(xc * xc, axis=-1, keepdims=True) + EPS)
        xh = xc * r
        y = xh * g + be_ref[...]
        s = _sigmoid(y)
        dy = dua_ref[...] * (s + y * s * (1.0 - s))
        _acc(db_ref, jnp.sum(dy, axis=0, keepdims=True))
        _acc(dg_ref, jnp.sum(dy * xh, axis=0, keepdims=True))
        gy = dy * g
        dco = r * (gy - jnp.mean(gy, axis=-1, keepdims=True) - xh * jnp.mean(gy * xh, axis=-1, keepdims=True))
        dco_ref[...] = dco
        _acc(dcb_ref, jnp.sum(dco, axis=0, keepdims=True))

    d = D_MODEL
    return _row_call(body, "bwd_ln", tb, [dua, co], [ln_g, ln_b], [(d, F32)],
                     [((1, d), F32), ((1, d), F32), ((1, d), F32)])


def _conv_bwd(dco, u, a, sb, conv_w, exchange, tb=256):
    t, c = u.shape
    halo = 32
    ratio = tb // halo
    nblk = t // tb

    def body(d_ref, dn_ref, u_ref, up_ref, a_ref, sb_ref, w_ref, da_ref, db_ref, dw_ref, bufd, bufu, shd, shu):
        i = pl.program_id(0)

        @pl.when(i == 0)
        def _():
            dw_ref[...] = jnp.zeros_like(dw_ref)

        dco = d_ref[...]
        bufd[0:tb, :] = dco
        bufd[tb:tb + halo, :] = jnp.where(i == nblk - 1, 0.0, dn_ref[...])
        bufu[0:halo, :] = jnp.where(i == 0, 0.0, up_ref[...])
        bufu[halo:halo + tb, :] = u_ref[...]
        _shifted_copies(bufd, shd, tb)
        _shifted_copies(bufu, shu, tb)
        du = jnp.zeros((tb, c), F32)
        for k in range(CONV_W):
            du = du + w_ref[k:k + 1, :] * _rows_at(bufd, shd, CONV_W - 1 - k, tb)
            dw_ref[k:k + 1, :] += jnp.sum(dco * _rows_at(bufu, shu, halo - (CONV_W - 1) + k, tb), axis=0,
                                          keepdims=True)
        sb_ = sb_ref[...]
        da_ref[...] = (du * sb_).astype(BF16)
        db_ref[...] = (du * a_ref[...] * sb_ * (1.0 - sb_)).astype(BF16)

    steps = _phase_steps(len(exchange.phases), nblk)
    fn = _hosted(body, 7, 3, exchange, lambda p: pl.program_id(0) == steps[p])
    res = pl.pallas_call(
        fn, name="conv_bwd", grid=(nblk,),
        in_specs=[pl.BlockSpec((tb, c), lambda i: (i, 0)),
                  pl.BlockSpec((halo, c), lambda i: (jnp.minimum((i + 1) * ratio, t // halo - 1), 0)),
                  pl.BlockSpec((tb, c), lambda i: (i, 0)),
                  pl.BlockSpec((halo, c), lambda i: (jnp.maximum(i * ratio - 1, 0), 0)),
                  pl.BlockSpec((tb, c), lambda i: (i, 0)), pl.BlockSpec((tb, c), lambda i: (i, 0)),
                  pl.BlockSpec(conv_w.shape, lambda i: (0, 0))] + [_ANY] * len(exchange.ins),
        out_specs=[pl.BlockSpec((tb, c), lambda i: (i, 0)), pl.BlockSpec((tb, c), lambda i: (i, 0)),
                   pl.BlockSpec((32, c), lambda i: (0, 0))] + [_ANY] * len(exchange.out_shapes),
        out_shape=[jax.ShapeDtypeStruct((t, c), BF16), jax.ShapeDtypeStruct((t, c), BF16),
                   jax.ShapeDtypeStruct((32, c), F32)] + exchange.out_shapes,
        scratch_shapes=[pltpu.VMEM((tb + halo, c), F32), pltpu.VMEM((tb + halo, c), F32),
                        pltpu.VMEM((SUBLANES - 1, tb + halo - SUBLANES, c), F32),
                        pltpu.VMEM((SUBLANES - 1, tb + halo - SUBLANES, c), F32)] + exchange.scratch,
        compiler_params=_params(1),
    )(dco, dco, u, u, a, sb, conv_w, *exchange.ins)
    return res[:3], res[3:]


def _attn_bwd(q, k, v, do, lse2, delta, exchange, tq=512):
    t = q.shape[0]
    nq = t // tq
    td = delta.shape[-1]
    per = tq // td

    def body(q_ref, k_ref, v_ref, do_ref, lse_ref, dl_ref, dq_ref, dk_ref, dv_ref, dk_sc, dv_sc, s_sc, dp_sc):
        j = pl.program_id(1)

        @pl.when(j == 0)
        def _():
            dq_ref[...] = jnp.zeros_like(dq_ref)

        dk_sc[...] = jnp.zeros_like(dk_sc)
        dv_sc[...] = jnp.zeros_like(dv_sc)
        kb, vb = k_ref[...], v_ref[...]

        def rows(i):
            return pl.ds(pl.multiple_of(i * tq, tq), tq)

        def ahead(i, slot):
            i = jnp.minimum(i, nq - 1)
            s_sc[slot] = _dot_nt(kb, q_ref[rows(i), :])
            dp_sc[slot] = _dot_nt(vb, do_ref[rows(i), :])

        def finish(i, slot, masked):
            qb, dob = q_ref[rows(i), :], do_ref[rows(i), :]
            pt = jnp.exp2(s_sc[slot] - lse_ref[i])
            if masked:
                key = lax.broadcasted_iota(jnp.int32, (tq, tq), 0)
                qry = lax.broadcasted_iota(jnp.int32, (tq, tq), 1)
                pt = jnp.where(key <= qry, pt, 0.0)
            dv_sc[...] += _dot(pt.astype(BF16), dob)
            dl = jnp.concatenate([dl_ref[per * i + r] for r in range(per)], axis=-1)
            dst = (pt * (dp_sc[slot] - dl)).astype(BF16)
            dk_sc[...] += _dot(dst, qb)
            dq_ref[rows(i), :] += _dot_tn(dst, kb)

        def pair(p, carry):
            i = j + 1 + 2 * p
            ahead(i + 1, 0)
            finish(i, 1, False)
            ahead(i + 2, 1)
            finish(i + 1, 0, False)
            return carry

        n_after = nq - 1 - j
        ahead(j, 0)
        ahead(j + 1, 1)
        finish(j, 0, True)
        lax.fori_loop(0, n_after // 2, pair, 0)

        @pl.when(n_after % 2 == 1)
        def _():
            finish(nq - 1, 1, False)

        dk_ref[...] = dk_sc[...] * (1.0 / LOG2E)
        dv_ref[...] = dv_sc[...].astype(BF16)

    hp = N_HEADS * HEAD_PAD
    steps = _phase_steps(len(exchange.phases), N_HEADS * nq)
    fn = _hosted(body, 6, 3, exchange, lambda p: pl.program_id(0) * nq + pl.program_id(1) == steps[p])
    res = pl.pallas_call(
        fn, name="attn_bwd", grid=(N_HEADS, nq),
        in_specs=[pl.BlockSpec((t, HEAD_PAD), lambda h, j: (0, h)),
                  pl.BlockSpec((tq, HEAD_PAD), lambda h, j: (j, h)),
                  pl.BlockSpec((tq, NOPE), lambda h, j: (j, h)),
                  pl.BlockSpec((t, NOPE), lambda h, j: (0, h)),
                  pl.BlockSpec((None, nq, 1, tq), lambda h, j: (h, 0, 0, 0)),
                  pl.BlockSpec((None, t // td, 1, td), lambda h, j: (h, 0, 0, 0))] + [_ANY] * len(exchange.ins),
        out_specs=[pl.BlockSpec((t, HEAD_PAD), lambda h, j: (0, h)),
                   pl.BlockSpec((tq, HEAD_PAD), lambda h, j: (j, h)),
                   pl.BlockSpec((tq, NOPE), lambda h, j: (j, h))] + [_ANY] * len(exchange.out_shapes),
        out_shape=[jax.ShapeDtypeStruct((t, hp), F32), jax.ShapeDtypeStruct((t, hp), F32),
                   jax.ShapeDtypeStruct((t, D_MODEL), BF16)] + exchange.out_shapes,
        scratch_shapes=[pltpu.VMEM((tq, HEAD_PAD), F32), pltpu.VMEM((tq, NOPE), F32), pltpu.VMEM((2, tq, tq), F32),
                        pltpu.VMEM((2, tq, tq), F32)] + exchange.scratch,
        compiler_params=_params(2),
    )(q, k, v, do, lse2, delta, *exchange.ins)
    return res[:3], res[3:]


def _bwd_qkv(dq, dk, dv, zs, tc, tsa, tsb, q_norm, kv_norm, w_uq_p, w_uk, w_uv, tb=256):
    def body(dq_ref, dk_ref, dv_ref, zs_ref, c_ref, sa_ref, sb_ref, qg_ref, kg_ref, wq_ref, wk_ref, wv_ref,
             dqp_ref, dkn_ref, dzs_ref, dqg_ref, dkg_ref):
        c, sa, sb = c_ref[...], sa_ref[...], sb_ref[...]
        zs_ = zs_ref[...]
        dqp = (_rope_bwd(dq_ref[...], jnp.tile(c, (1, N_HEADS)), jnp.tile(sa, (1, N_HEADS)),
                         jnp.tile(sb, (1, N_HEADS))) * QK_SCALE).astype(BF16)
        dqp_ref[...] = dqp
        dcqn = _dot_nt(dqp[:, 0:HEAD_PAD], wq_ref[0])
        for h in range(1, N_HEADS):
            dcqn = dcqn + _dot_nt(dqp[:, h * HEAD_PAD:(h + 1) * HEAD_PAD], wq_ref[h])
        dcq, dqg = _rms_bwd(dcqn, zs_[:, 0:Q_RANK], qg_ref[...])
        _acc(dqg_ref, dqg)
        dzs_ref[:, 0:Q_RANK] = dcq.astype(BF16)
        dkr = jnp.zeros((tb, NOPE), F32)
        for h in range(N_HEADS):
            dkn_ref[:, h * NOPE:(h + 1) * NOPE] = dk_ref[:, h * HEAD_PAD:h * HEAD_PAD + NOPE].astype(BF16)
            dkr = dkr + dk_ref[:, h * HEAD_PAD + NOPE:(h + 1) * HEAD_PAD]
        dzs_ref[:, Q_RANK + KV_RANK:ZS] = _rope_bwd(dkr, c[:, NOPE:], sa[:, NOPE:], sb[:, NOPE:]).astype(BF16)
        dckvn = _dot_nt(dkn_ref[...], wk_ref[...]) + _dot_nt(dv_ref[...], wv_ref[...])
        dckv, dkg = _rms_bwd(dckvn, zs_[:, Q_RANK:Q_RANK + KV_RANK], kg_ref[...])
        _acc(dkg_ref, dkg)
        dzs_ref[:, Q_RANK:Q_RANK + KV_RANK] = dckv.astype(BF16)

    hp = N_HEADS * HEAD_PAD
    return _row_call(body, "bwd_qkv", tb, [dq, dk, dv, zs, tc, tsa, tsb], [q_norm, kv_norm, w_uq_p, w_uk, w_uv],
                     [(hp, BF16), (D_MODEL, BF16), (ZS, BF16)], [((1, Q_RANK), F32), ((1, KV_RANK), F32)])


def _bwd_in_proj(dzs, da, db, dga, dgc, x, dx1, w_in_p, g_pre, exchange, tb=256):
    def body(dzs_ref, da_ref, db_ref, dga_ref, dgc_ref, x_ref, dx1_ref, w_ref, g_ref, gx_ref, dg_ref):
        dh = _dot_nt(dzs_ref[...], w_ref[:, 0:ZS])
        dh = dh + _dot_nt(da_ref[...], w_ref[:, OFF_A:OFF_B])
        dh = dh + _dot_nt(db_ref[...], w_ref[:, OFF_B:OFF_GA])
        dh = dh + _dot_nt(dga_ref[...], w_ref[:, OFF_GA:OFF_GC])
        dh = dh + _dot_nt(dgc_ref[...], w_ref[:, OFF_GC:D_IN_PAD])
        dxn, dg = _rms_bwd(dh, x_ref[...], g_ref[...])
        _acc(dg_ref, dg)
        gx_ref[...] = dx1_ref[...] + dxn

    return _row_call(body, "bwd_in_proj", tb, [dzs, da, db, dga, dgc, x, dx1], [w_in_p, g_pre],
                     [(D_MODEL, F32)], [((1, D_MODEL), F32)], exchange=exchange)


def _mm_tn(a, b, name, shard_cols=None, tt=2048):
    t, m = a.shape
    n = b.shape[1]
    tm = min(m, 1024)
    tn = min(n, 1024)
    tt = min(t, tt)
    nt = t // tt
    per = tn // shard_cols if shard_cols else 1

    def body(a_ref, b_ref, o_ref, acc):
        k = pl.program_id(2)

        @pl.when(k == 0)
        def _():
            acc[...] = jnp.zeros_like(acc)

        acc[...] += _dot_tn(a_ref[...], b_ref[...])

        @pl.when(k == nt - 1)
        def _():
            if shard_cols:
                for s in range(per):
                    o_ref[s] = acc[:, s * shard_cols:(s + 1) * shard_cols].astype(BF16)
            else:
                o_ref[...] = acc[...].astype(BF16)

    if shard_cols:
        out_spec = pl.BlockSpec((per, tm, shard_cols), lambda i, j, k: (j, i, 0))
        out_shape = jax.ShapeDtypeStruct((n // shard_cols, m, shard_cols), BF16)
    else:
        out_spec = pl.BlockSpec((tm, tn), lambda i, j, k: (i, j))
        out_shape = jax.ShapeDtypeStruct((m, n), BF16)
    return pl.pallas_call(
        functools.partial(body), name=name, grid=(m // tm, n // tn, nt),
        in_specs=[pl.BlockSpec((tt, tm), lambda i, j, k: (k, i)), pl.BlockSpec((tt, tn), lambda i, j, k: (k, j))],
        out_specs=out_spec, out_shape=out_shape, scratch_shapes=[pltpu.VMEM((tm, tn), F32)],
        compiler_params=_params(3),
    )(a, b)


_ANY = pl.BlockSpec(memory_space=pl.ANY)
_MESH = pl.DeviceIdType.MESH

_Exchange = collections.namedtuple("_Exchange", "ins out_shapes scratch phases")


def _ag_exchange(shards):
    n = len(shards)

    def parts(ins, outs, sems):
        send_sems, recv_sems, _ = sems
        x, y, c = lax.axis_index("x"), lax.axis_index("y"), lax.axis_index("c")
        chips = [(1 - x, y), (x, 1 - y), (1 - x, 1 - y)]

        def copy(w, k, block, to, src=None):
            dst = outs[w].at[4 * block[0] + 2 * block[1] + block[2]]
            return pltpu.make_async_remote_copy(
                src_ref=dst if src is None else src, dst_ref=dst, send_sem=send_sems.at[7 * w + k],
                recv_sem=recv_sems.at[7 * w + k], device_id=to, device_id_type=_MESH)

        def first(w):
            return [copy(w, 0, (x, y, c), (x, y, 1 - c), src=ins[w])] + [
                copy(w, 1 + j, (x, y, c), (*chip, c), src=ins[w]) for j, chip in enumerate(chips)]

        def mine(w):
            return pltpu.make_async_copy(ins[w], outs[w].at[4 * x + 2 * y + c], sems[2].at[w])

        return (x, y, c), chips, copy, first, mine

    def start(ins, outs, sems):
        _, _, _, first, mine = parts(ins, outs, sems)
        for w in range(n):
            mine(w).start()
            for cp in first(w):
                cp.start()

    def forward(ins, outs, sems):
        (x, y, c), chips, copy, _, _ = parts(ins, outs, sems)
        for j, chip in enumerate(chips):
            for w in range(n):
                copy(w, 1 + j, (*chip, c), (x, y, c)).wait_recv()
                copy(w, 4 + j, (*chip, c), (x, y, 1 - c)).start()

    def finish(ins, outs, sems):
        (x, y, c), chips, copy, first, mine = parts(ins, outs, sems)
        for w in range(n):
            copy(w, 0, (x, y, 1 - c), (x, y, c)).wait_recv()
        for j, chip in enumerate(chips):
            for w in range(n):
                copy(w, 4 + j, (*chip, 1 - c), (x, y, c)).wait_recv()
        for w in range(n):
            for cp in first(w) + [copy(w, 4 + j, (*chip, c), (x, y, 1 - c)) for j, chip in enumerate(chips)]:
                cp.wait_send()
            mine(w).wait()

    return _Exchange(
        ins=list(shards), out_shapes=[jax.ShapeDtypeStruct((N_DEV,) + s.shape, s.dtype) for s in shards],
        scratch=[pltpu.SemaphoreType.DMA((7 * n,)), pltpu.SemaphoreType.DMA((7 * n,)), pltpu.SemaphoreType.DMA((n,))],
        phases=[start, forward, finish])


def _pair_exchange(gs):
    n = len(gs)

    def copies(ins, outs, sems):
        x, y, c = lax.axis_index("x"), lax.axis_index("y"), lax.axis_index("c")
        return [pltpu.make_async_remote_copy(
            src_ref=ins[w].at[:, 1 - c], dst_ref=outs[w], send_sem=sems[0].at[w], recv_sem=sems[1].at[w],
            device_id=(x, y, 1 - c), device_id_type=_MESH) for w in range(n)]

    return _start_then_wait(gs, [jax.ShapeDtypeStruct((4,) + g.shape[2:], g.dtype) for g in gs], n, copies)


def _start_then_wait(ins, out_shapes, n_copies, copies):
    def start(ins_, outs, sems):
        for cp in copies(ins_, outs, sems):
            cp.start()

    def finish(ins_, outs, sems):
        for cp in copies(ins_, outs, sems):
            cp.wait()

    return _Exchange(ins=list(ins), out_shapes=out_shapes,
                     scratch=[pltpu.SemaphoreType.DMA((n_copies,)), pltpu.SemaphoreType.DMA((n_copies,))],
                     phases=[start, finish])


def _run_exchange(ex, name):
    ni, no = len(ex.ins), len(ex.out_shapes)

    def body(*refs):
        for phase in ex.phases:
            phase(refs[:ni], refs[ni:ni + no], refs[ni + no:])

    return pl.pallas_call(functools.partial(body), name=name, out_shape=ex.out_shapes, in_specs=[_ANY] * ni,
                          out_specs=[_ANY] * no, scratch_shapes=ex.scratch)(*ex.ins)


def _hosted(body, n_in, n_out, ex, when):
    ni, no, ns = len(ex.ins), len(ex.out_shapes), len(ex.scratch)

    def fn(*refs):
        ins, ex_ins = refs[:n_in], refs[n_in:n_in + ni]
        outs = refs[n_in + ni:n_in + ni + n_out]
        ex_outs = refs[n_in + ni + n_out:n_in + ni + n_out + no]
        scratch, sems = refs[n_in + ni + n_out + no:len(refs) - ns], refs[len(refs) - ns:]
        last = len(ex.phases) - 1
        for k in range(last):
            pl.when(when(k))(functools.partial(ex.phases[k], ex_ins, ex_outs, sems))
        body(*ins, *outs, *scratch)
        pl.when(when(last))(functools.partial(ex.phases[last], ex_ins, ex_outs, sems))

    return fn


def _phase_steps(n_phases, n_steps):
    return [0, n_steps - 1] if n_phases == 2 else [0, 2 * n_steps // 3, n_steps - 1]


def _row_block(shape, steps, lead, pick):
    blk = (None,) * lead + (shape[0] // steps,) + tuple(shape[1:])
    return pl.BlockSpec(blk, lambda *a: tuple(pick(*a)) + (a[-2],) + (0,) * (len(shape) - 1))


def _pair_sum(gs, ls, c_idx, name, steps):
    n = len(gs)

    def body(c_ref, *refs):
        for w in range(n):
            refs[2 * n + w][...] = (refs[w][...].astype(F32) + refs[n + w][...].astype(F32)).astype(BF16)

    shapes = [g.shape[2:] for g in gs]
    return pl.pallas_call(
        functools.partial(body), name=name,
        grid_spec=pltpu.PrefetchScalarGridSpec(
            num_scalar_prefetch=1, grid=(4, steps),
            in_specs=[_row_block(s, steps, 2, lambda k, i, c: (k, c[0])) for s in shapes]
            + [_row_block(s, steps, 1, lambda k, i, c: (k,)) for s in shapes],
            out_specs=[_row_block(s, steps, 1, lambda k, i, c: (k,)) for s in shapes]),
        out_shape=[jax.ShapeDtypeStruct((4,) + tuple(s), BF16) for s in shapes], compiler_params=_params(2),
    )(c_idx, *gs, *ls)


def _chip_exchange(ps):
    n = len(ps)

    def copies(ins, outs, sems):
        x, y, c = lax.axis_index("x"), lax.axis_index("y"), lax.axis_index("c")
        chips = [(1 - x, y), (x, 1 - y), (1 - x, 1 - y)]
        return [pltpu.make_async_remote_copy(
            src_ref=ins[w].at[2 * px + py], dst_ref=outs[w].at[s], send_sem=sems[0].at[3 * w + s],
            recv_sem=sems[1].at[3 * w + s], device_id=(px, py, c), device_id_type=_MESH)
            for w in range(n) for s, (px, py) in enumerate(chips)]

    return _start_then_wait(ps, [jax.ShapeDtypeStruct((3,) + p.shape[1:], p.dtype) for p in ps], 3 * n, copies)


def _adamw(w, g, m, v):
    m2 = ADAM_B1 * m + (1.0 - ADAM_B1) * g
    v2 = ADAM_B2 * v + (1.0 - ADAM_B2) * (g * g)
    m_hat = m2 / (1.0 - ADAM_B1 ** ADAM_STEP)
    v_hat = v2 / (1.0 - ADAM_B2 ** ADAM_STEP)
    delta = -ADAM_LR * (m_hat / (jnp.sqrt(v_hat) + ADAM_EPS) + ADAM_WD * w)
    return delta, m2, v2


def _update(gs, ls, qs, ws, ms, vs, idx, name, steps):
    n = len(gs)

    def body(idx_ref, *refs):
        g, l, q, w, m, v = (refs[k * n:(k + 1) * n] for k in range(6))
        outs = refs[6 * n:]
        for i in range(n):
            gr = g[i][...].astype(F32) + l[i][...].astype(F32)
            gr = gr + q[i][0].astype(F32)
            gr = gr + q[i][1].astype(F32)
            gr = gr + q[i][2].astype(F32)
            outs[4 * i][...] = gr
            outs[4 * i + 1][...], outs[4 * i + 2][...], outs[4 * i + 3][...] = _adamw(w[i][...], gr, m[i][...], v[i][...])

    shapes = [w.shape for w in ws]
    own = [_row_block(s, steps, 0, lambda i, c: ()) for s in shapes]
    res = pl.pallas_call(
        functools.partial(body), name=name,
        grid_spec=pltpu.PrefetchScalarGridSpec(
            num_scalar_prefetch=1, grid=(steps,),
            in_specs=[_row_block(s, steps, 2, lambda i, c: (c[0], c[1])) for s in shapes]
            + [_row_block(s, steps, 1, lambda i, c: (c[0],)) for s in shapes]
            + [pl.BlockSpec((3, s[0] // steps) + tuple(s[1:]), lambda i, c, nd=len(s): (0, i) + (0,) * (nd - 1))
               for s in shapes] + own * 3,
            out_specs=[b for b in own for _ in range(4)]),
        out_shape=[jax.ShapeDtypeStruct(s, F32) for s in shapes for _ in range(4)], compiler_params=_params(1),
    )(idx, *gs, *ls, *qs, *ws, *ms, *vs)
    return [res[4 * i:4 * i + 4] for i in range(n)]


def _update_small(sv_all, ws, ms, vs):
    n = len(ws)

    def body(all_ref, *refs):
        w, m, v = refs[:n], refs[n:2 * n], refs[2 * n:3 * n]
        loss_ref, outs = refs[3 * n], refs[3 * n + 1:]
        total = all_ref[0]
        for dev in range(1, N_DEV):
            total = total + all_ref[dev]
        loss_ref[...] = total[LOSS_ROW:LOSS_ROW + 1, :]
        for i in range(n):
            gr = total[i:i + 1, 0:w[i].shape[1]]
            outs[4 * i][...] = gr
            outs[4 * i + 1][...], outs[4 * i + 2][...], outs[4 * i + 3][...] = _adamw(w[i][...], gr, m[i][...], v[i][...])

    res = pl.pallas_call(
        functools.partial(body), name="update_small",
        out_shape=[jax.ShapeDtypeStruct((1, 1024), F32)]
        + [jax.ShapeDtypeStruct(a.shape, F32) for a in ws for _ in range(4)],
    )(sv_all, *ws, *ms, *vs)
    return res[0], [res[1 + 4 * i:5 + 4 * i] for i in range(n)]


def _to_exchange(shards):
    return [jnp.pad(shards[n][0], ((0, 0), (0, HEAD_PAD - NOPE - ROPE))) if n == "w_uq" else shards[n][0]
            for n in _BIG]


def _from_exchange(arrs, like):
    return {n: (a[:, :NOPE + ROPE] if n == "w_uq" else a).reshape(like[n].shape) for n, a in zip(_BIG, arrs)}


def _padded_w_in(g_in):
    w_in = g_in.transpose(1, 0, 2).reshape(D_MODEL, N_DEV * IN_SHARD)
    kr_end = Q_RANK + KV_RANK + ROPE
    return jnp.concatenate([w_in[:, :kr_end], jnp.zeros((D_MODEL, 128 - ROPE), BF16), w_in[:, kr_end:]], axis=1)


def _full_conv_w(g_conv):
    taps = (g_conv[:, 0].astype(F32) + g_conv[:, 1].astype(F32)).reshape(N_DEV, CONV_W, D_MODEL // N_DEV)
    return jnp.pad(taps.transpose(1, 0, 2).reshape(CONV_W, D_MODEL), ((0, 1), (0, 0)))


def _pad_rows(a, rows):
    return jnp.pad(a, ((0, rows - a.shape[0]), (0, 0)))


def _small_pack(vals):
    rows = [jnp.pad(v.reshape(1, -1), ((0, 0), (0, 1024 - v.size))) for v in vals]
    return _pad_rows(jnp.concatenate(rows, axis=0), SMALL_ROWS)


def _rope_tables(positions):
    inv_freq = ROPE_THETA ** (-jnp.arange(0, ROPE, 2, dtype=F32) / ROPE)
    ang = positions.reshape(-1).astype(F32)[:, None] * inv_freq
    cos, sin = jnp.cos(ang), jnp.sin(ang)
    t = cos.shape[0]
    z32, z64 = jnp.zeros((t, HALF), F32), jnp.zeros((t, HEAD_PAD - NOPE - ROPE), F32)
    z128 = jnp.zeros((t, NOPE), F32)
    tc = jnp.concatenate([jnp.ones((t, NOPE), F32), cos, cos, z64], axis=1)
    tsa = jnp.concatenate([z128, -sin, z32, z64], axis=1)
    tsb = jnp.concatenate([z128, z32, sin, z64], axis=1)
    return tc, tsa, tsb


def _blocks(dw):
    if dw.ndim == 2:
        dw = dw.reshape(N_DEV, dw.shape[0] // N_DEV, dw.shape[1])
    return dw.reshape((4, 2) + dw.shape[1:])


def _step(x, positions, target, small, send, c_idx):
    s_in, s_uq, s_uk, s_uv, s_o, s_conv, s_pw2, s_out, s_ff1, s_ff2 = send
    tc, tsa, tsb = _rope_tables(positions)

    w_in_p = _padded_w_in(_run_exchange(_ag_exchange([s_in]), "ag_w_in")[0])
    (h, zs, a, sb, u, sa, sc), (w_uq, w_uk, w_uv) = _fwd_in_proj(
        x, small["norm_mix_pre"], w_in_p, _ag_exchange([s_uq, s_uk, s_uv]))
    w_uk, w_uv = w_uk.reshape(KV_RANK, -1), w_uv.reshape(KV_RANK, -1)
    cqn, ckvn, q, k, v = _fwd_qkv(zs, tc, tsa, tsb, small["q_norm"], small["kv_norm"], w_uq, w_uk, w_uv)
    (attn, lse), (w_o, g_conv, w_pw2, w_out, w_ff1, w_ff2) = _attn_fwd(
        q, k, v, _ag_exchange([s_o, s_conv, s_pw2, s_out, s_ff1, s_ff2]))
    w_o, w_pw2, w_out = (w.reshape(D_MODEL, D_MODEL) for w in (w_o, w_pw2, w_out))
    w_ff2, conv_w = w_ff2.reshape(D_FF, D_MODEL), _full_conv_w(g_conv)
    co, uact = _conv_fwd(u, conv_w, small["conv_b"], small["conv_ln_g"], small["conv_ln_b"])
    ya, yc, mb, m, x1 = _fwd_merge(attn, uact, sa, sc, x, w_o, w_pw2, small["b_pw2"], w_out, small["norm_mix_post"])
    h2, r1, act = _fwd_ff1(x1, small["norm_mlp_pre"], w_ff1)
    f, dy, loss_blk = _fwd_ff2_loss(act, x1, target, w_ff2, small["norm_mlp_post"])

    df, df1, dg_mlp_post = _bwd_ff2(dy, f, r1, w_ff2, small["norm_mlp_post"])
    g_ff2 = _blocks(_mm_tn(act, df, "dw_ff2"))
    (dx1, dg_mlp_pre), l_ff2 = _bwd_ff1(df1, x1, dy, w_ff1, small["norm_mlp_pre"], _pair_exchange([g_ff2]))
    g_ff1 = _blocks(_mm_tn(h2, df1, "dw_ff1", shard_cols=FF_SHARD))
    (dmb, dya, dyc, dga, dgc, dat, dua, dg_mix_post, db_pw2, delta), l_ff1 = _bwd_merge(
        dx1, m, sa, sc, ya, yc, attn, w_out, w_o, w_pw2, small["norm_mix_post"], _pair_exchange([g_ff1]))
    g_ff, l_ff = [g_ff1, g_ff2], [l_ff1[0], l_ff2[0]]
    p_ff = _pair_sum(g_ff, l_ff, c_idx, "rs_pair_sum_ff", 2)
    g_mix = [_blocks(_mm_tn(attn, dya, "dw_o")), _blocks(_mm_tn(uact, dyc, "dw_pw2")),
             _blocks(_mm_tn(mb, dmb, "dw_out"))]
    dco, dln_g, dln_b, dconv_b = _bwd_ln(dua, co, small["conv_ln_g"], small["conv_ln_b"])
    (da, db, dconv), l_mix = _conv_bwd(dco, u, a, sb, conv_w, _pair_exchange(g_mix))
    p_mix = _pair_sum(g_mix, l_mix, c_idx, "rs_pair_sum_mix", 1)
    (dq, dk, dv), q_early = _attn_bwd(q, k, v, dat, lse, delta, _chip_exchange(p_ff + p_mix))
    dqp, dkn, dzs, dq_norm, dkv_norm = _bwd_qkv(dq, dk, dv, zs, tc, tsa, tsb, small["q_norm"], small["kv_norm"],
                                                w_uq, w_uk, w_uv)
    dw_in = jnp.concatenate([_mm_tn(h, dzs, "dw_in_zs")[:, :Q_RANK + KV_RANK + ROPE], _mm_tn(h, da, "dw_in_a"),
                             _mm_tn(h, db, "dw_in_b"), _mm_tn(h, dga, "dw_in_ga"), _mm_tn(h, dgc, "dw_in_gc")],
                            axis=1)
    g_late = [_blocks(dw_in.reshape(D_MODEL, N_DEV, IN_SHARD).transpose(1, 0, 2)),
              _blocks(_mm_tn(cqn, dqp, "dw_uq", shard_cols=HEAD_PAD)),
              _blocks(_mm_tn(ckvn, dkn, "dw_uk").reshape(N_DEV, -1, N_HEADS, NOPE)),
              _blocks(_mm_tn(ckvn, dv, "dw_uv").reshape(N_DEV, -1, N_HEADS, NOPE)),
              _blocks(dconv[:CONV_W].reshape(CONV_W, N_DEV, 1, -1).transpose(1, 0, 2, 3).astype(BF16))]
    l_late = _run_exchange(_pair_exchange(g_late), "rs_pair_exchange_late")
    p_late = _pair_sum(g_late, l_late, c_idx, "rs_pair_sum_late", 1)
    (grad_x, dg_pre), q_late = _bwd_in_proj(dzs, da, db, dga, dgc, x, dx1, w_in_p, small["norm_mix_pre"],
                                            _chip_exchange(p_late))

    order = lambda late, mix, ff: list(late[:4]) + [mix[0], late[4], mix[1], mix[2]] + list(ff)
    exchanged = [order(g_late, g_mix, g_ff), order(l_late, l_mix, l_ff),
                 order(q_late, q_early[2:], q_early[:2])]
    small_grads = (dg_pre, dq_norm, dkv_norm, dconv_b, dln_g, dln_b, db_pw2, dg_mix_post, dg_mlp_pre, dg_mlp_post)
    return loss_blk, grad_x, small_grads, exchanged


def kernel(x, positions, norm_mix_pre, w_in, q_norm, w_uq, kv_norm, w_uk, w_uv, w_o_attn, conv_w, conv_b, conv_ln_g, conv_ln_b, w_pw2, b_pw2, w_out, norm_mix_post, norm_mlp_pre, w_ff1, w_ff2, norm_mlp_post, loss_target, m_norm_mix_pre, m_w_in, m_q_norm, m_w_uq, m_kv_norm, m_w_uk, m_w_uv, m_w_o_attn, m_conv_w, m_conv_b, m_conv_ln_g, m_conv_ln_b, m_w_pw2, m_b_pw2, m_w_out, m_norm_mix_post, m_norm_mlp_pre, m_w_ff1, m_w_ff2, m_norm_mlp_post, v_norm_mix_pre, v_w_in, v_q_norm, v_w_uq, v_kv_norm, v_w_uk, v_w_uv, v_w_o_attn, v_conv_w, v_conv_b, v_conv_ln_g, v_conv_ln_b, v_w_pw2, v_b_pw2, v_w_out, v_norm_mix_post, v_norm_mlp_pre, v_w_ff1, v_w_ff2, v_norm_mlp_post):
    wts = dict(norm_mix_pre=norm_mix_pre, w_in=w_in, q_norm=q_norm, w_uq=w_uq, kv_norm=kv_norm, w_uk=w_uk, w_uv=w_uv,
               w_o_attn=w_o_attn, conv_w=conv_w, conv_b=conv_b, conv_ln_g=conv_ln_g, conv_ln_b=conv_ln_b,
               w_pw2=w_pw2, b_pw2=b_pw2, w_out=w_out, norm_mix_post=norm_mix_post, norm_mlp_pre=norm_mlp_pre,
               w_ff1=w_ff1, w_ff2=w_ff2, norm_mlp_post=norm_mlp_post)
    mom_m = dict(norm_mix_pre=m_norm_mix_pre, w_in=m_w_in, q_norm=m_q_norm, w_uq=m_w_uq, kv_norm=m_kv_norm,
                 w_uk=m_w_uk, w_uv=m_w_uv, w_o_attn=m_w_o_attn, conv_w=m_conv_w, conv_b=m_conv_b,
                 conv_ln_g=m_conv_ln_g, conv_ln_b=m_conv_ln_b, w_pw2=m_w_pw2, b_pw2=m_b_pw2, w_out=m_w_out,
                 norm_mix_post=m_norm_mix_post, norm_mlp_pre=m_norm_mlp_pre, w_ff1=m_w_ff1, w_ff2=m_w_ff2,
                 norm_mlp_post=m_norm_mlp_post)
    mom_v = dict(norm_mix_pre=v_norm_mix_pre, w_in=v_w_in, q_norm=v_q_norm, w_uq=v_w_uq, kv_norm=v_kv_norm,
                 w_uk=v_w_uk, w_uv=v_w_uv, w_o_attn=v_w_o_attn, conv_w=v_conv_w, conv_b=v_conv_b,
                 conv_ln_g=v_conv_ln_g, conv_ln_b=v_conv_ln_b, w_pw2=v_w_pw2, b_pw2=v_b_pw2, w_out=v_w_out,
                 norm_mix_post=v_norm_mix_post, norm_mlp_pre=v_norm_mlp_pre, w_ff1=v_w_ff1, w_ff2=v_w_ff2,
                 norm_mlp_post=v_norm_mlp_post)
    cx, cy, cc = lax.axis_index("x"), lax.axis_index("y"), lax.axis_index("c")

    big_local = {n: wts[n] for n in _BIG}
    w_ex = _to_exchange(big_local)
    send = [a.astype(BF16) for a in w_ex]
    conv_i = _BIG.index("conv_w")
    conv_lo = (w_ex[conv_i] - send[conv_i].astype(F32)).astype(BF16)
    send[conv_i] = jnp.stack([send[conv_i], conv_lo])

    small = {n: wts[n].reshape(1, -1) for n in _SMALL}
    c_idx = cc.reshape(1).astype(jnp.int32)
    loss_blk, grad_x, small_grads, (g4, l_sib, q_in) = _step(x[0], positions, loss_target[0], small, send, c_idx)

    idx = jnp.stack([2 * cx + cy, cc]).astype(jnp.int32)
    m_ex, v_ex = _to_exchange({n: mom_m[n] for n in _BIG}), _to_exchange({n: mom_v[n] for n in _BIG})
    upd = [None] * len(_BIG)
    for group, steps in ((("w_in", "w_uq", "w_ff1", "w_ff2"), 4),
                         (("w_uk", "w_uv", "w_o_attn", "conv_w", "w_pw2", "w_out"), 1)):
        ids = [_BIG.index(n) for n in group]
        pick = lambda arrs: [arrs[i] for i in ids]
        res = _update(pick(g4), pick(l_sib), pick(q_in), pick(w_ex), pick(m_ex), pick(v_ex), idx,
                      "update_" + group[0], steps)
        for i, r in zip(ids, res):
            upd[i] = r
    out_g, out_d, out_m, out_v = (_from_exchange([u[j] for u in upd], big_local) for j in range(4))

    loss_row = jnp.broadcast_to(loss_blk[0:1, 0:1], (1, 1024))
    sv = _small_pack(list(small_grads) + [loss_row])
    sv_all = _run_exchange(_ag_exchange([sv]), "ag_small")[0]
    loss_sum, upd_small = _update_small(sv_all, [wts[n] for n in _SMALL], [mom_m[n] for n in _SMALL],
                                        [mom_v[n] for n in _SMALL])
    for n, r in zip(_SMALL, upd_small):
        out_g[n], out_d[n], out_m[n], out_v[n] = r
    loss = loss_sum[0, 0] * (0.5 / D_MODEL)

    return (loss, grad_x[None], *[out_g[n] for n in _WEIGHTS], *[out_d[n] for n in _WEIGHTS],
            *[out_m[n] for n in _WEIGHTS], *[out_v[n] for n in _WEIGHTS])
```

```python
import collections
import functools

import jax
import jax.numpy as jnp
from jax import lax
from jax.experimental import pallas as pl
from jax.experimental.pallas import tpu as pltpu

F32 = jnp.float32
BF16 = jnp.bfloat16

D_MODEL = 1024
N_HEADS = 8
NOPE = 128
ROPE = 64
HALF = ROPE // 2
Q_RANK = 384
KV_RANK = 256
CONV_W = 31
D_FF = 4096
EPS = 1e-6
ROPE_THETA = 10000.0
HEAD_PAD = 256
QK_SCALE = (NOPE + ROPE) ** -0.5
LOG2E = 1.4426950408889634
SUBLANES = 8
N_DEV = 8
FF_SHARD = D_FF // N_DEV
IN_SHARD = 4800 // N_DEV

ZS = Q_RANK + KV_RANK + 128
OFF_A = ZS
OFF_B = OFF_A + D_MODEL
OFF_GA = OFF_B + D_MODEL
OFF_GC = OFF_GA + D_MODEL
D_IN_PAD = OFF_GC + D_MODEL

ADAM_LR = 0.001
ADAM_B1 = 0.9
ADAM_B2 = 0.999
ADAM_EPS = 1e-08
ADAM_WD = 0.01
ADAM_STEP = 10

VMEM_LIMIT = 56 * 1024 * 1024

_SMALL = ("norm_mix_pre", "q_norm", "kv_norm", "conv_b", "conv_ln_g", "conv_ln_b", "b_pw2", "norm_mix_post",
          "norm_mlp_pre", "norm_mlp_post")
SMALL_ROWS = 16
LOSS_ROW = len(_SMALL)

_BIG = ("w_in", "w_uq", "w_uk", "w_uv", "w_o_attn", "conv_w", "w_pw2", "w_out", "w_ff1", "w_ff2")
_WEIGHTS = ("norm_mix_pre", "w_in", "q_norm", "w_uq", "kv_norm", "w_uk", "w_uv", "w_o_attn", "conv_w", "conv_b",
            "conv_ln_g", "conv_ln_b", "w_pw2", "b_pw2", "w_out", "norm_mix_post", "norm_mlp_pre", "w_ff1", "w_ff2",
            "norm_mlp_post")


def _dot(a, b):
    return jnp.dot(a, b, preferred_element_type=F32)


def _dot_nt(a, b):
    return lax.dot_general(a, b, (((1,), (1,)), ((), ())), preferred_element_type=F32)


def _dot_tn(a, b):
    return lax.dot_general(a, b, (((0,), (0,)), ((), ())), preferred_element_type=F32)


def _sigmoid(x):
    return 1.0 / (1.0 + jnp.exp(-x))


def _rms_fwd(x, g):
    r = lax.rsqrt(jnp.mean(x * x, axis=-1, keepdims=True) + EPS)
    return x * r * g


def _rms_bwd(dy, x, g):
    r = lax.rsqrt(jnp.mean(x * x, axis=-1, keepdims=True) + EPS)
    xh = x * r
    gy = dy * g
    dx = r * (gy - xh * jnp.mean(gy * xh, axis=-1, keepdims=True))
    return dx, jnp.sum(dy * xh, axis=0, keepdims=True)


def _rope(q, c, sa, sb):
    n = q.shape[-1]
    return q * c + pltpu.roll(q, n - HALF, 1) * sa + pltpu.roll(q, HALF, 1) * sb


def _rope_bwd(d, c, sa, sb):
    n = d.shape[-1]
    return d * c - pltpu.roll(d, n - HALF, 1) * sa - pltpu.roll(d, HALF, 1) * sb


def _shifted_copies(buf, shifted, tb):
    n = shifted.shape[1]
    for b in range(1, SUBLANES):
        shifted[b - 1] = buf[pl.ds(b, n), :]


def _rows_at(buf, shifted, start, tb):
    a, b = divmod(start, SUBLANES)
    src = buf if b == 0 else shifted.at[b - 1]
    return src[pl.ds(SUBLANES * a, tb), :]


def _params(n_axes=1):
    return pltpu.CompilerParams(dimension_semantics=("arbitrary",) * n_axes, vmem_limit_bytes=VMEM_LIMIT)


def _row_call(body, name, tb, row_ins, full_ins, row_outs, acc_outs, lane_outs=(), exchange=None):
    t = row_ins[0].shape[0]
    in_specs = [pl.BlockSpec((tb, a.shape[1]), lambda i: (i, 0)) for a in row_ins]
    in_specs += [pl.BlockSpec(a.shape, lambda i, nd=a.ndim: (0,) * nd) for a in full_ins]
    out_specs = [pl.BlockSpec((tb, c), lambda i: (i, 0)) for c, _ in row_outs]
    out_specs += [pl.BlockSpec(s, lambda i, nd=len(s): (0,) * nd) for s, _ in acc_outs]
    out_specs += [pl.BlockSpec((n, None, 1, tb), lambda i: (0, i, 0, 0)) for n in lane_outs]
    out_shape = [jax.ShapeDtypeStruct((t, c), dt) for c, dt in row_outs]
    out_shape += [jax.ShapeDtypeStruct(s, dt) for s, dt in acc_outs]
    out_shape += [jax.ShapeDtypeStruct((n, t // tb, 1, tb), F32) for n in lane_outs]
    if exchange is None:
        return pl.pallas_call(
            functools.partial(body), name=name, grid=(t // tb,), in_specs=in_specs, out_specs=out_specs,
            out_shape=out_shape, compiler_params=_params(1),
        )(*row_ins, *full_ins)
    steps = _phase_steps(len(exchange.phases), t // tb)
    fn = _hosted(body, len(in_specs), len(out_specs), exchange, lambda k: pl.program_id(0) == steps[k])
    res = pl.pallas_call(
        fn, name=name, grid=(t // tb,), in_specs=in_specs + [_ANY] * len(exchange.ins),
        out_specs=out_specs + [_ANY] * len(exchange.out_shapes), out_shape=out_shape + exchange.out_shapes,
        scratch_shapes=exchange.scratch, compiler_params=_params(1),
    )(*row_ins, *full_ins, *exchange.ins)
    return res[:len(out_specs)], res[len(out_specs):]


def _acc(ref, val):
    @pl.when(pl.program_id(0) == 0)
    def _():
        ref[...] = jnp.zeros_like(ref)
    ref[...] += val


def _fwd_in_proj(x, g_pre, w_in_p, exchange, tb=256):
    def body(x_ref, g_ref, w_ref, h_ref, zs_ref, a_ref, sb_ref, u_ref, sa_ref, sc_ref):
        hb = _rms_fwd(x_ref[...], g_ref[...]).astype(BF16)
        h_ref[...] = hb
        zs_ref[...] = _dot(hb, w_ref[:, 0:ZS])
        a = _dot(hb, w_ref[:, OFF_A:OFF_B])
        sb = _sigmoid(_dot(hb, w_ref[:, OFF_B:OFF_GA]))
        a_ref[...] = a
        sb_ref[...] = sb
        u_ref[...] = a * sb
        sa_ref[...] = _sigmoid(_dot(hb, w_ref[:, OFF_GA:OFF_GC]))
        sc_ref[...] = _sigmoid(_dot(hb, w_ref[:, OFF_GC:D_IN_PAD]))

    d = D_MODEL
    return _row_call(body, "fwd_in_proj", tb, [x], [g_pre, w_in_p],
                     [(d, BF16), (ZS, F32), (d, F32), (d, F32), (d, F32), (d, F32), (d, F32)], [], exchange=exchange)


def _fwd_qkv(zs, tc, tsa, tsb, q_norm, kv_norm, w_uq_p, w_uk, w_uv, tb=256):
    def body(zs_ref, c_ref, sa_ref, sb_ref, qg_ref, kg_ref, wq_ref, wk_ref, wv_ref,
             cqn_ref, ckvn_ref, q_ref, k_ref, v_ref):
        zs_ = zs_ref[...]
        c, sa, sb = c_ref[...], sa_ref[...], sb_ref[...]
        cqn = _rms_fwd(zs_[:, 0:Q_RANK], qg_ref[...]).astype(BF16)
        cqn_ref[...] = cqn
        for h in range(N_HEADS):
            qh = _dot(cqn, wq_ref[h]) * (QK_SCALE * LOG2E)
            q_ref[:, h * HEAD_PAD:h * HEAD_PAD + NOPE] = qh[:, :NOPE].astype(BF16)
            q_ref[:, h * HEAD_PAD + NOPE:(h + 1) * HEAD_PAD] = _rope(qh[:, NOPE:], c, sa, sb).astype(BF16)
        kr = _rope(zs_[:, Q_RANK + KV_RANK:ZS], c, sa, sb).astype(BF16)
        ckvn = _rms_fwd(zs_[:, Q_RANK:Q_RANK + KV_RANK], kg_ref[...]).astype(BF16)
        ckvn_ref[...] = ckvn
        kn = _dot(ckvn, wk_ref[...]).astype(BF16)
        v_ref[...] = _dot(ckvn, wv_ref[...]).astype(BF16)
        for h in range(N_HEADS):
            k_ref[:, h * HEAD_PAD:h * HEAD_PAD + NOPE] = kn[:, h * NOPE:(h + 1) * NOPE]
            k_ref[:, h * HEAD_PAD + NOPE:(h + 1) * HEAD_PAD] = kr

    hp = N_HEADS * HEAD_PAD
    return _row_call(body, "fwd_qkv", tb, [zs, tc, tsa, tsb], [q_norm, kv_norm, w_uq_p, w_uk, w_uv],
                     [(Q_RANK, BF16), (KV_RANK, BF16), (hp, BF16), (hp, BF16), (D_MODEL, BF16)], [])


def _attn_fwd(q, k, v, exchange, tq=512):
    t = q.shape[0]
    nq = t // tq

    def body(q_ref, k_ref, v_ref, o_ref, lse_ref, m_sc, l_sc, acc_sc, s_sc):
        i = pl.program_id(1)
        m_sc[...] = jnp.full_like(m_sc, -1e30)
        l_sc[...] = jnp.zeros_like(l_sc)
        acc_sc[...] = jnp.zeros_like(acc_sc)
        qb = q_ref[...]

        def rows(j):
            return pl.ds(pl.multiple_of(j * tq, tq), tq)

        def scores(j, slot):
            s_sc[slot] = _dot_nt(k_ref[rows(j), :], qb)

        def update(j, slot, masked):
            st = s_sc[slot]
            if masked:
                key = lax.broadcasted_iota(jnp.int32, (tq, tq), 0)
                qry = lax.broadcasted_iota(jnp.int32, (tq, tq), 1)
                st = jnp.where(key <= qry, st, -1e30)
            m_prev = m_sc[...]
            m_new = jnp.maximum(m_prev, jnp.max(st, axis=0, keepdims=True))
            alpha = jnp.exp2(m_prev - m_new)
            pt = jnp.exp2(st - m_new)
            l_sc[...] = alpha * l_sc[...] + jnp.sum(pt, axis=0, keepdims=True)
            acc_sc[...] = alpha * acc_sc[...] + _dot_tn(v_ref[rows(j), :], pt.astype(BF16))
            m_sc[...] = m_new

        def pair(p, carry):
            scores(2 * p + 1, 1)
            update(2 * p, 0, False)
            scores(2 * p + 2, 0)
            update(2 * p + 1, 1, False)
            return carry

        scores(0, 0)
        lax.fori_loop(0, i // 2, pair, 0)

        @pl.when(i % 2 == 1)
        def _():
            scores(i, 1)
            update(i - 1, 0, False)
            update(i, 1, True)

        @pl.when(i % 2 == 0)
        def _():
            update(i, 0, True)

        l = l_sc[...]
        o_ref[...] = (acc_sc[...] / l).T.astype(BF16)
        lse_ref[...] = m_sc[...] + jnp.log2(l)

    steps = _phase_steps(len(exchange.phases), N_HEADS * nq)
    fn = _hosted(body, 3, 2, exchange, lambda p: pl.program_id(0) * nq + pl.program_id(1) == steps[p])
    res = pl.pallas_call(
        fn, name="attn_fwd", grid=(N_HEADS, nq),
        in_specs=[pl.BlockSpec((tq, HEAD_PAD), lambda h, i: (i, h)),
                  pl.BlockSpec((t, HEAD_PAD), lambda h, i: (0, h)),
                  pl.BlockSpec((t, NOPE), lambda h, i: (0, h))] + [_ANY] * len(exchange.ins),
        out_specs=[pl.BlockSpec((tq, NOPE), lambda h, i: (i, h)),
                   pl.BlockSpec((None, None, 1, tq), lambda h, i: (h, i, 0, 0))] + [_ANY] * len(exchange.out_shapes),
        out_shape=[jax.ShapeDtypeStruct((t, D_MODEL), BF16),
                   jax.ShapeDtypeStruct((N_HEADS, nq, 1, tq), F32)] + exchange.out_shapes,
        scratch_shapes=[pltpu.VMEM((1, tq), F32), pltpu.VMEM((1, tq), F32), pltpu.VMEM((NOPE, tq), F32),
                        pltpu.VMEM((2, tq, tq), F32)] + exchange.scratch,
        compiler_params=_params(2),
    )(q, k, v, *exchange.ins)
    return res[:2], res[2:]


def _conv_fwd(u, conv_w, conv_b, ln_g, ln_b, tb=256):
    t, c = u.shape
    halo = 32

    def body(u_ref, up_ref, w_ref, b_ref, g_ref, be_ref, co_ref, act_ref, buf, shifted):
        i = pl.program_id(0)
        buf[0:halo, :] = jnp.where(i == 0, 0.0, up_ref[...])
        buf[halo:halo + tb, :] = u_ref[...]
        _shifted_copies(buf, shifted, tb)
        acc = jnp.zeros((tb, c), F32)
        for k in range(CONV_W):
            acc = acc + w_ref[k:k + 1, :] * _rows_at(buf, shifted, halo - (CONV_W - 1) + k, tb)
        co = acc + b_ref[...]
        co_ref[...] = co
        mu = jnp.mean(co, axis=-1, keepdims=True)
        xc = co - mu
        r = lax.rsqrt(jnp.mean(xc * xc, axis=-1, keepdims=True) + EPS)
        y = xc * r * g_ref[...] + be_ref[...]
        act_ref[...] = (y * _sigmoid(y)).astype(BF16)

    ratio = tb // halo
    return pl.pallas_call(
        functools.partial(body), name="conv_fwd", grid=(t // tb,),
        in_specs=[pl.BlockSpec((tb, c), lambda i: (i, 0)),
                  pl.BlockSpec((halo, c), lambda i: (jnp.maximum(i * ratio - 1, 0), 0)),
                  pl.BlockSpec(conv_w.shape, lambda i: (0, 0)),
                  pl.BlockSpec((1, c), lambda i: (0, 0)), pl.BlockSpec((1, c), lambda i: (0, 0)),
                  pl.BlockSpec((1, c), lambda i: (0, 0))],
        out_specs=[pl.BlockSpec((tb, c), lambda i: (i, 0)), pl.BlockSpec((tb, c), lambda i: (i, 0))],
        out_shape=[jax.ShapeDtypeStruct((t, c), F32), jax.ShapeDtypeStruct((t, c), BF16)],
        scratch_shapes=[pltpu.VMEM((tb + halo, c), F32), pltpu.VMEM((SUBLANES - 1, tb + halo - SUBLANES, c), F32)],
        compiler_params=_params(1),
    )(u, u, conv_w, conv_b, ln_g, ln_b)


def _fwd_merge(attn, uact, sa, sc, x, w_o, w_pw2, b_pw2, w_out, g_post, tb=256):
    def body(at_ref, ua_ref, sa_ref, sc_ref, x_ref, wo_ref, wp_ref, bp_ref, wout_ref, g_ref,
             ya_ref, yc_ref, mb_ref, m_ref, x1_ref):
        ya = _dot(at_ref[...], wo_ref[...])
        yc = _dot(ua_ref[...], wp_ref[...]) + bp_ref[...]
        ya_ref[...] = ya.astype(BF16)
        yc_ref[...] = yc.astype(BF16)
        mb = (sa_ref[...] * ya + sc_ref[...] * yc).astype(BF16)
        mb_ref[...] = mb
        m = _dot(mb, wout_ref[...])
        m_ref[...] = m
        x1_ref[...] = x_ref[...] + _rms_fwd(m, g_ref[...])

    d = D_MODEL
    return _row_call(body, "fwd_merge", tb, [attn, uact, sa, sc, x], [w_o, w_pw2, b_pw2, w_out, g_post],
                     [(d, BF16), (d, BF16), (d, BF16), (d, F32), (d, F32)], [])


def _fwd_ff1(x1, g, w_ff1, tb=256):
    def body(x1_ref, g_ref, w_ref, h2_ref, r1_ref, act_ref):
        h2 = _rms_fwd(x1_ref[...], g_ref[...]).astype(BF16)
        h2_ref[...] = h2
        for j in range(N_DEV):
            cols = slice(j * FF_SHARD, (j + 1) * FF_SHARD)
            r1 = jnp.maximum(_dot(h2, w_ref[j]), 0.0)
            r1_ref[:, cols] = r1.astype(BF16)
            act_ref[:, cols] = (r1 * r1).astype(BF16)

    return _row_call(body, "fwd_ff1", tb, [x1], [g, w_ff1], [(D_MODEL, BF16), (D_FF, BF16), (D_FF, BF16)], [])


def _fwd_ff2_loss(act, x1, target, w_ff2, g, tb=256):
    def body(act_ref, x1_ref, tg_ref, w_ref, g_ref, f_ref, dy_ref, loss_ref):
        f = _dot(act_ref[...], w_ref[...])
        f_ref[...] = f
        e = x1_ref[...] + _rms_fwd(f, g_ref[...]) - tg_ref[...]
        dy_ref[...] = e * (1.0 / D_MODEL)
        _acc(loss_ref, jnp.sum(e * e))

    return _row_call(body, "fwd_ff2_loss", tb, [act, x1, target], [w_ff2, g],
                     [(D_MODEL, F32), (D_MODEL, F32)], [((8, 128), F32)])


def _bwd_ff2(dy, f, r1, w_ff2, g, tb=256):
    def body(dy_ref, f_ref, r1_ref, w_ref, g_ref, df_ref, df1_ref, dg_ref):
        df, dg = _rms_bwd(dy_ref[...], f_ref[...], g_ref[...])
        _acc(dg_ref, dg)
        dfb = df.astype(BF16)
        df_ref[...] = dfb
        dact = _dot_nt(dfb, w_ref[...])
        df1_ref[...] = (dact * (2.0 * r1_ref[...].astype(F32))).astype(BF16)

    return _row_call(body, "bwd_ff2", tb, [dy, f, r1], [w_ff2, g], [(D_MODEL, BF16), (D_FF, BF16)],
                     [((1, D_MODEL), F32)])


def _bwd_ff1(df1, x1, dy, w_ff1, g, exchange, tb=256):
    def body(df1_ref, x1_ref, dy_ref, w_ref, g_ref, dx1_ref, dg_ref):
        dh2 = _dot_nt(df1_ref[:, 0:FF_SHARD], w_ref[0])
        for j in range(1, N_DEV):
            dh2 = dh2 + _dot_nt(df1_ref[:, j * FF_SHARD:(j + 1) * FF_SHARD], w_ref[j])
        dxn, dg = _rms_bwd(dh2, x1_ref[...], g_ref[...])
        _acc(dg_ref, dg)
        dx1_ref[...] = dy_ref[...] + dxn

    return _row_call(body, "bwd_ff1", tb, [df1, x1, dy], [w_ff1, g], [(D_MODEL, F32)], [((1, D_MODEL), F32)],
                     exchange=exchange)


def _bwd_merge(dx1, m, sa, sc, ya, yc, attn, w_out, w_o, w_pw2, g_post, exchange, tb=256):
    def body(dx1_ref, m_ref, sa_ref, sc_ref, ya_ref, yc_ref, at_ref, wout_ref, wo_ref, wp_ref, g_ref,
             dm_ref, dya_ref, dyc_ref, dga_ref, dgc_ref, dat_ref, dua_ref, dg_ref, dbp_ref, delta_ref):
        dm, dg = _rms_bwd(dx1_ref[...], m_ref[...], g_ref[...])
        _acc(dg_ref, dg)
        dmb = dm.astype(BF16)
        dm_ref[...] = dmb
        dmerged = _dot_nt(dmb, wout_ref[...])
        sa, sc = sa_ref[...], sc_ref[...]
        dya = dmerged * sa
        dyc = dmerged * sc
        _acc(dbp_ref, jnp.sum(dyc, axis=0, keepdims=True))
        dyab = dya.astype(BF16)
        dycb = dyc.astype(BF16)
        dya_ref[...] = dyab
        dyc_ref[...] = dycb
        dga_ref[...] = (dmerged * ya_ref[...].astype(F32) * sa * (1.0 - sa)).astype(BF16)
        dgc_ref[...] = (dmerged * yc_ref[...].astype(F32) * sc * (1.0 - sc)).astype(BF16)
        dat = _dot_nt(dyab, wo_ref[...])
        dat_ref[...] = dat.astype(BF16)
        prod = dat * at_ref[...].astype(F32)
        lane = lax.broadcasted_iota(jnp.int32, (tb, NOPE), 1)
        dl = jnp.zeros((tb, NOPE), F32)
        for h in range(N_HEADS):
            dl = dl + jnp.where(lane == h, jnp.sum(prod[:, h * NOPE:(h + 1) * NOPE], axis=1, keepdims=True), 0.0)
        dlt = dl.T
        for h in range(N_HEADS):
            delta_ref[h] = dlt[h:h + 1, :]
        dua_ref[...] = _dot_nt(dycb, wp_ref[...]).astype(BF16)

    d = D_MODEL
    return _row_call(body, "bwd_merge", tb, [dx1, m, sa, sc, ya, yc, attn], [w_out, w_o, w_pw2, g_post],
                     [(d, BF16), (d, BF16), (d, BF16), (d, BF16), (d, BF16), (d, BF16), (d, BF16)],
                     [((1, d), F32), ((1, d), F32)], lane_outs=(N_HEADS,), exchange=exchange)


def _bwd_ln(dua, co, ln_g, ln_b, tb=256):
    def body(dua_ref, co_ref, g_ref, be_ref, dco_ref, dg_ref, db_ref, dcb_ref):
        co = co_ref[...]
        g = g_ref[...]
        mu = jnp.mean(co, axis=-1, keepdims=True)
        xc = co - mu
        r = lax.rsqrt(jnp.mean(xc * xc, axis=-1, keepdims=True) + EPS)
        xh = xc * r
        y = xh * g + be_ref[...]
        s = _sigmoid(y)
        dy = dua_ref[...].astype(F32) * (s + y * s * (1.0 - s))
        _acc(db_ref, jnp.sum(dy, axis=0, keepdims=True))
        _acc(dg_ref, jnp.sum(dy * xh, axis=0, keepdims=True))
        gy = dy * g
        dco = r * (gy - jnp.mean(gy, axis=-1, keepdims=True) - xh * jnp.mean(gy * xh, axis=-1, keepdims=True))
        dco_ref[...] = dco
        _acc(dcb_ref, jnp.sum(dco, axis=0, keepdims=True))

    d = D_MODEL
    return _row_call(body, "bwd_ln", tb, [dua, co], [ln_g, ln_b], [(d, F32)],
                     [((1, d), F32), ((1, d), F32), ((1, d), F32)])


def _conv_bwd(dco, u, a, sb, conv_w, exchange, tb=256):
    t, c = u.shape
    halo = 32
    ratio = tb // halo
    nblk = t // tb

    def body(d_ref, dn_ref, u_ref, up_ref, a_ref, sb_ref, w_ref, da_ref, db_ref, dw_ref, bufd, bufu, shd, shu):
        i = pl.program_id(0)

        @pl.when(i == 0)
        def _():
            dw_ref[...] = jnp.zeros_like(dw_ref)

        dco = d_ref[...]
        bufd[0:tb, :] = dco
        bufd[tb:tb + halo, :] = jnp.where(i == nblk - 1, 0.0, dn_ref[...])
        bufu[0:halo, :] = jnp.where(i == 0, 0.0, up_ref[...])
        bufu[halo:halo + tb, :] = u_ref[...]
        _shifted_copies(bufd, shd, tb)
        _shifted_copies(bufu, shu, tb)
        du = jnp.zeros((tb, c), F32)
        for k in range(CONV_W):
            du = du + w_ref[k:k + 1, :] * _rows_at(bufd, shd, CONV_W - 1 - k, tb)
            dw_ref[k:k + 1, :] += jnp.sum(dco * _rows_at(bufu, shu, halo - (CONV_W - 1) + k, tb), axis=0,
                                          keepdims=True)
        sb_ = sb_ref[...]
        da_ref[...] = (du * sb_).astype(BF16)
        db_ref[...] = (du * a_ref[...] * sb_ * (1.0 - sb_)).astype(BF16)

    steps = _phase_steps(len(exchange.phases), nblk)
    fn = _hosted(body, 7, 3, exchange, lambda p: pl.program_id(0) == steps[p])
    res = pl.pallas_call(
        fn, name="conv_bwd", grid=(nblk,),
        in_specs=[pl.BlockSpec((tb, c), lambda i: (i, 0)),
                  pl.BlockSpec((halo, c), lambda i: (jnp.minimum((i + 1) * ratio, t // halo - 1), 0)),
                  pl.BlockSpec((tb, c), lambda i: (i, 0)),
                  pl.BlockSpec((halo, c), lambda i: (jnp.maximum(i * ratio - 1, 0), 0)),
                  pl.BlockSpec((tb, c), lambda i: (i, 0)), pl.BlockSpec((tb, c), lambda i: (i, 0)),
                  pl.BlockSpec(conv_w.shape, lambda i: (0, 0))] + [_ANY] * len(exchange.ins),
        out_specs=[pl.BlockSpec((tb, c), lambda i: (i, 0)), pl.BlockSpec((tb, c), lambda i: (i, 0)),
                   pl.BlockSpec((32, c), lambda i: (0, 0))] + [_ANY] * len(exchange.out_shapes),
        out_shape=[jax.ShapeDtypeStruct((t, c), BF16), jax.ShapeDtypeStruct((t, c), BF16),
                   jax.ShapeDtypeStruct((32, c), F32)] + exchange.out_shapes,
        scratch_shapes=[pltpu.VMEM((tb + halo, c), F32), pltpu.VMEM((tb + halo, c), F32),
                        pltpu.VMEM((SUBLANES - 1, tb + halo - SUBLANES, c), F32),
                        pltpu.VMEM((SUBLANES - 1, tb + halo - SUBLANES, c), F32)] + exchange.scratch,
        compiler_params=_params(1),
    )(dco, dco, u, u, a, sb, conv_w, *exchange.ins)
    return res[:3], res[3:]


def _attn_bwd(q, k, v, do, lse2, delta, exchange, tq=512):
    t = q.shape[0]
    nq = t // tq
    td = delta.shape[-1]
    per = tq // td

    def body(q_ref, k_ref, v_ref, do_ref, lse_ref, dl_ref, dq_ref, dk_ref, dv_ref, dk_sc, dv_sc, s_sc, dp_sc, dq_sc):
        j = pl.program_id(1)

        @pl.when(j == 0)
        def _():
            dq_sc[...] = jnp.zeros_like(dq_sc)

        dk_sc[...] = jnp.zeros_like(dk_sc)
        dv_sc[...] = jnp.zeros_like(dv_sc)
        kb, vb = k_ref[...], v_ref[...]

        def rows(i):
            return pl.ds(pl.multiple_of(i * tq, tq), tq)

        def ahead(i, slot):
            i = jnp.minimum(i, nq - 1)
            s_sc[slot] = _dot_nt(kb, q_ref[rows(i), :])
            dp_sc[slot] = _dot_nt(vb, do_ref[rows(i), :])

        def finish(i, slot, masked):
            qb, dob = q_ref[rows(i), :], do_ref[rows(i), :]
            pt = jnp.exp2(s_sc[slot] - lse_ref[i])
            if masked:
                key = lax.broadcasted_iota(jnp.int32, (tq, tq), 0)
                qry = lax.broadcasted_iota(jnp.int32, (tq, tq), 1)
                pt = jnp.where(key <= qry, pt, 0.0)
            dv_sc[...] += _dot(pt.astype(BF16), dob)
            dl = jnp.concatenate([dl_ref[per * i + r] for r in range(per)], axis=-1)
            dst = (pt * (dp_sc[slot] - dl)).astype(BF16)
            dk_sc[...] += _dot(dst, qb)
            dq_sc[rows(i), :] += _dot_tn(dst, kb)

        def pair(p, carry):
            i = j + 1 + 2 * p
            ahead(i + 1, 0)
            finish(i, 1, False)
            ahead(i + 2, 1)
            finish(i + 1, 0, False)
            return carry

        n_after = nq - 1 - j
        ahead(j, 0)
        ahead(j + 1, 1)
        finish(j, 0, True)
        lax.fori_loop(0, n_after // 2, pair, 0)

        @pl.when(n_after % 2 == 1)
        def _():
            finish(nq - 1, 1, False)

        @pl.when(j == nq - 1)
        def _():
            dq_ref[...] = dq_sc[...].astype(BF16)

        dk_ref[...] = (dk_sc[...] * (1.0 / LOG2E)).astype(BF16)
        dv_ref[...] = dv_sc[...].astype(BF16)

    hp = N_HEADS * HEAD_PAD
    steps = _phase_steps(len(exchange.phases), N_HEADS * nq)
    fn = _hosted(body, 6, 3, exchange, lambda p: pl.program_id(0) * nq + pl.program_id(1) == steps[p])
    res = pl.pallas_call(
        fn, name="attn_bwd", grid=(N_HEADS, nq),
        in_specs=[pl.BlockSpec((t, HEAD_PAD), lambda h, j: (0, h)),
                  pl.BlockSpec((tq, HEAD_PAD), lambda h, j: (j, h)),
                  pl.BlockSpec((tq, NOPE), lambda h, j: (j, h)),
                  pl.BlockSpec((t, NOPE), lambda h, j: (0, h)),
                  pl.BlockSpec((None, nq, 1, tq), lambda h, j: (h, 0, 0, 0)),
                  pl.BlockSpec((None, t // td, 1, td), lambda h, j: (h, 0, 0, 0))] + [_ANY] * len(exchange.ins),
        out_specs=[pl.BlockSpec((t, HEAD_PAD), lambda h, j: (0, h)),
                   pl.BlockSpec((tq, HEAD_PAD), lambda h, j: (j, h)),
                   pl.BlockSpec((tq, NOPE), lambda h, j: (j, h))] + [_ANY] * len(exchange.out_shapes),
        out_shape=[jax.ShapeDtypeStruct((t, hp), BF16), jax.ShapeDtypeStruct((t, hp), BF16),
                   jax.ShapeDtypeStruct((t, D_MODEL), BF16)] + exchange.out_shapes,
        scratch_shapes=[pltpu.VMEM((tq, HEAD_PAD), F32), pltpu.VMEM((tq, NOPE), F32), pltpu.VMEM((2, tq, tq), F32),
                        pltpu.VMEM((2, tq, tq), F32), pltpu.VMEM((t, HEAD_PAD), F32)] + exchange.scratch,
        compiler_params=_params(2),
    )(q, k, v, do, lse2, delta, *exchange.ins)
    return res[:3], res[3:]


def _bwd_qkv(dq, dk, dv, zs, tc, tsa, tsb, q_norm, kv_norm, w_uq_p, w_uk, w_uv, tb=256):
    def body(dq_ref, dk_ref, dv_ref, zs_ref, c_ref, sa_ref, sb_ref, qg_ref, kg_ref, wq_ref, wk_ref, wv_ref,
             dqp_ref, dkn_ref, dzs_ref, dqg_ref, dkg_ref):
        c, sa, sb = c_ref[...], sa_ref[...], sb_ref[...]
        zs_ = zs_ref[...]
        dcqn = jnp.zeros((tb, Q_RANK), F32)
        dkr = jnp.zeros((tb, NOPE), F32)
        for h in range(N_HEADS):
            nope = slice(h * HEAD_PAD, h * HEAD_PAD + NOPE)
            rope = slice(h * HEAD_PAD + NOPE, (h + 1) * HEAD_PAD)
            dqp_ref[:, nope] = (dq_ref[:, nope].astype(F32) * QK_SCALE).astype(BF16)
            dqp_ref[:, rope] = (_rope_bwd(dq_ref[:, rope].astype(F32), c, sa, sb) * QK_SCALE).astype(BF16)
            dcqn = dcqn + _dot_nt(dqp_ref[:, h * HEAD_PAD:(h + 1) * HEAD_PAD], wq_ref[h])
            dkn_ref[:, h * NOPE:(h + 1) * NOPE] = dk_ref[:, nope]
            dkr = dkr + dk_ref[:, rope].astype(F32)
        dcq, dqg = _rms_bwd(dcqn, zs_[:, 0:Q_RANK], qg_ref[...])
        _acc(dqg_ref, dqg)
        dzs_ref[:, 0:Q_RANK] = dcq.astype(BF16)
        dzs_ref[:, Q_RANK + KV_RANK:ZS] = _rope_bwd(dkr, c, sa, sb).astype(BF16)
        dckvn = _dot_nt(dkn_ref[...], wk_ref[...]) + _dot_nt(dv_ref[...], wv_ref[...])
        dckv, dkg = _rms_bwd(dckvn, zs_[:, Q_RANK:Q_RANK + KV_RANK], kg_ref[...])
        _acc(dkg_ref, dkg)
        dzs_ref[:, Q_RANK:Q_RANK + KV_RANK] = dckv.astype(BF16)

    hp = N_HEADS * HEAD_PAD
    return _row_call(body, "bwd_qkv", tb, [dq, dk, dv, zs, tc, tsa, tsb], [q_norm, kv_norm, w_uq_p, w_uk, w_uv],
                     [(hp, BF16), (D_MODEL, BF16), (ZS, BF16)], [((1, Q_RANK), F32), ((1, KV_RANK), F32)])


def _bwd_in_proj(dzs, da, db, dga, dgc, x, dx1, w_in_p, g_pre, exchange, tb=256):
    def body(dzs_ref, da_ref, db_ref, dga_ref, dgc_ref, x_ref, dx1_ref, w_ref, g_ref, gx_ref, dg_ref):
        dh = _dot_nt(dzs_ref[...], w_ref[:, 0:ZS])
        dh = dh + _dot_nt(da_ref[...], w_ref[:, OFF_A:OFF_B])
        dh = dh + _dot_nt(db_ref[...], w_ref[:, OFF_B:OFF_GA])
        dh = dh + _dot_nt(dga_ref[...], w_ref[:, OFF_GA:OFF_GC])
        dh = dh + _dot_nt(dgc_ref[...], w_ref[:, OFF_GC:D_IN_PAD])
        dxn, dg = _rms_bwd(dh, x_ref[...], g_ref[...])
        _acc(dg_ref, dg)
        gx_ref[...] = dx1_ref[...] + dxn

    return _row_call(body, "bwd_in_proj", tb, [dzs, da, db, dga, dgc, x, dx1], [w_in_p, g_pre],
                     [(D_MODEL, F32)], [((1, D_MODEL), F32)], exchange=exchange)


def _mm_tn(a, b, name, shard_cols=None, tt=2048):
    t, m = a.shape
    n = b.shape[1]
    tm = min(m, 1024)
    tn = min(n, 1024)
    tt = min(t, tt)
    nt = t // tt
    per = tn // shard_cols if shard_cols else 1

    def body(a_ref, b_ref, o_ref, acc):
        k = pl.program_id(2)

        @pl.when(k == 0)
        def _():
            acc[...] = jnp.zeros_like(acc)

        acc[...] += _dot_tn(a_ref[...], b_ref[...])

        @pl.when(k == nt - 1)
        def _():
            if shard_cols:
                for s in range(per):
                    o_ref[s] = acc[:, s * shard_cols:(s + 1) * shard_cols].astype(BF16)
            else:
                o_ref[...] = acc[...].astype(BF16)

    if shard_cols:
        out_spec = pl.BlockSpec((per, tm, shard_cols), lambda i, j, k: (j, i, 0))
        out_shape = jax.ShapeDtypeStruct((n // shard_cols, m, shard_cols), BF16)
    else:
        out_spec = pl.BlockSpec((tm, tn), lambda i, j, k: (i, j))
        out_shape = jax.ShapeDtypeStruct((m, n), BF16)
    return pl.pallas_call(
        functools.partial(body), name=name, grid=(m // tm, n // tn, nt),
        in_specs=[pl.BlockSpec((tt, tm), lambda i, j, k: (k, i)), pl.BlockSpec((tt, tn), lambda i, j, k: (k, j))],
        out_specs=out_spec, out_shape=out_shape, scratch_shapes=[pltpu.VMEM((tm, tn), F32)],
        compiler_params=_params(3),
    )(a, b)


_ANY = pl.BlockSpec(memory_space=pl.ANY)
_MESH = pl.DeviceIdType.MESH

_Exchange = collections.namedtuple("_Exchange", "ins out_shapes scratch phases")


def _ag_exchange(shards):
    n = len(shards)

    def parts(ins, outs, sems):
        send_sems, recv_sems, _ = sems
        x, y, c = lax.axis_index("x"), lax.axis_index("y"), lax.axis_index("c")
        chips = [(1 - x, y), (x, 1 - y), (1 - x, 1 - y)]

        def copy(w, k, block, to, src=None):
            dst = outs[w].at[4 * block[0] + 2 * block[1] + block[2]]
            return pltpu.make_async_remote_copy(
                src_ref=dst if src is None else src, dst_ref=dst, send_sem=send_sems.at[7 * w + k],
                recv_sem=recv_sems.at[7 * w + k], device_id=to, device_id_type=_MESH)

        def first(w):
            return [copy(w, 0, (x, y, c), (x, y, 1 - c), src=ins[w])] + [
                copy(w, 1 + j, (x, y, c), (*chip, c), src=ins[w]) for j, chip in enumerate(chips)]

        def mine(w):
            return pltpu.make_async_copy(ins[w], outs[w].at[4 * x + 2 * y + c], sems[2].at[w])

        return (x, y, c), chips, copy, first, mine

    def start(ins, outs, sems):
        _, _, _, first, mine = parts(ins, outs, sems)
        for w in range(n):
            mine(w).start()
            for cp in first(w):
                cp.start()

    def forward(ins, outs, sems):
        (x, y, c), chips, copy, _, _ = parts(ins, outs, sems)
        for j, chip in enumerate(chips):
            for w in range(n):
                copy(w, 1 + j, (*chip, c), (x, y, c)).wait_recv()
                copy(w, 4 + j, (*chip, c), (x, y, 1 - c)).start()

    def finish(ins, outs, sems):
        (x, y, c), chips, copy, first, mine = parts(ins, outs, sems)
        for w in range(n):
            copy(w, 0, (x, y, 1 - c), (x, y, c)).wait_recv()
        for j, chip in enumerate(chips):
            for w in range(n):
                copy(w, 4 + j, (*chip, 1 - c), (x, y, c)).wait_recv()
        for w in range(n):
            for cp in first(w) + [copy(w, 4 + j, (*chip, c), (x, y, 1 - c)) for j, chip in enumerate(chips)]:
                cp.wait_send()
            mine(w).wait()

    return _Exchange(
        ins=list(shards), out_shapes=[jax.ShapeDtypeStruct((N_DEV,) + s.shape, s.dtype) for s in shards],
        scratch=[pltpu.SemaphoreType.DMA((7 * n,)), pltpu.SemaphoreType.DMA((7 * n,)), pltpu.SemaphoreType.DMA((n,))],
        phases=[start, forward, finish])


def _pair_exchange(gs):
    n = len(gs)

    def copies(ins, outs, sems):
        x, y, c = lax.axis_index("x"), lax.axis_index("y"), lax.axis_index("c")
        return [pltpu.make_async_remote_copy(
            src_ref=ins[w].at[:, 1 - c], dst_ref=outs[w], send_sem=sems[0].at[w], recv_sem=sems[1].at[w],
            device_id=(x, y, 1 - c), device_id_type=_MESH) for w in range(n)]

    return _start_then_wait(gs, [jax.ShapeDtypeStruct((4,) + g.shape[2:], g.dtype) for g in gs], n, copies)


def _start_then_wait(ins, out_shapes, n_copies, copies):
    def start(ins_, outs, sems):
        for cp in copies(ins_, outs, sems):
            cp.start()

    def finish(ins_, outs, sems):
        for cp in copies(ins_, outs, sems):
            cp.wait()

    return _Exchange(ins=list(ins), out_shapes=out_shapes,
                     scratch=[pltpu.SemaphoreType.DMA((n_copies,)), pltpu.SemaphoreType.DMA((n_copies,))],
                     phases=[start, finish])


def _run_exchange(ex, name):
    ni, no = len(ex.ins), len(ex.out_shapes)

    def body(*refs):
        for phase in ex.phases:
            phase(refs[:ni], refs[ni:ni + no], refs[ni + no:])

    return pl.pallas_call(functools.partial(body), name=name, out_shape=ex.out_shapes, in_specs=[_ANY] * ni,
                          out_specs=[_ANY] * no, scratch_shapes=ex.scratch)(*ex.ins)


def _hosted(body, n_in, n_out, ex, when):
    ni, no, ns = len(ex.ins), len(ex.out_shapes), len(ex.scratch)

    def fn(*refs):
        ins, ex_ins = refs[:n_in], refs[n_in:n_in + ni]
        outs = refs[n_in + ni:n_in + ni + n_out]
        ex_outs = refs[n_in + ni + n_out:n_in + ni + n_out + no]
        scratch, sems = refs[n_in + ni + n_out + no:len(refs) - ns], refs[len(refs) - ns:]
        last = len(ex.phases) - 1
        for k in range(last):
            pl.when(when(k))(functools.partial(ex.phases[k], ex_ins, ex_outs, sems))
        body(*ins, *outs, *scratch)
        pl.when(when(last))(functools.partial(ex.phases[last], ex_ins, ex_outs, sems))

    return fn


def _phase_steps(n_phases, n_steps):
    return [0, n_steps - 1] if n_phases == 2 else [0, 2 * n_steps // 3, n_steps - 1]


def _row_block(shape, steps, lead, pick):
    blk = (None,) * lead + (shape[0] // steps,) + tuple(shape[1:])
    return pl.BlockSpec(blk, lambda *a: tuple(pick(*a)) + (a[-2],) + (0,) * (len(shape) - 1))


def _pair_sum(gs, ls, c_idx, name, steps):
    n = len(gs)

    def body(c_ref, *refs):
        for w in range(n):
            refs[2 * n + w][...] = (refs[w][...].astype(F32) + refs[n + w][...].astype(F32)).astype(BF16)

    shapes = [g.shape[2:] for g in gs]
    return pl.pallas_call(
        functools.partial(body), name=name,
        grid_spec=pltpu.PrefetchScalarGridSpec(
            num_scalar_prefetch=1, grid=(4, steps),
            in_specs=[_row_block(s, steps, 2, lambda k, i, c: (k, c[0])) for s in shapes]
            + [_row_block(s, steps, 1, lambda k, i, c: (k,)) for s in shapes],
            out_specs=[_row_block(s, steps, 1, lambda k, i, c: (k,)) for s in shapes]),
        out_shape=[jax.ShapeDtypeStruct((4,) + tuple(s), BF16) for s in shapes], compiler_params=_params(2),
    )(c_idx, *gs, *ls)


def _chip_exchange(ps):
    n = len(ps)

    def copies(ins, outs, sems):
        x, y, c = lax.axis_index("x"), lax.axis_index("y"), lax.axis_index("c")
        chips = [(1 - x, y), (x, 1 - y), (1 - x, 1 - y)]
        return [pltpu.make_async_remote_copy(
            src_ref=ins[w].at[2 * px + py], dst_ref=outs[w].at[s], send_sem=sems[0].at[3 * w + s],
            recv_sem=sems[1].at[3 * w + s], device_id=(px, py, c), device_id_type=_MESH)
            for w in range(n) for s, (px, py) in enumerate(chips)]

    return _start_then_wait(ps, [jax.ShapeDtypeStruct((3,) + p.shape[1:], p.dtype) for p in ps], 3 * n, copies)


def _adamw(w, g, m, v):
    m2 = ADAM_B1 * m + (1.0 - ADAM_B1) * g
    v2 = ADAM_B2 * v + (1.0 - ADAM_B2) * (g * g)
    m_hat = m2 / (1.0 - ADAM_B1 ** ADAM_STEP)
    v_hat = v2 / (1.0 - ADAM_B2 ** ADAM_STEP)
    delta = -ADAM_LR * (m_hat / (jnp.sqrt(v_hat) + ADAM_EPS) + ADAM_WD * w)
    return delta, m2, v2


def _update(gs, ls, qs, ws, ms, vs, idx, name, steps):
    n = len(gs)

    def body(idx_ref, *refs):
        g, l, q, w, m, v = (refs[k * n:(k + 1) * n] for k in range(6))
        outs = refs[6 * n:]
        for i in range(n):
            gr = g[i][...].astype(F32) + l[i][...].astype(F32)
            gr = gr + q[i][0].astype(F32)
            gr = gr + q[i][1].astype(F32)
            gr = gr + q[i][2].astype(F32)
            outs[4 * i][...] = gr
            outs[4 * i + 1][...], outs[4 * i + 2][...], outs[4 * i + 3][...] = _adamw(w[i][...], gr, m[i][...], v[i][...])

    shapes = [w.shape for w in ws]
    own = [_row_block(s, steps, 0, lambda i, c: ()) for s in shapes]
    res = pl.pallas_call(
        functools.partial(body), name=name,
        grid_spec=pltpu.PrefetchScalarGridSpec(
            num_scalar_prefetch=1, grid=(steps,),
            in_specs=[_row_block(s, steps, 2, lambda i, c: (c[0], c[1])) for s in shapes]
            + [_row_block(s, steps, 1, lambda i, c: (c[0],)) for s in shapes]
            + [pl.BlockSpec((3, s[0] // steps) + tuple(s[1:]), lambda i, c, nd=len(s): (0, i) + (0,) * (nd - 1))
               for s in shapes] + own * 3,
            out_specs=[b for b in own for _ in range(4)]),
        out_shape=[jax.ShapeDtypeStruct(s, F32) for s in shapes for _ in range(4)], compiler_params=_params(1),
    )(idx, *gs, *ls, *qs, *ws, *ms, *vs)
    return [res[4 * i:4 * i + 4] for i in range(n)]


def _update_small(sv_all, ws, ms, vs):
    n = len(ws)

    def body(all_ref, *refs):
        w, m, v = refs[:n], refs[n:2 * n], refs[2 * n:3 * n]
        loss_ref, outs = refs[3 * n], refs[3 * n + 1:]
        total = all_ref[0]
        for dev in range(1, N_DEV):
            total = total + all_ref[dev]
        loss_ref[...] = total[LOSS_ROW:LOSS_ROW + 1, :]
        for i in range(n):
            gr = total[i:i + 1, 0:w[i].shape[1]]
            outs[4 * i][...] = gr
            outs[4 * i + 1][...], outs[4 * i + 2][...], outs[4 * i + 3][...] = _adamw(w[i][...], gr, m[i][...], v[i][...])

    res = pl.pallas_call(
        functools.partial(body), name="update_small",
        out_shape=[jax.ShapeDtypeStruct((1, 1024), F32)]
        + [jax.ShapeDtypeStruct(a.shape, F32) for a in ws for _ in range(4)],
    )(sv_all, *ws, *ms, *vs)
    return res[0], [res[1 + 4 * i:5 + 4 * i] for i in range(n)]


def _to_exchange(shards):
    return [jnp.pad(shards[n][0], ((0, 0), (0, HEAD_PAD - NOPE - ROPE))) if n == "w_uq" else shards[n][0]
            for n in _BIG]


def _from_exchange(arrs, like):
    return {n: (a[:, :NOPE + ROPE] if n == "w_uq" else a).reshape(like[n].shape) for n, a in zip(_BIG, arrs)}


def _padded_w_in(g_in):
    w_in = g_in.transpose(1, 0, 2).reshape(D_MODEL, N_DEV * IN_SHARD)
    kr_end = Q_RANK + KV_RANK + ROPE
    return jnp.concatenate([w_in[:, :kr_end], jnp.zeros((D_MODEL, 128 - ROPE), BF16), w_in[:, kr_end:]], axis=1)


def _full_conv_w(g_conv):
    taps = (g_conv[:, 0].astype(F32) + g_conv[:, 1].astype(F32)).reshape(N_DEV, CONV_W, D_MODEL // N_DEV)
    return jnp.pad(taps.transpose(1, 0, 2).reshape(CONV_W, D_MODEL), ((0, 1), (0, 0)))


def _pad_rows(a, rows):
    return jnp.pad(a, ((0, rows - a.shape[0]), (0, 0)))


def _small_pack(vals):
    rows = [jnp.pad(v.reshape(1, -1), ((0, 0), (0, 1024 - v.size))) for v in vals]
    return _pad_rows(jnp.concatenate(rows, axis=0), SMALL_ROWS)


def _rope_tables(positions):
    inv_freq = ROPE_THETA ** (-jnp.arange(0, ROPE, 2, dtype=F32) / ROPE)
    ang = positions.reshape(-1).astype(F32)[:, None] * inv_freq
    cos, sin = jnp.cos(ang), jnp.sin(ang)
    t = cos.shape[0]
    z32, z64 = jnp.zeros((t, HALF), F32), jnp.zeros((t, HEAD_PAD - NOPE - ROPE), F32)
    tc = jnp.concatenate([cos, cos, z64], axis=1)
    tsa = jnp.concatenate([-sin, z32, z64], axis=1)
    tsb = jnp.concatenate([z32, sin, z64], axis=1)
    return tc, tsa, tsb


def _blocks(dw):
    if dw.ndim == 2:
        dw = dw.reshape(N_DEV, dw.shape[0] // N_DEV, dw.shape[1])
    return dw.reshape((4, 2) + dw.shape[1:])


def _step(x, positions, target, small, send, c_idx):
    s_in, s_uq, s_uk, s_uv, s_o, s_conv, s_pw2, s_out, s_ff1, s_ff2 = send
    tc, tsa, tsb = _rope_tables(positions)

    w_in_p = _padded_w_in(_run_exchange(_ag_exchange([s_in]), "ag_w_in")[0])
    (h, zs, a, sb, u, sa, sc), (w_uq, w_uk, w_uv) = _fwd_in_proj(
        x, small["norm_mix_pre"], w_in_p, _ag_exchange([s_uq, s_uk, s_uv]))
    w_uk, w_uv = w_uk.reshape(KV_RANK, -1), w_uv.reshape(KV_RANK, -1)
    cqn, ckvn, q, k, v = _fwd_qkv(zs, tc, tsa, tsb, small["q_norm"], small["kv_norm"], w_uq, w_uk, w_uv)
    (attn, lse), (w_o, g_conv, w_pw2, w_out, w_ff1, w_ff2) = _attn_fwd(
        q, k, v, _ag_exchange([s_o, s_conv, s_pw2, s_out, s_ff1, s_ff2]))
    w_o, w_pw2, w_out = (w.reshape(D_MODEL, D_MODEL) for w in (w_o, w_pw2, w_out))
    w_ff2, conv_w = w_ff2.reshape(D_FF, D_MODEL), _full_conv_w(g_conv)
    co, uact = _conv_fwd(u, conv_w, small["conv_b"], small["conv_ln_g"], small["conv_ln_b"])
    ya, yc, mb, m, x1 = _fwd_merge(attn, uact, sa, sc, x, w_o, w_pw2, small["b_pw2"], w_out, small["norm_mix_post"])
    h2, r1, act = _fwd_ff1(x1, small["norm_mlp_pre"], w_ff1)
    f, dy, loss_blk = _fwd_ff2_loss(act, x1, target, w_ff2, small["norm_mlp_post"])

    df, df1, dg_mlp_post = _bwd_ff2(dy, f, r1, w_ff2, small["norm_mlp_post"])
    g_ff2 = _blocks(_mm_tn(act, df, "dw_ff2"))
    (dx1, dg_mlp_pre), l_ff2 = _bwd_ff1(df1, x1, dy, w_ff1, small["norm_mlp_pre"], _pair_exchange([g_ff2]))
    g_ff1 = _blocks(_mm_tn(h2, df1, "dw_ff1", shard_cols=FF_SHARD))
    (dmb, dya, dyc, dga, dgc, dat, dua, dg_mix_post, db_pw2, delta), l_ff1 = _bwd_merge(
        dx1, m, sa, sc, ya, yc, attn, w_out, w_o, w_pw2, small["norm_mix_post"], _pair_exchange([g_ff1]))
    g_ff, l_ff = [g_ff1, g_ff2], [l_ff1[0], l_ff2[0]]
    p_ff = _pair_sum(g_ff, l_ff, c_idx, "rs_pair_sum_ff", 2)
    g_mix = [_blocks(_mm_tn(attn, dya, "dw_o")), _blocks(_mm_tn(uact, dyc, "dw_pw2")),
             _blocks(_mm_tn(mb, dmb, "dw_out"))]
    dco, dln_g, dln_b, dconv_b = _bwd_ln(dua, co, small["conv_ln_g"], small["conv_ln_b"])
    (da, db, dconv), l_mix = _conv_bwd(dco, u, a, sb, conv_w, _pair_exchange(g_mix))
    p_mix = _pair_sum(g_mix, l_mix, c_idx, "rs_pair_sum_mix", 1)
    (dq, dk, dv), q_early = _attn_bwd(q, k, v, dat, lse, delta, _chip_exchange(p_ff + p_mix))
    dqp, dkn, dzs, dq_norm, dkv_norm = _bwd_qkv(dq, dk, dv, zs, tc, tsa, tsb, small["q_norm"], small["kv_norm"],
                                                w_uq, w_uk, w_uv)
    dw_in = jnp.concatenate([_mm_tn(h, dzs, "dw_in_zs")[:, :Q_RANK + KV_RANK + ROPE], _mm_tn(h, da, "dw_in_a"),
                             _mm_tn(h, db, "dw_in_b"), _mm_tn(h, dga, "dw_in_ga"), _mm_tn(h, dgc, "dw_in_gc")],
                            axis=1)
    g_late = [_blocks(dw_in.reshape(D_MODEL, N_DEV, IN_SHARD).transpose(1, 0, 2)),
              _blocks(_mm_tn(cqn, dqp, "dw_uq", shard_cols=HEAD_PAD)),
              _blocks(_mm_tn(ckvn, dkn, "dw_uk").reshape(N_DEV, -1, N_HEADS, NOPE)),
              _blocks(_mm_tn(ckvn, dv, "dw_uv").reshape(N_DEV, -1, N_HEADS, NOPE)),
              _blocks(dconv[:CONV_W].reshape(CONV_W, N_DEV, 1, -1).transpose(1, 0, 2, 3).astype(BF16))]
    l_late = _run_exchange(_pair_exchange(g_late), "rs_pair_exchange_late")
    p_late = _pair_sum(g_late, l_late, c_idx, "rs_pair_sum_late", 1)
    (grad_x, dg_pre), q_late = _bwd_in_proj(dzs, da, db, dga, dgc, x, dx1, w_in_p, small["norm_mix_pre"],
                                            _chip_exchange(p_late))

    order = lambda late, mix, ff: list(late[:4]) + [mix[0], late[4], mix[1], mix[2]] + list(ff)
    exchanged = [order(g_late, g_mix, g_ff), order(l_late, l_mix, l_ff),
                 order(q_late, q_early[2:], q_early[:2])]
    small_grads = (dg_pre, dq_norm, dkv_norm, dconv_b, dln_g, dln_b, db_pw2, dg_mix_post, dg_mlp_pre, dg_mlp_post)
    return loss_blk, grad_x, small_grads, exchanged


def kernel(x, positions, norm_mix_pre, w_in, q_norm, w_uq, kv_norm, w_uk, w_uv, w_o_attn, conv_w, conv_b, conv_ln_g, conv_ln_b, w_pw2, b_pw2, w_out, norm_mix_post, norm_mlp_pre, w_ff1, w_ff2, norm_mlp_post, loss_target, m_norm_mix_pre, m_w_in, m_q_norm, m_w_uq, m_kv_norm, m_w_uk, m_w_uv, m_w_o_attn, m_conv_w, m_conv_b, m_conv_ln_g, m_conv_ln_b, m_w_pw2, m_b_pw2, m_w_out, m_norm_mix_post, m_norm_mlp_pre, m_w_ff1, m_w_ff2, m_norm_mlp_post, v_norm_mix_pre, v_w_in, v_q_norm, v_w_uq, v_kv_norm, v_w_uk, v_w_uv, v_w_o_attn, v_conv_w, v_conv_b, v_conv_ln_g, v_conv_ln_b, v_w_pw2, v_b_pw2, v_w_out, v_norm_mix_post, v_norm_mlp_pre, v_w_ff1, v_w_ff2, v_norm_mlp_post):
    wts = dict(norm_mix_pre=norm_mix_pre, w_in=w_in, q_norm=q_norm, w_uq=w_uq, kv_norm=kv_norm, w_uk=w_uk, w_uv=w_uv,
               w_o_attn=w_o_attn, conv_w=conv_w, conv_b=conv_b, conv_ln_g=conv_ln_g, conv_ln_b=conv_ln_b,
               w_pw2=w_pw2, b_pw2=b_pw2, w_out=w_out, norm_mix_post=norm_mix_post, norm_mlp_pre=norm_mlp_pre,
               w_ff1=w_ff1, w_ff2=w_ff2, norm_mlp_post=norm_mlp_post)
    mom_m = dict(norm_mix_pre=m_norm_mix_pre, w_in=m_w_in, q_norm=m_q_norm, w_uq=m_w_uq, kv_norm=m_kv_norm,
                 w_uk=m_w_uk, w_uv=m_w_uv, w_o_attn=m_w_o_attn, conv_w=m_conv_w, conv_b=m_conv_b,
                 conv_ln_g=m_conv_ln_g, conv_ln_b=m_conv_ln_b, w_pw2=m_w_pw2, b_pw2=m_b_pw2, w_out=m_w_out,
                 norm_mix_post=m_norm_mix_post, norm_mlp_pre=m_norm_mlp_pre, w_ff1=m_w_ff1, w_ff2=m_w_ff2,
                 norm_mlp_post=m_norm_mlp_post)
    mom_v = dict(norm_mix_pre=v_norm_mix_pre, w_in=v_w_in, q_norm=v_q_norm, w_uq=v_w_uq, kv_norm=v_kv_norm,
                 w_uk=v_w_uk, w_uv=v_w_uv, w_o_attn=v_w_o_attn, conv_w=v_conv_w, conv_b=v_conv_b,
                 conv_ln_g=v_conv_ln_g, conv_ln_b=v_conv_ln_b, w_pw2=v_w_pw2, b_pw2=v_b_pw2, w_out=v_w_out,
                 norm_mix_post=v_norm_mix_post, norm_mlp_pre=v_norm_mlp_pre, w_ff1=v_w_ff1, w_ff2=v_w_ff2,
                 norm_mlp_post=v_norm_mlp_post)
    cx, cy, cc = lax.axis_index("x"), lax.axis_index("y"), lax.axis_index("c")

    big_local = {n: wts[n] for n in _BIG}
    w_ex = _to_exchange(big_local)
    send = [a.astype(BF16) for a in w_ex]
    conv_i = _BIG.index("conv_w")
    conv_lo = (w_ex[conv_i] - send[conv_i].astype(F32)).astype(BF16)
    send[conv_i] = jnp.stack([send[conv_i], conv_lo])

    small = {n: wts[n].reshape(1, -1) for n in _SMALL}
    c_idx = cc.reshape(1).astype(jnp.int32)
    loss_blk, grad_x, small_grads, (g4, l_sib, q_in) = _step(x[0], positions, loss_target[0], small, send, c_idx)

    idx = jnp.stack([2 * cx + cy, cc]).astype(jnp.int32)
    m_ex, v_ex = _to_exchange({n: mom_m[n] for n in _BIG}), _to_exchange({n: mom_v[n] for n in _BIG})
    upd = [None] * len(_BIG)
    for group, steps in ((("w_in", "w_uq", "w_ff1", "w_ff2"), 4),
                         (("w_uk", "w_uv", "w_o_attn", "conv_w", "w_pw2", "w_out"), 1)):
        ids = [_BIG.index(n) for n in group]
        pick = lambda arrs: [arrs[i] for i in ids]
        res = _update(pick(g4), pick(l_sib), pick(q_in), pick(w_ex), pick(m_ex), pick(v_ex), idx,
                      "update_" + group[0], steps)
        for i, r in zip(ids, res):
            upd[i] = r
    out_g, out_d, out_m, out_v = (_from_exchange([u[j] for u in upd], big_local) for j in range(4))

    loss_row = jnp.broadcast_to(loss_blk[0:1, 0:1], (1, 1024))
    sv = _small_pack(list(small_grads) + [loss_row])
    sv_all = _run_exchange(_ag_exchange([sv]), "ag_small")[0]
    loss_sum, upd_small = _update_small(sv_all, [wts[n] for n in _SMALL], [mom_m[n] for n in _SMALL],
                                        [mom_v[n] for n in _SMALL])
    for n, r in zip(_SMALL, upd_small):
        out_g[n], out_d[n], out_m[n], out_v[n] = r
    loss = loss_sum[0, 0] * (0.5 / D_MODEL)

    return (loss, grad_x[None], *[out_g[n] for n in _WEIGHTS], *[out_d[n] for n in _WEIGHTS],
            *[out_m[n] for n in _WEIGHTS], *[out_v[n] for n in _WEIGHTS])
```

```python
import collections
import functools

import jax
import jax.numpy as jnp
from jax import lax
from jax.experimental import pallas as pl
from jax.experimental.pallas import tpu as pltpu

F32 = jnp.float32
BF16 = jnp.bfloat16

D_MODEL = 1024
N_HEADS = 8
NOPE = 128
ROPE = 64
HALF = ROPE // 2
Q_RANK = 384
KV_RANK = 256
CONV_W = 31
D_FF = 4096
EPS = 1e-6
ROPE_THETA = 10000.0
HEAD_PAD = 256
QK_SCALE = (NOPE + ROPE) ** -0.5
LOG2E = 1.4426950408889634
SUBLANES = 8
N_DEV = 8
FF_SHARD = D_FF // N_DEV
IN_SHARD = 4800 // N_DEV

ZS = Q_RANK + KV_RANK + 128
OFF_A = ZS
OFF_B = OFF_A + D_MODEL
OFF_GA = OFF_B + D_MODEL
OFF_GC = OFF_GA + D_MODEL
D_IN_PAD = OFF_GC + D_MODEL

ADAM_LR = 0.001
ADAM_B1 = 0.9
ADAM_B2 = 0.999
ADAM_EPS = 1e-08
ADAM_WD = 0.01
ADAM_STEP = 10

VMEM_LIMIT = 56 * 1024 * 1024

_SMALL = ("norm_mix_pre", "q_norm", "kv_norm", "conv_b", "conv_ln_g", "conv_ln_b", "b_pw2", "norm_mix_post",
          "norm_mlp_pre", "norm_mlp_post")
SMALL_ROWS = 16
LOSS_ROW = len(_SMALL)

_BIG = ("w_in", "w_uq", "w_uk", "w_uv", "w_o_attn", "conv_w", "w_pw2", "w_out", "w_ff1", "w_ff2")
_WEIGHTS = ("norm_mix_pre", "w_in", "q_norm", "w_uq", "kv_norm", "w_uk", "w_uv", "w_o_attn", "conv_w", "conv_b",
            "conv_ln_g", "conv_ln_b", "w_pw2", "b_pw2", "w_out", "norm_mix_post", "norm_mlp_pre", "w_ff1", "w_ff2",
            "norm_mlp_post")


def _dot(a, b):
    return jnp.dot(a, b, preferred_element_type=F32)


def _dot_nt(a, b):
    return lax.dot_general(a, b, (((1,), (1,)), ((), ())), preferred_element_type=F32)


def _dot_tn(a, b):
    return lax.dot_general(a, b, (((0,), (0,)), ((), ())), preferred_element_type=F32)


def _sigmoid(x):
    return 1.0 / (1.0 + jnp.exp(-x))


def _rms_fwd(x, g):
    r = lax.rsqrt(jnp.mean(x * x, axis=-1, keepdims=True) + EPS)
    return x * r * g


def _rms_bwd(dy, x, g):
    r = lax.rsqrt(jnp.mean(x * x, axis=-1, keepdims=True) + EPS)
    xh = x * r
    gy = dy * g
    dx = r * (gy - xh * jnp.mean(gy * xh, axis=-1, keepdims=True))
    return dx, jnp.sum(dy * xh, axis=0, keepdims=True)


def _rope(q, c, sa, sb):
    n = q.shape[-1]
    return q * c + pltpu.roll(q, n - HALF, 1) * sa + pltpu.roll(q, HALF, 1) * sb


def _rope_bwd(d, c, sa, sb):
    n = d.shape[-1]
    return d * c - pltpu.roll(d, n - HALF, 1) * sa - pltpu.roll(d, HALF, 1) * sb


def _shifted_copies(buf, shifted, tb):
    n = shifted.shape[1]
    for b in range(1, SUBLANES):
        shifted[b - 1] = buf[pl.ds(b, n), :]


def _rows_at(buf, shifted, start, tb):
    a, b = divmod(start, SUBLANES)
    src = buf if b == 0 else shifted.at[b - 1]
    return src[pl.ds(SUBLANES * a, tb), :]


def _params(n_axes=1):
    return pltpu.CompilerParams(dimension_semantics=("arbitrary",) * n_axes, vmem_limit_bytes=VMEM_LIMIT)


def _row_call(body, name, tb, row_ins, full_ins, row_outs, acc_outs, lane_outs=(), exchange=None):
    t = row_ins[0].shape[0]
    in_specs = [pl.BlockSpec((tb, a.shape[1]), lambda i: (i, 0)) for a in row_ins]
    in_specs += [pl.BlockSpec(a.shape, lambda i, nd=a.ndim: (0,) * nd) for a in full_ins]
    out_specs = [pl.BlockSpec((tb, c), lambda i: (i, 0)) for c, _ in row_outs]
    out_specs += [pl.BlockSpec(s, lambda i, nd=len(s): (0,) * nd) for s, _ in acc_outs]
    out_specs += [pl.BlockSpec((n, None, 1, tb), lambda i: (0, i, 0, 0)) for n in lane_outs]
    out_shape = [jax.ShapeDtypeStruct((t, c), dt) for c, dt in row_outs]
    out_shape += [jax.ShapeDtypeStruct(s, dt) for s, dt in acc_outs]
    out_shape += [jax.ShapeDtypeStruct((n, t // tb, 1, tb), F32) for n in lane_outs]
    if exchange is None:
        return pl.pallas_call(
            functools.partial(body), name=name, grid=(t // tb,), in_specs=in_specs, out_specs=out_specs,
            out_shape=out_shape, compiler_params=_params(1),
        )(*row_ins, *full_ins)
    steps = _phase_steps(len(exchange.phases), t // tb)
    fn = _hosted(body, len(in_specs), len(out_specs), exchange, lambda k: pl.program_id(0) == steps[k])
    res = pl.pallas_call(
        fn, name=name, grid=(t // tb,), in_specs=in_specs + [_ANY] * len(exchange.ins),
        out_specs=out_specs + [_ANY] * len(exchange.out_shapes), out_shape=out_shape + exchange.out_shapes,
        scratch_shapes=exchange.scratch, compiler_params=_params(1),
    )(*row_ins, *full_ins, *exchange.ins)
    return res[:len(out_specs)], res[len(out_specs):]


def _acc(ref, val):
    @pl.when(pl.program_id(0) == 0)
    def _():
        ref[...] = jnp.zeros_like(ref)
    ref[...] += val


def _fwd_in_proj(x, g_pre, w_in_p, exchange, tb=512):
    def body(x_ref, g_ref, w_ref, h_ref, zs_ref, a_ref, sb_ref, u_ref, sa_ref, sc_ref):
        hb = _rms_fwd(x_ref[...], g_ref[...]).astype(BF16)
        h_ref[...] = hb
        zs_ref[...] = _dot(hb, w_ref[:, 0:ZS])
        a = _dot(hb, w_ref[:, OFF_A:OFF_B])
        sb = _sigmoid(_dot(hb, w_ref[:, OFF_B:OFF_GA]))
        a_ref[...] = a
        sb_ref[...] = sb
        u_ref[...] = a * sb
        sa_ref[...] = _sigmoid(_dot(hb, w_ref[:, OFF_GA:OFF_GC]))
        sc_ref[...] = _sigmoid(_dot(hb, w_ref[:, OFF_GC:D_IN_PAD]))

    d = D_MODEL
    return _row_call(body, "fwd_in_proj", tb, [x], [g_pre, w_in_p],
                     [(d, BF16), (ZS, F32), (d, F32), (d, F32), (d, F32), (d, F32), (d, F32)], [], exchange=exchange)


def _fwd_qkv(zs, tc, tsa, tsb, q_norm, kv_norm, w_uq_p, w_uk, w_uv, tb=256):
    def body(zs_ref, c_ref, sa_ref, sb_ref, qg_ref, kg_ref, wq_ref, wk_ref, wv_ref,
             cqn_ref, ckvn_ref, q_ref, k_ref, v_ref):
        zs_ = zs_ref[...]
        c, sa, sb = c_ref[...], sa_ref[...], sb_ref[...]
        cqn = _rms_fwd(zs_[:, 0:Q_RANK], qg_ref[...]).astype(BF16)
        cqn_ref[...] = cqn
        for h in range(N_HEADS):
            qh = _dot(cqn, wq_ref[h]) * (QK_SCALE * LOG2E)
            q_ref[:, h * HEAD_PAD:h * HEAD_PAD + NOPE] = qh[:, :NOPE].astype(BF16)
            q_ref[:, h * HEAD_PAD + NOPE:(h + 1) * HEAD_PAD] = _rope(qh[:, NOPE:], c, sa, sb).astype(BF16)
        kr = _rope(zs_[:, Q_RANK + KV_RANK:ZS], c, sa, sb).astype(BF16)
        ckvn = _rms_fwd(zs_[:, Q_RANK:Q_RANK + KV_RANK], kg_ref[...]).astype(BF16)
        ckvn_ref[...] = ckvn
        kn = _dot(ckvn, wk_ref[...]).astype(BF16)
        v_ref[...] = _dot(ckvn, wv_ref[...]).astype(BF16)
        for h in range(N_HEADS):
            k_ref[:, h * HEAD_PAD:h * HEAD_PAD + NOPE] = kn[:, h * NOPE:(h + 1) * NOPE]
            k_ref[:, h * HEAD_PAD + NOPE:(h + 1) * HEAD_PAD] = kr

    hp = N_HEADS * HEAD_PAD
    return _row_call(body, "fwd_qkv", tb, [zs, tc, tsa, tsb], [q_norm, kv_norm, w_uq_p, w_uk, w_uv],
                     [(Q_RANK, BF16), (KV_RANK, BF16), (hp, BF16), (hp, BF16), (D_MODEL, BF16)], [])


def _attn_fwd(q, k, v, exchange, tq=512):
    t = q.shape[0]
    nq = t // tq

    def body(q_ref, k_ref, v_ref, o_ref, lse_ref, m_sc, l_sc, acc_sc, s_sc):
        i = pl.program_id(1)
        m_sc[...] = jnp.full_like(m_sc, -1e30)
        l_sc[...] = jnp.zeros_like(l_sc)
        acc_sc[...] = jnp.zeros_like(acc_sc)
        qb = q_ref[...]

        def rows(j):
            return pl.ds(pl.multiple_of(j * tq, tq), tq)

        def scores(j, slot):
            s_sc[slot] = _dot_nt(k_ref[rows(j), :], qb)

        def update(j, slot, masked):
            st = s_sc[slot]
            if masked:
                key = lax.broadcasted_iota(jnp.int32, (tq, tq), 0)
                qry = lax.broadcasted_iota(jnp.int32, (tq, tq), 1)
                st = jnp.where(key <= qry, st, -1e30)
            m_prev = m_sc[...]
            m_new = jnp.maximum(m_prev, jnp.max(st, axis=0, keepdims=True))
            alpha = jnp.exp2(m_prev - m_new)
            pt = jnp.exp2(st - m_new)
            l_sc[...] = alpha * l_sc[...] + jnp.sum(pt, axis=0, keepdims=True)
            acc_sc[...] = alpha * acc_sc[...] + _dot_tn(v_ref[rows(j), :], pt.astype(BF16))
            m_sc[...] = m_new

        def pair(p, carry):
            scores(2 * p + 1, 1)
            update(2 * p, 0, False)
            scores(2 * p + 2, 0)
            update(2 * p + 1, 1, False)
            return carry

        scores(0, 0)
        lax.fori_loop(0, i // 2, pair, 0)

        @pl.when(i % 2 == 1)
        def _():
            scores(i, 1)
            update(i - 1, 0, False)
            update(i, 1, True)

        @pl.when(i % 2 == 0)
        def _():
            update(i, 0, True)

        l = l_sc[...]
        o_ref[...] = (acc_sc[...] / l).T.astype(BF16)
        lse_ref[...] = m_sc[...] + jnp.log2(l)

    steps = _phase_steps(len(exchange.phases), N_HEADS * nq)
    fn = _hosted(body, 3, 2, exchange, lambda p: pl.program_id(0) * nq + pl.program_id(1) == steps[p])
    res = pl.pallas_call(
        fn, name="attn_fwd", grid=(N_HEADS, nq),
        in_specs=[pl.BlockSpec((tq, HEAD_PAD), lambda h, i: (i, h)),
                  pl.BlockSpec((t, HEAD_PAD), lambda h, i: (0, h)),
                  pl.BlockSpec((t, NOPE), lambda h, i: (0, h))] + [_ANY] * len(exchange.ins),
        out_specs=[pl.BlockSpec((tq, NOPE), lambda h, i: (i, h)),
                   pl.BlockSpec((None, None, 1, tq), lambda h, i: (h, i, 0, 0))] + [_ANY] * len(exchange.out_shapes),
        out_shape=[jax.ShapeDtypeStruct((t, D_MODEL), BF16),
                   jax.ShapeDtypeStruct((N_HEADS, nq, 1, tq), F32)] + exchange.out_shapes,
        scratch_shapes=[pltpu.VMEM((1, tq), F32), pltpu.VMEM((1, tq), F32), pltpu.VMEM((NOPE, tq), F32),
                        pltpu.VMEM((2, tq, tq), F32)] + exchange.scratch,
        compiler_params=_params(2),
    )(q, k, v, *exchange.ins)
    return res[:2], res[2:]


def _conv_fwd(u, conv_w, conv_b, ln_g, ln_b, tb=256):
    t, c = u.shape
    halo = 32

    def body(u_ref, up_ref, w_ref, b_ref, g_ref, be_ref, co_ref, act_ref, buf, shifted):
        i = pl.program_id(0)
        buf[0:halo, :] = jnp.where(i == 0, 0.0, up_ref[...])
        buf[halo:halo + tb, :] = u_ref[...]
        _shifted_copies(buf, shifted, tb)
        acc = jnp.zeros((tb, c), F32)
        for k in range(CONV_W):
            acc = acc + w_ref[k:k + 1, :] * _rows_at(buf, shifted, halo - (CONV_W - 1) + k, tb)
        co = acc + b_ref[...]
        co_ref[...] = co
        mu = jnp.mean(co, axis=-1, keepdims=True)
        xc = co - mu
        r = lax.rsqrt(jnp.mean(xc * xc, axis=-1, keepdims=True) + EPS)
        y = xc * r * g_ref[...] + be_ref[...]
        act_ref[...] = (y * _sigmoid(y)).astype(BF16)

    ratio = tb // halo
    return pl.pallas_call(
        functools.partial(body), name="conv_fwd", grid=(t // tb,),
        in_specs=[pl.BlockSpec((tb, c), lambda i: (i, 0)),
                  pl.BlockSpec((halo, c), lambda i: (jnp.maximum(i * ratio - 1, 0), 0)),
                  pl.BlockSpec(conv_w.shape, lambda i: (0, 0)),
                  pl.BlockSpec((1, c), lambda i: (0, 0)), pl.BlockSpec((1, c), lambda i: (0, 0)),
                  pl.BlockSpec((1, c), lambda i: (0, 0))],
        out_specs=[pl.BlockSpec((tb, c), lambda i: (i, 0)), pl.BlockSpec((tb, c), lambda i: (i, 0))],
        out_shape=[jax.ShapeDtypeStruct((t, c), F32), jax.ShapeDtypeStruct((t, c), BF16)],
        scratch_shapes=[pltpu.VMEM((tb + halo, c), F32), pltpu.VMEM((SUBLANES - 1, tb + halo - SUBLANES, c), F32)],
        compiler_params=_params(1),
    )(u, u, conv_w, conv_b, ln_g, ln_b)


def _fwd_merge(attn, uact, sa, sc, x, w_o, w_pw2, b_pw2, w_out, g_post, tb=512):
    def body(at_ref, ua_ref, sa_ref, sc_ref, x_ref, wo_ref, wp_ref, bp_ref, wout_ref, g_ref,
             ya_ref, yc_ref, mb_ref, m_ref, x1_ref):
        ya = _dot(at_ref[...], wo_ref[...])
        yc = _dot(ua_ref[...], wp_ref[...]) + bp_ref[...]
        ya_ref[...] = ya.astype(BF16)
        yc_ref[...] = yc.astype(BF16)
        mb = (sa_ref[...] * ya + sc_ref[...] * yc).astype(BF16)
        mb_ref[...] = mb
        m = _dot(mb, wout_ref[...])
        m_ref[...] = m
        x1_ref[...] = x_ref[...] + _rms_fwd(m, g_ref[...])

    d = D_MODEL
    return _row_call(body, "fwd_merge", tb, [attn, uact, sa, sc, x], [w_o, w_pw2, b_pw2, w_out, g_post],
                     [(d, BF16), (d, BF16), (d, BF16), (d, F32), (d, F32)], [])


def _fwd_ff1(x1, g, w_ff1, tb=512):
    def body(x1_ref, g_ref, w_ref, h2_ref, r1_ref, act_ref):
        h2 = _rms_fwd(x1_ref[...], g_ref[...]).astype(BF16)
        h2_ref[...] = h2
        for j in range(N_DEV):
            cols = slice(j * FF_SHARD, (j + 1) * FF_SHARD)
            r1 = jnp.maximum(_dot(h2, w_ref[j]), 0.0)
            r1_ref[:, cols] = r1.astype(BF16)
            act_ref[:, cols] = (r1 * r1).astype(BF16)

    return _row_call(body, "fwd_ff1", tb, [x1], [g, w_ff1], [(D_MODEL, BF16), (D_FF, BF16), (D_FF, BF16)], [])


def _fwd_ff2_loss(act, x1, target, w_ff2, g, tb=512):
    def body(act_ref, x1_ref, tg_ref, w_ref, g_ref, f_ref, dy_ref, loss_ref):
        f = _dot(act_ref[...], w_ref[...])
        f_ref[...] = f
        e = x1_ref[...] + _rms_fwd(f, g_ref[...]) - tg_ref[...]
        dy_ref[...] = e * (1.0 / D_MODEL)
        _acc(loss_ref, jnp.sum(e * e))

    return _row_call(body, "fwd_ff2_loss", tb, [act, x1, target], [w_ff2, g],
                     [(D_MODEL, F32), (D_MODEL, F32)], [((8, 128), F32)])


def _bwd_ff2(dy, f, r1, w_ff2, g, tb=512):
    def body(dy_ref, f_ref, r1_ref, w_ref, g_ref, df_ref, df1_ref, dg_ref):
        df, dg = _rms_bwd(dy_ref[...], f_ref[...], g_ref[...])
        _acc(dg_ref, dg)
        dfb = df.astype(BF16)
        df_ref[...] = dfb
        dact = _dot_nt(dfb, w_ref[...])
        df1_ref[...] = (dact * (2.0 * r1_ref[...].astype(F32))).astype(BF16)

    return _row_call(body, "bwd_ff2", tb, [dy, f, r1], [w_ff2, g], [(D_MODEL, BF16), (D_FF, BF16)],
                     [((1, D_MODEL), F32)])


def _bwd_ff1(df1, x1, dy, w_ff1, g, exchange, tb=512):
    def body(df1_ref, x1_ref, dy_ref, w_ref, g_ref, dx1_ref, dg_ref):
        dh2 = _dot_nt(df1_ref[:, 0:FF_SHARD], w_ref[0])
        for j in range(1, N_DEV):
            dh2 = dh2 + _dot_nt(df1_ref[:, j * FF_SHARD:(j + 1) * FF_SHARD], w_ref[j])
        dxn, dg = _rms_bwd(dh2, x1_ref[...], g_ref[...])
        _acc(dg_ref, dg)
        dx1_ref[...] = dy_ref[...] + dxn

    return _row_call(body, "bwd_ff1", tb, [df1, x1, dy], [w_ff1, g], [(D_MODEL, F32)], [((1, D_MODEL), F32)],
                     exchange=exchange)


def _bwd_merge(dx1, m, sa, sc, ya, yc, attn, w_out, w_o, w_pw2, g_post, exchange, tb=256):
    def body(dx1_ref, m_ref, sa_ref, sc_ref, ya_ref, yc_ref, at_ref, wout_ref, wo_ref, wp_ref, g_ref,
             dm_ref, dya_ref, dyc_ref, dga_ref, dgc_ref, dat_ref, dua_ref, dg_ref, dbp_ref, delta_ref):
        dm, dg = _rms_bwd(dx1_ref[...], m_ref[...], g_ref[...])
        _acc(dg_ref, dg)
        dmb = dm.astype(BF16)
        dm_ref[...] = dmb
        dmerged = _dot_nt(dmb, wout_ref[...])
        sa, sc = sa_ref[...], sc_ref[...]
        dya = dmerged * sa
        dyc = dmerged * sc
        _acc(dbp_ref, jnp.sum(dyc, axis=0, keepdims=True))
        dyab = dya.astype(BF16)
        dycb = dyc.astype(BF16)
        dya_ref[...] = dyab
        dyc_ref[...] = dycb
        dga_ref[...] = (dmerged * ya_ref[...].astype(F32) * sa * (1.0 - sa)).astype(BF16)
        dgc_ref[...] = (dmerged * yc_ref[...].astype(F32) * sc * (1.0 - sc)).astype(BF16)
        dat = _dot_nt(dyab, wo_ref[...])
        dat_ref[...] = dat.astype(BF16)
        prod = dat * at_ref[...].astype(F32)
        lane = lax.broadcasted_iota(jnp.int32, (tb, NOPE), 1)
        dl = jnp.zeros((tb, NOPE), F32)
        for h in range(N_HEADS):
            dl = dl + jnp.where(lane == h, jnp.sum(prod[:, h * NOPE:(h + 1) * NOPE], axis=1, keepdims=True), 0.0)
        dlt = dl.T
        for h in range(N_HEADS):
            delta_ref[h] = dlt[h:h + 1, :]
        dua_ref[...] = _dot_nt(dycb, wp_ref[...]).astype(BF16)

    d = D_MODEL
    return _row_call(body, "bwd_merge", tb, [dx1, m, sa, sc, ya, yc, attn], [w_out, w_o, w_pw2, g_post],
                     [(d, BF16), (d, BF16), (d, BF16), (d, BF16), (d, BF16), (d, BF16), (d, BF16)],
                     [((1, d), F32), ((1, d), F32)], lane_outs=(N_HEADS,), exchange=exchange)


def _bwd_ln(dua, co, ln_g, ln_b, tb=256):
    def body(dua_ref, co_ref, g_ref, be_ref, dco_ref, dg_ref, db_ref, dcb_ref):
        co = co_ref[...]
        g = g_ref[...]
        mu = jnp.mean(co, axis=-1, keepdims=True)
        xc = co - mu
        r = lax.rsqrt(jnp.mean(xc * xc, axis=-1, keepdims=True) + EPS)
        xh = xc * r
        y = xh * g + be_ref[...]
        s = _sigmoid(y)
        dy = dua_ref[...].astype(F32) * (s + y * s * (1.0 - s))
        _acc(db_ref, jnp.sum(dy, axis=0, keepdims=True))
        _acc(dg_ref, jnp.sum(dy * xh, axis=0, keepdims=True))
        gy = dy * g
        dco = r * (gy - jnp.mean(gy, axis=-1, keepdims=True) - xh * jnp.mean(gy * xh, axis=-1, keepdims=True))
        dco_ref[...] = dco
        _acc(dcb_ref, jnp.sum(dco, axis=0, keepdims=True))

    d = D_MODEL
    return _row_call(body, "bwd_ln", tb, [dua, co], [ln_g, ln_b], [(d, F32)],
                     [((1, d), F32), ((1, d), F32), ((1, d), F32)])


def _conv_bwd(dco, u, a, sb, conv_w, exchange, tb=256):
    t, c = u.shape
    halo = 32
    ratio = tb // halo
    nblk = t // tb

    def body(d_ref, dn_ref, u_ref, up_ref, a_ref, sb_ref, w_ref, da_ref, db_ref, dw_ref, bufd, bufu, shd, shu):
        i = pl.program_id(0)

        @pl.when(i == 0)
        def _():
            dw_ref[...] = jnp.zeros_like(dw_ref)

        dco = d_ref[...]
        bufd[0:tb, :] = dco
        bufd[tb:tb + halo, :] = jnp.where(i == nblk - 1, 0.0, dn_ref[...])
        bufu[0:halo, :] = jnp.where(i == 0, 0.0, up_ref[...])
        bufu[halo:halo + tb, :] = u_ref[...]
        _shifted_copies(bufd, shd, tb)
        _shifted_copies(bufu, shu, tb)
        du = jnp.zeros((tb, c), F32)
        for k in range(CONV_W):
            du = du + w_ref[k:k + 1, :] * _rows_at(bufd, shd, CONV_W - 1 - k, tb)
            dw_ref[k:k + 1, :] += jnp.sum(dco * _rows_at(bufu, shu, halo - (CONV_W - 1) + k, tb), axis=0,
                                          keepdims=True)
        sb_ = sb_ref[...]
        da_ref[...] = (du * sb_).astype(BF16)
        db_ref[...] = (du * a_ref[...] * sb_ * (1.0 - sb_)).astype(BF16)

    steps = _phase_steps(len(exchange.phases), nblk)
    fn = _hosted(body, 7, 3, exchange, lambda p: pl.program_id(0) == steps[p])
    res = pl.pallas_call(
        fn, name="conv_bwd", grid=(nblk,),
        in_specs=[pl.BlockSpec((tb, c), lambda i: (i, 0)),
                  pl.BlockSpec((halo, c), lambda i: (jnp.minimum((i + 1) * ratio, t // halo - 1), 0)),
                  pl.BlockSpec((tb, c), lambda i: (i, 0)),
                  pl.BlockSpec((halo, c), lambda i: (jnp.maximum(i * ratio - 1, 0), 0)),
                  pl.BlockSpec((tb, c), lambda i: (i, 0)), pl.BlockSpec((tb, c), lambda i: (i, 0)),
                  pl.BlockSpec(conv_w.shape, lambda i: (0, 0))] + [_ANY] * len(exchange.ins),
        out_specs=[pl.BlockSpec((tb, c), lambda i: (i, 0)), pl.BlockSpec((tb, c), lambda i: (i, 0)),
                   pl.BlockSpec((32, c), lambda i: (0, 0))] + [_ANY] * len(exchange.out_shapes),
        out_shape=[jax.ShapeDtypeStruct((t, c), BF16), jax.ShapeDtypeStruct((t, c), BF16),
                   jax.ShapeDtypeStruct((32, c), F32)] + exchange.out_shapes,
        scratch_shapes=[pltpu.VMEM((tb + halo, c), F32), pltpu.VMEM((tb + halo, c), F32),
                        pltpu.VMEM((SUBLANES - 1, tb + halo - SUBLANES, c), F32),
                        pltpu.VMEM((SUBLANES - 1, tb + halo - SUBLANES, c), F32)] + exchange.scratch,
        compiler_params=_params(1),
    )(dco, dco, u, u, a, sb, conv_w, *exchange.ins)
    return res[:3], res[3:]


def _attn_bwd(q, k, v, do, lse2, delta, exchange, tq=512):
    t = q.shape[0]
    nq = t // tq
    td = delta.shape[-1]
    per = tq // td

    def body(q_ref, k_ref, v_ref, do_ref, lse_ref, dl_ref, dq_ref, dk_ref, dv_ref, dk_sc, dv_sc, s_sc, dp_sc, dq_sc):
        j = pl.program_id(1)

        @pl.when(j == 0)
        def _():
            dq_sc[...] = jnp.zeros_like(dq_sc)

        dk_sc[...] = jnp.zeros_like(dk_sc)
        dv_sc[...] = jnp.zeros_like(dv_sc)
        kb, vb = k_ref[...], v_ref[...]

        def rows(i):
            return pl.ds(pl.multiple_of(i * tq, tq), tq)

        def ahead(i, slot):
            i = jnp.minimum(i, nq - 1)
            s_sc[slot] = _dot_nt(kb, q_ref[rows(i), :])
            dp_sc[slot] = _dot_nt(vb, do_ref[rows(i), :])

        def finish(i, slot, masked):
            qb, dob = q_ref[rows(i), :], do_ref[rows(i), :]
            pt = jnp.exp2(s_sc[slot] - lse_ref[i])
            if masked:
                key = lax.broadcasted_iota(jnp.int32, (tq, tq), 0)
                qry = lax.broadcasted_iota(jnp.int32, (tq, tq), 1)
                pt = jnp.where(key <= qry, pt, 0.0)
            dv_sc[...] += _dot(pt.astype(BF16), dob)
            dl = jnp.concatenate([dl_ref[per * i + r] for r in range(per)], axis=-1)
            dst = (pt * (dp_sc[slot] - dl)).astype(BF16)
            dk_sc[...] += _dot(dst, qb)
            dq_sc[rows(i), :] += _dot_tn(dst, kb)

        def pair(p, carry):
            i = j + 1 + 2 * p
            ahead(i + 1, 0)
            finish(i, 1, False)
            ahead(i + 2, 1)
            finish(i + 1, 0, False)
            return carry

        n_after = nq - 1 - j
        ahead(j, 0)
        ahead(j + 1, 1)
        finish(j, 0, True)
        lax.fori_loop(0, n_after // 2, pair, 0)

        @pl.when(n_after % 2 == 1)
        def _():
            finish(nq - 1, 1, False)

        @pl.when(j == nq - 1)
        def _():
            dq_ref[...] = dq_sc[...].astype(BF16)

        dk_ref[...] = (dk_sc[...] * (1.0 / LOG2E)).astype(BF16)
        dv_ref[...] = dv_sc[...].astype(BF16)

    hp = N_HEADS * HEAD_PAD
    steps = _phase_steps(len(exchange.phases), N_HEADS * nq)
    fn = _hosted(body, 6, 3, exchange, lambda p: pl.program_id(0) * nq + pl.program_id(1) == steps[p])
    res = pl.pallas_call(
        fn, name="attn_bwd", grid=(N_HEADS, nq),
        in_specs=[pl.BlockSpec((t, HEAD_PAD), lambda h, j: (0, h)),
                  pl.BlockSpec((tq, HEAD_PAD), lambda h, j: (j, h)),
                  pl.BlockSpec((tq, NOPE), lambda h, j: (j, h)),
                  pl.BlockSpec((t, NOPE), lambda h, j: (0, h)),
                  pl.BlockSpec((None, nq, 1, tq), lambda h, j: (h, 0, 0, 0)),
                  pl.BlockSpec((None, t // td, 1, td), lambda h, j: (h, 0, 0, 0))] + [_ANY] * len(exchange.ins),
        out_specs=[pl.BlockSpec((t, HEAD_PAD), lambda h, j: (0, h)),
                   pl.BlockSpec((tq, HEAD_PAD), lambda h, j: (j, h)),
                   pl.BlockSpec((tq, NOPE), lambda h, j: (j, h))] + [_ANY] * len(exchange.out_shapes),
        out_shape=[jax.ShapeDtypeStruct((t, hp), BF16), jax.ShapeDtypeStruct((t, hp), BF16),
                   jax.ShapeDtypeStruct((t, D_MODEL), BF16)] + exchange.out_shapes,
        scratch_shapes=[pltpu.VMEM((tq, HEAD_PAD), F32), pltpu.VMEM((tq, NOPE), F32), pltpu.VMEM((2, tq, tq), F32),
                        pltpu.VMEM((2, tq, tq), F32), pltpu.VMEM((t, HEAD_PAD), F32)] + exchange.scratch,
        compiler_params=_params(2),
    )(q, k, v, do, lse2, delta, *exchange.ins)
    return res[:3], res[3:]


def _bwd_qkv(dq, dk, dv, zs, tc, tsa, tsb, q_norm, kv_norm, w_uq_p, w_uk, w_uv, tb=256):
    def body(dq_ref, dk_ref, dv_ref, zs_ref, c_ref, sa_ref, sb_ref, qg_ref, kg_ref, wq_ref, wk_ref, wv_ref,
             dqp_ref, dkn_ref, dzs_ref, dqg_ref, dkg_ref):
        c, sa, sb = c_ref[...], sa_ref[...], sb_ref[...]
        zs_ = zs_ref[...]
        dcqn = jnp.zeros((tb, Q_RANK), F32)
        dkr = jnp.zeros((tb, NOPE), F32)
        for h in range(N_HEADS):
            nope = slice(h * HEAD_PAD, h * HEAD_PAD + NOPE)
            rope = slice(h * HEAD_PAD + NOPE, (h + 1) * HEAD_PAD)
            dqp_ref[:, nope] = (dq_ref[:, nope].astype(F32) * QK_SCALE).astype(BF16)
            dqp_ref[:, rope] = (_rope_bwd(dq_ref[:, rope].astype(F32), c, sa, sb) * QK_SCALE).astype(BF16)
            dcqn = dcqn + _dot_nt(dqp_ref[:, h * HEAD_PAD:(h + 1) * HEAD_PAD], wq_ref[h])
            dkn_ref[:, h * NOPE:(h + 1) * NOPE] = dk_ref[:, nope]
            dkr = dkr + dk_ref[:, rope].astype(F32)
        dcq, dqg = _rms_bwd(dcqn, zs_[:, 0:Q_RANK], qg_ref[...])
        _acc(dqg_ref, dqg)
        dzs_ref[:, 0:Q_RANK] = dcq.astype(BF16)
        dzs_ref[:, Q_RANK + KV_RANK:ZS] = _rope_bwd(dkr, c, sa, sb).astype(BF16)
        dckvn = _dot_nt(dkn_ref[...], wk_ref[...]) + _dot_nt(dv_ref[...], wv_ref[...])
        dckv, dkg = _rms_bwd(dckvn, zs_[:, Q_RANK:Q_RANK + KV_RANK], kg_ref[...])
        _acc(dkg_ref, dkg)
        dzs_ref[:, Q_RANK:Q_RANK + KV_RANK] = dckv.astype(BF16)

    hp = N_HEADS * HEAD_PAD
    return _row_call(body, "bwd_qkv", tb, [dq, dk, dv, zs, tc, tsa, tsb], [q_norm, kv_norm, w_uq_p, w_uk, w_uv],
                     [(hp, BF16), (D_MODEL, BF16), (ZS, BF16)], [((1, Q_RANK), F32), ((1, KV_RANK), F32)])


def _bwd_in_proj(dzs, da, db, dga, dgc, x, dx1, w_in_p, g_pre, exchange, tb=512):
    def body(dzs_ref, da_ref, db_ref, dga_ref, dgc_ref, x_ref, dx1_ref, w_ref, g_ref, gx_ref, dg_ref):
        dh = _dot_nt(dzs_ref[...], w_ref[:, 0:ZS])
        dh = dh + _dot_nt(da_ref[...], w_ref[:, OFF_A:OFF_B])
        dh = dh + _dot_nt(db_ref[...], w_ref[:, OFF_B:OFF_GA])
        dh = dh + _dot_nt(dga_ref[...], w_ref[:, OFF_GA:OFF_GC])
        dh = dh + _dot_nt(dgc_ref[...], w_ref[:, OFF_GC:D_IN_PAD])
        dxn, dg = _rms_bwd(dh, x_ref[...], g_ref[...])
        _acc(dg_ref, dg)
        gx_ref[...] = dx1_ref[...] + dxn

    return _row_call(body, "bwd_in_proj", tb, [dzs, da, db, dga, dgc, x, dx1], [w_in_p, g_pre],
                     [(D_MODEL, F32)], [((1, D_MODEL), F32)], exchange=exchange)


def _mm_tn(a, b, name, shard_cols=None, tt=2048):
    t, m = a.shape
    n = b.shape[1]
    tm = min(m, 1024)
    tn = min(n, 1024)
    tt = min(t, tt)
    nt = t // tt
    per = tn // shard_cols if shard_cols else 1

    def body(a_ref, b_ref, o_ref, acc):
        k = pl.program_id(2)

        @pl.when(k == 0)
        def _():
            acc[...] = jnp.zeros_like(acc)

        acc[...] += _dot_tn(a_ref[...], b_ref[...])

        @pl.when(k == nt - 1)
        def _():
            if shard_cols:
                for s in range(per):
                    o_ref[s] = acc[:, s * shard_cols:(s + 1) * shard_cols].astype(BF16)
            else:
                o_ref[...] = acc[...].astype(BF16)

    if shard_cols:
        out_spec = pl.BlockSpec((per, tm, shard_cols), lambda i, j, k: (j, i, 0))
        out_shape = jax.ShapeDtypeStruct((n // shard_cols, m, shard_cols), BF16)
    else:
        out_spec = pl.BlockSpec((tm, tn), lambda i, j, k: (i, j))
        out_shape = jax.ShapeDtypeStruct((m, n), BF16)
    return pl.pallas_call(
        functools.partial(body), name=name, grid=(m // tm, n // tn, nt),
        in_specs=[pl.BlockSpec((tt, tm), lambda i, j, k: (k, i)), pl.BlockSpec((tt, tn), lambda i, j, k: (k, j))],
        out_specs=out_spec, out_shape=out_shape, scratch_shapes=[pltpu.VMEM((tm, tn), F32)],
        compiler_params=_params(3),
    )(a, b)


_ANY = pl.BlockSpec(memory_space=pl.ANY)
_MESH = pl.DeviceIdType.MESH

_Exchange = collections.namedtuple("_Exchange", "ins out_shapes scratch phases")


def _ag_exchange(shards):
    n = len(shards)

    def parts(ins, outs, sems):
        send_sems, recv_sems, _ = sems
        x, y, c = lax.axis_index("x"), lax.axis_index("y"), lax.axis_index("c")
        chips = [(1 - x, y), (x, 1 - y), (1 - x, 1 - y)]

        def copy(w, k, block, to, src=None):
            dst = outs[w].at[4 * block[0] + 2 * block[1] + block[2]]
            return pltpu.make_async_remote_copy(
                src_ref=dst if src is None else src, dst_ref=dst, send_sem=send_sems.at[7 * w + k],
                recv_sem=recv_sems.at[7 * w + k], device_id=to, device_id_type=_MESH)

        def first(w):
            return [copy(w, 0, (x, y, c), (x, y, 1 - c), src=ins[w])] + [
                copy(w, 1 + j, (x, y, c), (*chip, c), src=ins[w]) for j, chip in enumerate(chips)]

        def mine(w):
            return pltpu.make_async_copy(ins[w], outs[w].at[4 * x + 2 * y + c], sems[2].at[w])

        return (x, y, c), chips, copy, first, mine

    def start(ins, outs, sems):
        _, _, _, first, mine = parts(ins, outs, sems)
        for w in range(n):
            mine(w).start()
            for cp in first(w):
                cp.start()

    def forward(ins, outs, sems):
        (x, y, c), chips, copy, _, _ = parts(ins, outs, sems)
        for j, chip in enumerate(chips):
            for w in range(n):
                copy(w, 1 + j, (*chip, c), (x, y, c)).wait_recv()
                copy(w, 4 + j, (*chip, c), (x, y, 1 - c)).start()

    def finish(ins, outs, sems):
        (x, y, c), chips, copy, first, mine = parts(ins, outs, sems)
        for w in range(n):
            copy(w, 0, (x, y, 1 - c), (x, y, c)).wait_recv()
        for j, chip in enumerate(chips):
            for w in range(n):
                copy(w, 4 + j, (*chip, 1 - c), (x, y, c)).wait_recv()
        for w in range(n):
            for cp in first(w) + [copy(w, 4 + j, (*chip, c), (x, y, 1 - c)) for j, chip in enumerate(chips)]:
                cp.wait_send()
            mine(w).wait()

    return _Exchange(
        ins=list(shards), out_shapes=[jax.ShapeDtypeStruct((N_DEV,) + s.shape, s.dtype) for s in shards],
        scratch=[pltpu.SemaphoreType.DMA((7 * n,)), pltpu.SemaphoreType.DMA((7 * n,)), pltpu.SemaphoreType.DMA((n,))],
        phases=[start, forward, finish])


def _pair_exchange(gs):
    n = len(gs)

    def copies(ins, outs, sems):
        x, y, c = lax.axis_index("x"), lax.axis_index("y"), lax.axis_index("c")
        return [pltpu.make_async_remote_copy(
            src_ref=ins[w].at[:, 1 - c], dst_ref=outs[w], send_sem=sems[0].at[w], recv_sem=sems[1].at[w],
            device_id=(x, y, 1 - c), device_id_type=_MESH) for w in range(n)]

    return _start_then_wait(gs, [jax.ShapeDtypeStruct((4,) + g.shape[2:], g.dtype) for g in gs], n, copies)


def _start_then_wait(ins, out_shapes, n_copies, copies):
    def start(ins_, outs, sems):
        for cp in copies(ins_, outs, sems):
            cp.start()

    def finish(ins_, outs, sems):
        for cp in copies(ins_, outs, sems):
            cp.wait()

    return _Exchange(ins=list(ins), out_shapes=out_shapes,
                     scratch=[pltpu.SemaphoreType.DMA((n_copies,)), pltpu.SemaphoreType.DMA((n_copies,))],
                     phases=[start, finish])


def _run_exchange(ex, name):
    ni, no = len(ex.ins), len(ex.out_shapes)

    def body(*refs):
        for phase in ex.phases:
            phase(refs[:ni], refs[ni:ni + no], refs[ni + no:])

    return pl.pallas_call(functools.partial(body), name=name, out_shape=ex.out_shapes, in_specs=[_ANY] * ni,
                          out_specs=[_ANY] * no, scratch_shapes=ex.scratch)(*ex.ins)


def _hosted(body, n_in, n_out, ex, when):
    ni, no, ns = len(ex.ins), len(ex.out_shapes), len(ex.scratch)

    def fn(*refs):
        ins, ex_ins = refs[:n_in], refs[n_in:n_in + ni]
        outs = refs[n_in + ni:n_in + ni + n_out]
        ex_outs = refs[n_in + ni + n_out:n_in + ni + n_out + no]
        scratch, sems = refs[n_in + ni + n_out + no:len(refs) - ns], refs[len(refs) - ns:]
        last = len(ex.phases) - 1
        for k in range(last):
            pl.when(when(k))(functools.partial(ex.phases[k], ex_ins, ex_outs, sems))
        body(*ins, *outs, *scratch)
        pl.when(when(last))(functools.partial(ex.phases[last], ex_ins, ex_outs, sems))

    return fn


def _phase_steps(n_phases, n_steps):
    return [0, n_steps - 1] if n_phases == 2 else [0, 2 * n_steps // 3, n_steps - 1]


def _row_block(shape, steps, lead, pick):
    blk = (None,) * lead + (shape[0] // steps,) + tuple(shape[1:])
    return pl.BlockSpec(blk, lambda *a: tuple(pick(*a)) + (a[-2],) + (0,) * (len(shape) - 1))


def _pair_sum(gs, ls, c_idx, name, steps):
    n = len(gs)

    def body(c_ref, *refs):
        for w in range(n):
            refs[2 * n + w][...] = (refs[w][...].astype(F32) + refs[n + w][...].astype(F32)).astype(BF16)

    shapes = [g.shape[2:] for g in gs]
    return pl.pallas_call(
        functools.partial(body), name=name,
        grid_spec=pltpu.PrefetchScalarGridSpec(
            num_scalar_prefetch=1, grid=(4, steps),
            in_specs=[_row_block(s, steps, 2, lambda k, i, c: (k, c[0])) for s in shapes]
            + [_row_block(s, steps, 1, lambda k, i, c: (k,)) for s in shapes],
            out_specs=[_row_block(s, steps, 1, lambda k, i, c: (k,)) for s in shapes]),
        out_shape=[jax.ShapeDtypeStruct((4,) + tuple(s), BF16) for s in shapes], compiler_params=_params(2),
    )(c_idx, *gs, *ls)


def _chip_exchange(ps):
    n = len(ps)

    def copies(ins, outs, sems):
        x, y, c = lax.axis_index("x"), lax.axis_index("y"), lax.axis_index("c")
        chips = [(1 - x, y), (x, 1 - y), (1 - x, 1 - y)]
        return [pltpu.make_async_remote_copy(
            src_ref=ins[w].at[2 * px + py], dst_ref=outs[w].at[s], send_sem=sems[0].at[3 * w + s],
            recv_sem=sems[1].at[3 * w + s], device_id=(px, py, c), device_id_type=_MESH)
            for w in range(n) for s, (px, py) in enumerate(chips)]

    return _start_then_wait(ps, [jax.ShapeDtypeStruct((3,) + p.shape[1:], p.dtype) for p in ps], 3 * n, copies)


def _adamw(w, g, m, v):
    m2 = ADAM_B1 * m + (1.0 - ADAM_B1) * g
    v2 = ADAM_B2 * v + (1.0 - ADAM_B2) * (g * g)
    m_hat = m2 / (1.0 - ADAM_B1 ** ADAM_STEP)
    v_hat = v2 / (1.0 - ADAM_B2 ** ADAM_STEP)
    delta = -ADAM_LR * (m_hat / (jnp.sqrt(v_hat) + ADAM_EPS) + ADAM_WD * w)
    return delta, m2, v2


def _update(gs, ls, qs, ws, ms, vs, idx, name, steps):
    n = len(gs)

    def body(idx_ref, *refs):
        g, l, q, w, m, v = (refs[k * n:(k + 1) * n] for k in range(6))
        outs = refs[6 * n:]
        for i in range(n):
            gr = g[i][...].astype(F32) + l[i][...].astype(F32)
            gr = gr + q[i][0].astype(F32)
            gr = gr + q[i][1].astype(F32)
            gr = gr + q[i][2].astype(F32)
            outs[4 * i][...] = gr
            outs[4 * i + 1][...], outs[4 * i + 2][...], outs[4 * i + 3][...] = _adamw(w[i][...], gr, m[i][...], v[i][...])

    shapes = [w.shape for w in ws]
    own = [_row_block(s, steps, 0, lambda i, c: ()) for s in shapes]
    res = pl.pallas_call(
        functools.partial(body), name=name,
        grid_spec=pltpu.PrefetchScalarGridSpec(
            num_scalar_prefetch=1, grid=(steps,),
            in_specs=[_row_block(s, steps, 2, lambda i, c: (c[0], c[1])) for s in shapes]
            + [_row_block(s, steps, 1, lambda i, c: (c[0],)) for s in shapes]
            + [pl.BlockSpec((3, s[0] // steps) + tuple(s[1:]), lambda i, c, nd=len(s): (0, i) + (0,) * (nd - 1))
               for s in shapes] + own * 3,
            out_specs=[b for b in own for _ in range(4)]),
        out_shape=[jax.ShapeDtypeStruct(s, F32) for s in shapes for _ in range(4)], compiler_params=_params(1),
    )(idx, *gs, *ls, *qs, *ws, *ms, *vs)
    return [res[4 * i:4 * i + 4] for i in range(n)]


def _update_small(sv_all, ws, ms, vs):
    n = len(ws)

    def body(all_ref, *refs):
        w, m, v = refs[:n], refs[n:2 * n], refs[2 * n:3 * n]
        loss_ref, outs = refs[3 * n], refs[3 * n + 1:]
        total = all_ref[0]
        for dev in range(1, N_DEV):
            total = total + all_ref[dev]
        loss_ref[...] = total[LOSS_ROW:LOSS_ROW + 1, :]
        for i in range(n):
            gr = total[i:i + 1, 0:w[i].shape[1]]
            outs[4 * i][...] = gr
            outs[4 * i + 1][...], outs[4 * i + 2][...], outs[4 * i + 3][...] = _adamw(w[i][...], gr, m[i][...], v[i][...])

    res = pl.pallas_call(
        functools.partial(body), name="update_small",
        out_shape=[jax.ShapeDtypeStruct((1, 1024), F32)]
        + [jax.ShapeDtypeStruct(a.shape, F32) for a in ws for _ in range(4)],
    )(sv_all, *ws, *ms, *vs)
    return res[0], [res[1 + 4 * i:5 + 4 * i] for i in range(n)]


def _to_exchange(shards):
    return [jnp.pad(shards[n][0], ((0, 0), (0, HEAD_PAD - NOPE - ROPE))) if n == "w_uq" else shards[n][0]
            for n in _BIG]


def _from_exchange(arrs, like):
    return {n: (a[:, :NOPE + ROPE] if n == "w_uq" else a).reshape(like[n].shape) for n, a in zip(_BIG, arrs)}


def _padded_w_in(g_in):
    w_in = g_in.transpose(1, 0, 2).reshape(D_MODEL, N_DEV * IN_SHARD)
    kr_end = Q_RANK + KV_RANK + ROPE
    return jnp.concatenate([w_in[:, :kr_end], jnp.zeros((D_MODEL, 128 - ROPE), BF16), w_in[:, kr_end:]], axis=1)


def _full_conv_w(g_conv):
    taps = (g_conv[:, 0].astype(F32) + g_conv[:, 1].astype(F32)).reshape(N_DEV, CONV_W, D_MODEL // N_DEV)
    return jnp.pad(taps.transpose(1, 0, 2).reshape(CONV_W, D_MODEL), ((0, 1), (0, 0)))


def _pad_rows(a, rows):
    return jnp.pad(a, ((0, rows - a.shape[0]), (0, 0)))


def _small_pack(vals):
    rows = [jnp.pad(v.reshape(1, -1), ((0, 0), (0, 1024 - v.size))) for v in vals]
    return _pad_rows(jnp.concatenate(rows, axis=0), SMALL_ROWS)


def _rope_tables(positions):
    inv_freq = ROPE_THETA ** (-jnp.arange(0, ROPE, 2, dtype=F32) / ROPE)
    ang = positions.reshape(-1).astype(F32)[:, None] * inv_freq
    cos, sin = jnp.cos(ang), jnp.sin(ang)
    t = cos.shape[0]
    z32, z64 = jnp.zeros((t, HALF), F32), jnp.zeros((t, HEAD_PAD - NOPE - ROPE), F32)
    tc = jnp.concatenate([cos, cos, z64], axis=1)
    tsa = jnp.concatenate([-sin, z32, z64], axis=1)
    tsb = jnp.concatenate([z32, sin, z64], axis=1)
    return tc, tsa, tsb


def _blocks(dw):
    if dw.ndim == 2:
        dw = dw.reshape(N_DEV, dw.shape[0] // N_DEV, dw.shape[1])
    return dw.reshape((4, 2) + dw.shape[1:])


def _step(x, positions, target, small, send, c_idx):
    s_in, s_uq, s_uk, s_uv, s_o, s_conv, s_pw2, s_out, s_ff1, s_ff2 = send
    tc, tsa, tsb = _rope_tables(positions)

    w_in_p = _padded_w_in(_run_exchange(_ag_exchange([s_in]), "ag_w_in")[0])
    (h, zs, a, sb, u, sa, sc), (w_uq, w_uk, w_uv) = _fwd_in_proj(
        x, small["norm_mix_pre"], w_in_p, _ag_exchange([s_uq, s_uk, s_uv]))
    w_uk, w_uv = w_uk.reshape(KV_RANK, -1), w_uv.reshape(KV_RANK, -1)
    cqn, ckvn, q, k, v = _fwd_qkv(zs, tc, tsa, tsb, small["q_norm"], small["kv_norm"], w_uq, w_uk, w_uv)
    (attn, lse), (w_o, g_conv, w_pw2, w_out, w_ff1, w_ff2) = _attn_fwd(
        q, k, v, _ag_exchange([s_o, s_conv, s_pw2, s_out, s_ff1, s_ff2]))
    w_o, w_pw2, w_out = (w.reshape(D_MODEL, D_MODEL) for w in (w_o, w_pw2, w_out))
    w_ff2, conv_w = w_ff2.reshape(D_FF, D_MODEL), _full_conv_w(g_conv)
    co, uact = _conv_fwd(u, conv_w, small["conv_b"], small["conv_ln_g"], small["conv_ln_b"])
    ya, yc, mb, m, x1 = _fwd_merge(attn, uact, sa, sc, x, w_o, w_pw2, small["b_pw2"], w_out, small["norm_mix_post"])
    h2, r1, act = _fwd_ff1(x1, small["norm_mlp_pre"], w_ff1)
    f, dy, loss_blk = _fwd_ff2_loss(act, x1, target, w_ff2, small["norm_mlp_post"])

    df, df1, dg_mlp_post = _bwd_ff2(dy, f, r1, w_ff2, small["norm_mlp_post"])
    g_ff2 = _blocks(_mm_tn(act, df, "dw_ff2"))
    (dx1, dg_mlp_pre), l_ff2 = _bwd_ff1(df1, x1, dy, w_ff1, small["norm_mlp_pre"], _pair_exchange([g_ff2]))
    g_ff1 = _blocks(_mm_tn(h2, df1, "dw_ff1", shard_cols=FF_SHARD))
    (dmb, dya, dyc, dga, dgc, dat, dua, dg_mix_post, db_pw2, delta), l_ff1 = _bwd_merge(
        dx1, m, sa, sc, ya, yc, attn, w_out, w_o, w_pw2, small["norm_mix_post"], _pair_exchange([g_ff1]))
    g_ff, l_ff = [g_ff1, g_ff2], [l_ff1[0], l_ff2[0]]
    p_ff = _pair_sum(g_ff, l_ff, c_idx, "rs_pair_sum_ff", 2)
    g_mix = [_blocks(_mm_tn(attn, dya, "dw_o")), _blocks(_mm_tn(uact, dyc, "dw_pw2")),
             _blocks(_mm_tn(mb, dmb, "dw_out"))]
    dco, dln_g, dln_b, dconv_b = _bwd_ln(dua, co, small["conv_ln_g"], small["conv_ln_b"])
    (da, db, dconv), l_mix = _conv_bwd(dco, u, a, sb, conv_w, _pair_exchange(g_mix))
    p_mix = _pair_sum(g_mix, l_mix, c_idx, "rs_pair_sum_mix", 1)
    (dq, dk, dv), q_early = _attn_bwd(q, k, v, dat, lse, delta, _chip_exchange(p_ff + p_mix))
    dqp, dkn, dzs, dq_norm, dkv_norm = _bwd_qkv(dq, dk, dv, zs, tc, tsa, tsb, small["q_norm"], small["kv_norm"],
                                                w_uq, w_uk, w_uv)
    dw_in = jnp.concatenate([_mm_tn(h, dzs, "dw_in_zs")[:, :Q_RANK + KV_RANK + ROPE], _mm_tn(h, da, "dw_in_a"),
                             _mm_tn(h, db, "dw_in_b"), _mm_tn(h, dga, "dw_in_ga"), _mm_tn(h, dgc, "dw_in_gc")],
                            axis=1)
    g_late = [_blocks(dw_in.reshape(D_MODEL, N_DEV, IN_SHARD).transpose(1, 0, 2)),
              _blocks(_mm_tn(cqn, dqp, "dw_uq", shard_cols=HEAD_PAD)),
              _blocks(_mm_tn(ckvn, dkn, "dw_uk").reshape(N_DEV, -1, N_HEADS, NOPE)),
              _blocks(_mm_tn(ckvn, dv, "dw_uv").reshape(N_DEV, -1, N_HEADS, NOPE)),
              _blocks(dconv[:CONV_W].reshape(CONV_W, N_DEV, 1, -1).transpose(1, 0, 2, 3).astype(BF16))]
    l_late = _run_exchange(_pair_exchange(g_late), "rs_pair_exchange_late")
    p_late = _pair_sum(g_late, l_late, c_idx, "rs_pair_sum_late", 1)
    (grad_x, dg_pre), q_late = _bwd_in_proj(dzs, da, db, dga, dgc, x, dx1, w_in_p, small["norm_mix_pre"],
                                            _chip_exchange(p_late))

    order = lambda late, mix, ff: list(late[:4]) + [mix[0], late[4], mix[1], mix[2]] + list(ff)
    exchanged = [order(g_late, g_mix, g_ff), order(l_late, l_mix, l_ff),
                 order(q_late, q_early[2:], q_early[:2])]
    small_grads = (dg_pre, dq_norm, dkv_norm, dconv_b, dln_g, dln_b, db_pw2, dg_mix_post, dg_mlp_pre, dg_mlp_post)
    return loss_blk, grad_x, small_grads, exchanged


def kernel(x, positions, norm_mix_pre, w_in, q_norm, w_uq, kv_norm, w_uk, w_uv, w_o_attn, conv_w, conv_b, conv_ln_g, conv_ln_b, w_pw2, b_pw2, w_out, norm_mix_post, norm_mlp_pre, w_ff1, w_ff2, norm_mlp_post, loss_target, m_norm_mix_pre, m_w_in, m_q_norm, m_w_uq, m_kv_norm, m_w_uk, m_w_uv, m_w_o_attn, m_conv_w, m_conv_b, m_conv_ln_g, m_conv_ln_b, m_w_pw2, m_b_pw2, m_w_out, m_norm_mix_post, m_norm_mlp_pre, m_w_ff1, m_w_ff2, m_norm_mlp_post, v_norm_mix_pre, v_w_in, v_q_norm, v_w_uq, v_kv_norm, v_w_uk, v_w_uv, v_w_o_attn, v_conv_w, v_conv_b, v_conv_ln_g, v_conv_ln_b, v_w_pw2, v_b_pw2, v_w_out, v_norm_mix_post, v_norm_mlp_pre, v_w_ff1, v_w_ff2, v_norm_mlp_post):
    wts = dict(norm_mix_pre=norm_mix_pre, w_in=w_in, q_norm=q_norm, w_uq=w_uq, kv_norm=kv_norm, w_uk=w_uk, w_uv=w_uv,
               w_o_attn=w_o_attn, conv_w=conv_w, conv_b=conv_b, conv_ln_g=conv_ln_g, conv_ln_b=conv_ln_b,
               w_pw2=w_pw2, b_pw2=b_pw2, w_out=w_out, norm_mix_post=norm_mix_post, norm_mlp_pre=norm_mlp_pre,
               w_ff1=w_ff1, w_ff2=w_ff2, norm_mlp_post=norm_mlp_post)
    mom_m = dict(norm_mix_pre=m_norm_mix_pre, w_in=m_w_in, q_norm=m_q_norm, w_uq=m_w_uq, kv_norm=m_kv_norm,
                 w_uk=m_w_uk, w_uv=m_w_uv, w_o_attn=m_w_o_attn, conv_w=m_conv_w, conv_b=m_conv_b,
                 conv_ln_g=m_conv_ln_g, conv_ln_b=m_conv_ln_b, w_pw2=m_w_pw2, b_pw2=m_b_pw2, w_out=m_w_out,
                 norm_mix_post=m_norm_mix_post, norm_mlp_pre=m_norm_mlp_pre, w_ff1=m_w_ff1, w_ff2=m_w_ff2,
                 norm_mlp_post=m_norm_mlp_post)
    mom_v = dict(norm_mix_pre=v_norm_mix_pre, w_in=v_w_in, q_norm=v_q_norm, w_uq=v_w_uq, kv_norm=v_kv_norm,
                 w_uk=v_w_uk, w_uv=v_w_uv, w_o_attn=v_w_o_attn, conv_w=v_conv_w, conv_b=v_conv_b,
                 conv_ln_g=v_conv_ln_g, conv_ln_b=v_conv_ln_b, w_pw2=v_w_pw2, b_pw2=v_b_pw2, w_out=v_w_out,
                 norm_mix_post=v_norm_mix_post, norm_mlp_pre=v_norm_mlp_pre, w_ff1=v_w_ff1, w_ff2=v_w_ff2,
                 norm_mlp_post=v_norm_mlp_post)
    cx, cy, cc = lax.axis_index("x"), lax.axis_index("y"), lax.axis_index("c")

    big_local = {n: wts[n] for n in _BIG}
    w_ex = _to_exchange(big_local)
    send = [a.astype(BF16) for a in w_ex]
    conv_i = _BIG.index("conv_w")
    conv_lo = (w_ex[conv_i] - send[conv_i].astype(F32)).astype(BF16)
    send[conv_i] = jnp.stack([send[conv_i], conv_lo])

    small = {n: wts[n].reshape(1, -1) for n in _SMALL}
    c_idx = cc.reshape(1).astype(jnp.int32)
    loss_blk, grad_x, small_grads, (g4, l_sib, q_in) = _step(x[0], positions, loss_target[0], small, send, c_idx)

    idx = jnp.stack([2 * cx + cy, cc]).astype(jnp.int32)
    m_ex, v_ex = _to_exchange({n: mom_m[n] for n in _BIG}), _to_exchange({n: mom_v[n] for n in _BIG})
    upd = [None] * len(_BIG)
    for group, steps in ((("w_in", "w_uq", "w_ff1", "w_ff2"), 4),
                         (("w_uk", "w_uv", "w_o_attn", "conv_w", "w_pw2", "w_out"), 1)):
        ids = [_BIG.index(n) for n in group]
        pick = lambda arrs: [arrs[i] for i in ids]
        res = _update(pick(g4), pick(l_sib), pick(q_in), pick(w_ex), pick(m_ex), pick(v_ex), idx,
                      "update_" + group[0], steps)
        for i, r in zip(ids, res):
            upd[i] = r
    out_g, out_d, out_m, out_v = (_from_exchange([u[j] for u in upd], big_local) for j in range(4))

    loss_row = jnp.broadcast_to(loss_blk[0:1, 0:1], (1, 1024))
    sv = _small_pack(list(small_grads) + [loss_row])
    sv_all = _run_exchange(_ag_exchange([sv]), "ag_small")[0]
    loss_sum, upd_small = _update_small(sv_all, [wts[n] for n in _SMALL], [mom_m[n] for n in _SMALL],
                                        [mom_v[n] for n in _SMALL])
    for n, r in zip(_SMALL, upd_small):
        out_g[n], out_d[n], out_m[n], out_v[n] = r
    loss = loss_sum[0, 0] * (0.5 / D_MODEL)

    return (loss, grad_x[None], *[out_g[n] for n in _WEIGHTS], *[out_d[n] for n in _WEIGHTS],
            *[out_m[n] for n in _WEIGHTS], *[out_v[n] for n in _WEIGHTS])
```

```python
import collections
import functools

import jax
import jax.numpy as jnp
from jax import lax
from jax.experimental import pallas as pl
from jax.experimental.pallas import tpu as pltpu

F32 = jnp.float32
BF16 = jnp.bfloat16

D_MODEL = 1024
N_HEADS = 8
NOPE = 128
ROPE = 64
HALF = ROPE // 2
Q_RANK = 384
KV_RANK = 256
CONV_W = 31
D_FF = 4096
EPS = 1e-6
ROPE_THETA = 10000.0
HEAD_PAD = 256
QK_SCALE = (NOPE + ROPE) ** -0.5
LOG2E = 1.4426950408889634
SUBLANES = 8
N_DEV = 8
FF_SHARD = D_FF // N_DEV
IN_SHARD = 4800 // N_DEV

ZS = Q_RANK + KV_RANK + 128
OFF_A = ZS
OFF_B = OFF_A + D_MODEL
OFF_GA = OFF_B + D_MODEL
OFF_GC = OFF_GA + D_MODEL
D_IN_PAD = OFF_GC + D_MODEL

ADAM_LR = 0.001
ADAM_B1 = 0.9
ADAM_B2 = 0.999
ADAM_EPS = 1e-08
ADAM_WD = 0.01
ADAM_STEP = 10

VMEM_LIMIT = 56 * 1024 * 1024

_SMALL = ("norm_mix_pre", "q_norm", "kv_norm", "conv_b", "conv_ln_g", "conv_ln_b", "b_pw2", "norm_mix_post",
          "norm_mlp_pre", "norm_mlp_post")
SMALL_ROWS = 16
LOSS_ROW = len(_SMALL)

_BIG = ("w_in", "w_uq", "w_uk", "w_uv", "w_o_attn", "conv_w", "w_pw2", "w_out", "w_ff1", "w_ff2")
_WEIGHTS = ("norm_mix_pre", "w_in", "q_norm", "w_uq", "kv_norm", "w_uk", "w_uv", "w_o_attn", "conv_w", "conv_b",
            "conv_ln_g", "conv_ln_b", "w_pw2", "b_pw2", "w_out", "norm_mix_post", "norm_mlp_pre", "w_ff1", "w_ff2",
            "norm_mlp_post")


def _dot(a, b):
    return jnp.dot(a, b, preferred_element_type=F32)


def _dot_nt(a, b):
    return lax.dot_general(a, b, (((1,), (1,)), ((), ())), preferred_element_type=F32)


def _dot_tn(a, b):
    return lax.dot_general(a, b, (((0,), (0,)), ((), ())), preferred_element_type=F32)


def _sigmoid(x):
    return 1.0 / (1.0 + jnp.exp(-x))


def _rms_fwd(x, g):
    r = lax.rsqrt(jnp.mean(x * x, axis=-1, keepdims=True) + EPS)
    return x * r * g


def _rms_bwd(dy, x, g):
    r = lax.rsqrt(jnp.mean(x * x, axis=-1, keepdims=True) + EPS)
    xh = x * r
    gy = dy * g
    dx = r * (gy - xh * jnp.mean(gy * xh, axis=-1, keepdims=True))
    return dx, jnp.sum(dy * xh, axis=0, keepdims=True)


def _rope(q, c, sa, sb):
    n = q.shape[-1]
    return q * c + pltpu.roll(q, n - HALF, 1) * sa + pltpu.roll(q, HALF, 1) * sb


def _rope_bwd(d, c, sa, sb):
    n = d.shape[-1]
    return d * c - pltpu.roll(d, n - HALF, 1) * sa - pltpu.roll(d, HALF, 1) * sb


def _shifted_copies(buf, shifted, tb):
    n = shifted.shape[1]
    for b in range(1, SUBLANES):
        shifted[b - 1] = buf[pl.ds(b, n), :]


def _rows_at(buf, shifted, start, tb):
    a, b = divmod(start, SUBLANES)
    src = buf if b == 0 else shifted.at[b - 1]
    return src[pl.ds(SUBLANES * a, tb), :]


def _params(n_axes=1):
    return pltpu.CompilerParams(dimension_semantics=("arbitrary",) * n_axes, vmem_limit_bytes=VMEM_LIMIT)


def _row_call(body, name, tb, row_ins, full_ins, row_outs, acc_outs, lane_outs=(), exchange=None):
    t = row_ins[0].shape[0]
    in_specs = [pl.BlockSpec((tb, a.shape[1]), lambda i: (i, 0)) for a in row_ins]
    in_specs += [pl.BlockSpec(a.shape, lambda i, nd=a.ndim: (0,) * nd) for a in full_ins]
    out_specs = [pl.BlockSpec((tb, c), lambda i: (i, 0)) for c, _ in row_outs]
    out_specs += [pl.BlockSpec(s, lambda i, nd=len(s): (0,) * nd) for s, _ in acc_outs]
    out_specs += [pl.BlockSpec((n, None, 1, tb), lambda i: (0, i, 0, 0)) for n in lane_outs]
    out_shape = [jax.ShapeDtypeStruct((t, c), dt) for c, dt in row_outs]
    out_shape += [jax.ShapeDtypeStruct(s, dt) for s, dt in acc_outs]
    out_shape += [jax.ShapeDtypeStruct((n, t // tb, 1, tb), F32) for n in lane_outs]
    if exchange is None:
        return pl.pallas_call(
            functools.partial(body), name=name, grid=(t // tb,), in_specs=in_specs, out_specs=out_specs,
            out_shape=out_shape, compiler_params=_params(1),
        )(*row_ins, *full_ins)
    steps = _phase_steps(len(exchange.phases), t // tb)
    fn = _hosted(body, len(in_specs), len(out_specs), exchange, lambda k: pl.program_id(0) == steps[k])
    res = pl.pallas_call(
        fn, name=name, grid=(t // tb,), in_specs=in_specs + [_ANY] * len(exchange.ins),
        out_specs=out_specs + [_ANY] * len(exchange.out_shapes), out_shape=out_shape + exchange.out_shapes,
        scratch_shapes=exchange.scratch, compiler_params=_params(1),
    )(*row_ins, *full_ins, *exchange.ins)
    return res[:len(out_specs)], res[len(out_specs):]


def _acc(ref, val):
    @pl.when(pl.program_id(0) == 0)
    def _():
        ref[...] = jnp.zeros_like(ref)
    ref[...] += val


def _fwd_in_proj(x, g_pre, w_in_p, exchange, tb=256):
    def body(x_ref, g_ref, w_ref, h_ref, zs_ref, a_ref, sb_ref, u_ref, sa_ref, sc_ref):
        hb = _rms_fwd(x_ref[...], g_ref[...]).astype(BF16)
        h_ref[...] = hb
        zs_ref[...] = _dot(hb, w_ref[:, 0:ZS])
        a = _dot(hb, w_ref[:, OFF_A:OFF_B])
        sb = _sigmoid(_dot(hb, w_ref[:, OFF_B:OFF_GA]))
        a_ref[...] = a
        sb_ref[...] = sb
        u_ref[...] = a * sb
        sa_ref[...] = _sigmoid(_dot(hb, w_ref[:, OFF_GA:OFF_GC]))
        sc_ref[...] = _sigmoid(_dot(hb, w_ref[:, OFF_GC:D_IN_PAD]))

    d = D_MODEL
    return _row_call(body, "fwd_in_proj", tb, [x], [g_pre, w_in_p],
                     [(d, BF16), (ZS, F32), (d, F32), (d, F32), (d, F32), (d, F32), (d, F32)], [], exchange=exchange)


def _fwd_qkv(zs, tc, tsa, tsb, q_norm, kv_norm, w_uq_p, w_uk, w_uv, tb=256):
    def body(zs_ref, c_ref, sa_ref, sb_ref, qg_ref, kg_ref, wq_ref, wk_ref, wv_ref,
             cqn_ref, ckvn_ref, q_ref, k_ref, v_ref):
        zs_ = zs_ref[...]
        c, sa, sb = c_ref[...], sa_ref[...], sb_ref[...]
        cqn = _rms_fwd(zs_[:, 0:Q_RANK], qg_ref[...]).astype(BF16)
        cqn_ref[...] = cqn
        for h in range(N_HEADS):
            qh = _dot(cqn, wq_ref[h]) * (QK_SCALE * LOG2E)
            q_ref[:, h * HEAD_PAD:h * HEAD_PAD + NOPE] = qh[:, :NOPE].astype(BF16)
            q_ref[:, h * HEAD_PAD + NOPE:(h + 1) * HEAD_PAD] = _rope(qh[:, NOPE:], c, sa, sb).astype(BF16)
        kr = _rope(zs_[:, Q_RANK + KV_RANK:ZS], c, sa, sb).astype(BF16)
        ckvn = _rms_fwd(zs_[:, Q_RANK:Q_RANK + KV_RANK], kg_ref[...]).astype(BF16)
        ckvn_ref[...] = ckvn
        kn = _dot(ckvn, wk_ref[...]).astype(BF16)
        v_ref[...] = _dot(ckvn, wv_ref[...]).astype(BF16)
        for h in range(N_HEADS):
            k_ref[:, h * HEAD_PAD:h * HEAD_PAD + NOPE] = kn[:, h * NOPE:(h + 1) * NOPE]
            k_ref[:, h * HEAD_PAD + NOPE:(h + 1) * HEAD_PAD] = kr

    hp = N_HEADS * HEAD_PAD
    return _row_call(body, "fwd_qkv", tb, [zs, tc, tsa, tsb], [q_norm, kv_norm, w_uq_p, w_uk, w_uv],
                     [(Q_RANK, BF16), (KV_RANK, BF16), (hp, BF16), (hp, BF16), (D_MODEL, BF16)], [])


def _attn_fwd(q, k, v, exchange, tq=512):
    t = q.shape[0]
    nq = t // tq

    def body(q_ref, k_ref, v_ref, o_ref, lse_ref, m_sc, l_sc, acc_sc, s_sc):
        i = pl.program_id(1)
        m_sc[...] = jnp.full_like(m_sc, -1e30)
        l_sc[...] = jnp.zeros_like(l_sc)
        acc_sc[...] = jnp.zeros_like(acc_sc)
        qb = q_ref[...]

        def rows(j):
            return pl.ds(pl.multiple_of(j * tq, tq), tq)

        def scores(j, slot):
            s_sc[slot] = _dot_nt(k_ref[rows(j), :], qb)

        def update(j, slot, masked):
            st = s_sc[slot]
            if masked:
                key = lax.broadcasted_iota(jnp.int32, (tq, tq), 0)
                qry = lax.broadcasted_iota(jnp.int32, (tq, tq), 1)
                st = jnp.where(key <= qry, st, -1e30)
            m_prev = m_sc[...]
            m_new = jnp.maximum(m_prev, jnp.max(st, axis=0, keepdims=True))
            alpha = jnp.exp2(m_prev - m_new)
            pt = jnp.exp2(st - m_new)
            l_sc[...] = alpha * l_sc[...] + jnp.sum(pt, axis=0, keepdims=True)
            acc_sc[...] = alpha * acc_sc[...] + _dot_tn(v_ref[rows(j), :], pt.astype(BF16))
            m_sc[...] = m_new

        def pair(p, carry):
            scores(2 * p + 1, 1)
            update(2 * p, 0, False)
            scores(2 * p + 2, 0)
            update(2 * p + 1, 1, False)
            return carry

        scores(0, 0)
        lax.fori_loop(0, i // 2, pair, 0)

        @pl.when(i % 2 == 1)
        def _():
            scores(i, 1)
            update(i - 1, 0, False)
            update(i, 1, True)

        @pl.when(i % 2 == 0)
        def _():
            update(i, 0, True)

        l = l_sc[...]
        o_ref[...] = (acc_sc[...] / l).T.astype(BF16)
        lse_ref[...] = m_sc[...] + jnp.log2(l)

    steps = _phase_steps(len(exchange.phases), N_HEADS * nq)
    fn = _hosted(body, 3, 2, exchange, lambda p: pl.program_id(0) * nq + pl.program_id(1) == steps[p])
    res = pl.pallas_call(
        fn, name="attn_fwd", grid=(N_HEADS, nq),
        in_specs=[pl.BlockSpec((tq, HEAD_PAD), lambda h, i: (i, h)),
                  pl.BlockSpec((t, HEAD_PAD), lambda h, i: (0, h)),
                  pl.BlockSpec((t, NOPE), lambda h, i: (0, h))] + [_ANY] * len(exchange.ins),
        out_specs=[pl.BlockSpec((tq, NOPE), lambda h, i: (i, h)),
                   pl.BlockSpec((None, None, 1, tq), lambda h, i: (h, i, 0, 0))] + [_ANY] * len(exchange.out_shapes),
        out_shape=[jax.ShapeDtypeStruct((t, D_MODEL), BF16),
                   jax.ShapeDtypeStruct((N_HEADS, nq, 1, tq), F32)] + exchange.out_shapes,
        scratch_shapes=[pltpu.VMEM((1, tq), F32), pltpu.VMEM((1, tq), F32), pltpu.VMEM((NOPE, tq), F32),
                        pltpu.VMEM((2, tq, tq), F32)] + exchange.scratch,
        compiler_params=_params(2),
    )(q, k, v, *exchange.ins)
    return res[:2], res[2:]


def _conv_fwd(u, conv_w, conv_b, ln_g, ln_b, tb=256):
    t, c = u.shape
    halo = 32

    def body(u_ref, up_ref, w_ref, b_ref, g_ref, be_ref, co_ref, act_ref, buf, shifted):
        i = pl.program_id(0)
        buf[0:halo, :] = jnp.where(i == 0, 0.0, up_ref[...])
        buf[halo:halo + tb, :] = u_ref[...]
        _shifted_copies(buf, shifted, tb)
        acc = jnp.zeros((tb, c), F32)
        for k in range(CONV_W):
            acc = acc + w_ref[k:k + 1, :] * _rows_at(buf, shifted, halo - (CONV_W - 1) + k, tb)
        co = acc + b_ref[...]
        co_ref[...] = co
        mu = jnp.mean(co, axis=-1, keepdims=True)
        xc = co - mu
        r = lax.rsqrt(jnp.mean(xc * xc, axis=-1, keepdims=True) + EPS)
        y = xc * r * g_ref[...] + be_ref[...]
        act_ref[...] = (y * _sigmoid(y)).astype(BF16)

    ratio = tb // halo
    return pl.pallas_call(
        functools.partial(body), name="conv_fwd", grid=(t // tb,),
        in_specs=[pl.BlockSpec((tb, c), lambda i: (i, 0)),
                  pl.BlockSpec((halo, c), lambda i: (jnp.maximum(i * ratio - 1, 0), 0)),
                  pl.BlockSpec(conv_w.shape, lambda i: (0, 0)),
                  pl.BlockSpec((1, c), lambda i: (0, 0)), pl.BlockSpec((1, c), lambda i: (0, 0)),
                  pl.BlockSpec((1, c), lambda i: (0, 0))],
        out_specs=[pl.BlockSpec((tb, c), lambda i: (i, 0)), pl.BlockSpec((tb, c), lambda i: (i, 0))],
        out_shape=[jax.ShapeDtypeStruct((t, c), F32), jax.ShapeDtypeStruct((t, c), BF16)],
        scratch_shapes=[pltpu.VMEM((tb + halo, c), F32), pltpu.VMEM((SUBLANES - 1, tb + halo - SUBLANES, c), F32)],
        compiler_params=_params(1),
    )(u, u, conv_w, conv_b, ln_g, ln_b)


def _fwd_merge(attn, uact, sa, sc, x, w_o, w_pw2, b_pw2, w_out, g_post, tb=512):
    def body(at_ref, ua_ref, sa_ref, sc_ref, x_ref, wo_ref, wp_ref, bp_ref, wout_ref, g_ref,
             ya_ref, yc_ref, mb_ref, m_ref, x1_ref):
        ya = _dot(at_ref[...], wo_ref[...])
        yc = _dot(ua_ref[...], wp_ref[...]) + bp_ref[...]
        ya_ref[...] = ya.astype(BF16)
        yc_ref[...] = yc.astype(BF16)
        mb = (sa_ref[...] * ya + sc_ref[...] * yc).astype(BF16)
        mb_ref[...] = mb
        m = _dot(mb, wout_ref[...])
        m_ref[...] = m
        x1_ref[...] = x_ref[...] + _rms_fwd(m, g_ref[...])

    d = D_MODEL
    return _row_call(body, "fwd_merge", tb, [attn, uact, sa, sc, x], [w_o, w_pw2, b_pw2, w_out, g_post],
                     [(d, BF16), (d, BF16), (d, BF16), (d, F32), (d, F32)], [])


def _fwd_ff1(x1, g, w_ff1, tb=512):
    def body(x1_ref, g_ref, w_ref, h2_ref, r1_ref, act_ref):
        h2 = _rms_fwd(x1_ref[...], g_ref[...]).astype(BF16)
        h2_ref[...] = h2
        for j in range(N_DEV):
            cols = slice(j * FF_SHARD, (j + 1) * FF_SHARD)
            r1 = jnp.maximum(_dot(h2, w_ref[j]), 0.0)
            r1_ref[:, cols] = r1.astype(BF16)
            act_ref[:, cols] = (r1 * r1).astype(BF16)

    return _row_call(body, "fwd_ff1", tb, [x1], [g, w_ff1], [(D_MODEL, BF16), (D_FF, BF16), (D_FF, BF16)], [])


def _fwd_ff2_loss(act, x1, target, w_ff2, g, tb=512):
    def body(act_ref, x1_ref, tg_ref, w_ref, g_ref, f_ref, dy_ref, loss_ref):
        f = _dot(act_ref[...], w_ref[...])
        f_ref[...] = f
        e = x1_ref[...] + _rms_fwd(f, g_ref[...]) - tg_ref[...]
        dy_ref[...] = e * (1.0 / D_MODEL)
        _acc(loss_ref, jnp.sum(e * e))

    return _row_call(body, "fwd_ff2_loss", tb, [act, x1, target], [w_ff2, g],
                     [(D_MODEL, F32), (D_MODEL, F32)], [((8, 128), F32)])


def _bwd_ff2(dy, f, r1, w_ff2, g, tb=512):
    def body(dy_ref, f_ref, r1_ref, w_ref, g_ref, df_ref, df1_ref, dg_ref):
        df, dg = _rms_bwd(dy_ref[...], f_ref[...], g_ref[...])
        _acc(dg_ref, dg)
        dfb = df.astype(BF16)
        df_ref[...] = dfb
        dact = _dot_nt(dfb, w_ref[...])
        df1_ref[...] = (dact * (2.0 * r1_ref[...].astype(F32))).astype(BF16)

    return _row_call(body, "bwd_ff2", tb, [dy, f, r1], [w_ff2, g], [(D_MODEL, BF16), (D_FF, BF16)],
                     [((1, D_MODEL), F32)])


def _bwd_ff1(df1, x1, dy, w_ff1, g, exchange, tb=512):
    def body(df1_ref, x1_ref, dy_ref, w_ref, g_ref, dx1_ref, dg_ref):
        dh2 = _dot_nt(df1_ref[:, 0:FF_SHARD], w_ref[0])
        for j in range(1, N_DEV):
            dh2 = dh2 + _dot_nt(df1_ref[:, j * FF_SHARD:(j + 1) * FF_SHARD], w_ref[j])
        dxn, dg = _rms_bwd(dh2, x1_ref[...], g_ref[...])
        _acc(dg_ref, dg)
        dx1_ref[...] = dy_ref[...] + dxn

    return _row_call(body, "bwd_ff1", tb, [df1, x1, dy], [w_ff1, g], [(D_MODEL, F32)], [((1, D_MODEL), F32)],
                     exchange=exchange)


def _bwd_merge(dx1, m, sa, sc, ya, yc, attn, w_out, w_o, w_pw2, g_post, exchange, tb=256):
    def body(dx1_ref, m_ref, sa_ref, sc_ref, ya_ref, yc_ref, at_ref, wout_ref, wo_ref, wp_ref, g_ref,
             dm_ref, dya_ref, dyc_ref, dga_ref, dgc_ref, dat_ref, dua_ref, dg_ref, dbp_ref, delta_ref):
        dm, dg = _rms_bwd(dx1_ref[...], m_ref[...], g_ref[...])
        _acc(dg_ref, dg)
        dmb = dm.astype(BF16)
        dm_ref[...] = dmb
        dmerged = _dot_nt(dmb, wout_ref[...])
        sa, sc = sa_ref[...], sc_ref[...]
        dya = dmerged * sa
        dyc = dmerged * sc
        _acc(dbp_ref, jnp.sum(dyc, axis=0, keepdims=True))
        dyab = dya.astype(BF16)
        dycb = dyc.astype(BF16)
        dya_ref[...] = dyab
        dyc_ref[...] = dycb
        dga_ref[...] = (dmerged * ya_ref[...].astype(F32) * sa * (1.0 - sa)).astype(BF16)
        dgc_ref[...] = (dmerged * yc_ref[...].astype(F32) * sc * (1.0 - sc)).astype(BF16)
        dat = _dot_nt(dyab, wo_ref[...])
        dat_ref[...] = dat.astype(BF16)
        prod = dat * at_ref[...].astype(F32)
        lane = lax.broadcasted_iota(jnp.int32, (tb, NOPE), 1)
        dl = jnp.zeros((tb, NOPE), F32)
        for h in range(N_HEADS):
            dl = dl + jnp.where(lane == h, jnp.sum(prod[:, h * NOPE:(h + 1) * NOPE], axis=1, keepdims=True), 0.0)
        dlt = dl.T
        for h in range(N_HEADS):
            delta_ref[h] = dlt[h:h + 1, :]
        dua_ref[...] = _dot_nt(dycb, wp_ref[...]).astype(BF16)

    d = D_MODEL
    return _row_call(body, "bwd_merge", tb, [dx1, m, sa, sc, ya, yc, attn], [w_out, w_o, w_pw2, g_post],
                     [(d, BF16), (d, BF16), (d, BF16), (d, BF16), (d, BF16), (d, BF16), (d, BF16)],
                     [((1, d), F32), ((1, d), F32)], lane_outs=(N_HEADS,), exchange=exchange)


def _bwd_ln(dua, co, ln_g, ln_b, tb=256):
    def body(dua_ref, co_ref, g_ref, be_ref, dco_ref, dg_ref, db_ref, dcb_ref):
        co = co_ref[...]
        g = g_ref[...]
        mu = jnp.mean(co, axis=-1, keepdims=True)
        xc = co - mu
        r = lax.rsqrt(jnp.mean(xc * xc, axis=-1, keepdims=True) + EPS)
        xh = xc * r
        y = xh * g + be_ref[...]
        s = _sigmoid(y)
        dy = dua_ref[...].astype(F32) * (s + y * s * (1.0 - s))
        _acc(db_ref, jnp.sum(dy, axis=0, keepdims=True))
        _acc(dg_ref, jnp.sum(dy * xh, axis=0, keepdims=True))
        gy = dy * g
        dco = r * (gy - jnp.mean(gy, axis=-1, keepdims=True) - xh * jnp.mean(gy * xh, axis=-1, keepdims=True))
        dco_ref[...] = dco
        _acc(dcb_ref, jnp.sum(dco, axis=0, keepdims=True))

    d = D_MODEL
    return _row_call(body, "bwd_ln", tb, [dua, co], [ln_g, ln_b], [(d, F32)],
                     [((1, d), F32), ((1, d), F32), ((1, d), F32)])


def _conv_bwd(dco, u, a, sb, conv_w, exchange, tb=256):
    t, c = u.shape
    halo = 32
    ratio = tb // halo
    nblk = t // tb

    def body(d_ref, dn_ref, u_ref, up_ref, a_ref, sb_ref, w_ref, da_ref, db_ref, dw_ref, bufd, bufu, shd, shu):
        i = pl.program_id(0)

        @pl.when(i == 0)
        def _():
            dw_ref[...] = jnp.zeros_like(dw_ref)

        dco = d_ref[...]
        bufd[0:tb, :] = dco
        bufd[tb:tb + halo, :] = jnp.where(i == nblk - 1, 0.0, dn_ref[...])
        bufu[0:halo, :] = jnp.where(i == 0, 0.0, up_ref[...])
        bufu[halo:halo + tb, :] = u_ref[...]
        _shifted_copies(bufd, shd, tb)
        _shifted_copies(bufu, shu, tb)
        du = jnp.zeros((tb, c), F32)
        for k in range(CONV_W):
            du = du + w_ref[k:k + 1, :] * _rows_at(bufd, shd, CONV_W - 1 - k, tb)
            dw_ref[k:k + 1, :] += jnp.sum(dco * _rows_at(bufu, shu, halo - (CONV_W - 1) + k, tb), axis=0,
                                          keepdims=True)
        sb_ = sb_ref[...]
        da_ref[...] = (du * sb_).astype(BF16)
        db_ref[...] = (du * a_ref[...] * sb_ * (1.0 - sb_)).astype(BF16)

    steps = _phase_steps(len(exchange.phases), nblk)
    fn = _hosted(body, 7, 3, exchange, lambda p: pl.program_id(0) == steps[p])
    res = pl.pallas_call(
        fn, name="conv_bwd", grid=(nblk,),
        in_specs=[pl.BlockSpec((tb, c), lambda i: (i, 0)),
                  pl.BlockSpec((halo, c), lambda i: (jnp.minimum((i + 1) * ratio, t // halo - 1), 0)),
                  pl.BlockSpec((tb, c), lambda i: (i, 0)),
                  pl.BlockSpec((halo, c), lambda i: (jnp.maximum(i * ratio - 1, 0), 0)),
                  pl.BlockSpec((tb, c), lambda i: (i, 0)), pl.BlockSpec((tb, c), lambda i: (i, 0)),
                  pl.BlockSpec(conv_w.shape, lambda i: (0, 0))] + [_ANY] * len(exchange.ins),
        out_specs=[pl.BlockSpec((tb, c), lambda i: (i, 0)), pl.BlockSpec((tb, c), lambda i: (i, 0)),
                   pl.BlockSpec((32, c), lambda i: (0, 0))] + [_ANY] * len(exchange.out_shapes),
        out_shape=[jax.ShapeDtypeStruct((t, c), BF16), jax.ShapeDtypeStruct((t, c), BF16),
                   jax.ShapeDtypeStruct((32, c), F32)] + exchange.out_shapes,
        scratch_shapes=[pltpu.VMEM((tb + halo, c), F32), pltpu.VMEM((tb + halo, c), F32),
                        pltpu.VMEM((SUBLANES - 1, tb + halo - SUBLANES, c), F32),
                        pltpu.VMEM((SUBLANES - 1, tb + halo - SUBLANES, c), F32)] + exchange.scratch,
        compiler_params=_params(1),
    )(dco, dco, u, u, a, sb, conv_w, *exchange.ins)
    return res[:3], res[3:]


def _attn_bwd(q, k, v, do, lse2, delta, exchange, tq=512):
    t = q.shape[0]
    nq = t // tq
    td = delta.shape[-1]
    per = tq // td

    def body(q_ref, k_ref, v_ref, do_ref, lse_ref, dl_ref, dq_ref, dk_ref, dv_ref, dk_sc, dv_sc, s_sc, dp_sc, dq_sc):
        j = pl.program_id(1)

        @pl.when(j == 0)
        def _():
            dq_sc[...] = jnp.zeros_like(dq_sc)

        dk_sc[...] = jnp.zeros_like(dk_sc)
        dv_sc[...] = jnp.zeros_like(dv_sc)
        kb, vb = k_ref[...], v_ref[...]

        def rows(i):
            return pl.ds(pl.multiple_of(i * tq, tq), tq)

        def ahead(i, slot):
            i = jnp.minimum(i, nq - 1)
            s_sc[slot] = _dot_nt(kb, q_ref[rows(i), :])
            dp_sc[slot] = _dot_nt(vb, do_ref[rows(i), :])

        def finish(i, slot, masked):
            qb, dob = q_ref[rows(i), :], do_ref[rows(i), :]
            pt = jnp.exp2(s_sc[slot] - lse_ref[i])
            if masked:
                key = lax.broadcasted_iota(jnp.int32, (tq, tq), 0)
                qry = lax.broadcasted_iota(jnp.int32, (tq, tq), 1)
                pt = jnp.where(key <= qry, pt, 0.0)
            dv_sc[...] += _dot(pt.astype(BF16), dob)
            dl = jnp.concatenate([dl_ref[per * i + r] for r in range(per)], axis=-1)
            dst = (pt * (dp_sc[slot] - dl)).astype(BF16)
            dk_sc[...] += _dot(dst, qb)
            dq_sc[rows(i), :] += _dot_tn(dst, kb)

        def pair(p, carry):
            i = j + 1 + 2 * p
            ahead(i + 1, 0)
            finish(i, 1, False)
            ahead(i + 2, 1)
            finish(i + 1, 0, False)
            return carry

        n_after = nq - 1 - j
        ahead(j, 0)
        ahead(j + 1, 1)
        finish(j, 0, True)
        lax.fori_loop(0, n_after // 2, pair, 0)

        @pl.when(n_after % 2 == 1)
        def _():
            finish(nq - 1, 1, False)

        @pl.when(j == nq - 1)
        def _():
            dq_ref[...] = dq_sc[...].astype(BF16)

        dk_ref[...] = (dk_sc[...] * (1.0 / LOG2E)).astype(BF16)
        dv_ref[...] = dv_sc[...].astype(BF16)

    hp = N_HEADS * HEAD_PAD
    steps = _phase_steps(len(exchange.phases), N_HEADS * nq)
    fn = _hosted(body, 6, 3, exchange, lambda p: pl.program_id(0) * nq + pl.program_id(1) == steps[p])
    res = pl.pallas_call(
        fn, name="attn_bwd", grid=(N_HEADS, nq),
        in_specs=[pl.BlockSpec((t, HEAD_PAD), lambda h, j: (0, h)),
                  pl.BlockSpec((tq, HEAD_PAD), lambda h, j: (j, h)),
                  pl.BlockSpec((tq, NOPE), lambda h, j: (j, h)),
                  pl.BlockSpec((t, NOPE), lambda h, j: (0, h)),
                  pl.BlockSpec((None, nq, 1, tq), lambda h, j: (h, 0, 0, 0)),
                  pl.BlockSpec((None, t // td, 1, td), lambda h, j: (h, 0, 0, 0))] + [_ANY] * len(exchange.ins),
        out_specs=[pl.BlockSpec((t, HEAD_PAD), lambda h, j: (0, h)),
                   pl.BlockSpec((tq, HEAD_PAD), lambda h, j: (j, h)),
                   pl.BlockSpec((tq, NOPE), lambda h, j: (j, h))] + [_ANY] * len(exchange.out_shapes),
        out_shape=[jax.ShapeDtypeStruct((t, hp), BF16), jax.ShapeDtypeStruct((t, hp), BF16),
                   jax.ShapeDtypeStruct((t, D_MODEL), BF16)] + exchange.out_shapes,
        scratch_shapes=[pltpu.VMEM((tq, HEAD_PAD), F32), pltpu.VMEM((tq, NOPE), F32), pltpu.VMEM((2, tq, tq), F32),
                        pltpu.VMEM((2, tq, tq), F32), pltpu.VMEM((t, HEAD_PAD), F32)] + exchange.scratch,
        compiler_params=_params(2),
    )(q, k, v, do, lse2, delta, *exchange.ins)
    return res[:3], res[3:]


def _bwd_qkv(dq, dk, dv, zs, tc, tsa, tsb, q_norm, kv_norm, w_uq_p, w_uk, w_uv, tb=256):
    def body(dq_ref, dk_ref, dv_ref, zs_ref, c_ref, sa_ref, sb_ref, qg_ref, kg_ref, wq_ref, wk_ref, wv_ref,
             dqp_ref, dkn_ref, dzs_ref, dqg_ref, dkg_ref):
        c, sa, sb = c_ref[...], sa_ref[...], sb_ref[...]
        zs_ = zs_ref[...]
        dcqn = jnp.zeros((tb, Q_RANK), F32)
        dkr = jnp.zeros((tb, NOPE), F32)
        for h in range(N_HEADS):
            nope = slice(h * HEAD_PAD, h * HEAD_PAD + NOPE)
            rope = slice(h * HEAD_PAD + NOPE, (h + 1) * HEAD_PAD)
            dqp_ref[:, nope] = (dq_ref[:, nope].astype(F32) * QK_SCALE).astype(BF16)
            dqp_ref[:, rope] = (_rope_bwd(dq_ref[:, rope].astype(F32), c, sa, sb) * QK_SCALE).astype(BF16)
            dcqn = dcqn + _dot_nt(dqp_ref[:, h * HEAD_PAD:(h + 1) * HEAD_PAD], wq_ref[h])
            dkn_ref[:, h * NOPE:(h + 1) * NOPE] = dk_ref[:, nope]
            dkr = dkr + dk_ref[:, rope].astype(F32)
        dcq, dqg = _rms_bwd(dcqn, zs_[:, 0:Q_RANK], qg_ref[...])
        _acc(dqg_ref, dqg)
        dzs_ref[:, 0:Q_RANK] = dcq.astype(BF16)
        dzs_ref[:, Q_RANK + KV_RANK:ZS] = _rope_bwd(dkr, c, sa, sb).astype(BF16)
        dckvn = _dot_nt(dkn_ref[...], wk_ref[...]) + _dot_nt(dv_ref[...], wv_ref[...])
        dckv, dkg = _rms_bwd(dckvn, zs_[:, Q_RANK:Q_RANK + KV_RANK], kg_ref[...])
        _acc(dkg_ref, dkg)
        dzs_ref[:, Q_RANK:Q_RANK + KV_RANK] = dckv.astype(BF16)

    hp = N_HEADS * HEAD_PAD
    return _row_call(body, "bwd_qkv", tb, [dq, dk, dv, zs, tc, tsa, tsb], [q_norm, kv_norm, w_uq_p, w_uk, w_uv],
                     [(hp, BF16), (D_MODEL, BF16), (ZS, BF16)], [((1, Q_RANK), F32), ((1, KV_RANK), F32)])


def _bwd_in_proj(dzs, da, db, dga, dgc, x, dx1, w_in_p, g_pre, exchange, tb=256):
    def body(dzs_ref, da_ref, db_ref, dga_ref, dgc_ref, x_ref, dx1_ref, w_ref, g_ref, gx_ref, dg_ref):
        dh = _dot_nt(dzs_ref[...], w_ref[:, 0:ZS])
        dh = dh + _dot_nt(da_ref[...], w_ref[:, OFF_A:OFF_B])
        dh = dh + _dot_nt(db_ref[...], w_ref[:, OFF_B:OFF_GA])
        dh = dh + _dot_nt(dga_ref[...], w_ref[:, OFF_GA:OFF_GC])
        dh = dh + _dot_nt(dgc_ref[...], w_ref[:, OFF_GC:D_IN_PAD])
        dxn, dg = _rms_bwd(dh, x_ref[...], g_ref[...])
        _acc(dg_ref, dg)
        gx_ref[...] = dx1_ref[...] + dxn

    return _row_call(body, "bwd_in_proj", tb, [dzs, da, db, dga, dgc, x, dx1], [w_in_p, g_pre],
                     [(D_MODEL, F32)], [((1, D_MODEL), F32)], exchange=exchange)


def _mm_tn(a, b, name, shard_cols=None, tt=2048):
    t, m = a.shape
    n = b.shape[1]
    tm = min(m, 1024)
    tn = min(n, 1024)
    tt = t if (m // tm) * (n // tn) >= 4 else min(t, tt)
    nt = t // tt
    per = tn // shard_cols if shard_cols else 1

    def body(a_ref, b_ref, o_ref, acc):
        k = pl.program_id(2)

        @pl.when(k == 0)
        def _():
            acc[...] = jnp.zeros_like(acc)

        acc[...] += _dot_tn(a_ref[...], b_ref[...])

        @pl.when(k == nt - 1)
        def _():
            if shard_cols:
                for s in range(per):
                    o_ref[s] = acc[:, s * shard_cols:(s + 1) * shard_cols].astype(BF16)
            else:
                o_ref[...] = acc[...].astype(BF16)

    if shard_cols:
        out_spec = pl.BlockSpec((per, tm, shard_cols), lambda i, j, k: (j, i, 0))
        out_shape = jax.ShapeDtypeStruct((n // shard_cols, m, shard_cols), BF16)
    else:
        out_spec = pl.BlockSpec((tm, tn), lambda i, j, k: (i, j))
        out_shape = jax.ShapeDtypeStruct((m, n), BF16)
    return pl.pallas_call(
        functools.partial(body), name=name, grid=(m // tm, n // tn, nt),
        in_specs=[pl.BlockSpec((tt, tm), lambda i, j, k: (k, i)), pl.BlockSpec((tt, tn), lambda i, j, k: (k, j))],
        out_specs=out_spec, out_shape=out_shape, scratch_shapes=[pltpu.VMEM((tm, tn), F32)],
        compiler_params=_params(3),
    )(a, b)


_ANY = pl.BlockSpec(memory_space=pl.ANY)
_MESH = pl.DeviceIdType.MESH

_Exchange = collections.namedtuple("_Exchange", "ins out_shapes scratch phases")


def _ag_exchange(shards):
    n = len(shards)

    def parts(ins, outs, sems):
        send_sems, recv_sems, _ = sems
        x, y, c = lax.axis_index("x"), lax.axis_index("y"), lax.axis_index("c")
        chips = [(1 - x, y), (x, 1 - y), (1 - x, 1 - y)]

        def copy(w, k, block, to, src=None):
            dst = outs[w].at[4 * block[0] + 2 * block[1] + block[2]]
            return pltpu.make_async_remote_copy(
                src_ref=dst if src is None else src, dst_ref=dst, send_sem=send_sems.at[7 * w + k],
                recv_sem=recv_sems.at[7 * w + k], device_id=to, device_id_type=_MESH)

        def first(w):
            return [copy(w, 0, (x, y, c), (x, y, 1 - c), src=ins[w])] + [
                copy(w, 1 + j, (x, y, c), (*chip, c), src=ins[w]) for j, chip in enumerate(chips)]

        def mine(w):
            return pltpu.make_async_copy(ins[w], outs[w].at[4 * x + 2 * y + c], sems[2].at[w])

        return (x, y, c), chips, copy, first, mine

    def start(ins, outs, sems):
        _, _, _, first, mine = parts(ins, outs, sems)
        for w in range(n):
            mine(w).start()
            for cp in first(w):
                cp.start()

    def forward(ins, outs, sems):
        (x, y, c), chips, copy, _, _ = parts(ins, outs, sems)
        for j, chip in enumerate(chips):
            for w in range(n):
                copy(w, 1 + j, (*chip, c), (x, y, c)).wait_recv()
                copy(w, 4 + j, (*chip, c), (x, y, 1 - c)).start()

    def finish(ins, outs, sems):
        (x, y, c), chips, copy, first, mine = parts(ins, outs, sems)
        for w in range(n):
            copy(w, 0, (x, y, 1 - c), (x, y, c)).wait_recv()
        for j, chip in enumerate(chips):
            for w in range(n):
                copy(w, 4 + j, (*chip, 1 - c), (x, y, c)).wait_recv()
        for w in range(n):
            for cp in first(w) + [copy(w, 4 + j, (*chip, c), (x, y, 1 - c)) for j, chip in enumerate(chips)]:
                cp.wait_send()
            mine(w).wait()

    return _Exchange(
        ins=list(shards), out_shapes=[jax.ShapeDtypeStruct((N_DEV,) + s.shape, s.dtype) for s in shards],
        scratch=[pltpu.SemaphoreType.DMA((7 * n,)), pltpu.SemaphoreType.DMA((7 * n,)), pltpu.SemaphoreType.DMA((n,))],
        phases=[start, forward, finish])


def _pair_exchange(gs):
    n = len(gs)

    def copies(ins, outs, sems):
        x, y, c = lax.axis_index("x"), lax.axis_index("y"), lax.axis_index("c")
        return [pltpu.make_async_remote_copy(
            src_ref=ins[w].at[:, 1 - c], dst_ref=outs[w], send_sem=sems[0].at[w], recv_sem=sems[1].at[w],
            device_id=(x, y, 1 - c), device_id_type=_MESH) for w in range(n)]

    return _start_then_wait(gs, [jax.ShapeDtypeStruct((4,) + g.shape[2:], g.dtype) for g in gs], n, copies)


def _start_then_wait(ins, out_shapes, n_copies, copies):
    def start(ins_, outs, sems):
        for cp in copies(ins_, outs, sems):
            cp.start()

    def finish(ins_, outs, sems):
        for cp in copies(ins_, outs, sems):
            cp.wait()

    return _Exchange(ins=list(ins), out_shapes=out_shapes,
                     scratch=[pltpu.SemaphoreType.DMA((n_copies,)), pltpu.SemaphoreType.DMA((n_copies,))],
                     phases=[start, finish])


def _run_exchange(ex, name):
    ni, no = len(ex.ins), len(ex.out_shapes)

    def body(*refs):
        for phase in ex.phases:
            phase(refs[:ni], refs[ni:ni + no], refs[ni + no:])

    return pl.pallas_call(functools.partial(body), name=name, out_shape=ex.out_shapes, in_specs=[_ANY] * ni,
                          out_specs=[_ANY] * no, scratch_shapes=ex.scratch)(*ex.ins)


def _hosted(body, n_in, n_out, ex, when):
    ni, no, ns = len(ex.ins), len(ex.out_shapes), len(ex.scratch)

    def fn(*refs):
        ins, ex_ins = refs[:n_in], refs[n_in:n_in + ni]
        outs = refs[n_in + ni:n_in + ni + n_out]
        ex_outs = refs[n_in + ni + n_out:n_in + ni + n_out + no]
        scratch, sems = refs[n_in + ni + n_out + no:len(refs) - ns], refs[len(refs) - ns:]
        last = len(ex.phases) - 1
        for k in range(last):
            pl.when(when(k))(functools.partial(ex.phases[k], ex_ins, ex_outs, sems))
        body(*ins, *outs, *scratch)
        pl.when(when(last))(functools.partial(ex.phases[last], ex_ins, ex_outs, sems))

    return fn


def _phase_steps(n_phases, n_steps):
    return [0, n_steps - 1] if n_phases == 2 else [0, 2 * n_steps // 3, n_steps - 1]


def _row_block(shape, steps, lead, pick):
    blk = (None,) * lead + (shape[0] // steps,) + tuple(shape[1:])
    return pl.BlockSpec(blk, lambda *a: tuple(pick(*a)) + (a[-2],) + (0,) * (len(shape) - 1))


def _pair_sum(gs, ls, c_idx, name, steps):
    n = len(gs)

    def body(c_ref, *refs):
        for w in range(n):
            refs[2 * n + w][...] = (refs[w][...].astype(F32) + refs[n + w][...].astype(F32)).astype(BF16)

    shapes = [g.shape[2:] for g in gs]
    return pl.pallas_call(
        functools.partial(body), name=name,
        grid_spec=pltpu.PrefetchScalarGridSpec(
            num_scalar_prefetch=1, grid=(4, steps),
            in_specs=[_row_block(s, steps, 2, lambda k, i, c: (k, c[0])) for s in shapes]
            + [_row_block(s, steps, 1, lambda k, i, c: (k,)) for s in shapes],
            out_specs=[_row_block(s, steps, 1, lambda k, i, c: (k,)) for s in shapes]),
        out_shape=[jax.ShapeDtypeStruct((4,) + tuple(s), BF16) for s in shapes], compiler_params=_params(2),
    )(c_idx, *gs, *ls)


def _chip_exchange(ps):
    n = len(ps)

    def copies(ins, outs, sems):
        x, y, c = lax.axis_index("x"), lax.axis_index("y"), lax.axis_index("c")
        chips = [(1 - x, y), (x, 1 - y), (1 - x, 1 - y)]
        return [pltpu.make_async_remote_copy(
            src_ref=ins[w].at[2 * px + py], dst_ref=outs[w].at[s], send_sem=sems[0].at[3 * w + s],
            recv_sem=sems[1].at[3 * w + s], device_id=(px, py, c), device_id_type=_MESH)
            for w in range(n) for s, (px, py) in enumerate(chips)]

    return _start_then_wait(ps, [jax.ShapeDtypeStruct((3,) + p.shape[1:], p.dtype) for p in ps], 3 * n, copies)


def _adamw(w, g, m, v):
    m2 = ADAM_B1 * m + (1.0 - ADAM_B1) * g
    v2 = ADAM_B2 * v + (1.0 - ADAM_B2) * (g * g)
    m_hat = m2 / (1.0 - ADAM_B1 ** ADAM_STEP)
    v_hat = v2 / (1.0 - ADAM_B2 ** ADAM_STEP)
    delta = -ADAM_LR * (m_hat / (jnp.sqrt(v_hat) + ADAM_EPS) + ADAM_WD * w)
    return delta, m2, v2


def _update(gs, ls, qs, ws, ms, vs, idx, name, steps):
    n = len(gs)

    def body(idx_ref, *refs):
        g, l, q, w, m, v = (refs[k * n:(k + 1) * n] for k in range(6))
        outs = refs[6 * n:]
        for i in range(n):
            gr = g[i][...].astype(F32) + l[i][...].astype(F32)
            gr = gr + q[i][0].astype(F32)
            gr = gr + q[i][1].astype(F32)
            gr = gr + q[i][2].astype(F32)
            outs[4 * i][...] = gr
            outs[4 * i + 1][...], outs[4 * i + 2][...], outs[4 * i + 3][...] = _adamw(w[i][...], gr, m[i][...], v[i][...])

    shapes = [w.shape for w in ws]
    own = [_row_block(s, steps, 0, lambda i, c: ()) for s in shapes]
    res = pl.pallas_call(
        functools.partial(body), name=name,
        grid_spec=pltpu.PrefetchScalarGridSpec(
            num_scalar_prefetch=1, grid=(steps,),
            in_specs=[_row_block(s, steps, 2, lambda i, c: (c[0], c[1])) for s in shapes]
            + [_row_block(s, steps, 1, lambda i, c: (c[0],)) for s in shapes]
            + [pl.BlockSpec((3, s[0] // steps) + tuple(s[1:]), lambda i, c, nd=len(s): (0, i) + (0,) * (nd - 1))
               for s in shapes] + own * 3,
            out_specs=[b for b in own for _ in range(4)]),
        out_shape=[jax.ShapeDtypeStruct(s, F32) for s in shapes for _ in range(4)], compiler_params=_params(1),
    )(idx, *gs, *ls, *qs, *ws, *ms, *vs)
    return [res[4 * i:4 * i + 4] for i in range(n)]


def _update_small(sv_all, ws, ms, vs):
    n = len(ws)

    def body(all_ref, *refs):
        w, m, v = refs[:n], refs[n:2 * n], refs[2 * n:3 * n]
        loss_ref, outs = refs[3 * n], refs[3 * n + 1:]
        total = all_ref[0]
        for dev in range(1, N_DEV):
            total = total + all_ref[dev]
        loss_ref[...] = total[LOSS_ROW:LOSS_ROW + 1, :]
        for i in range(n):
            gr = total[i:i + 1, 0:w[i].shape[1]]
            outs[4 * i][...] = gr
            outs[4 * i + 1][...], outs[4 * i + 2][...], outs[4 * i + 3][...] = _adamw(w[i][...], gr, m[i][...], v[i][...])

    res = pl.pallas_call(
        functools.partial(body), name="update_small",
        out_shape=[jax.ShapeDtypeStruct((1, 1024), F32)]
        + [jax.ShapeDtypeStruct(a.shape, F32) for a in ws for _ in range(4)],
    )(sv_all, *ws, *ms, *vs)
    return res[0], [res[1 + 4 * i:5 + 4 * i] for i in range(n)]


def _to_exchange(shards):
    return [jnp.pad(shards[n][0], ((0, 0), (0, HEAD_PAD - NOPE - ROPE))) if n == "w_uq" else shards[n][0]
            for n in _BIG]


def _from_exchange(arrs, like):
    return {n: (a[:, :NOPE + ROPE] if n == "w_uq" else a).reshape(like[n].shape) for n, a in zip(_BIG, arrs)}


def _padded_w_in(g_in):
    w_in = g_in.transpose(1, 0, 2).reshape(D_MODEL, N_DEV * IN_SHARD)
    kr_end = Q_RANK + KV_RANK + ROPE
    return jnp.concatenate([w_in[:, :kr_end], jnp.zeros((D_MODEL, 128 - ROPE), BF16), w_in[:, kr_end:]], axis=1)


def _full_conv_w(g_conv):
    taps = (g_conv[:, 0].astype(F32) + g_conv[:, 1].astype(F32)).reshape(N_DEV, CONV_W, D_MODEL // N_DEV)
    return jnp.pad(taps.transpose(1, 0, 2).reshape(CONV_W, D_MODEL), ((0, 1), (0, 0)))


def _pad_rows(a, rows):
    return jnp.pad(a, ((0, rows - a.shape[0]), (0, 0)))


def _small_pack(vals):
    rows = [jnp.pad(v.reshape(1, -1), ((0, 0), (0, 1024 - v.size))) for v in vals]
    return _pad_rows(jnp.concatenate(rows, axis=0), SMALL_ROWS)


def _rope_tables(positions):
    inv_freq = ROPE_THETA ** (-jnp.arange(0, ROPE, 2, dtype=F32) / ROPE)
    ang = positions.reshape(-1).astype(F32)[:, None] * inv_freq
    cos, sin = jnp.cos(ang), jnp.sin(ang)
    t = cos.shape[0]
    z32, z64 = jnp.zeros((t, HALF), F32), jnp.zeros((t, HEAD_PAD - NOPE - ROPE), F32)
    tc = jnp.concatenate([cos, cos, z64], axis=1)
    tsa = jnp.concatenate([-sin, z32, z64], axis=1)
    tsb = jnp.concatenate([z32, sin, z64], axis=1)
    return tc, tsa, tsb


def _blocks(dw):
    if dw.ndim == 2:
        dw = dw.reshape(N_DEV, dw.shape[0] // N_DEV, dw.shape[1])
    return dw.reshape((4, 2) + dw.shape[1:])


def _step(x, positions, target, small, send, c_idx):
    s_in, s_uq, s_uk, s_uv, s_o, s_conv, s_pw2, s_out, s_ff1, s_ff2 = send
    tc, tsa, tsb = _rope_tables(positions)

    w_in_p = _padded_w_in(_run_exchange(_ag_exchange([s_in]), "ag_w_in")[0])
    (h, zs, a, sb, u, sa, sc), (w_uq, w_uk, w_uv) = _fwd_in_proj(
        x, small["norm_mix_pre"], w_in_p, _ag_exchange([s_uq, s_uk, s_uv]))
    w_uk, w_uv = w_uk.reshape(KV_RANK, -1), w_uv.reshape(KV_RANK, -1)
    cqn, ckvn, q, k, v = _fwd_qkv(zs, tc, tsa, tsb, small["q_norm"], small["kv_norm"], w_uq, w_uk, w_uv)
    (attn, lse), (w_o, g_conv, w_pw2, w_out, w_ff1, w_ff2) = _attn_fwd(
        q, k, v, _ag_exchange([s_o, s_conv, s_pw2, s_out, s_ff1, s_ff2]))
    w_o, w_pw2, w_out = (w.reshape(D_MODEL, D_MODEL) for w in (w_o, w_pw2, w_out))
    w_ff2, conv_w = w_ff2.reshape(D_FF, D_MODEL), _full_conv_w(g_conv)
    co, uact = _conv_fwd(u, conv_w, small["conv_b"], small["conv_ln_g"], small["conv_ln_b"])
    ya, yc, mb, m, x1 = _fwd_merge(attn, uact, sa, sc, x, w_o, w_pw2, small["b_pw2"], w_out, small["norm_mix_post"])
    h2, r1, act = _fwd_ff1(x1, small["norm_mlp_pre"], w_ff1)
    f, dy, loss_blk = _fwd_ff2_loss(act, x1, target, w_ff2, small["norm_mlp_post"])

    df, df1, dg_mlp_post = _bwd_ff2(dy, f, r1, w_ff2, small["norm_mlp_post"])
    g_ff2 = _blocks(_mm_tn(act, df, "dw_ff2"))
    (dx1, dg_mlp_pre), l_ff2 = _bwd_ff1(df1, x1, dy, w_ff1, small["norm_mlp_pre"], _pair_exchange([g_ff2]))
    g_ff1 = _blocks(_mm_tn(h2, df1, "dw_ff1", shard_cols=FF_SHARD))
    (dmb, dya, dyc, dga, dgc, dat, dua, dg_mix_post, db_pw2, delta), l_ff1 = _bwd_merge(
        dx1, m, sa, sc, ya, yc, attn, w_out, w_o, w_pw2, small["norm_mix_post"], _pair_exchange([g_ff1]))
    g_ff, l_ff = [g_ff1, g_ff2], [l_ff1[0], l_ff2[0]]
    p_ff = _pair_sum(g_ff, l_ff, c_idx, "rs_pair_sum_ff", 2)
    g_mix = [_blocks(_mm_tn(attn, dya, "dw_o")), _blocks(_mm_tn(uact, dyc, "dw_pw2")),
             _blocks(_mm_tn(mb, dmb, "dw_out"))]
    dco, dln_g, dln_b, dconv_b = _bwd_ln(dua, co, small["conv_ln_g"], small["conv_ln_b"])
    (da, db, dconv), l_mix = _conv_bwd(dco, u, a, sb, conv_w, _pair_exchange(g_mix))
    p_mix = _pair_sum(g_mix, l_mix, c_idx, "rs_pair_sum_mix", 1)
    (dq, dk, dv), q_early = _attn_bwd(q, k, v, dat, lse, delta, _chip_exchange(p_ff + p_mix))
    dqp, dkn, dzs, dq_norm, dkv_norm = _bwd_qkv(dq, dk, dv, zs, tc, tsa, tsb, small["q_norm"], small["kv_norm"],
                                                w_uq, w_uk, w_uv)
    dw_in = jnp.concatenate([_mm_tn(h, dzs, "dw_in_zs")[:, :Q_RANK + KV_RANK + ROPE], _mm_tn(h, da, "dw_in_a"),
                             _mm_tn(h, db, "dw_in_b"), _mm_tn(h, dga, "dw_in_ga"), _mm_tn(h, dgc, "dw_in_gc")],
                            axis=1)
    g_late = [_blocks(dw_in.reshape(D_MODEL, N_DEV, IN_SHARD).transpose(1, 0, 2)),
              _blocks(_mm_tn(cqn, dqp, "dw_uq", shard_cols=HEAD_PAD)),
              _blocks(_mm_tn(ckvn, dkn, "dw_uk").reshape(N_DEV, -1, N_HEADS, NOPE)),
              _blocks(_mm_tn(ckvn, dv, "dw_uv").reshape(N_DEV, -1, N_HEADS, NOPE)),
              _blocks(dconv[:CONV_W].reshape(CONV_W, N_DEV, 1, -1).transpose(1, 0, 2, 3).astype(BF16))]
    l_late = _run_exchange(_pair_exchange(g_late), "rs_pair_exchange_late")
    p_late = _pair_sum(g_late, l_late, c_idx, "rs_pair_sum_late", 1)
    (grad_x, dg_pre), q_late = _bwd_in_proj(dzs, da, db, dga, dgc, x, dx1, w_in_p, small["norm_mix_pre"],
                                            _chip_exchange(p_late))

    order = lambda late, mix, ff: list(late[:4]) + [mix[0], late[4], mix[1], mix[2]] + list(ff)
    exchanged = [order(g_late, g_mix, g_ff), order(l_late, l_mix, l_ff),
                 order(q_late, q_early[2:], q_early[:2])]
    small_grads = (dg_pre, dq_norm, dkv_norm, dconv_b, dln_g, dln_b, db_pw2, dg_mix_post, dg_mlp_pre, dg_mlp_post)
    return loss_blk, grad_x, small_grads, exchanged


def kernel(x, positions, norm_mix_pre, w_in, q_norm, w_uq, kv_norm, w_uk, w_uv, w_o_attn, conv_w, conv_b, conv_ln_g, conv_ln_b, w_pw2, b_pw2, w_out, norm_mix_post, norm_mlp_pre, w_ff1, w_ff2, norm_mlp_post, loss_target, m_norm_mix_pre, m_w_in, m_q_norm, m_w_uq, m_kv_norm, m_w_uk, m_w_uv, m_w_o_attn, m_conv_w, m_conv_b, m_conv_ln_g, m_conv_ln_b, m_w_pw2, m_b_pw2, m_w_out, m_norm_mix_post, m_norm_mlp_pre, m_w_ff1, m_w_ff2, m_norm_mlp_post, v_norm_mix_pre, v_w_in, v_q_norm, v_w_uq, v_kv_norm, v_w_uk, v_w_uv, v_w_o_attn, v_conv_w, v_conv_b, v_conv_ln_g, v_conv_ln_b, v_w_pw2, v_b_pw2, v_w_out, v_norm_mix_post, v_norm_mlp_pre, v_w_ff1, v_w_ff2, v_norm_mlp_post):
    wts = dict(norm_mix_pre=norm_mix_pre, w_in=w_in, q_norm=q_norm, w_uq=w_uq, kv_norm=kv_norm, w_uk=w_uk, w_uv=w_uv,
               w_o_attn=w_o_attn, conv_w=conv_w, conv_b=conv_b, conv_ln_g=conv_ln_g, conv_ln_b=conv_ln_b,
               w_pw2=w_pw2, b_pw2=b_pw2, w_out=w_out, norm_mix_post=norm_mix_post, norm_mlp_pre=norm_mlp_pre,
               w_ff1=w_ff1, w_ff2=w_ff2, norm_mlp_post=norm_mlp_post)
    mom_m = dict(norm_mix_pre=m_norm_mix_pre, w_in=m_w_in, q_norm=m_q_norm, w_uq=m_w_uq, kv_norm=m_kv_norm,
                 w_uk=m_w_uk, w_uv=m_w_uv, w_o_attn=m_w_o_attn, conv_w=m_conv_w, conv_b=m_conv_b,
                 conv_ln_g=m_conv_ln_g, conv_ln_b=m_conv_ln_b, w_pw2=m_w_pw2, b_pw2=m_b_pw2, w_out=m_w_out,
                 norm_mix_post=m_norm_mix_post, norm_mlp_pre=m_norm_mlp_pre, w_ff1=m_w_ff1, w_ff2=m_w_ff2,
                 norm_mlp_post=m_norm_mlp_post)
    mom_v = dict(norm_mix_pre=v_norm_mix_pre, w_in=v_w_in, q_norm=v_q_norm, w_uq=v_w_uq, kv_norm=v_kv_norm,
                 w_uk=v_w_uk, w_uv=v_w_uv, w_o_attn=v_w_o_attn, conv_w=v_conv_w, conv_b=v_conv_b,
                 conv_ln_g=v_conv_ln_g, conv_ln_b=v_conv_ln_b, w_pw2=v_w_pw2, b_pw2=v_b_pw2, w_out=v_w_out,
                 norm_mix_post=v_norm_mix_post, norm_mlp_pre=v_norm_mlp_pre, w_ff1=v_w_ff1, w_ff2=v_w_ff2,
                 norm_mlp_post=v_norm_mlp_post)
    cx, cy, cc = lax.axis_index("x"), lax.axis_index("y"), lax.axis_index("c")

    big_local = {n: wts[n] for n in _BIG}
    w_ex = _to_exchange(big_local)
    send = [a.astype(BF16) for a in w_ex]
    conv_i = _BIG.index("conv_w")
    conv_lo = (w_ex[conv_i] - send[conv_i].astype(F32)).astype(BF16)
    send[conv_i] = jnp.stack([send[conv_i], conv_lo])

    small = {n: wts[n].reshape(1, -1) for n in _SMALL}
    c_idx = cc.reshape(1).astype(jnp.int32)
    loss_blk, grad_x, small_grads, (g4, l_sib, q_in) = _step(x[0], positions, loss_target[0], small, send, c_idx)

    idx = jnp.stack([2 * cx + cy, cc]).astype(jnp.int32)
    m_ex, v_ex = _to_exchange({n: mom_m[n] for n in _BIG}), _to_exchange({n: mom_v[n] for n in _BIG})
    upd = [None] * len(_BIG)
    for group, steps in ((("w_in", "w_uq", "w_ff1", "w_ff2"), 4),
                         (("w_uk", "w_uv", "w_o_attn", "conv_w", "w_pw2", "w_out"), 1)):
        ids = [_BIG.index(n) for n in group]
        pick = lambda arrs: [arrs[i] for i in ids]
        res = _update(pick(g4), pick(l_sib), pick(q_in), pick(w_ex), pick(m_ex), pick(v_ex), idx,
                      "update_" + group[0], steps)
        for i, r in zip(ids, res):
            upd[i] = r
    out_g, out_d, out_m, out_v = (_from_exchange([u[j] for u in upd], big_local) for j in range(4))

    loss_row = jnp.broadcast_to(loss_blk[0:1, 0:1], (1, 1024))
    sv = _small_pack(list(small_grads) + [loss_row])
    sv_all = _run_exchange(_ag_exchange([sv]), "ag_small")[0]
    loss_sum, upd_small = _update_small(sv_all, [wts[n] for n in _SMALL], [mom_m[n] for n in _SMALL],
                                        [mom_v[n] for n in _SMALL])
    for n, r in zip(_SMALL, upd_small):
        out_g[n], out_d[n], out_m[n], out_v[n] = r
    loss = loss_sum[0, 0] * (0.5 / D_MODEL)

    return (loss, grad_x[None], *[out_g[n] for n in _WEIGHTS], *[out_d[n] for n in _WEIGHTS],
            *[out_m[n] for n in _WEIGHTS], *[out_v[n] for n in _WEIGHTS])
```

```python
import collections
import functools

import jax
import jax.numpy as jnp
from jax import lax
from jax.experimental import pallas as pl
from jax.experimental.pallas import tpu as pltpu
from jax.experimental.pallas import tpu_sc as plsc

F32 = jnp.float32
BF16 = jnp.bfloat16

D_MODEL = 1024
N_HEADS = 8
NOPE = 128
ROPE = 64
HALF = ROPE // 2
Q_RANK = 384
KV_RANK = 256
CONV_W = 31
D_FF = 4096
EPS = 1e-6
ROPE_THETA = 10000.0
HEAD_PAD = 256
QK_SCALE = (NOPE + ROPE) ** -0.5
LOG2E = 1.4426950408889634
SUBLANES = 8
N_DEV = 8
FF_SHARD = D_FF // N_DEV
IN_SHARD = 4800 // N_DEV

ZS = Q_RANK + KV_RANK + 128
OFF_A = ZS
OFF_B = OFF_A + D_MODEL
OFF_GA = OFF_B + D_MODEL
OFF_GC = OFF_GA + D_MODEL
D_IN_PAD = OFF_GC + D_MODEL

ADAM_LR = 0.001
ADAM_B1 = 0.9
ADAM_B2 = 0.999
ADAM_EPS = 1e-08
ADAM_WD = 0.01
ADAM_STEP = 10

VMEM_LIMIT = 56 * 1024 * 1024

_SMALL = ("norm_mix_pre", "q_norm", "kv_norm", "conv_b", "conv_ln_g", "conv_ln_b", "b_pw2", "norm_mix_post",
          "norm_mlp_pre", "norm_mlp_post")
SMALL_ROWS = 16
LOSS_ROW = len(_SMALL)

_BIG = ("w_in", "w_uq", "w_uk", "w_uv", "w_o_attn", "conv_w", "w_pw2", "w_out", "w_ff1", "w_ff2")
_WEIGHTS = ("norm_mix_pre", "w_in", "q_norm", "w_uq", "kv_norm", "w_uk", "w_uv", "w_o_attn", "conv_w", "conv_b",
            "conv_ln_g", "conv_ln_b", "w_pw2", "b_pw2", "w_out", "norm_mix_post", "norm_mlp_pre", "w_ff1", "w_ff2",
            "norm_mlp_post")


def _dot(a, b):
    return jnp.dot(a, b, preferred_element_type=F32)


def _dot_nt(a, b):
    return lax.dot_general(a, b, (((1,), (1,)), ((), ())), preferred_element_type=F32)


def _dot_tn(a, b):
    return lax.dot_general(a, b, (((0,), (0,)), ((), ())), preferred_element_type=F32)


def _sigmoid(x):
    return 1.0 / (1.0 + jnp.exp(-x))


def _rms_fwd(x, g):
    r = lax.rsqrt(jnp.mean(x * x, axis=-1, keepdims=True) + EPS)
    return x * r * g


def _rms_bwd(dy, x, g):
    r = lax.rsqrt(jnp.mean(x * x, axis=-1, keepdims=True) + EPS)
    xh = x * r
    gy = dy * g
    dx = r * (gy - xh * jnp.mean(gy * xh, axis=-1, keepdims=True))
    return dx, jnp.sum(dy * xh, axis=0, keepdims=True)


def _rope(q, c, sa, sb):
    n = q.shape[-1]
    return q * c + pltpu.roll(q, n - HALF, 1) * sa + pltpu.roll(q, HALF, 1) * sb


def _rope_bwd(d, c, sa, sb):
    n = d.shape[-1]
    return d * c - pltpu.roll(d, n - HALF, 1) * sa - pltpu.roll(d, HALF, 1) * sb


def _shifted_copies(buf, shifted, tb):
    n = shifted.shape[1]
    for b in range(1, SUBLANES):
        shifted[b - 1] = buf[pl.ds(b, n), :]


def _rows_at(buf, shifted, start, tb):
    a, b = divmod(start, SUBLANES)
    src = buf if b == 0 else shifted.at[b - 1]
    return src[pl.ds(SUBLANES * a, tb), :]


def _params(n_axes=1):
    return pltpu.CompilerParams(dimension_semantics=("arbitrary",) * n_axes, vmem_limit_bytes=VMEM_LIMIT)


def _row_call(body, name, tb, row_ins, full_ins, row_outs, acc_outs, lane_outs=(), exchange=None):
    t = row_ins[0].shape[0]
    in_specs = [pl.BlockSpec((tb, a.shape[1]), lambda i: (i, 0)) for a in row_ins]
    in_specs += [pl.BlockSpec(a.shape, lambda i, nd=a.ndim: (0,) * nd) for a in full_ins]
    out_specs = [pl.BlockSpec((tb, c), lambda i: (i, 0)) for c, _ in row_outs]
    out_specs += [pl.BlockSpec(s, lambda i, nd=len(s): (0,) * nd) for s, _ in acc_outs]
    out_specs += [pl.BlockSpec((n, None, 1, tb), lambda i: (0, i, 0, 0)) for n in lane_outs]
    out_shape = [jax.ShapeDtypeStruct((t, c), dt) for c, dt in row_outs]
    out_shape += [jax.ShapeDtypeStruct(s, dt) for s, dt in acc_outs]
    out_shape += [jax.ShapeDtypeStruct((n, t // tb, 1, tb), F32) for n in lane_outs]
    if exchange is None:
        return pl.pallas_call(
            functools.partial(body), name=name, grid=(t // tb,), in_specs=in_specs, out_specs=out_specs,
            out_shape=out_shape, compiler_params=_params(1),
        )(*row_ins, *full_ins)
    steps = _phase_steps(len(exchange.phases), t // tb)
    fn = _hosted(body, len(in_specs), len(out_specs), exchange, lambda k: pl.program_id(0) == steps[k])
    res = pl.pallas_call(
        fn, name=name, grid=(t // tb,), in_specs=in_specs + [_ANY] * len(exchange.ins),
        out_specs=out_specs + [_ANY] * len(exchange.out_shapes), out_shape=out_shape + exchange.out_shapes,
        scratch_shapes=exchange.scratch, compiler_params=_params(1),
    )(*row_ins, *full_ins, *exchange.ins)
    return res[:len(out_specs)], res[len(out_specs):]


def _acc(ref, val):
    @pl.when(pl.program_id(0) == 0)
    def _():
        ref[...] = jnp.zeros_like(ref)
    ref[...] += val


def _fwd_in_proj(x, g_pre, w_in_p, exchange, tb=512):
    def body(x_ref, g_ref, w_ref, h_ref, zs_ref, a_ref, sb_ref, u_ref, sa_ref, sc_ref):
        hb = _rms_fwd(x_ref[...], g_ref[...]).astype(BF16)
        h_ref[...] = hb
        zs_ref[...] = _dot(hb, w_ref[:, 0:ZS])
        a = _dot(hb, w_ref[:, OFF_A:OFF_B])
        sb = _sigmoid(_dot(hb, w_ref[:, OFF_B:OFF_GA]))
        a_ref[...] = a
        sb_ref[...] = sb
        u_ref[...] = a * sb
        sa_ref[...] = _sigmoid(_dot(hb, w_ref[:, OFF_GA:OFF_GC]))
        sc_ref[...] = _sigmoid(_dot(hb, w_ref[:, OFF_GC:D_IN_PAD]))

    d = D_MODEL
    return _row_call(body, "fwd_in_proj", tb, [x], [g_pre, w_in_p],
                     [(d, BF16), (ZS, F32), (d, F32), (d, F32), (d, F32), (d, F32), (d, F32)], [], exchange=exchange)


def _fwd_qkv(zs, tc, tsa, tsb, q_norm, kv_norm, w_uq_p, w_uk, w_uv, tb=256):
    def body(zs_ref, c_ref, sa_ref, sb_ref, qg_ref, kg_ref, wq_ref, wk_ref, wv_ref,
             cqn_ref, ckvn_ref, q_ref, k_ref, v_ref):
        zs_ = zs_ref[...]
        c, sa, sb = c_ref[...], sa_ref[...], sb_ref[...]
        cqn = _rms_fwd(zs_[:, 0:Q_RANK], qg_ref[...]).astype(BF16)
        cqn_ref[...] = cqn
        for h in range(N_HEADS):
            qh = _dot(cqn, wq_ref[h]) * (QK_SCALE * LOG2E)
            q_ref[:, h * HEAD_PAD:h * HEAD_PAD + NOPE] = qh[:, :NOPE].astype(BF16)
            q_ref[:, h * HEAD_PAD + NOPE:(h + 1) * HEAD_PAD] = _rope(qh[:, NOPE:], c, sa, sb).astype(BF16)
        kr = _rope(zs_[:, Q_RANK + KV_RANK:ZS], c, sa, sb).astype(BF16)
        ckvn = _rms_fwd(zs_[:, Q_RANK:Q_RANK + KV_RANK], kg_ref[...]).astype(BF16)
        ckvn_ref[...] = ckvn
        kn = _dot(ckvn, wk_ref[...]).astype(BF16)
        v_ref[...] = _dot(ckvn, wv_ref[...]).astype(BF16)
        for h in range(N_HEADS):
            k_ref[:, h * HEAD_PAD:h * HEAD_PAD + NOPE] = kn[:, h * NOPE:(h + 1) * NOPE]
            k_ref[:, h * HEAD_PAD + NOPE:(h + 1) * HEAD_PAD] = kr

    hp = N_HEADS * HEAD_PAD
    return _row_call(body, "fwd_qkv", tb, [zs, tc, tsa, tsb], [q_norm, kv_norm, w_uq_p, w_uk, w_uv],
                     [(Q_RANK, BF16), (KV_RANK, BF16), (hp, BF16), (hp, BF16), (D_MODEL, BF16)], [])


def _attn_fwd(q, k, v, exchange, tq=512):
    t = q.shape[0]
    nq = t // tq

    def body(q_ref, k_ref, v_ref, o_ref, lse_ref, m_sc, l_sc, acc_sc, s_sc):
        i = pl.program_id(1)
        m_sc[...] = jnp.full_like(m_sc, -1e30)
        l_sc[...] = jnp.zeros_like(l_sc)
        acc_sc[...] = jnp.zeros_like(acc_sc)
        qb = q_ref[...]

        def rows(j):
            return pl.ds(pl.multiple_of(j * tq, tq), tq)

        def scores(j, slot):
            s_sc[slot] = _dot_nt(k_ref[rows(j), :], qb)

        def update(j, slot, masked):
            st = s_sc[slot]
            if masked:
                key = lax.broadcasted_iota(jnp.int32, (tq, tq), 0)
                qry = lax.broadcasted_iota(jnp.int32, (tq, tq), 1)
                st = jnp.where(key <= qry, st, -1e30)
            m_prev = m_sc[...]
            m_new = jnp.maximum(m_prev, jnp.max(st, axis=0, keepdims=True))
            alpha = jnp.exp2(m_prev - m_new)
            pt = jnp.exp2(st - m_new)
            l_sc[...] = alpha * l_sc[...] + jnp.sum(pt, axis=0, keepdims=True)
            acc_sc[...] = alpha * acc_sc[...] + _dot_tn(v_ref[rows(j), :], pt.astype(BF16))
            m_sc[...] = m_new

        def pair(p, carry):
            scores(2 * p + 1, 1)
            update(2 * p, 0, False)
            scores(2 * p + 2, 0)
            update(2 * p + 1, 1, False)
            return carry

        scores(0, 0)
        lax.fori_loop(0, i // 2, pair, 0)

        @pl.when(i % 2 == 1)
        def _():
            scores(i, 1)
            update(i - 1, 0, False)
            update(i, 1, True)

        @pl.when(i % 2 == 0)
        def _():
            update(i, 0, True)

        l = l_sc[...]
        o_ref[...] = (acc_sc[...] / l).T.astype(BF16)
        lse_ref[...] = m_sc[...] + jnp.log2(l)

    steps = _phase_steps(len(exchange.phases), N_HEADS * nq)
    fn = _hosted(body, 3, 2, exchange, lambda p: pl.program_id(0) * nq + pl.program_id(1) == steps[p])
    res = pl.pallas_call(
        fn, name="attn_fwd", grid=(N_HEADS, nq),
        in_specs=[pl.BlockSpec((tq, HEAD_PAD), lambda h, i: (i, h)),
                  pl.BlockSpec((t, HEAD_PAD), lambda h, i: (0, h)),
                  pl.BlockSpec((t, NOPE), lambda h, i: (0, h))] + [_ANY] * len(exchange.ins),
        out_specs=[pl.BlockSpec((tq, NOPE), lambda h, i: (i, h)),
                   pl.BlockSpec((None, None, 1, tq), lambda h, i: (h, i, 0, 0))] + [_ANY] * len(exchange.out_shapes),
        out_shape=[jax.ShapeDtypeStruct((t, D_MODEL), BF16),
                   jax.ShapeDtypeStruct((N_HEADS, nq, 1, tq), F32)] + exchange.out_shapes,
        scratch_shapes=[pltpu.VMEM((1, tq), F32), pltpu.VMEM((1, tq), F32), pltpu.VMEM((NOPE, tq), F32),
                        pltpu.VMEM((2, tq, tq), F32)] + exchange.scratch,
        compiler_params=_params(2),
    )(q, k, v, *exchange.ins)
    return res[:2], res[2:]


def _conv_fwd(u, conv_w, conv_b, ln_g, ln_b, tb=256):
    t, c = u.shape
    halo = 32

    def body(u_ref, up_ref, w_ref, b_ref, g_ref, be_ref, co_ref, act_ref, buf, shifted):
        i = pl.program_id(0)
        buf[0:halo, :] = jnp.where(i == 0, 0.0, up_ref[...])
        buf[halo:halo + tb, :] = u_ref[...]
        _shifted_copies(buf, shifted, tb)
        acc = jnp.zeros((tb, c), F32)
        for k in range(CONV_W):
            acc = acc + w_ref[k:k + 1, :] * _rows_at(buf, shifted, halo - (CONV_W - 1) + k, tb)
        co = acc + b_ref[...]
        co_ref[...] = co
        mu = jnp.mean(co, axis=-1, keepdims=True)
        xc = co - mu
        r = lax.rsqrt(jnp.mean(xc * xc, axis=-1, keepdims=True) + EPS)
        y = xc * r * g_ref[...] + be_ref[...]
        act_ref[...] = (y * _sigmoid(y)).astype(BF16)

    ratio = tb // halo
    return pl.pallas_call(
        functools.partial(body), name="conv_fwd", grid=(t // tb,),
        in_specs=[pl.BlockSpec((tb, c), lambda i: (i, 0)),
                  pl.BlockSpec((halo, c), lambda i: (jnp.maximum(i * ratio - 1, 0), 0)),
                  pl.BlockSpec(conv_w.shape, lambda i: (0, 0)),
                  pl.BlockSpec((1, c), lambda i: (0, 0)), pl.BlockSpec((1, c), lambda i: (0, 0)),
                  pl.BlockSpec((1, c), lambda i: (0, 0))],
        out_specs=[pl.BlockSpec((tb, c), lambda i: (i, 0)), pl.BlockSpec((tb, c), lambda i: (i, 0))],
        out_shape=[jax.ShapeDtypeStruct((t, c), F32), jax.ShapeDtypeStruct((t, c), BF16)],
        scratch_shapes=[pltpu.VMEM((tb + halo, c), F32), pltpu.VMEM((SUBLANES - 1, tb + halo - SUBLANES, c), F32)],
        compiler_params=_params(1),
    )(u, u, conv_w, conv_b, ln_g, ln_b)


def _fwd_merge(attn, uact, sa, sc, x, w_o, w_pw2, b_pw2, w_out, g_post, tb=512):
    def body(at_ref, ua_ref, sa_ref, sc_ref, x_ref, wo_ref, wp_ref, bp_ref, wout_ref, g_ref,
             ya_ref, yc_ref, mb_ref, m_ref, x1_ref):
        ya = _dot(at_ref[...], wo_ref[...])
        yc = _dot(ua_ref[...], wp_ref[...]) + bp_ref[...]
        ya_ref[...] = ya.astype(BF16)
        yc_ref[...] = yc.astype(BF16)
        mb = (sa_ref[...] * ya + sc_ref[...] * yc).astype(BF16)
        mb_ref[...] = mb
        m = _dot(mb, wout_ref[...])
        m_ref[...] = m
        x1_ref[...] = x_ref[...] + _rms_fwd(m, g_ref[...])

    d = D_MODEL
    return _row_call(body, "fwd_merge", tb, [attn, uact, sa, sc, x], [w_o, w_pw2, b_pw2, w_out, g_post],
                     [(d, BF16), (d, BF16), (d, BF16), (d, F32), (d, F32)], [])


def _fwd_ff1(x1, g, w_ff1, tb=512):
    def body(x1_ref, g_ref, w_ref, h2_ref, r1_ref, act_ref):
        h2 = _rms_fwd(x1_ref[...], g_ref[...]).astype(BF16)
        h2_ref[...] = h2
        for j in range(N_DEV):
            cols = slice(j * FF_SHARD, (j + 1) * FF_SHARD)
            r1 = jnp.maximum(_dot(h2, w_ref[j]), 0.0)
            r1_ref[:, cols] = r1.astype(BF16)
            act_ref[:, cols] = (r1 * r1).astype(BF16)

    return _row_call(body, "fwd_ff1", tb, [x1], [g, w_ff1], [(D_MODEL, BF16), (D_FF, BF16), (D_FF, BF16)], [])


def _fwd_ff2_loss(act, x1, target, w_ff2, g, tb=512):
    def body(act_ref, x1_ref, tg_ref, w_ref, g_ref, f_ref, dy_ref, loss_ref):
        f = _dot(act_ref[...], w_ref[...])
        f_ref[...] = f
        e = x1_ref[...] + _rms_fwd(f, g_ref[...]) - tg_ref[...]
        dy_ref[...] = e * (1.0 / D_MODEL)
        _acc(loss_ref, jnp.sum(e * e))

    return _row_call(body, "fwd_ff2_loss", tb, [act, x1, target], [w_ff2, g],
                     [(D_MODEL, F32), (D_MODEL, F32)], [((8, 128), F32)])


def _bwd_ff2(dy, f, r1, w_ff2, g, tb=512):
    def body(dy_ref, f_ref, r1_ref, w_ref, g_ref, df_ref, df1_ref, dg_ref):
        df, dg = _rms_bwd(dy_ref[...], f_ref[...], g_ref[...])
        _acc(dg_ref, dg)
        dfb = df.astype(BF16)
        df_ref[...] = dfb
        dact = _dot_nt(dfb, w_ref[...])
        df1_ref[...] = (dact * (2.0 * r1_ref[...].astype(F32))).astype(BF16)

    return _row_call(body, "bwd_ff2", tb, [dy, f, r1], [w_ff2, g], [(D_MODEL, BF16), (D_FF, BF16)],
                     [((1, D_MODEL), F32)])


def _bwd_ff1(df1, x1, dy, w_ff1, g, exchange, tb=512):
    def body(df1_ref, x1_ref, dy_ref, w_ref, g_ref, dx1_ref, dg_ref):
        dh2 = _dot_nt(df1_ref[:, 0:FF_SHARD], w_ref[0])
        for j in range(1, N_DEV):
            dh2 = dh2 + _dot_nt(df1_ref[:, j * FF_SHARD:(j + 1) * FF_SHARD], w_ref[j])
        dxn, dg = _rms_bwd(dh2, x1_ref[...], g_ref[...])
        _acc(dg_ref, dg)
        dx1_ref[...] = dy_ref[...] + dxn

    return _row_call(body, "bwd_ff1", tb, [df1, x1, dy], [w_ff1, g], [(D_MODEL, F32)], [((1, D_MODEL), F32)],
                     exchange=exchange)


def _bwd_merge(dx1, m, sa, sc, ya, yc, attn, w_out, w_o, w_pw2, g_post, exchange, tb=256):
    def body(dx1_ref, m_ref, sa_ref, sc_ref, ya_ref, yc_ref, at_ref, wout_ref, wo_ref, wp_ref, g_ref,
             dm_ref, dya_ref, dyc_ref, dga_ref, dgc_ref, dat_ref, dua_ref, dg_ref, dbp_ref, delta_ref):
        dm, dg = _rms_bwd(dx1_ref[...], m_ref[...], g_ref[...])
        _acc(dg_ref, dg)
        dmb = dm.astype(BF16)
        dm_ref[...] = dmb
        dmerged = _dot_nt(dmb, wout_ref[...])
        sa, sc = sa_ref[...], sc_ref[...]
        dya = dmerged * sa
        dyc = dmerged * sc
        _acc(dbp_ref, jnp.sum(dyc, axis=0, keepdims=True))
        dyab = dya.astype(BF16)
        dycb = dyc.astype(BF16)
        dya_ref[...] = dyab
        dyc_ref[...] = dycb
        dga_ref[...] = (dmerged * ya_ref[...].astype(F32) * sa * (1.0 - sa)).astype(BF16)
        dgc_ref[...] = (dmerged * yc_ref[...].astype(F32) * sc * (1.0 - sc)).astype(BF16)
        dat = _dot_nt(dyab, wo_ref[...])
        dat_ref[...] = dat.astype(BF16)
        prod = dat * at_ref[...].astype(F32)
        lane = lax.broadcasted_iota(jnp.int32, (tb, NOPE), 1)
        dl = jnp.zeros((tb, NOPE), F32)
        for h in range(N_HEADS):
            dl = dl + jnp.where(lane == h, jnp.sum(prod[:, h * NOPE:(h + 1) * NOPE], axis=1, keepdims=True), 0.0)
        dlt = dl.T
        for h in range(N_HEADS):
            delta_ref[h] = dlt[h:h + 1, :]
        dua_ref[...] = _dot_nt(dycb, wp_ref[...]).astype(BF16)

    d = D_MODEL
    return _row_call(body, "bwd_merge", tb, [dx1, m, sa, sc, ya, yc, attn], [w_out, w_o, w_pw2, g_post],
                     [(d, BF16), (d, BF16), (d, BF16), (d, BF16), (d, BF16), (d, BF16), (d, BF16)],
                     [((1, d), F32), ((1, d), F32)], lane_outs=(N_HEADS,), exchange=exchange)


def _bwd_ln(dua, co, ln_g, ln_b, tb=256):
    def body(dua_ref, co_ref, g_ref, be_ref, dco_ref, dg_ref, db_ref, dcb_ref):
        co = co_ref[...]
        g = g_ref[...]
        mu = jnp.mean(co, axis=-1, keepdims=True)
        xc = co - mu
        r = lax.rsqrt(jnp.mean(xc * xc, axis=-1, keepdims=True) + EPS)
        xh = xc * r
        y = xh * g + be_ref[...]
        s = _sigmoid(y)
        dy = dua_ref[...].astype(F32) * (s + y * s * (1.0 - s))
        _acc(db_ref, jnp.sum(dy, axis=0, keepdims=True))
        _acc(dg_ref, jnp.sum(dy * xh, axis=0, keepdims=True))
        gy = dy * g
        dco = r * (gy - jnp.mean(gy, axis=-1, keepdims=True) - xh * jnp.mean(gy * xh, axis=-1, keepdims=True))
        dco_ref[...] = dco
        _acc(dcb_ref, jnp.sum(dco, axis=0, keepdims=True))

    d = D_MODEL
    return _row_call(body, "bwd_ln", tb, [dua, co], [ln_g, ln_b], [(d, F32)],
                     [((1, d), F32), ((1, d), F32), ((1, d), F32)])


def _conv_bwd(dco, u, a, sb, conv_w, exchange, tb=256):
    t, c = u.shape
    halo = 32
    ratio = tb // halo
    nblk = t // tb

    def body(d_ref, dn_ref, u_ref, up_ref, a_ref, sb_ref, w_ref, da_ref, db_ref, dw_ref, bufd, bufu, shd, shu):
        i = pl.program_id(0)

        @pl.when(i == 0)
        def _():
            dw_ref[...] = jnp.zeros_like(dw_ref)

        dco = d_ref[...]
        bufd[0:tb, :] = dco
        bufd[tb:tb + halo, :] = jnp.where(i == nblk - 1, 0.0, dn_ref[...])
        bufu[0:halo, :] = jnp.where(i == 0, 0.0, up_ref[...])
        bufu[halo:halo + tb, :] = u_ref[...]
        _shifted_copies(bufd, shd, tb)
        _shifted_copies(bufu, shu, tb)
        du = jnp.zeros((tb, c), F32)
        for k in range(CONV_W):
            du = du + w_ref[k:k + 1, :] * _rows_at(bufd, shd, CONV_W - 1 - k, tb)
            dw_ref[k:k + 1, :] += jnp.sum(dco * _rows_at(bufu, shu, halo - (CONV_W - 1) + k, tb), axis=0,
                                          keepdims=True)
        sb_ = sb_ref[...]
        da_ref[...] = (du * sb_).astype(BF16)
        db_ref[...] = (du * a_ref[...] * sb_ * (1.0 - sb_)).astype(BF16)

    steps = _phase_steps(len(exchange.phases), nblk)
    fn = _hosted(body, 7, 3, exchange, lambda p: pl.program_id(0) == steps[p])
    res = pl.pallas_call(
        fn, name="conv_bwd", grid=(nblk,),
        in_specs=[pl.BlockSpec((tb, c), lambda i: (i, 0)),
                  pl.BlockSpec((halo, c), lambda i: (jnp.minimum((i + 1) * ratio, t // halo - 1), 0)),
                  pl.BlockSpec((tb, c), lambda i: (i, 0)),
                  pl.BlockSpec((halo, c), lambda i: (jnp.maximum(i * ratio - 1, 0), 0)),
                  pl.BlockSpec((tb, c), lambda i: (i, 0)), pl.BlockSpec((tb, c), lambda i: (i, 0)),
                  pl.BlockSpec(conv_w.shape, lambda i: (0, 0))] + [_ANY] * len(exchange.ins),
        out_specs=[pl.BlockSpec((tb, c), lambda i: (i, 0)), pl.BlockSpec((tb, c), lambda i: (i, 0)),
                   pl.BlockSpec((32, c), lambda i: (0, 0))] + [_ANY] * len(exchange.out_shapes),
        out_shape=[jax.ShapeDtypeStruct((t, c), BF16), jax.ShapeDtypeStruct((t, c), BF16),
                   jax.ShapeDtypeStruct((32, c), F32)] + exchange.out_shapes,
        scratch_shapes=[pltpu.VMEM((tb + halo, c), F32), pltpu.VMEM((tb + halo, c), F32),
                        pltpu.VMEM((SUBLANES - 1, tb + halo - SUBLANES, c), F32),
                        pltpu.VMEM((SUBLANES - 1, tb + halo - SUBLANES, c), F32)] + exchange.scratch,
        compiler_params=_params(1),
    )(dco, dco, u, u, a, sb, conv_w, *exchange.ins)
    return res[:3], res[3:]


def _attn_bwd(q, k, v, do, lse2, delta, exchange, tq=512):
    t = q.shape[0]
    nq = t // tq
    td = delta.shape[-1]
    per = tq // td

    def body(q_ref, k_ref, v_ref, do_ref, lse_ref, dl_ref, dq_ref, dk_ref, dv_ref, dk_sc, dv_sc, s_sc, dp_sc, dq_sc):
        j = pl.program_id(1)

        @pl.when(j == 0)
        def _():
            dq_sc[...] = jnp.zeros_like(dq_sc)

        dk_sc[...] = jnp.zeros_like(dk_sc)
        dv_sc[...] = jnp.zeros_like(dv_sc)
        kb, vb = k_ref[...], v_ref[...]

        def rows(i):
            return pl.ds(pl.multiple_of(i * tq, tq), tq)

        def ahead(i, slot):
            i = jnp.minimum(i, nq - 1)
            s_sc[slot] = _dot_nt(kb, q_ref[rows(i), :])
            dp_sc[slot] = _dot_nt(vb, do_ref[rows(i), :])

        def finish(i, slot, masked):
            qb, dob = q_ref[rows(i), :], do_ref[rows(i), :]
            pt = jnp.exp2(s_sc[slot] - lse_ref[i])
            if masked:
                key = lax.broadcasted_iota(jnp.int32, (tq, tq), 0)
                qry = lax.broadcasted_iota(jnp.int32, (tq, tq), 1)
                pt = jnp.where(key <= qry, pt, 0.0)
            dv_sc[...] += _dot(pt.astype(BF16), dob)
            dl = jnp.concatenate([dl_ref[per * i + r] for r in range(per)], axis=-1)
            dst = (pt * (dp_sc[slot] - dl)).astype(BF16)
            dk_sc[...] += _dot(dst, qb)
            dq_sc[rows(i), :] += _dot_tn(dst, kb)

        def pair(p, carry):
            i = j + 1 + 2 * p
            ahead(i + 1, 0)
            finish(i, 1, False)
            ahead(i + 2, 1)
            finish(i + 1, 0, False)
            return carry

        n_after = nq - 1 - j
        ahead(j, 0)
        ahead(j + 1, 1)
        finish(j, 0, True)
        lax.fori_loop(0, n_after // 2, pair, 0)

        @pl.when(n_after % 2 == 1)
        def _():
            finish(nq - 1, 1, False)

        @pl.when(j == nq - 1)
        def _():
            dq_ref[...] = dq_sc[...].astype(BF16)

        dk_ref[...] = (dk_sc[...] * (1.0 / LOG2E)).astype(BF16)
        dv_ref[...] = dv_sc[...].astype(BF16)

    hp = N_HEADS * HEAD_PAD
    steps = _phase_steps(len(exchange.phases), N_HEADS * nq)
    fn = _hosted(body, 6, 3, exchange, lambda p: pl.program_id(0) * nq + pl.program_id(1) == steps[p])
    res = pl.pallas_call(
        fn, name="attn_bwd", grid=(N_HEADS, nq),
        in_specs=[pl.BlockSpec((t, HEAD_PAD), lambda h, j: (0, h)),
                  pl.BlockSpec((tq, HEAD_PAD), lambda h, j: (j, h)),
                  pl.BlockSpec((tq, NOPE), lambda h, j: (j, h)),
                  pl.BlockSpec((t, NOPE), lambda h, j: (0, h)),
                  pl.BlockSpec((None, nq, 1, tq), lambda h, j: (h, 0, 0, 0)),
                  pl.BlockSpec((None, t // td, 1, td), lambda h, j: (h, 0, 0, 0))] + [_ANY] * len(exchange.ins),
        out_specs=[pl.BlockSpec((t, HEAD_PAD), lambda h, j: (0, h)),
                   pl.BlockSpec((tq, HEAD_PAD), lambda h, j: (j, h)),
                   pl.BlockSpec((tq, NOPE), lambda h, j: (j, h))] + [_ANY] * len(exchange.out_shapes),
        out_shape=[jax.ShapeDtypeStruct((t, hp), BF16), jax.ShapeDtypeStruct((t, hp), BF16),
                   jax.ShapeDtypeStruct((t, D_MODEL), BF16)] + exchange.out_shapes,
        scratch_shapes=[pltpu.VMEM((tq, HEAD_PAD), F32), pltpu.VMEM((tq, NOPE), F32), pltpu.VMEM((2, tq, tq), F32),
                        pltpu.VMEM((2, tq, tq), F32), pltpu.VMEM((t, HEAD_PAD), F32)] + exchange.scratch,
        compiler_params=_params(2),
    )(q, k, v, do, lse2, delta, *exchange.ins)
    return res[:3], res[3:]


def _bwd_qkv(dq, dk, dv, zs, tc, tsa, tsb, q_norm, kv_norm, w_uq_p, w_uk, w_uv, tb=256):
    def body(dq_ref, dk_ref, dv_ref, zs_ref, c_ref, sa_ref, sb_ref, qg_ref, kg_ref, wq_ref, wk_ref, wv_ref,
             dqp_ref, dkn_ref, dzs_ref, dqg_ref, dkg_ref):
        c, sa, sb = c_ref[...], sa_ref[...], sb_ref[...]
        zs_ = zs_ref[...]
        dcqn = jnp.zeros((tb, Q_RANK), F32)
        dkr = jnp.zeros((tb, NOPE), F32)
        for h in range(N_HEADS):
            nope = slice(h * HEAD_PAD, h * HEAD_PAD + NOPE)
            rope = slice(h * HEAD_PAD + NOPE, (h + 1) * HEAD_PAD)
            dqp_ref[:, nope] = (dq_ref[:, nope].astype(F32) * QK_SCALE).astype(BF16)
            dqp_ref[:, rope] = (_rope_bwd(dq_ref[:, rope].astype(F32), c, sa, sb) * QK_SCALE).astype(BF16)
            dcqn = dcqn + _dot_nt(dqp_ref[:, h * HEAD_PAD:(h + 1) * HEAD_PAD], wq_ref[h])
            dkn_ref[:, h * NOPE:(h + 1) * NOPE] = dk_ref[:, nope]
            dkr = dkr + dk_ref[:, rope].astype(F32)
        dcq, dqg = _rms_bwd(dcqn, zs_[:, 0:Q_RANK], qg_ref[...])
        _acc(dqg_ref, dqg)
        dzs_ref[:, 0:Q_RANK] = dcq.astype(BF16)
        dzs_ref[:, Q_RANK + KV_RANK:ZS] = _rope_bwd(dkr, c, sa, sb).astype(BF16)
        dckvn = _dot_nt(dkn_ref[...], wk_ref[...]) + _dot_nt(dv_ref[...], wv_ref[...])
        dckv, dkg = _rms_bwd(dckvn, zs_[:, Q_RANK:Q_RANK + KV_RANK], kg_ref[...])
        _acc(dkg_ref, dkg)
        dzs_ref[:, Q_RANK:Q_RANK + KV_RANK] = dckv.astype(BF16)

    hp = N_HEADS * HEAD_PAD
    return _row_call(body, "bwd_qkv", tb, [dq, dk, dv, zs, tc, tsa, tsb], [q_norm, kv_norm, w_uq_p, w_uk, w_uv],
                     [(hp, BF16), (D_MODEL, BF16), (ZS, BF16)], [((1, Q_RANK), F32), ((1, KV_RANK), F32)])


def _bwd_in_proj(dzs, da, db, dga, dgc, x, dx1, w_in_p, g_pre, exchange, tb=512):
    def body(dzs_ref, da_ref, db_ref, dga_ref, dgc_ref, x_ref, dx1_ref, w_ref, g_ref, gx_ref, dg_ref):
        dh = _dot_nt(dzs_ref[...], w_ref[:, 0:ZS])
        dh = dh + _dot_nt(da_ref[...], w_ref[:, OFF_A:OFF_B])
        dh = dh + _dot_nt(db_ref[...], w_ref[:, OFF_B:OFF_GA])
        dh = dh + _dot_nt(dga_ref[...], w_ref[:, OFF_GA:OFF_GC])
        dh = dh + _dot_nt(dgc_ref[...], w_ref[:, OFF_GC:D_IN_PAD])
        dxn, dg = _rms_bwd(dh, x_ref[...], g_ref[...])
        _acc(dg_ref, dg)
        gx_ref[...] = dx1_ref[...] + dxn

    return _row_call(body, "bwd_in_proj", tb, [dzs, da, db, dga, dgc, x, dx1], [w_in_p, g_pre],
                     [(D_MODEL, F32)], [((1, D_MODEL), F32)], exchange=exchange)


def _mm_tn(a, b, name, shard_cols=None, tt=2048):
    t, m = a.shape
    n = b.shape[1]
    tm = min(m, 1024)
    tn = min(n, 1024)
    tt = min(t, tt)
    nt = t // tt
    per = tn // shard_cols if shard_cols else 1

    def body(a_ref, b_ref, o_ref, acc):
        k = pl.program_id(2)

        @pl.when(k == 0)
        def _():
            acc[...] = jnp.zeros_like(acc)

        acc[...] += _dot_tn(a_ref[...], b_ref[...])

        @pl.when(k == nt - 1)
        def _():
            if shard_cols:
                for s in range(per):
                    o_ref[s] = acc[:, s * shard_cols:(s + 1) * shard_cols].astype(BF16)
            else:
                o_ref[...] = acc[...].astype(BF16)

    if shard_cols:
        out_spec = pl.BlockSpec((per, tm, shard_cols), lambda i, j, k: (j, i, 0))
        out_shape = jax.ShapeDtypeStruct((n // shard_cols, m, shard_cols), BF16)
    else:
        out_spec = pl.BlockSpec((tm, tn), lambda i, j, k: (i, j))
        out_shape = jax.ShapeDtypeStruct((m, n), BF16)
    return pl.pallas_call(
        functools.partial(body), name=name, grid=(m // tm, n // tn, nt),
        in_specs=[pl.BlockSpec((tt, tm), lambda i, j, k: (k, i)), pl.BlockSpec((tt, tn), lambda i, j, k: (k, j))],
        out_specs=out_spec, out_shape=out_shape, scratch_shapes=[pltpu.VMEM((tm, tn), F32)],
        compiler_params=_params(3),
    )(a, b)


_ANY = pl.BlockSpec(memory_space=pl.ANY)
_MESH = pl.DeviceIdType.MESH

_Exchange = collections.namedtuple("_Exchange", "ins out_shapes scratch phases")


def _ag_exchange(shards):
    n = len(shards)

    def parts(ins, outs, sems):
        send_sems, recv_sems, _ = sems
        x, y, c = lax.axis_index("x"), lax.axis_index("y"), lax.axis_index("c")
        chips = [(1 - x, y), (x, 1 - y), (1 - x, 1 - y)]

        def copy(w, k, block, to, src=None):
            dst = outs[w].at[4 * block[0] + 2 * block[1] + block[2]]
            return pltpu.make_async_remote_copy(
                src_ref=dst if src is None else src, dst_ref=dst, send_sem=send_sems.at[7 * w + k],
                recv_sem=recv_sems.at[7 * w + k], device_id=to, device_id_type=_MESH)

        def first(w):
            return [copy(w, 0, (x, y, c), (x, y, 1 - c), src=ins[w])] + [
                copy(w, 1 + j, (x, y, c), (*chip, c), src=ins[w]) for j, chip in enumerate(chips)]

        def mine(w):
            return pltpu.make_async_copy(ins[w], outs[w].at[4 * x + 2 * y + c], sems[2].at[w])

        return (x, y, c), chips, copy, first, mine

    def start(ins, outs, sems):
        _, _, _, first, mine = parts(ins, outs, sems)
        for w in range(n):
            mine(w).start()
            for cp in first(w):
                cp.start()

    def forward(ins, outs, sems):
        (x, y, c), chips, copy, _, _ = parts(ins, outs, sems)
        for j, chip in enumerate(chips):
            for w in range(n):
                copy(w, 1 + j, (*chip, c), (x, y, c)).wait_recv()
                copy(w, 4 + j, (*chip, c), (x, y, 1 - c)).start()

    def finish(ins, outs, sems):
        (x, y, c), chips, copy, first, mine = parts(ins, outs, sems)
        for w in range(n):
            copy(w, 0, (x, y, 1 - c), (x, y, c)).wait_recv()
        for j, chip in enumerate(chips):
            for w in range(n):
                copy(w, 4 + j, (*chip, 1 - c), (x, y, c)).wait_recv()
        for w in range(n):
            for cp in first(w) + [copy(w, 4 + j, (*chip, c), (x, y, 1 - c)) for j, chip in enumerate(chips)]:
                cp.wait_send()
            mine(w).wait()

    return _Exchange(
        ins=list(shards), out_shapes=[jax.ShapeDtypeStruct((N_DEV,) + s.shape, s.dtype) for s in shards],
        scratch=[pltpu.SemaphoreType.DMA((7 * n,)), pltpu.SemaphoreType.DMA((7 * n,)), pltpu.SemaphoreType.DMA((n,))],
        phases=[start, forward, finish])


def _pair_exchange(gs):
    n = len(gs)

    def copies(ins, outs, sems):
        x, y, c = lax.axis_index("x"), lax.axis_index("y"), lax.axis_index("c")
        return [pltpu.make_async_remote_copy(
            src_ref=ins[w].at[:, 1 - c], dst_ref=outs[w], send_sem=sems[0].at[w], recv_sem=sems[1].at[w],
            device_id=(x, y, 1 - c), device_id_type=_MESH) for w in range(n)]

    return _start_then_wait(gs, [jax.ShapeDtypeStruct((4,) + g.shape[2:], g.dtype) for g in gs], n, copies)


def _start_then_wait(ins, out_shapes, n_copies, copies):
    def start(ins_, outs, sems):
        for cp in copies(ins_, outs, sems):
            cp.start()

    def finish(ins_, outs, sems):
        for cp in copies(ins_, outs, sems):
            cp.wait()

    return _Exchange(ins=list(ins), out_shapes=out_shapes,
                     scratch=[pltpu.SemaphoreType.DMA((n_copies,)), pltpu.SemaphoreType.DMA((n_copies,))],
                     phases=[start, finish])


def _run_exchange(ex, name):
    ni, no = len(ex.ins), len(ex.out_shapes)

    def body(*refs):
        for phase in ex.phases:
            phase(refs[:ni], refs[ni:ni + no], refs[ni + no:])

    return pl.pallas_call(functools.partial(body), name=name, out_shape=ex.out_shapes, in_specs=[_ANY] * ni,
                          out_specs=[_ANY] * no, scratch_shapes=ex.scratch)(*ex.ins)


def _run_exchange_on_sequencer(ex, name, collective_id):
    src = [jax.new_ref(a, memory_space=pltpu.MemorySpace.HBM) for a in ex.ins]
    dst = [jax.empty_ref(s, memory_space=pltpu.MemorySpace.HBM) for s in ex.out_shapes]

    def body(*sems):
        x, y, c = lax.axis_index("x"), lax.axis_index("y"), lax.axis_index("c")
        barrier = pltpu.get_barrier_semaphore()
        for peer in [(x, y, 1 - c), (1 - x, y, c), (x, 1 - y, c), (1 - x, 1 - y, c)]:
            pl.semaphore_signal(barrier, inc=1, device_id=peer, device_id_type=_MESH)
        pl.semaphore_wait(barrier, 4)
        for phase in ex.phases:
            phase(src, dst, sems)

    pl.kernel(body, mesh=plsc.ScalarSubcoreMesh(axis_name="sequencer", num_cores=1), name=name,
              scratch_types=ex.scratch, compiler_params=pltpu.CompilerParams(collective_id=collective_id))()
    return [d[...] for d in dst]


def _hosted(body, n_in, n_out, ex, when):
    ni, no, ns = len(ex.ins), len(ex.out_shapes), len(ex.scratch)

    def fn(*refs):
        ins, ex_ins = refs[:n_in], refs[n_in:n_in + ni]
        outs = refs[n_in + ni:n_in + ni + n_out]
        ex_outs = refs[n_in + ni + n_out:n_in + ni + n_out + no]
        scratch, sems = refs[n_in + ni + n_out + no:len(refs) - ns], refs[len(refs) - ns:]
        last = len(ex.phases) - 1
        for k in range(last):
            pl.when(when(k))(functools.partial(ex.phases[k], ex_ins, ex_outs, sems))
        body(*ins, *outs, *scratch)
        pl.when(when(last))(functools.partial(ex.phases[last], ex_ins, ex_outs, sems))

    return fn


def _phase_steps(n_phases, n_steps):
    return [0, n_steps - 1] if n_phases == 2 else [0, 2 * n_steps // 3, n_steps - 1]


def _row_block(shape, steps, lead, pick):
    blk = (None,) * lead + (shape[0] // steps,) + tuple(shape[1:])
    return pl.BlockSpec(blk, lambda *a: tuple(pick(*a)) + (a[-2],) + (0,) * (len(shape) - 1))


def _pair_sum(gs, ls, c_idx, name, steps):
    n = len(gs)

    def body(c_ref, *refs):
        for w in range(n):
            refs[2 * n + w][...] = (refs[w][...].astype(F32) + refs[n + w][...].astype(F32)).astype(BF16)

    shapes = [g.shape[2:] for g in gs]
    return pl.pallas_call(
        functools.partial(body), name=name,
        grid_spec=pltpu.PrefetchScalarGridSpec(
            num_scalar_prefetch=1, grid=(4, steps),
            in_specs=[_row_block(s, steps, 2, lambda k, i, c: (k, c[0])) for s in shapes]
            + [_row_block(s, steps, 1, lambda k, i, c: (k,)) for s in shapes],
            out_specs=[_row_block(s, steps, 1, lambda k, i, c: (k,)) for s in shapes]),
        out_shape=[jax.ShapeDtypeStruct((4,) + tuple(s), BF16) for s in shapes], compiler_params=_params(2),
    )(c_idx, *gs, *ls)


def _chip_exchange(ps):
    n = len(ps)

    def copies(ins, outs, sems):
        x, y, c = lax.axis_index("x"), lax.axis_index("y"), lax.axis_index("c")
        chips = [(1 - x, y), (x, 1 - y), (1 - x, 1 - y)]
        return [pltpu.make_async_remote_copy(
            src_ref=ins[w].at[2 * px + py], dst_ref=outs[w].at[s], send_sem=sems[0].at[3 * w + s],
            recv_sem=sems[1].at[3 * w + s], device_id=(px, py, c), device_id_type=_MESH)
            for w in range(n) for s, (px, py) in enumerate(chips)]

    return _start_then_wait(ps, [jax.ShapeDtypeStruct((3,) + p.shape[1:], p.dtype) for p in ps], 3 * n, copies)


def _adamw(w, g, m, v):
    m2 = ADAM_B1 * m + (1.0 - ADAM_B1) * g
    v2 = ADAM_B2 * v + (1.0 - ADAM_B2) * (g * g)
    m_hat = m2 / (1.0 - ADAM_B1 ** ADAM_STEP)
    v_hat = v2 / (1.0 - ADAM_B2 ** ADAM_STEP)
    delta = -ADAM_LR * (m_hat / (jnp.sqrt(v_hat) + ADAM_EPS) + ADAM_WD * w)
    return delta, m2, v2


def _update(gs, ls, qs, ws, ms, vs, idx, name, steps):
    n = len(gs)

    def body(idx_ref, *refs):
        g, l, q, w, m, v = (refs[k * n:(k + 1) * n] for k in range(6))
        outs = refs[6 * n:]
        for i in range(n):
            gr = g[i][...].astype(F32) + l[i][...].astype(F32)
            gr = gr + q[i][0].astype(F32)
            gr = gr + q[i][1].astype(F32)
            gr = gr + q[i][2].astype(F32)
            outs[4 * i][...] = gr
            outs[4 * i + 1][...], outs[4 * i + 2][...], outs[4 * i + 3][...] = _adamw(w[i][...], gr, m[i][...], v[i][...])

    shapes = [w.shape for w in ws]
    own = [_row_block(s, steps, 0, lambda i, c: ()) for s in shapes]
    res = pl.pallas_call(
        functools.partial(body), name=name,
        grid_spec=pltpu.PrefetchScalarGridSpec(
            num_scalar_prefetch=1, grid=(steps,),
            in_specs=[_row_block(s, steps, 2, lambda i, c: (c[0], c[1])) for s in shapes]
            + [_row_block(s, steps, 1, lambda i, c: (c[0],)) for s in shapes]
            + [pl.BlockSpec((3, s[0] // steps) + tuple(s[1:]), lambda i, c, nd=len(s): (0, i) + (0,) * (nd - 1))
               for s in shapes] + own * 3,
            out_specs=[b for b in own for _ in range(4)]),
        out_shape=[jax.ShapeDtypeStruct(s, F32) for s in shapes for _ in range(4)], compiler_params=_params(1),
    )(idx, *gs, *ls, *qs, *ws, *ms, *vs)
    return [res[4 * i:4 * i + 4] for i in range(n)]


def _update_small(sv_all, ws, ms, vs):
    n = len(ws)

    def body(all_ref, *refs):
        w, m, v = refs[:n], refs[n:2 * n], refs[2 * n:3 * n]
        loss_ref, outs = refs[3 * n], refs[3 * n + 1:]
        total = all_ref[0]
        for dev in range(1, N_DEV):
            total = total + all_ref[dev]
        loss_ref[...] = total[LOSS_ROW:LOSS_ROW + 1, :]
        for i in range(n):
            gr = total[i:i + 1, 0:w[i].shape[1]]
            outs[4 * i][...] = gr
            outs[4 * i + 1][...], outs[4 * i + 2][...], outs[4 * i + 3][...] = _adamw(w[i][...], gr, m[i][...], v[i][...])

    res = pl.pallas_call(
        functools.partial(body), name="update_small",
        out_shape=[jax.ShapeDtypeStruct((1, 1024), F32)]
        + [jax.ShapeDtypeStruct(a.shape, F32) for a in ws for _ in range(4)],
    )(sv_all, *ws, *ms, *vs)
    return res[0], [res[1 + 4 * i:5 + 4 * i] for i in range(n)]


def _to_exchange(shards):
    return [jnp.pad(shards[n][0], ((0, 0), (0, HEAD_PAD - NOPE - ROPE))) if n == "w_uq" else shards[n][0]
            for n in _BIG]


def _from_exchange(arrs, like):
    return {n: (a[:, :NOPE + ROPE] if n == "w_uq" else a).reshape(like[n].shape) for n, a in zip(_BIG, arrs)}


def _padded_w_in(g_in):
    w_in = g_in.transpose(1, 0, 2).reshape(D_MODEL, N_DEV * IN_SHARD)
    kr_end = Q_RANK + KV_RANK + ROPE
    return jnp.concatenate([w_in[:, :kr_end], jnp.zeros((D_MODEL, 128 - ROPE), BF16), w_in[:, kr_end:]], axis=1)


def _full_conv_w(g_conv):
    taps = (g_conv[:, 0].astype(F32) + g_conv[:, 1].astype(F32)).reshape(N_DEV, CONV_W, D_MODEL // N_DEV)
    return jnp.pad(taps.transpose(1, 0, 2).reshape(CONV_W, D_MODEL), ((0, 1), (0, 0)))


def _pad_rows(a, rows):
    return jnp.pad(a, ((0, rows - a.shape[0]), (0, 0)))


def _small_pack(vals):
    rows = [jnp.pad(v.reshape(1, -1), ((0, 0), (0, 1024 - v.size))) for v in vals]
    return _pad_rows(jnp.concatenate(rows, axis=0), SMALL_ROWS)


def _rope_tables(positions):
    inv_freq = ROPE_THETA ** (-jnp.arange(0, ROPE, 2, dtype=F32) / ROPE)
    ang = positions.reshape(-1).astype(F32)[:, None] * inv_freq
    cos, sin = jnp.cos(ang), jnp.sin(ang)
    t = cos.shape[0]
    z32, z64 = jnp.zeros((t, HALF), F32), jnp.zeros((t, HEAD_PAD - NOPE - ROPE), F32)
    tc = jnp.concatenate([cos, cos, z64], axis=1)
    tsa = jnp.concatenate([-sin, z32, z64], axis=1)
    tsb = jnp.concatenate([z32, sin, z64], axis=1)
    return tc, tsa, tsb


def _blocks(dw):
    if dw.ndim == 2:
        dw = dw.reshape(N_DEV, dw.shape[0] // N_DEV, dw.shape[1])
    return dw.reshape((4, 2) + dw.shape[1:])


def _step(x, positions, target, small, send, c_idx):
    s_in, s_uq, s_uk, s_uv, s_o, s_conv, s_pw2, s_out, s_ff1, s_ff2 = send
    tc, tsa, tsb = _rope_tables(positions)

    w_in_p = _padded_w_in(_run_exchange_on_sequencer(_ag_exchange([s_in]), "ag_w_in", 0)[0])
    (h, zs, a, sb, u, sa, sc), (w_uq, w_uk, w_uv) = _fwd_in_proj(
        x, small["norm_mix_pre"], w_in_p, _ag_exchange([s_uq, s_uk, s_uv]))
    w_uk, w_uv = w_uk.reshape(KV_RANK, -1), w_uv.reshape(KV_RANK, -1)
    cqn, ckvn, q, k, v = _fwd_qkv(zs, tc, tsa, tsb, small["q_norm"], small["kv_norm"], w_uq, w_uk, w_uv)
    (attn, lse), (w_o, g_conv, w_pw2, w_out, w_ff1, w_ff2) = _attn_fwd(
        q, k, v, _ag_exchange([s_o, s_conv, s_pw2, s_out, s_ff1, s_ff2]))
    w_o, w_pw2, w_out = (w.reshape(D_MODEL, D_MODEL) for w in (w_o, w_pw2, w_out))
    w_ff2, conv_w = w_ff2.reshape(D_FF, D_MODEL), _full_conv_w(g_conv)
    co, uact = _conv_fwd(u, conv_w, small["conv_b"], small["conv_ln_g"], small["conv_ln_b"])
    ya, yc, mb, m, x1 = _fwd_merge(attn, uact, sa, sc, x, w_o, w_pw2, small["b_pw2"], w_out, small["norm_mix_post"])
    h2, r1, act = _fwd_ff1(x1, small["norm_mlp_pre"], w_ff1)
    f, dy, loss_blk = _fwd_ff2_loss(act, x1, target, w_ff2, small["norm_mlp_post"])

    df, df1, dg_mlp_post = _bwd_ff2(dy, f, r1, w_ff2, small["norm_mlp_post"])
    g_ff2 = _blocks(_mm_tn(act, df, "dw_ff2"))
    (dx1, dg_mlp_pre), l_ff2 = _bwd_ff1(df1, x1, dy, w_ff1, small["norm_mlp_pre"], _pair_exchange([g_ff2]))
    g_ff1 = _blocks(_mm_tn(h2, df1, "dw_ff1", shard_cols=FF_SHARD))
    (dmb, dya, dyc, dga, dgc, dat, dua, dg_mix_post, db_pw2, delta), l_ff1 = _bwd_merge(
        dx1, m, sa, sc, ya, yc, attn, w_out, w_o, w_pw2, small["norm_mix_post"], _pair_exchange([g_ff1]))
    g_ff, l_ff = [g_ff1, g_ff2], [l_ff1[0], l_ff2[0]]
    p_ff = _pair_sum(g_ff, l_ff, c_idx, "rs_pair_sum_ff", 2)
    g_mix = [_blocks(_mm_tn(attn, dya, "dw_o")), _blocks(_mm_tn(uact, dyc, "dw_pw2")),
             _blocks(_mm_tn(mb, dmb, "dw_out"))]
    dco, dln_g, dln_b, dconv_b = _bwd_ln(dua, co, small["conv_ln_g"], small["conv_ln_b"])
    (da, db, dconv), l_mix = _conv_bwd(dco, u, a, sb, conv_w, _pair_exchange(g_mix))
    p_mix = _pair_sum(g_mix, l_mix, c_idx, "rs_pair_sum_mix", 1)
    (dq, dk, dv), q_early = _attn_bwd(q, k, v, dat, lse, delta, _chip_exchange(p_ff + p_mix))
    dqp, dkn, dzs, dq_norm, dkv_norm = _bwd_qkv(dq, dk, dv, zs, tc, tsa, tsb, small["q_norm"], small["kv_norm"],
                                                w_uq, w_uk, w_uv)
    dw_in = jnp.concatenate([_mm_tn(h, dzs, "dw_in_zs")[:, :Q_RANK + KV_RANK + ROPE], _mm_tn(h, da, "dw_in_a"),
                             _mm_tn(h, db, "dw_in_b"), _mm_tn(h, dga, "dw_in_ga"), _mm_tn(h, dgc, "dw_in_gc")],
                            axis=1)
    g_late = [_blocks(dw_in.reshape(D_MODEL, N_DEV, IN_SHARD).transpose(1, 0, 2)),
              _blocks(_mm_tn(cqn, dqp, "dw_uq", shard_cols=HEAD_PAD)),
              _blocks(_mm_tn(ckvn, dkn, "dw_uk").reshape(N_DEV, -1, N_HEADS, NOPE)),
              _blocks(_mm_tn(ckvn, dv, "dw_uv").reshape(N_DEV, -1, N_HEADS, NOPE)),
              _blocks(dconv[:CONV_W].reshape(CONV_W, N_DEV, 1, -1).transpose(1, 0, 2, 3).astype(BF16))]
    l_late = _run_exchange(_pair_exchange(g_late), "rs_pair_exchange_late")
    p_late = _pair_sum(g_late, l_late, c_idx, "rs_pair_sum_late", 1)
    (grad_x, dg_pre), q_late = _bwd_in_proj(dzs, da, db, dga, dgc, x, dx1, w_in_p, small["norm_mix_pre"],
                                            _chip_exchange(p_late))

    order = lambda late, mix, ff: list(late[:4]) + [mix[0], late[4], mix[1], mix[2]] + list(ff)
    exchanged = [order(g_late, g_mix, g_ff), order(l_late, l_mix, l_ff),
                 order(q_late, q_early[2:], q_early[:2])]
    small_grads = (dg_pre, dq_norm, dkv_norm, dconv_b, dln_g, dln_b, db_pw2, dg_mix_post, dg_mlp_pre, dg_mlp_post)
    return loss_blk, grad_x, small_grads, exchanged


def kernel(x, positions, norm_mix_pre, w_in, q_norm, w_uq, kv_norm, w_uk, w_uv, w_o_attn, conv_w, conv_b, conv_ln_g, conv_ln_b, w_pw2, b_pw2, w_out, norm_mix_post, norm_mlp_pre, w_ff1, w_ff2, norm_mlp_post, loss_target, m_norm_mix_pre, m_w_in, m_q_norm, m_w_uq, m_kv_norm, m_w_uk, m_w_uv, m_w_o_attn, m_conv_w, m_conv_b, m_conv_ln_g, m_conv_ln_b, m_w_pw2, m_b_pw2, m_w_out, m_norm_mix_post, m_norm_mlp_pre, m_w_ff1, m_w_ff2, m_norm_mlp_post, v_norm_mix_pre, v_w_in, v_q_norm, v_w_uq, v_kv_norm, v_w_uk, v_w_uv, v_w_o_attn, v_conv_w, v_conv_b, v_conv_ln_g, v_conv_ln_b, v_w_pw2, v_b_pw2, v_w_out, v_norm_mix_post, v_norm_mlp_pre, v_w_ff1, v_w_ff2, v_norm_mlp_post):
    wts = dict(norm_mix_pre=norm_mix_pre, w_in=w_in, q_norm=q_norm, w_uq=w_uq, kv_norm=kv_norm, w_uk=w_uk, w_uv=w_uv,
               w_o_attn=w_o_attn, conv_w=conv_w, conv_b=conv_b, conv_ln_g=conv_ln_g, conv_ln_b=conv_ln_b,
               w_pw2=w_pw2, b_pw2=b_pw2, w_out=w_out, norm_mix_post=norm_mix_post, norm_mlp_pre=norm_mlp_pre,
               w_ff1=w_ff1, w_ff2=w_ff2, norm_mlp_post=norm_mlp_post)
    mom_m = dict(norm_mix_pre=m_norm_mix_pre, w_in=m_w_in, q_norm=m_q_norm, w_uq=m_w_uq, kv_norm=m_kv_norm,
                 w_uk=m_w_uk, w_uv=m_w_uv, w_o_attn=m_w_o_attn, conv_w=m_conv_w, conv_b=m_conv_b,
                 conv_ln_g=m_conv_ln_g, conv_ln_b=m_conv_ln_b, w_pw2=m_w_pw2, b_pw2=m_b_pw2, w_out=m_w_out,
                 norm_mix_post=m_norm_mix_post, norm_mlp_pre=m_norm_mlp_pre, w_ff1=m_w_ff1, w_ff2=m_w_ff2,
                 norm_mlp_post=m_norm_mlp_post)
    mom_v = dict(norm_mix_pre=v_norm_mix_pre, w_in=v_w_in, q_norm=v_q_norm, w_uq=v_w_uq, kv_norm=v_kv_norm,
                 w_uk=v_w_uk, w_uv=v_w_uv, w_o_attn=v_w_o_attn, conv_w=v_conv_w, conv_b=v_conv_b,
                 conv_ln_g=v_conv_ln_g, conv_ln_b=v_conv_ln_b, w_pw2=v_w_pw2, b_pw2=v_b_pw2, w_out=v_w_out,
                 norm_mix_post=v_norm_mix_post, norm_mlp_pre=v_norm_mlp_pre, w_ff1=v_w_ff1, w_ff2=v_w_ff2,
                 norm_mlp_post=v_norm_mlp_post)
    cx, cy, cc = lax.axis_index("x"), lax.axis_index("y"), lax.axis_index("c")

    big_local = {n: wts[n] for n in _BIG}
    w_ex = _to_exchange(big_local)
    send = [a.astype(BF16) for a in w_ex]
    conv_i = _BIG.index("conv_w")
    conv_lo = (w_ex[conv_i] - send[conv_i].astype(F32)).astype(BF16)
    send[conv_i] = jnp.stack([send[conv_i], conv_lo])

    small = {n: wts[n].reshape(1, -1) for n in _SMALL}
    c_idx = cc.reshape(1).astype(jnp.int32)
    loss_blk, grad_x, small_grads, (g4, l_sib, q_in) = _step(x[0], positions, loss_target[0], small, send, c_idx)

    idx = jnp.stack([2 * cx + cy, cc]).astype(jnp.int32)
    m_ex, v_ex = _to_exchange({n: mom_m[n] for n in _BIG}), _to_exchange({n: mom_v[n] for n in _BIG})
    upd = [None] * len(_BIG)
    for group, steps in ((("w_in", "w_uq", "w_ff1", "w_ff2"), 4),
                         (("w_uk", "w_uv", "w_o_attn", "conv_w", "w_pw2", "w_out"), 1)):
        ids = [_BIG.index(n) for n in group]
        pick = lambda arrs: [arrs[i] for i in ids]
        res = _update(pick(g4), pick(l_sib), pick(q_in), pick(w_ex), pick(m_ex), pick(v_ex), idx,
                      "update_" + group[0], steps)
        for i, r in zip(ids, res):
            upd[i] = r
    out_g, out_d, out_m, out_v = (_from_exchange([u[j] for u in upd], big_local) for j in range(4))

    loss_row = jnp.broadcast_to(loss_blk[0:1, 0:1], (1, 1024))
    sv = _small_pack(list(small_grads) + [loss_row])
    sv_all = _run_exchange(_ag_exchange([sv]), "ag_small")[0]
    loss_sum, upd_small = _update_small(sv_all, [wts[n] for n in _SMALL], [mom_m[n] for n in _SMALL],
                                        [mom_v[n] for n in _SMALL])
    for n, r in zip(_SMALL, upd_small):
        out_g[n], out_d[n], out_m[n], out_v[n] = r
    loss = loss_sum[0, 0] * (0.5 / D_MODEL)

    return (loss, grad_x[None], *[out_g[n] for n in _WEIGHTS], *[out_d[n] for n in _WEIGHTS],
            *[out_m[n] for n in _WEIGHTS], *[out_v[n] for n in _WEIGHTS])
```

```python
import collections
import functools

import jax
import jax.numpy as jnp
from jax import lax
from jax.experimental import pallas as pl
from jax.experimental.pallas import tpu as pltpu
from jax.experimental.pallas import tpu_sc as plsc

F32 = jnp.float32
BF16 = jnp.bfloat16

D_MODEL = 1024
N_HEADS = 8
NOPE = 128
ROPE = 64
HALF = ROPE // 2
Q_RANK = 384
KV_RANK = 256
CONV_W = 31
D_FF = 4096
EPS = 1e-6
ROPE_THETA = 10000.0
HEAD_PAD = 256
QK_SCALE = (NOPE + ROPE) ** -0.5
LOG2E = 1.4426950408889634
SUBLANES = 8
N_DEV = 8
FF_SHARD = D_FF // N_DEV
IN_SHARD = 4800 // N_DEV

ZS = Q_RANK + KV_RANK + 128
OFF_A = ZS
OFF_B = OFF_A + D_MODEL
OFF_GA = OFF_B + D_MODEL
OFF_GC = OFF_GA + D_MODEL
D_IN_PAD = OFF_GC + D_MODEL

ADAM_LR = 0.001
ADAM_B1 = 0.9
ADAM_B2 = 0.999
ADAM_EPS = 1e-08
ADAM_WD = 0.01
ADAM_STEP = 10

VMEM_LIMIT = 56 * 1024 * 1024

_SMALL = ("norm_mix_pre", "q_norm", "kv_norm", "conv_b", "conv_ln_g", "conv_ln_b", "b_pw2", "norm_mix_post",
          "norm_mlp_pre", "norm_mlp_post")
SMALL_ROWS = 16
LOSS_ROW = len(_SMALL)

_BIG = ("w_in", "w_uq", "w_uk", "w_uv", "w_o_attn", "conv_w", "w_pw2", "w_out", "w_ff1", "w_ff2")
_WEIGHTS = ("norm_mix_pre", "w_in", "q_norm", "w_uq", "kv_norm", "w_uk", "w_uv", "w_o_attn", "conv_w", "conv_b",
            "conv_ln_g", "conv_ln_b", "w_pw2", "b_pw2", "w_out", "norm_mix_post", "norm_mlp_pre", "w_ff1", "w_ff2",
            "norm_mlp_post")


def _dot(a, b):
    return jnp.dot(a, b, preferred_element_type=F32)


def _dot_nt(a, b):
    return lax.dot_general(a, b, (((1,), (1,)), ((), ())), preferred_element_type=F32)


def _dot_tn(a, b):
    return lax.dot_general(a, b, (((0,), (0,)), ((), ())), preferred_element_type=F32)


def _sigmoid(x):
    return 1.0 / (1.0 + jnp.exp(-x))


def _rms_fwd(x, g):
    r = lax.rsqrt(jnp.mean(x * x, axis=-1, keepdims=True) + EPS)
    return x * r * g


def _rms_bwd(dy, x, g):
    r = lax.rsqrt(jnp.mean(x * x, axis=-1, keepdims=True) + EPS)
    xh = x * r
    gy = dy * g
    dx = r * (gy - xh * jnp.mean(gy * xh, axis=-1, keepdims=True))
    return dx, jnp.sum(dy * xh, axis=0, keepdims=True)


def _rope(q, c, sa, sb):
    n = q.shape[-1]
    return q * c + pltpu.roll(q, n - HALF, 1) * sa + pltpu.roll(q, HALF, 1) * sb


def _rope_bwd(d, c, sa, sb):
    n = d.shape[-1]
    return d * c - pltpu.roll(d, n - HALF, 1) * sa - pltpu.roll(d, HALF, 1) * sb


def _shifted_copies(buf, shifted, tb):
    n = shifted.shape[1]
    for b in range(1, SUBLANES):
        shifted[b - 1] = buf[pl.ds(b, n), :]


def _rows_at(buf, shifted, start, tb):
    a, b = divmod(start, SUBLANES)
    src = buf if b == 0 else shifted.at[b - 1]
    return src[pl.ds(SUBLANES * a, tb), :]


def _params(n_axes=1):
    return pltpu.CompilerParams(dimension_semantics=("arbitrary",) * n_axes, vmem_limit_bytes=VMEM_LIMIT)


def _row_call(body, name, tb, row_ins, full_ins, row_outs, acc_outs, lane_outs=(), exchange=None):
    t = row_ins[0].shape[0]
    in_specs = [pl.BlockSpec((tb, a.shape[1]), lambda i: (i, 0)) for a in row_ins]
    in_specs += [pl.BlockSpec(a.shape, lambda i, nd=a.ndim: (0,) * nd) for a in full_ins]
    out_specs = [pl.BlockSpec((tb, c), lambda i: (i, 0)) for c, _ in row_outs]
    out_specs += [pl.BlockSpec(s, lambda i, nd=len(s): (0,) * nd) for s, _ in acc_outs]
    out_specs += [pl.BlockSpec((n, None, 1, tb), lambda i: (0, i, 0, 0)) for n in lane_outs]
    out_shape = [jax.ShapeDtypeStruct((t, c), dt) for c, dt in row_outs]
    out_shape += [jax.ShapeDtypeStruct(s, dt) for s, dt in acc_outs]
    out_shape += [jax.ShapeDtypeStruct((n, t // tb, 1, tb), F32) for n in lane_outs]
    if exchange is None:
        return pl.pallas_call(
            functools.partial(body), name=name, grid=(t // tb,), in_specs=in_specs, out_specs=out_specs,
            out_shape=out_shape, compiler_params=_params(1),
        )(*row_ins, *full_ins)
    steps = _phase_steps(len(exchange.phases), t // tb)
    fn = _hosted(body, len(in_specs), len(out_specs), exchange, lambda k: pl.program_id(0) == steps[k])
    res = pl.pallas_call(
        fn, name=name, grid=(t // tb,), in_specs=in_specs + [_ANY] * len(exchange.ins),
        out_specs=out_specs + [_ANY] * len(exchange.out_shapes), out_shape=out_shape + exchange.out_shapes,
        scratch_shapes=exchange.scratch, compiler_params=_params(1),
    )(*row_ins, *full_ins, *exchange.ins)
    return res[:len(out_specs)], res[len(out_specs):]


def _acc(ref, val):
    @pl.when(pl.program_id(0) == 0)
    def _():
        ref[...] = jnp.zeros_like(ref)
    ref[...] += val


def _fwd_in_proj(x, g_pre, w_in_p, exchange, tb=512):
    def body(x_ref, g_ref, w_ref, h_ref, zs_ref, a_ref, sb_ref, u_ref, sa_ref, sc_ref):
        hb = _rms_fwd(x_ref[...], g_ref[...]).astype(BF16)
        h_ref[...] = hb
        zs_ref[...] = _dot_nt(hb, w_ref[0:ZS, :])
        a = _dot_nt(hb, w_ref[OFF_A:OFF_B, :])
        sb = _sigmoid(_dot_nt(hb, w_ref[OFF_B:OFF_GA, :]))
        a_ref[...] = a
        sb_ref[...] = sb
        u_ref[...] = a * sb
        sa_ref[...] = _sigmoid(_dot_nt(hb, w_ref[OFF_GA:OFF_GC, :]))
        sc_ref[...] = _sigmoid(_dot_nt(hb, w_ref[OFF_GC:D_IN_PAD, :]))

    d = D_MODEL
    return _row_call(body, "fwd_in_proj", tb, [x], [g_pre, w_in_p],
                     [(d, BF16), (ZS, F32), (d, F32), (d, F32), (d, F32), (d, F32), (d, F32)], [], exchange=exchange)


def _fwd_qkv(zs, tc, tsa, tsb, q_norm, kv_norm, w_uq_p, w_uk, w_uv, tb=256):
    def body(zs_ref, c_ref, sa_ref, sb_ref, qg_ref, kg_ref, wq_ref, wk_ref, wv_ref,
             cqn_ref, ckvn_ref, q_ref, k_ref, v_ref):
        zs_ = zs_ref[...]
        c, sa, sb = c_ref[...], sa_ref[...], sb_ref[...]
        cqn = _rms_fwd(zs_[:, 0:Q_RANK], qg_ref[...]).astype(BF16)
        cqn_ref[...] = cqn
        for h in range(N_HEADS):
            qh = _dot(cqn, wq_ref[h]) * (QK_SCALE * LOG2E)
            q_ref[:, h * HEAD_PAD:h * HEAD_PAD + NOPE] = qh[:, :NOPE].astype(BF16)
            q_ref[:, h * HEAD_PAD + NOPE:(h + 1) * HEAD_PAD] = _rope(qh[:, NOPE:], c, sa, sb).astype(BF16)
        kr = _rope(zs_[:, Q_RANK + KV_RANK:ZS], c, sa, sb).astype(BF16)
        ckvn = _rms_fwd(zs_[:, Q_RANK:Q_RANK + KV_RANK], kg_ref[...]).astype(BF16)
        ckvn_ref[...] = ckvn
        kn = _dot(ckvn, wk_ref[...]).astype(BF16)
        v_ref[...] = _dot(ckvn, wv_ref[...]).astype(BF16)
        for h in range(N_HEADS):
            k_ref[:, h * HEAD_PAD:h * HEAD_PAD + NOPE] = kn[:, h * NOPE:(h + 1) * NOPE]
            k_ref[:, h * HEAD_PAD + NOPE:(h + 1) * HEAD_PAD] = kr

    hp = N_HEADS * HEAD_PAD
    return _row_call(body, "fwd_qkv", tb, [zs, tc, tsa, tsb], [q_norm, kv_norm, w_uq_p, w_uk, w_uv],
                     [(Q_RANK, BF16), (KV_RANK, BF16), (hp, BF16), (hp, BF16), (D_MODEL, BF16)], [])


def _attn_fwd(q, k, v, exchange, tq=512):
    t = q.shape[0]
    nq = t // tq

    def body(q_ref, k_ref, v_ref, o_ref, lse_ref, m_sc, l_sc, acc_sc, s_sc):
        i = pl.program_id(1)
        m_sc[...] = jnp.full_like(m_sc, -1e30)
        l_sc[...] = jnp.zeros_like(l_sc)
        acc_sc[...] = jnp.zeros_like(acc_sc)
        qb = q_ref[...]

        def rows(j):
            return pl.ds(pl.multiple_of(j * tq, tq), tq)

        def scores(j, slot):
            s_sc[slot] = _dot_nt(k_ref[rows(j), :], qb)

        def update(j, slot, masked):
            st = s_sc[slot]
            if masked:
                key = lax.broadcasted_iota(jnp.int32, (tq, tq), 0)
                qry = lax.broadcasted_iota(jnp.int32, (tq, tq), 1)
                st = jnp.where(key <= qry, st, -1e30)
            m_prev = m_sc[...]
            m_new = jnp.maximum(m_prev, jnp.max(st, axis=0, keepdims=True))
            alpha = jnp.exp2(m_prev - m_new)
            pt = jnp.exp2(st - m_new)
            l_sc[...] = alpha * l_sc[...] + jnp.sum(pt, axis=0, keepdims=True)
            acc_sc[...] = alpha * acc_sc[...] + _dot_tn(v_ref[rows(j), :], pt.astype(BF16))
            m_sc[...] = m_new

        def pair(p, carry):
            scores(2 * p + 1, 1)
            update(2 * p, 0, False)
            scores(2 * p + 2, 0)
            update(2 * p + 1, 1, False)
            return carry

        scores(0, 0)
        lax.fori_loop(0, i // 2, pair, 0)

        @pl.when(i % 2 == 1)
        def _():
            scores(i, 1)
            update(i - 1, 0, False)
            update(i, 1, True)

        @pl.when(i % 2 == 0)
        def _():
            update(i, 0, True)

        l = l_sc[...]
        o_ref[...] = (acc_sc[...] / l).T.astype(BF16)
        lse_ref[...] = m_sc[...] + jnp.log2(l)

    steps = _phase_steps(len(exchange.phases), N_HEADS * nq)
    fn = _hosted(body, 3, 2, exchange, lambda p: pl.program_id(0) * nq + pl.program_id(1) == steps[p])
    res = pl.pallas_call(
        fn, name="attn_fwd", grid=(N_HEADS, nq),
        in_specs=[pl.BlockSpec((tq, HEAD_PAD), lambda h, i: (i, h)),
                  pl.BlockSpec((t, HEAD_PAD), lambda h, i: (0, h)),
                  pl.BlockSpec((t, NOPE), lambda h, i: (0, h))] + [_ANY] * len(exchange.ins),
        out_specs=[pl.BlockSpec((tq, NOPE), lambda h, i: (i, h)),
                   pl.BlockSpec((None, None, 1, tq), lambda h, i: (h, i, 0, 0))] + [_ANY] * len(exchange.out_shapes),
        out_shape=[jax.ShapeDtypeStruct((t, D_MODEL), BF16),
                   jax.ShapeDtypeStruct((N_HEADS, nq, 1, tq), F32)] + exchange.out_shapes,
        scratch_shapes=[pltpu.VMEM((1, tq), F32), pltpu.VMEM((1, tq), F32), pltpu.VMEM((NOPE, tq), F32),
                        pltpu.VMEM((2, tq, tq), F32)] + exchange.scratch,
        compiler_params=_params(2),
    )(q, k, v, *exchange.ins)
    return res[:2], res[2:]


def _conv_fwd(u, conv_w, conv_b, ln_g, ln_b, tb=256):
    t, c = u.shape
    halo = 32

    def body(u_ref, up_ref, w_ref, b_ref, g_ref, be_ref, co_ref, act_ref, buf, shifted):
        i = pl.program_id(0)
        buf[0:halo, :] = jnp.where(i == 0, 0.0, up_ref[...])
        buf[halo:halo + tb, :] = u_ref[...]
        _shifted_copies(buf, shifted, tb)
        acc = jnp.zeros((tb, c), F32)
        for k in range(CONV_W):
            acc = acc + w_ref[k:k + 1, :] * _rows_at(buf, shifted, halo - (CONV_W - 1) + k, tb)
        co = acc + b_ref[...]
        co_ref[...] = co
        mu = jnp.mean(co, axis=-1, keepdims=True)
        xc = co - mu
        r = lax.rsqrt(jnp.mean(xc * xc, axis=-1, keepdims=True) + EPS)
        y = xc * r * g_ref[...] + be_ref[...]
        act_ref[...] = (y * _sigmoid(y)).astype(BF16)

    ratio = tb // halo
    return pl.pallas_call(
        functools.partial(body), name="conv_fwd", grid=(t // tb,),
        in_specs=[pl.BlockSpec((tb, c), lambda i: (i, 0)),
                  pl.BlockSpec((halo, c), lambda i: (jnp.maximum(i * ratio - 1, 0), 0)),
                  pl.BlockSpec(conv_w.shape, lambda i: (0, 0)),
                  pl.BlockSpec((1, c), lambda i: (0, 0)), pl.BlockSpec((1, c), lambda i: (0, 0)),
                  pl.BlockSpec((1, c), lambda i: (0, 0))],
        out_specs=[pl.BlockSpec((tb, c), lambda i: (i, 0)), pl.BlockSpec((tb, c), lambda i: (i, 0))],
        out_shape=[jax.ShapeDtypeStruct((t, c), F32), jax.ShapeDtypeStruct((t, c), BF16)],
        scratch_shapes=[pltpu.VMEM((tb + halo, c), F32), pltpu.VMEM((SUBLANES - 1, tb + halo - SUBLANES, c), F32)],
        compiler_params=_params(1),
    )(u, u, conv_w, conv_b, ln_g, ln_b)


def _fwd_merge(attn, uact, sa, sc, x, w_o, w_pw2, b_pw2, w_out, g_post, tb=512):
    def body(at_ref, ua_ref, sa_ref, sc_ref, x_ref, wo_ref, wp_ref, bp_ref, wout_ref, g_ref,
             ya_ref, yc_ref, mb_ref, m_ref, x1_ref):
        ya = _dot(at_ref[...], wo_ref[...])
        yc = _dot(ua_ref[...], wp_ref[...]) + bp_ref[...]
        ya_ref[...] = ya.astype(BF16)
        yc_ref[...] = yc.astype(BF16)
        mb = (sa_ref[...] * ya + sc_ref[...] * yc).astype(BF16)
        mb_ref[...] = mb
        m = _dot(mb, wout_ref[...])
        m_ref[...] = m
        x1_ref[...] = x_ref[...] + _rms_fwd(m, g_ref[...])

    d = D_MODEL
    return _row_call(body, "fwd_merge", tb, [attn, uact, sa, sc, x], [w_o, w_pw2, b_pw2, w_out, g_post],
                     [(d, BF16), (d, BF16), (d, BF16), (d, F32), (d, F32)], [])


def _fwd_ff1(x1, g, w_ff1, tb=512):
    def body(x1_ref, g_ref, w_ref, h2_ref, r1_ref, act_ref):
        h2 = _rms_fwd(x1_ref[...], g_ref[...]).astype(BF16)
        h2_ref[...] = h2
        for j in range(N_DEV):
            cols = slice(j * FF_SHARD, (j + 1) * FF_SHARD)
            r1 = jnp.maximum(_dot(h2, w_ref[j]), 0.0)
            r1_ref[:, cols] = r1.astype(BF16)
            act_ref[:, cols] = (r1 * r1).astype(BF16)

    return _row_call(body, "fwd_ff1", tb, [x1], [g, w_ff1], [(D_MODEL, BF16), (D_FF, BF16), (D_FF, BF16)], [])


def _fwd_ff2_loss(act, x1, target, w_ff2, g, tb=512):
    def body(act_ref, x1_ref, tg_ref, w_ref, g_ref, f_ref, dy_ref, loss_ref):
        f = _dot(act_ref[...], w_ref[...])
        f_ref[...] = f
        e = x1_ref[...] + _rms_fwd(f, g_ref[...]) - tg_ref[...]
        dy_ref[...] = e * (1.0 / D_MODEL)
        _acc(loss_ref, jnp.sum(e * e))

    return _row_call(body, "fwd_ff2_loss", tb, [act, x1, target], [w_ff2, g],
                     [(D_MODEL, F32), (D_MODEL, F32)], [((8, 128), F32)])


def _bwd_ff2(dy, f, r1, w_ff2, g, tb=512):
    def body(dy_ref, f_ref, r1_ref, w_ref, g_ref, df_ref, df1_ref, dg_ref):
        df, dg = _rms_bwd(dy_ref[...], f_ref[...], g_ref[...])
        _acc(dg_ref, dg)
        dfb = df.astype(BF16)
        df_ref[...] = dfb
        dact = _dot_nt(dfb, w_ref[...])
        df1_ref[...] = (dact * (2.0 * r1_ref[...].astype(F32))).astype(BF16)

    return _row_call(body, "bwd_ff2", tb, [dy, f, r1], [w_ff2, g], [(D_MODEL, BF16), (D_FF, BF16)],
                     [((1, D_MODEL), F32)])


def _bwd_ff1(df1, x1, dy, w_ff1, g, exchange, tb=512):
    def body(df1_ref, x1_ref, dy_ref, w_ref, g_ref, dx1_ref, dg_ref):
        dh2 = _dot_nt(df1_ref[:, 0:FF_SHARD], w_ref[0])
        for j in range(1, N_DEV):
            dh2 = dh2 + _dot_nt(df1_ref[:, j * FF_SHARD:(j + 1) * FF_SHARD], w_ref[j])
        dxn, dg = _rms_bwd(dh2, x1_ref[...], g_ref[...])
        _acc(dg_ref, dg)
        dx1_ref[...] = dy_ref[...] + dxn

    return _row_call(body, "bwd_ff1", tb, [df1, x1, dy], [w_ff1, g], [(D_MODEL, F32)], [((1, D_MODEL), F32)],
                     exchange=exchange)


def _bwd_merge(dx1, m, sa, sc, ya, yc, attn, w_out, w_o, w_pw2, g_post, exchange, tb=256):
    def body(dx1_ref, m_ref, sa_ref, sc_ref, ya_ref, yc_ref, at_ref, wout_ref, wo_ref, wp_ref, g_ref,
             dm_ref, dya_ref, dyc_ref, dga_ref, dgc_ref, dat_ref, dua_ref, dg_ref, dbp_ref, delta_ref):
        dm, dg = _rms_bwd(dx1_ref[...], m_ref[...], g_ref[...])
        _acc(dg_ref, dg)
        dmb = dm.astype(BF16)
        dm_ref[...] = dmb
        dmerged = _dot_nt(dmb, wout_ref[...])
        sa, sc = sa_ref[...], sc_ref[...]
        dya = dmerged * sa
        dyc = dmerged * sc
        _acc(dbp_ref, jnp.sum(dyc, axis=0, keepdims=True))
        dyab = dya.astype(BF16)
        dycb = dyc.astype(BF16)
        dya_ref[...] = dyab
        dyc_ref[...] = dycb
        dga_ref[...] = (dmerged * ya_ref[...].astype(F32) * sa * (1.0 - sa)).astype(BF16)
        dgc_ref[...] = (dmerged * yc_ref[...].astype(F32) * sc * (1.0 - sc)).astype(BF16)
        dat = _dot_nt(dyab, wo_ref[...])
        dat_ref[...] = dat.astype(BF16)
        prod = dat * at_ref[...].astype(F32)
        lane = lax.broadcasted_iota(jnp.int32, (tb, NOPE), 1)
        dl = jnp.zeros((tb, NOPE), F32)
        for h in range(N_HEADS):
            dl = dl + jnp.where(lane == h, jnp.sum(prod[:, h * NOPE:(h + 1) * NOPE], axis=1, keepdims=True), 0.0)
        dlt = dl.T
        for h in range(N_HEADS):
            delta_ref[h] = dlt[h:h + 1, :]
        dua_ref[...] = _dot_nt(dycb, wp_ref[...]).astype(BF16)

    d = D_MODEL
    return _row_call(body, "bwd_merge", tb, [dx1, m, sa, sc, ya, yc, attn], [w_out, w_o, w_pw2, g_post],
                     [(d, BF16), (d, BF16), (d, BF16), (d, BF16), (d, BF16), (d, BF16), (d, BF16)],
                     [((1, d), F32), ((1, d), F32)], lane_outs=(N_HEADS,), exchange=exchange)


def _bwd_ln(dua, co, ln_g, ln_b, tb=256):
    def body(dua_ref, co_ref, g_ref, be_ref, dco_ref, dg_ref, db_ref, dcb_ref):
        co = co_ref[...]
        g = g_ref[...]
        mu = jnp.mean(co, axis=-1, keepdims=True)
        xc = co - mu
        r = lax.rsqrt(jnp.mean(xc * xc, axis=-1, keepdims=True) + EPS)
        xh = xc * r
        y = xh * g + be_ref[...]
        s = _sigmoid(y)
        dy = dua_ref[...].astype(F32) * (s + y * s * (1.0 - s))
        _acc(db_ref, jnp.sum(dy, axis=0, keepdims=True))
        _acc(dg_ref, jnp.sum(dy * xh, axis=0, keepdims=True))
        gy = dy * g
        dco = r * (gy - jnp.mean(gy, axis=-1, keepdims=True) - xh * jnp.mean(gy * xh, axis=-1, keepdims=True))
        dco_ref[...] = dco
        _acc(dcb_ref, jnp.sum(dco, axis=0, keepdims=True))

    d = D_MODEL
    return _row_call(body, "bwd_ln", tb, [dua, co], [ln_g, ln_b], [(d, F32)],
                     [((1, d), F32), ((1, d), F32), ((1, d), F32)])


def _conv_bwd(dco, u, a, sb, conv_w, exchange, tb=256):
    t, c = u.shape
    halo = 32
    ratio = tb // halo
    nblk = t // tb

    def body(d_ref, dn_ref, u_ref, up_ref, a_ref, sb_ref, w_ref, da_ref, db_ref, dw_ref, bufd, bufu, shd, shu):
        i = pl.program_id(0)

        @pl.when(i == 0)
        def _():
            dw_ref[...] = jnp.zeros_like(dw_ref)

        dco = d_ref[...]
        bufd[0:tb, :] = dco
        bufd[tb:tb + halo, :] = jnp.where(i == nblk - 1, 0.0, dn_ref[...])
        bufu[0:halo, :] = jnp.where(i == 0, 0.0, up_ref[...])
        bufu[halo:halo + tb, :] = u_ref[...]
        _shifted_copies(bufd, shd, tb)
        _shifted_copies(bufu, shu, tb)
        du = jnp.zeros((tb, c), F32)
        for k in range(CONV_W):
            du = du + w_ref[k:k + 1, :] * _rows_at(bufd, shd, CONV_W - 1 - k, tb)
            dw_ref[k:k + 1, :] += jnp.sum(dco * _rows_at(bufu, shu, halo - (CONV_W - 1) + k, tb), axis=0,
                                          keepdims=True)
        sb_ = sb_ref[...]
        da_ref[...] = (du * sb_).astype(BF16)
        db_ref[...] = (du * a_ref[...] * sb_ * (1.0 - sb_)).astype(BF16)

    steps = _phase_steps(len(exchange.phases), nblk)
    fn = _hosted(body, 7, 3, exchange, lambda p: pl.program_id(0) == steps[p])
    res = pl.pallas_call(
        fn, name="conv_bwd", grid=(nblk,),
        in_specs=[pl.BlockSpec((tb, c), lambda i: (i, 0)),
                  pl.BlockSpec((halo, c), lambda i: (jnp.minimum((i + 1) * ratio, t // halo - 1), 0)),
                  pl.BlockSpec((tb, c), lambda i: (i, 0)),
                  pl.BlockSpec((halo, c), lambda i: (jnp.maximum(i * ratio - 1, 0), 0)),
                  pl.BlockSpec((tb, c), lambda i: (i, 0)), pl.BlockSpec((tb, c), lambda i: (i, 0)),
                  pl.BlockSpec(conv_w.shape, lambda i: (0, 0))] + [_ANY] * len(exchange.ins),
        out_specs=[pl.BlockSpec((tb, c), lambda i: (i, 0)), pl.BlockSpec((tb, c), lambda i: (i, 0)),
                   pl.BlockSpec((32, c), lambda i: (0, 0))] + [_ANY] * len(exchange.out_shapes),
        out_shape=[jax.ShapeDtypeStruct((t, c), BF16), jax.ShapeDtypeStruct((t, c), BF16),
                   jax.ShapeDtypeStruct((32, c), F32)] + exchange.out_shapes,
        scratch_shapes=[pltpu.VMEM((tb + halo, c), F32), pltpu.VMEM((tb + halo, c), F32),
                        pltpu.VMEM((SUBLANES - 1, tb + halo - SUBLANES, c), F32),
                        pltpu.VMEM((SUBLANES - 1, tb + halo - SUBLANES, c), F32)] + exchange.scratch,
        compiler_params=_params(1),
    )(dco, dco, u, u, a, sb, conv_w, *exchange.ins)
    return res[:3], res[3:]


def _attn_bwd(q, k, v, do, lse2, delta, exchange, tq=512):
    t = q.shape[0]
    nq = t // tq
    td = delta.shape[-1]
    per = tq // td

    def body(q_ref, k_ref, v_ref, do_ref, lse_ref, dl_ref, dq_ref, dk_ref, dv_ref, dk_sc, dv_sc, s_sc, dp_sc, dq_sc):
        j = pl.program_id(1)

        @pl.when(j == 0)
        def _():
            dq_sc[...] = jnp.zeros_like(dq_sc)

        dk_sc[...] = jnp.zeros_like(dk_sc)
        dv_sc[...] = jnp.zeros_like(dv_sc)
        kb, vb = k_ref[...], v_ref[...]

        def rows(i):
            return pl.ds(pl.multiple_of(i * tq, tq), tq)

        def ahead(i, slot):
            i = jnp.minimum(i, nq - 1)
            s_sc[slot] = _dot_nt(kb, q_ref[rows(i), :])
            dp_sc[slot] = _dot_nt(vb, do_ref[rows(i), :])

        def finish(i, slot, masked):
            qb, dob = q_ref[rows(i), :], do_ref[rows(i), :]
            pt = jnp.exp2(s_sc[slot] - lse_ref[i])
            if masked:
                key = lax.broadcasted_iota(jnp.int32, (tq, tq), 0)
                qry = lax.broadcasted_iota(jnp.int32, (tq, tq), 1)
                pt = jnp.where(key <= qry, pt, 0.0)
            dv_sc[...] += _dot(pt.astype(BF16), dob)
            dl = jnp.concatenate([dl_ref[per * i + r] for r in range(per)], axis=-1)
            dst = (pt * (dp_sc[slot] - dl)).astype(BF16)
            dk_sc[...] += _dot(dst, qb)
            dq_sc[rows(i), :] += _dot_tn(dst, kb)

        def pair(p, carry):
            i = j + 1 + 2 * p
            ahead(i + 1, 0)
            finish(i, 1, False)
            ahead(i + 2, 1)
            finish(i + 1, 0, False)
            return carry

        n_after = nq - 1 - j
        ahead(j, 0)
        ahead(j + 1, 1)
        finish(j, 0, True)
        lax.fori_loop(0, n_after // 2, pair, 0)

        @pl.when(n_after % 2 == 1)
        def _():
            finish(nq - 1, 1, False)

        @pl.when(j == nq - 1)
        def _():
            dq_ref[...] = dq_sc[...].astype(BF16)

        dk_ref[...] = (dk_sc[...] * (1.0 / LOG2E)).astype(BF16)
        dv_ref[...] = dv_sc[...].astype(BF16)

    hp = N_HEADS * HEAD_PAD
    steps = _phase_steps(len(exchange.phases), N_HEADS * nq)
    fn = _hosted(body, 6, 3, exchange, lambda p: pl.program_id(0) * nq + pl.program_id(1) == steps[p])
    res = pl.pallas_call(
        fn, name="attn_bwd", grid=(N_HEADS, nq),
        in_specs=[pl.BlockSpec((t, HEAD_PAD), lambda h, j: (0, h)),
                  pl.BlockSpec((tq, HEAD_PAD), lambda h, j: (j, h)),
                  pl.BlockSpec((tq, NOPE), lambda h, j: (j, h)),
                  pl.BlockSpec((t, NOPE), lambda h, j: (0, h)),
                  pl.BlockSpec((None, nq, 1, tq), lambda h, j: (h, 0, 0, 0)),
                  pl.BlockSpec((None, t // td, 1, td), lambda h, j: (h, 0, 0, 0))] + [_ANY] * len(exchange.ins),
        out_specs=[pl.BlockSpec((t, HEAD_PAD), lambda h, j: (0, h)),
                   pl.BlockSpec((tq, HEAD_PAD), lambda h, j: (j, h)),
                   pl.BlockSpec((tq, NOPE), lambda h, j: (j, h))] + [_ANY] * len(exchange.out_shapes),
        out_shape=[jax.ShapeDtypeStruct((t, hp), BF16), jax.ShapeDtypeStruct((t, hp), BF16),
                   jax.ShapeDtypeStruct((t, D_MODEL), BF16)] + exchange.out_shapes,
        scratch_shapes=[pltpu.VMEM((tq, HEAD_PAD), F32), pltpu.VMEM((tq, NOPE), F32), pltpu.VMEM((2, tq, tq), F32),
                        pltpu.VMEM((2, tq, tq), F32), pltpu.VMEM((t, HEAD_PAD), F32)] + exchange.scratch,
        compiler_params=_params(2),
    )(q, k, v, do, lse2, delta, *exchange.ins)
    return res[:3], res[3:]


def _bwd_qkv(dq, dk, dv, zs, tc, tsa, tsb, q_norm, kv_norm, w_uq_p, w_uk, w_uv, tb=256):
    def body(dq_ref, dk_ref, dv_ref, zs_ref, c_ref, sa_ref, sb_ref, qg_ref, kg_ref, wq_ref, wk_ref, wv_ref,
             dqp_ref, dkn_ref, dzs_ref, dqg_ref, dkg_ref):
        c, sa, sb = c_ref[...], sa_ref[...], sb_ref[...]
        zs_ = zs_ref[...]
        dcqn = jnp.zeros((tb, Q_RANK), F32)
        dkr = jnp.zeros((tb, NOPE), F32)
        for h in range(N_HEADS):
            nope = slice(h * HEAD_PAD, h * HEAD_PAD + NOPE)
            rope = slice(h * HEAD_PAD + NOPE, (h + 1) * HEAD_PAD)
            dqp_ref[:, nope] = (dq_ref[:, nope].astype(F32) * QK_SCALE).astype(BF16)
            dqp_ref[:, rope] = (_rope_bwd(dq_ref[:, rope].astype(F32), c, sa, sb) * QK_SCALE).astype(BF16)
            dcqn = dcqn + _dot_nt(dqp_ref[:, h * HEAD_PAD:(h + 1) * HEAD_PAD], wq_ref[h])
            dkn_ref[:, h * NOPE:(h + 1) * NOPE] = dk_ref[:, nope]
            dkr = dkr + dk_ref[:, rope].astype(F32)
        dcq, dqg = _rms_bwd(dcqn, zs_[:, 0:Q_RANK], qg_ref[...])
        _acc(dqg_ref, dqg)
        dzs_ref[:, 0:Q_RANK] = dcq.astype(BF16)
        dzs_ref[:, Q_RANK + KV_RANK:ZS] = _rope_bwd(dkr, c, sa, sb).astype(BF16)
        dckvn = _dot_nt(dkn_ref[...], wk_ref[...]) + _dot_nt(dv_ref[...], wv_ref[...])
        dckv, dkg = _rms_bwd(dckvn, zs_[:, Q_RANK:Q_RANK + KV_RANK], kg_ref[...])
        _acc(dkg_ref, dkg)
        dzs_ref[:, Q_RANK:Q_RANK + KV_RANK] = dckv.astype(BF16)

    hp = N_HEADS * HEAD_PAD
    return _row_call(body, "bwd_qkv", tb, [dq, dk, dv, zs, tc, tsa, tsb], [q_norm, kv_norm, w_uq_p, w_uk, w_uv],
                     [(hp, BF16), (D_MODEL, BF16), (ZS, BF16)], [((1, Q_RANK), F32), ((1, KV_RANK), F32)])


def _bwd_in_proj(dzs, da, db, dga, dgc, x, dx1, w_in_p, g_pre, exchange, tb=512):
    def body(dzs_ref, da_ref, db_ref, dga_ref, dgc_ref, x_ref, dx1_ref, w_ref, g_ref, gx_ref, dg_ref):
        dh = _dot(dzs_ref[...], w_ref[0:ZS, :])
        dh = dh + _dot(da_ref[...], w_ref[OFF_A:OFF_B, :])
        dh = dh + _dot(db_ref[...], w_ref[OFF_B:OFF_GA, :])
        dh = dh + _dot(dga_ref[...], w_ref[OFF_GA:OFF_GC, :])
        dh = dh + _dot(dgc_ref[...], w_ref[OFF_GC:D_IN_PAD, :])
        dxn, dg = _rms_bwd(dh, x_ref[...], g_ref[...])
        _acc(dg_ref, dg)
        gx_ref[...] = dx1_ref[...] + dxn

    return _row_call(body, "bwd_in_proj", tb, [dzs, da, db, dga, dgc, x, dx1], [w_in_p, g_pre],
                     [(D_MODEL, F32)], [((1, D_MODEL), F32)], exchange=exchange)


def _mm_tn(a, b, name, shard_cols=None, tt=2048):
    t, m = a.shape
    n = b.shape[1]
    tm = min(m, 1024)
    tn = min(n, 1024)
    tt = min(t, tt)
    nt = t // tt
    per = tn // shard_cols if shard_cols else 1

    def body(a_ref, b_ref, o_ref, acc):
        k = pl.program_id(2)

        @pl.when(k == 0)
        def _():
            acc[...] = jnp.zeros_like(acc)

        acc[...] += _dot_tn(a_ref[...], b_ref[...])

        @pl.when(k == nt - 1)
        def _():
            if shard_cols:
                for s in range(per):
                    o_ref[s] = acc[:, s * shard_cols:(s + 1) * shard_cols].astype(BF16)
            else:
                o_ref[...] = acc[...].astype(BF16)

    if shard_cols:
        out_spec = pl.BlockSpec((per, tm, shard_cols), lambda i, j, k: (j, i, 0))
        out_shape = jax.ShapeDtypeStruct((n // shard_cols, m, shard_cols), BF16)
    else:
        out_spec = pl.BlockSpec((tm, tn), lambda i, j, k: (i, j))
        out_shape = jax.ShapeDtypeStruct((m, n), BF16)
    return pl.pallas_call(
        functools.partial(body), name=name, grid=(m // tm, n // tn, nt),
        in_specs=[pl.BlockSpec((tt, tm), lambda i, j, k: (k, i)), pl.BlockSpec((tt, tn), lambda i, j, k: (k, j))],
        out_specs=out_spec, out_shape=out_shape, scratch_shapes=[pltpu.VMEM((tm, tn), F32)],
        compiler_params=_params(3),
    )(a, b)


_ANY = pl.BlockSpec(memory_space=pl.ANY)
_MESH = pl.DeviceIdType.MESH

_Exchange = collections.namedtuple("_Exchange", "ins out_shapes scratch phases")


def _ag_exchange(shards):
    n = len(shards)

    def parts(ins, outs, sems):
        send_sems, recv_sems, _ = sems
        x, y, c = lax.axis_index("x"), lax.axis_index("y"), lax.axis_index("c")
        chips = [(1 - x, y), (x, 1 - y), (1 - x, 1 - y)]

        def copy(w, k, block, to, src=None):
            dst = outs[w].at[4 * block[0] + 2 * block[1] + block[2]]
            return pltpu.make_async_remote_copy(
                src_ref=dst if src is None else src, dst_ref=dst, send_sem=send_sems.at[7 * w + k],
                recv_sem=recv_sems.at[7 * w + k], device_id=to, device_id_type=_MESH)

        def first(w):
            return [copy(w, 0, (x, y, c), (x, y, 1 - c), src=ins[w])] + [
                copy(w, 1 + j, (x, y, c), (*chip, c), src=ins[w]) for j, chip in enumerate(chips)]

        def mine(w):
            return pltpu.make_async_copy(ins[w], outs[w].at[4 * x + 2 * y + c], sems[2].at[w])

        return (x, y, c), chips, copy, first, mine

    def start(ins, outs, sems):
        _, _, _, first, mine = parts(ins, outs, sems)
        for w in range(n):
            mine(w).start()
            for cp in first(w):
                cp.start()

    def forward(ins, outs, sems):
        (x, y, c), chips, copy, _, _ = parts(ins, outs, sems)
        for j, chip in enumerate(chips):
            for w in range(n):
                copy(w, 1 + j, (*chip, c), (x, y, c)).wait_recv()
                copy(w, 4 + j, (*chip, c), (x, y, 1 - c)).start()

    def finish(ins, outs, sems):
        (x, y, c), chips, copy, first, mine = parts(ins, outs, sems)
        for w in range(n):
            copy(w, 0, (x, y, 1 - c), (x, y, c)).wait_recv()
        for j, chip in enumerate(chips):
            for w in range(n):
                copy(w, 4 + j, (*chip, 1 - c), (x, y, c)).wait_recv()
        for w in range(n):
            for cp in first(w) + [copy(w, 4 + j, (*chip, c), (x, y, 1 - c)) for j, chip in enumerate(chips)]:
                cp.wait_send()
            mine(w).wait()

    return _Exchange(
        ins=list(shards), out_shapes=[jax.ShapeDtypeStruct((N_DEV,) + s.shape, s.dtype) for s in shards],
        scratch=[pltpu.SemaphoreType.DMA((7 * n,)), pltpu.SemaphoreType.DMA((7 * n,)), pltpu.SemaphoreType.DMA((n,))],
        phases=[start, forward, finish])


def _pair_exchange(gs):
    n = len(gs)

    def copies(ins, outs, sems):
        x, y, c = lax.axis_index("x"), lax.axis_index("y"), lax.axis_index("c")
        return [pltpu.make_async_remote_copy(
            src_ref=ins[w].at[:, 1 - c], dst_ref=outs[w], send_sem=sems[0].at[w], recv_sem=sems[1].at[w],
            device_id=(x, y, 1 - c), device_id_type=_MESH) for w in range(n)]

    return _start_then_wait(gs, [jax.ShapeDtypeStruct((4,) + g.shape[2:], g.dtype) for g in gs], n, copies)


def _start_then_wait(ins, out_shapes, n_copies, copies):
    def start(ins_, outs, sems):
        for cp in copies(ins_, outs, sems):
            cp.start()

    def finish(ins_, outs, sems):
        for cp in copies(ins_, outs, sems):
            cp.wait()

    return _Exchange(ins=list(ins), out_shapes=out_shapes,
                     scratch=[pltpu.SemaphoreType.DMA((n_copies,)), pltpu.SemaphoreType.DMA((n_copies,))],
                     phases=[start, finish])


def _run_exchange(ex, name):
    ni, no = len(ex.ins), len(ex.out_shapes)

    def body(*refs):
        for phase in ex.phases:
            phase(refs[:ni], refs[ni:ni + no], refs[ni + no:])

    return pl.pallas_call(functools.partial(body), name=name, out_shape=ex.out_shapes, in_specs=[_ANY] * ni,
                          out_specs=[_ANY] * no, scratch_shapes=ex.scratch)(*ex.ins)


def _run_exchange_on_sequencer(ex, name, collective_id):
    src = [jax.new_ref(a, memory_space=pltpu.MemorySpace.HBM) for a in ex.ins]
    dst = [jax.empty_ref(s, memory_space=pltpu.MemorySpace.HBM) for s in ex.out_shapes]

    def body(*sems):
        x, y, c = lax.axis_index("x"), lax.axis_index("y"), lax.axis_index("c")
        barrier = pltpu.get_barrier_semaphore()
        for peer in [(x, y, 1 - c), (1 - x, y, c), (x, 1 - y, c), (1 - x, 1 - y, c)]:
            pl.semaphore_signal(barrier, inc=1, device_id=peer, device_id_type=_MESH)
        pl.semaphore_wait(barrier, 4)
        for phase in ex.phases:
            phase(src, dst, sems)

    pl.kernel(body, mesh=plsc.ScalarSubcoreMesh(axis_name="sequencer", num_cores=1), name=name,
              scratch_types=ex.scratch, compiler_params=pltpu.CompilerParams(collective_id=collective_id))()
    return [d[...] for d in dst]


def _hosted(body, n_in, n_out, ex, when):
    ni, no, ns = len(ex.ins), len(ex.out_shapes), len(ex.scratch)

    def fn(*refs):
        ins, ex_ins = refs[:n_in], refs[n_in:n_in + ni]
        outs = refs[n_in + ni:n_in + ni + n_out]
        ex_outs = refs[n_in + ni + n_out:n_in + ni + n_out + no]
        scratch, sems = refs[n_in + ni + n_out + no:len(refs) - ns], refs[len(refs) - ns:]
        last = len(ex.phases) - 1
        for k in range(last):
            pl.when(when(k))(functools.partial(ex.phases[k], ex_ins, ex_outs, sems))
        body(*ins, *outs, *scratch)
        pl.when(when(last))(functools.partial(ex.phases[last], ex_ins, ex_outs, sems))

    return fn


def _phase_steps(n_phases, n_steps):
    return [0, n_steps - 1] if n_phases == 2 else [0, 2 * n_steps // 3, n_steps - 1]


def _row_block(shape, steps, lead, pick):
    blk = (None,) * lead + (shape[0] // steps,) + tuple(shape[1:])
    return pl.BlockSpec(blk, lambda *a: tuple(pick(*a)) + (a[-2],) + (0,) * (len(shape) - 1))


def _pair_sum(gs, ls, c_idx, name, steps):
    n = len(gs)

    def body(c_ref, *refs):
        for w in range(n):
            refs[2 * n + w][...] = (refs[w][...].astype(F32) + refs[n + w][...].astype(F32)).astype(BF16)

    shapes = [g.shape[2:] for g in gs]
    return pl.pallas_call(
        functools.partial(body), name=name,
        grid_spec=pltpu.PrefetchScalarGridSpec(
            num_scalar_prefetch=1, grid=(4, steps),
            in_specs=[_row_block(s, steps, 2, lambda k, i, c: (k, c[0])) for s in shapes]
            + [_row_block(s, steps, 1, lambda k, i, c: (k,)) for s in shapes],
            out_specs=[_row_block(s, steps, 1, lambda k, i, c: (k,)) for s in shapes]),
        out_shape=[jax.ShapeDtypeStruct((4,) + tuple(s), BF16) for s in shapes], compiler_params=_params(2),
    )(c_idx, *gs, *ls)


def _chip_exchange(ps):
    n = len(ps)

    def copies(ins, outs, sems):
        x, y, c = lax.axis_index("x"), lax.axis_index("y"), lax.axis_index("c")
        chips = [(1 - x, y), (x, 1 - y), (1 - x, 1 - y)]
        return [pltpu.make_async_remote_copy(
            src_ref=ins[w].at[2 * px + py], dst_ref=outs[w].at[s], send_sem=sems[0].at[3 * w + s],
            recv_sem=sems[1].at[3 * w + s], device_id=(px, py, c), device_id_type=_MESH)
            for w in range(n) for s, (px, py) in enumerate(chips)]

    return _start_then_wait(ps, [jax.ShapeDtypeStruct((3,) + p.shape[1:], p.dtype) for p in ps], 3 * n, copies)


def _adamw(w, g, m, v):
    m2 = ADAM_B1 * m + (1.0 - ADAM_B1) * g
    v2 = ADAM_B2 * v + (1.0 - ADAM_B2) * (g * g)
    m_hat = m2 / (1.0 - ADAM_B1 ** ADAM_STEP)
    v_hat = v2 / (1.0 - ADAM_B2 ** ADAM_STEP)
    delta = -ADAM_LR * (m_hat / (jnp.sqrt(v_hat) + ADAM_EPS) + ADAM_WD * w)
    return delta, m2, v2


def _update(gs, ls, qs, ws, ms, vs, idx, name, steps):
    n = len(gs)

    def body(idx_ref, *refs):
        g, l, q, w, m, v = (refs[k * n:(k + 1) * n] for k in range(6))
        outs = refs[6 * n:]
        for i in range(n):
            gr = g[i][...].astype(F32) + l[i][...].astype(F32)
            gr = gr + q[i][0].astype(F32)
            gr = gr + q[i][1].astype(F32)
            gr = gr + q[i][2].astype(F32)
            outs[4 * i][...] = gr
            outs[4 * i + 1][...], outs[4 * i + 2][...], outs[4 * i + 3][...] = _adamw(w[i][...], gr, m[i][...], v[i][...])

    shapes = [w.shape for w in ws]
    own = [_row_block(s, steps, 0, lambda i, c: ()) for s in shapes]
    res = pl.pallas_call(
        functools.partial(body), name=name,
        grid_spec=pltpu.PrefetchScalarGridSpec(
            num_scalar_prefetch=1, grid=(steps,),
            in_specs=[_row_block(s, steps, 2, lambda i, c: (c[0], c[1])) for s in shapes]
            + [_row_block(s, steps, 1, lambda i, c: (c[0],)) for s in shapes]
            + [pl.BlockSpec((3, s[0] // steps) + tuple(s[1:]), lambda i, c, nd=len(s): (0, i) + (0,) * (nd - 1))
               for s in shapes] + own * 3,
            out_specs=[b for b in own for _ in range(4)]),
        out_shape=[jax.ShapeDtypeStruct(s, F32) for s in shapes for _ in range(4)], compiler_params=_params(1),
    )(idx, *gs, *ls, *qs, *ws, *ms, *vs)
    return [res[4 * i:4 * i + 4] for i in range(n)]


def _update_small(sv_all, ws, ms, vs):
    n = len(ws)

    def body(all_ref, *refs):
        w, m, v = refs[:n], refs[n:2 * n], refs[2 * n:3 * n]
        loss_ref, outs = refs[3 * n], refs[3 * n + 1:]
        total = all_ref[0]
        for dev in range(1, N_DEV):
            total = total + all_ref[dev]
        loss_ref[...] = total[LOSS_ROW:LOSS_ROW + 1, :]
        for i in range(n):
            gr = total[i:i + 1, 0:w[i].shape[1]]
            outs[4 * i][...] = gr
            outs[4 * i + 1][...], outs[4 * i + 2][...], outs[4 * i + 3][...] = _adamw(w[i][...], gr, m[i][...], v[i][...])

    res = pl.pallas_call(
        functools.partial(body), name="update_small",
        out_shape=[jax.ShapeDtypeStruct((1, 1024), F32)]
        + [jax.ShapeDtypeStruct(a.shape, F32) for a in ws for _ in range(4)],
    )(sv_all, *ws, *ms, *vs)
    return res[0], [res[1 + 4 * i:5 + 4 * i] for i in range(n)]


def _to_exchange(shards):
    return [jnp.pad(shards[n][0], ((0, 0), (0, HEAD_PAD - NOPE - ROPE))) if n == "w_uq" else shards[n][0]
            for n in _BIG]


def _from_exchange(arrs, like):
    return {n: (a[:, :NOPE + ROPE] if n == "w_uq" else a).reshape(like[n].shape) for n, a in zip(_BIG, arrs)}


def _padded_w_in(g_in):
    w_in_t = g_in.reshape(N_DEV * IN_SHARD, D_MODEL)
    kr_end = Q_RANK + KV_RANK + ROPE
    return jnp.concatenate([w_in_t[:kr_end], jnp.zeros((128 - ROPE, D_MODEL), BF16), w_in_t[kr_end:]], axis=0)


def _full_conv_w(g_conv):
    taps = (g_conv[:, 0].astype(F32) + g_conv[:, 1].astype(F32)).reshape(N_DEV, CONV_W, D_MODEL // N_DEV)
    return jnp.pad(taps.transpose(1, 0, 2).reshape(CONV_W, D_MODEL), ((0, 1), (0, 0)))


def _pad_rows(a, rows):
    return jnp.pad(a, ((0, rows - a.shape[0]), (0, 0)))


def _small_pack(vals):
    rows = [jnp.pad(v.reshape(1, -1), ((0, 0), (0, 1024 - v.size))) for v in vals]
    return _pad_rows(jnp.concatenate(rows, axis=0), SMALL_ROWS)


def _rope_tables(positions):
    inv_freq = ROPE_THETA ** (-jnp.arange(0, ROPE, 2, dtype=F32) / ROPE)
    ang = positions.reshape(-1).astype(F32)[:, None] * inv_freq
    cos, sin = jnp.cos(ang), jnp.sin(ang)
    t = cos.shape[0]
    z32, z64 = jnp.zeros((t, HALF), F32), jnp.zeros((t, HEAD_PAD - NOPE - ROPE), F32)
    tc = jnp.concatenate([cos, cos, z64], axis=1)
    tsa = jnp.concatenate([-sin, z32, z64], axis=1)
    tsb = jnp.concatenate([z32, sin, z64], axis=1)
    return tc, tsa, tsb


def _blocks(dw):
    if dw.ndim == 2:
        dw = dw.reshape(N_DEV, dw.shape[0] // N_DEV, dw.shape[1])
    return dw.reshape((4, 2) + dw.shape[1:])


def _step(x, positions, target, small, send, c_idx):
    s_in, s_uq, s_uk, s_uv, s_o, s_conv, s_pw2, s_out, s_ff1, s_ff2 = send
    tc, tsa, tsb = _rope_tables(positions)

    w_in_p = _padded_w_in(_run_exchange_on_sequencer(_ag_exchange([s_in.T]), "ag_w_in", 0)[0])
    (h, zs, a, sb, u, sa, sc), (w_uq, w_uk, w_uv) = _fwd_in_proj(
        x, small["norm_mix_pre"], w_in_p, _ag_exchange([s_uq, s_uk, s_uv]))
    w_uk, w_uv = w_uk.reshape(KV_RANK, -1), w_uv.reshape(KV_RANK, -1)
    cqn, ckvn, q, k, v = _fwd_qkv(zs, tc, tsa, tsb, small["q_norm"], small["kv_norm"], w_uq, w_uk, w_uv)
    (attn, lse), (w_o, g_conv, w_pw2, w_out, w_ff1, w_ff2) = _attn_fwd(
        q, k, v, _ag_exchange([s_o, s_conv, s_pw2, s_out, s_ff1, s_ff2]))
    w_o, w_pw2, w_out = (w.reshape(D_MODEL, D_MODEL) for w in (w_o, w_pw2, w_out))
    w_ff2, conv_w = w_ff2.reshape(D_FF, D_MODEL), _full_conv_w(g_conv)
    co, uact = _conv_fwd(u, conv_w, small["conv_b"], small["conv_ln_g"], small["conv_ln_b"])
    ya, yc, mb, m, x1 = _fwd_merge(attn, uact, sa, sc, x, w_o, w_pw2, small["b_pw2"], w_out, small["norm_mix_post"])
    h2, r1, act = _fwd_ff1(x1, small["norm_mlp_pre"], w_ff1)
    f, dy, loss_blk = _fwd_ff2_loss(act, x1, target, w_ff2, small["norm_mlp_post"])

    df, df1, dg_mlp_post = _bwd_ff2(dy, f, r1, w_ff2, small["norm_mlp_post"])
    g_ff2 = _blocks(_mm_tn(act, df, "dw_ff2"))
    (dx1, dg_mlp_pre), l_ff2 = _bwd_ff1(df1, x1, dy, w_ff1, small["norm_mlp_pre"], _pair_exchange([g_ff2]))
    g_ff1 = _blocks(_mm_tn(h2, df1, "dw_ff1", shard_cols=FF_SHARD))
    (dmb, dya, dyc, dga, dgc, dat, dua, dg_mix_post, db_pw2, delta), l_ff1 = _bwd_merge(
        dx1, m, sa, sc, ya, yc, attn, w_out, w_o, w_pw2, small["norm_mix_post"], _pair_exchange([g_ff1]))
    g_ff, l_ff = [g_ff1, g_ff2], [l_ff1[0], l_ff2[0]]
    p_ff = _pair_sum(g_ff, l_ff, c_idx, "rs_pair_sum_ff", 2)
    g_mix = [_blocks(_mm_tn(attn, dya, "dw_o")), _blocks(_mm_tn(uact, dyc, "dw_pw2")),
             _blocks(_mm_tn(mb, dmb, "dw_out"))]
    dco, dln_g, dln_b, dconv_b = _bwd_ln(dua, co, small["conv_ln_g"], small["conv_ln_b"])
    (da, db, dconv), l_mix = _conv_bwd(dco, u, a, sb, conv_w, _pair_exchange(g_mix))
    p_mix = _pair_sum(g_mix, l_mix, c_idx, "rs_pair_sum_mix", 1)
    (dq, dk, dv), q_early = _attn_bwd(q, k, v, dat, lse, delta, _chip_exchange(p_ff + p_mix))
    dqp, dkn, dzs, dq_norm, dkv_norm = _bwd_qkv(dq, dk, dv, zs, tc, tsa, tsb, small["q_norm"], small["kv_norm"],
                                                w_uq, w_uk, w_uv)
    dw_in = jnp.concatenate([_mm_tn(h, dzs, "dw_in_zs")[:, :Q_RANK + KV_RANK + ROPE], _mm_tn(h, da, "dw_in_a"),
                             _mm_tn(h, db, "dw_in_b"), _mm_tn(h, dga, "dw_in_ga"), _mm_tn(h, dgc, "dw_in_gc")],
                            axis=1)
    g_late = [_blocks(dw_in.reshape(D_MODEL, N_DEV, IN_SHARD).transpose(1, 0, 2)),
              _blocks(_mm_tn(cqn, dqp, "dw_uq", shard_cols=HEAD_PAD)),
              _blocks(_mm_tn(ckvn, dkn, "dw_uk").reshape(N_DEV, -1, N_HEADS, NOPE)),
              _blocks(_mm_tn(ckvn, dv, "dw_uv").reshape(N_DEV, -1, N_HEADS, NOPE)),
              _blocks(dconv[:CONV_W].reshape(CONV_W, N_DEV, 1, -1).transpose(1, 0, 2, 3).astype(BF16))]
    l_late = _run_exchange(_pair_exchange(g_late), "rs_pair_exchange_late")
    p_late = _pair_sum(g_late, l_late, c_idx, "rs_pair_sum_late", 1)
    (grad_x, dg_pre), q_late = _bwd_in_proj(dzs, da, db, dga, dgc, x, dx1, w_in_p, small["norm_mix_pre"],
                                            _chip_exchange(p_late))

    order = lambda late, mix, ff: list(late[:4]) + [mix[0], late[4], mix[1], mix[2]] + list(ff)
    exchanged = [order(g_late, g_mix, g_ff), order(l_late, l_mix, l_ff),
                 order(q_late, q_early[2:], q_early[:2])]
    small_grads = (dg_pre, dq_norm, dkv_norm, dconv_b, dln_g, dln_b, db_pw2, dg_mix_post, dg_mlp_pre, dg_mlp_post)
    return loss_blk, grad_x, small_grads, exchanged


def kernel(x, positions, norm_mix_pre, w_in, q_norm, w_uq, kv_norm, w_uk, w_uv, w_o_attn, conv_w, conv_b, conv_ln_g, conv_ln_b, w_pw2, b_pw2, w_out, norm_mix_post, norm_mlp_pre, w_ff1, w_ff2, norm_mlp_post, loss_target, m_norm_mix_pre, m_w_in, m_q_norm, m_w_uq, m_kv_norm, m_w_uk, m_w_uv, m_w_o_attn, m_conv_w, m_conv_b, m_conv_ln_g, m_conv_ln_b, m_w_pw2, m_b_pw2, m_w_out, m_norm_mix_post, m_norm_mlp_pre, m_w_ff1, m_w_ff2, m_norm_mlp_post, v_norm_mix_pre, v_w_in, v_q_norm, v_w_uq, v_kv_norm, v_w_uk, v_w_uv, v_w_o_attn, v_conv_w, v_conv_b, v_conv_ln_g, v_conv_ln_b, v_w_pw2, v_b_pw2, v_w_out, v_norm_mix_post, v_norm_mlp_pre, v_w_ff1, v_w_ff2, v_norm_mlp_post):
    wts = dict(norm_mix_pre=norm_mix_pre, w_in=w_in, q_norm=q_norm, w_uq=w_uq, kv_norm=kv_norm, w_uk=w_uk, w_uv=w_uv,
               w_o_attn=w_o_attn, conv_w=conv_w, conv_b=conv_b, conv_ln_g=conv_ln_g, conv_ln_b=conv_ln_b,
               w_pw2=w_pw2, b_pw2=b_pw2, w_out=w_out, norm_mix_post=norm_mix_post, norm_mlp_pre=norm_mlp_pre,
               w_ff1=w_ff1, w_ff2=w_ff2, norm_mlp_post=norm_mlp_post)
    mom_m = dict(norm_mix_pre=m_norm_mix_pre, w_in=m_w_in, q_norm=m_q_norm, w_uq=m_w_uq, kv_norm=m_kv_norm,
                 w_uk=m_w_uk, w_uv=m_w_uv, w_o_attn=m_w_o_attn, conv_w=m_conv_w, conv_b=m_conv_b,
                 conv_ln_g=m_conv_ln_g, conv_ln_b=m_conv_ln_b, w_pw2=m_w_pw2, b_pw2=m_b_pw2, w_out=m_w_out,
                 norm_mix_post=m_norm_mix_post, norm_mlp_pre=m_norm_mlp_pre, w_ff1=m_w_ff1, w_ff2=m_w_ff2,
                 norm_mlp_post=m_norm_mlp_post)
    mom_v = dict(norm_mix_pre=v_norm_mix_pre, w_in=v_w_in, q_norm=v_q_norm, w_uq=v_w_uq, kv_norm=v_kv_norm,
                 w_uk=v_w_uk, w_uv=v_w_uv, w_o_attn=v_w_o_attn, conv_w=v_conv_w, conv_b=v_conv_b,
                 conv_ln_g=v_conv_ln_g, conv_ln_b=v_conv_ln_b, w_pw2=v_w_pw2, b_pw2=v_b_pw2, w_out=v_w_out,
                 norm_mix_post=v_norm_mix_post, norm_mlp_pre=v_norm_mlp_pre, w_ff1=v_w_ff1, w_ff2=v_w_ff2,
                 norm_mlp_post=v_norm_mlp_post)
    cx, cy, cc = lax.axis_index("x"), lax.axis_index("y"), lax.axis_index("c")

    big_local = {n: wts[n] for n in _BIG}
    w_ex = _to_exchange(big_local)
    send = [a.astype(BF16) for a in w_ex]
    conv_i = _BIG.index("conv_w")
    conv_lo = (w_ex[conv_i] - send[conv_i].astype(F32)).astype(BF16)
    send[conv_i] = jnp.stack([send[conv_i], conv_lo])

    small = {n: wts[n].reshape(1, -1) for n in _SMALL}
    c_idx = cc.reshape(1).astype(jnp.int32)
    loss_blk, grad_x, small_grads, (g4, l_sib, q_in) = _step(x[0], positions, loss_target[0], small, send, c_idx)

    idx = jnp.stack([2 * cx + cy, cc]).astype(jnp.int32)
    m_ex, v_ex = _to_exchange({n: mom_m[n] for n in _BIG}), _to_exchange({n: mom_v[n] for n in _BIG})
    upd = [None] * len(_BIG)
    for group, steps in ((("w_in", "w_uq", "w_ff1", "w_ff2"), 4),
                         (("w_uk", "w_uv", "w_o_attn", "conv_w", "w_pw2", "w_out"), 1)):
        ids = [_BIG.index(n) for n in group]
        pick = lambda arrs: [arrs[i] for i in ids]
        res = _update(pick(g4), pick(l_sib), pick(q_in), pick(w_ex), pick(m_ex), pick(v_ex), idx,
                      "update_" + group[0], steps)
        for i, r in zip(ids, res):
            upd[i] = r
    out_g, out_d, out_m, out_v = (_from_exchange([u[j] for u in upd], big_local) for j in range(4))

    loss_row = jnp.broadcast_to(loss_blk[0:1, 0:1], (1, 1024))
    sv = _small_pack(list(small_grads) + [loss_row])
    sv_all = _run_exchange(_ag_exchange([sv]), "ag_small")[0]
    loss_sum, upd_small = _update_small(sv_all, [wts[n] for n in _SMALL], [mom_m[n] for n in _SMALL],
                                        [mom_v[n] for n in _SMALL])
    for n, r in zip(_SMALL, upd_small):
        out_g[n], out_d[n], out_m[n], out_v[n] = r
    loss = loss_sum[0, 0] * (0.5 / D_MODEL)

    return (loss, grad_x[None], *[out_g[n] for n in _WEIGHTS], *[out_d[n] for n in _WEIGHTS],
            *[out_m[n] for n in _WEIGHTS], *[out_v[n] for n in _WEIGHTS])
```

```python
import collections
import functools

import jax
import jax.numpy as jnp
from jax import lax
from jax.experimental import pallas as pl
from jax.experimental.pallas import tpu as pltpu
from jax.experimental.pallas import tpu_sc as plsc

F32 = jnp.float32
BF16 = jnp.bfloat16

D_MODEL = 1024
N_HEADS = 8
NOPE = 128
ROPE = 64
HALF = ROPE // 2
Q_RANK = 384
KV_RANK = 256
CONV_W = 31
D_FF = 4096
EPS = 1e-6
ROPE_THETA = 10000.0
HEAD_PAD = 256
QK_SCALE = (NOPE + ROPE) ** -0.5
LOG2E = 1.4426950408889634
SUBLANES = 8
N_DEV = 8
FF_SHARD = D_FF // N_DEV
IN_SHARD = 4800 // N_DEV

ZS = Q_RANK + KV_RANK + 128
OFF_A = ZS
OFF_B = OFF_A + D_MODEL
OFF_GA = OFF_B + D_MODEL
OFF_GC = OFF_GA + D_MODEL
D_IN_PAD = OFF_GC + D_MODEL

ADAM_LR = 0.001
ADAM_B1 = 0.9
ADAM_B2 = 0.999
ADAM_EPS = 1e-08
ADAM_WD = 0.01
ADAM_STEP = 10

VMEM_LIMIT = 56 * 1024 * 1024

_SMALL = ("norm_mix_pre", "q_norm", "kv_norm", "conv_b", "conv_ln_g", "conv_ln_b", "b_pw2", "norm_mix_post",
          "norm_mlp_pre", "norm_mlp_post")
SMALL_ROWS = 16
LOSS_ROW = len(_SMALL)

_BIG = ("w_in", "w_uq", "w_uk", "w_uv", "w_o_attn", "conv_w", "w_pw2", "w_out", "w_ff1", "w_ff2")
_WEIGHTS = ("norm_mix_pre", "w_in", "q_norm", "w_uq", "kv_norm", "w_uk", "w_uv", "w_o_attn", "conv_w", "conv_b",
            "conv_ln_g", "conv_ln_b", "w_pw2", "b_pw2", "w_out", "norm_mix_post", "norm_mlp_pre", "w_ff1", "w_ff2",
            "norm_mlp_post")


def _dot(a, b):
    return jnp.dot(a, b, preferred_element_type=F32)


def _dot_nt(a, b):
    return lax.dot_general(a, b, (((1,), (1,)), ((), ())), preferred_element_type=F32)


def _dot_tn(a, b):
    return lax.dot_general(a, b, (((0,), (0,)), ((), ())), preferred_element_type=F32)


def _sigmoid(x):
    return 1.0 / (1.0 + jnp.exp(-x))


def _rms_fwd(x, g):
    r = lax.rsqrt(jnp.mean(x * x, axis=-1, keepdims=True) + EPS)
    return x * r * g


def _rms_bwd(dy, x, g):
    r = lax.rsqrt(jnp.mean(x * x, axis=-1, keepdims=True) + EPS)
    xh = x * r
    gy = dy * g
    dx = r * (gy - xh * jnp.mean(gy * xh, axis=-1, keepdims=True))
    return dx, jnp.sum(dy * xh, axis=0, keepdims=True)


def _rope(q, c, sa, sb):
    n = q.shape[-1]
    return q * c + pltpu.roll(q, n - HALF, 1) * sa + pltpu.roll(q, HALF, 1) * sb


def _rope_bwd(d, c, sa, sb):
    n = d.shape[-1]
    return d * c - pltpu.roll(d, n - HALF, 1) * sa - pltpu.roll(d, HALF, 1) * sb


def _shifted_copies(buf, shifted, tb):
    n = shifted.shape[1]
    for b in range(1, SUBLANES):
        shifted[b - 1] = buf[pl.ds(b, n), :]


def _rows_at(buf, shifted, start, tb):
    a, b = divmod(start, SUBLANES)
    src = buf if b == 0 else shifted.at[b - 1]
    return src[pl.ds(SUBLANES * a, tb), :]


def _params(n_axes=1):
    return pltpu.CompilerParams(dimension_semantics=("arbitrary",) * n_axes, vmem_limit_bytes=VMEM_LIMIT)


def _row_call(body, name, tb, row_ins, full_ins, row_outs, acc_outs, lane_outs=(), exchange=None):
    t = row_ins[0].shape[0]
    in_specs = [pl.BlockSpec((tb, a.shape[1]), lambda i: (i, 0)) for a in row_ins]
    in_specs += [pl.BlockSpec(a.shape, lambda i, nd=a.ndim: (0,) * nd) for a in full_ins]
    out_specs = [pl.BlockSpec((tb, c), lambda i: (i, 0)) for c, _ in row_outs]
    out_specs += [pl.BlockSpec(s, lambda i, nd=len(s): (0,) * nd) for s, _ in acc_outs]
    out_specs += [pl.BlockSpec((n, None, 1, tb), lambda i: (0, i, 0, 0)) for n in lane_outs]
    out_shape = [jax.ShapeDtypeStruct((t, c), dt) for c, dt in row_outs]
    out_shape += [jax.ShapeDtypeStruct(s, dt) for s, dt in acc_outs]
    out_shape += [jax.ShapeDtypeStruct((n, t // tb, 1, tb), F32) for n in lane_outs]
    if exchange is None:
        return pl.pallas_call(
            functools.partial(body), name=name, grid=(t // tb,), in_specs=in_specs, out_specs=out_specs,
            out_shape=out_shape, compiler_params=_params(1),
        )(*row_ins, *full_ins)
    steps = _phase_steps(len(exchange.phases), t // tb)
    fn = _hosted(body, len(in_specs), len(out_specs), exchange, lambda k: pl.program_id(0) == steps[k])
    res = pl.pallas_call(
        fn, name=name, grid=(t // tb,), in_specs=in_specs + [_ANY] * len(exchange.ins),
        out_specs=out_specs + [_ANY] * len(exchange.out_shapes), out_shape=out_shape + exchange.out_shapes,
        scratch_shapes=exchange.scratch, compiler_params=_params(1),
    )(*row_ins, *full_ins, *exchange.ins)
    return res[:len(out_specs)], res[len(out_specs):]


def _acc(ref, val):
    @pl.when(pl.program_id(0) == 0)
    def _():
        ref[...] = jnp.zeros_like(ref)
    ref[...] += val


def _fwd_in_proj(x, g_pre, w_in_p, exchange, tb=512):
    def body(x_ref, g_ref, w_ref, h_ref, zs_ref, a_ref, sb_ref, u_ref, sa_ref, sc_ref):
        hb = _rms_fwd(x_ref[...], g_ref[...]).astype(BF16)
        h_ref[...] = hb
        zs_ref[...] = _dot_nt(hb, w_ref[0:ZS, :])
        a = _dot_nt(hb, w_ref[OFF_A:OFF_B, :])
        sb = _sigmoid(_dot_nt(hb, w_ref[OFF_B:OFF_GA, :]))
        a_ref[...] = a
        sb_ref[...] = sb
        u_ref[...] = a * sb
        sa_ref[...] = _sigmoid(_dot_nt(hb, w_ref[OFF_GA:OFF_GC, :]))
        sc_ref[...] = _sigmoid(_dot_nt(hb, w_ref[OFF_GC:D_IN_PAD, :]))

    d = D_MODEL
    return _row_call(body, "fwd_in_proj", tb, [x], [g_pre, w_in_p],
                     [(d, BF16), (ZS, F32), (d, F32), (d, F32), (d, F32), (d, F32), (d, F32)], [], exchange=exchange)


def _fwd_qkv(zs, tc, tsa, tsb, q_norm, kv_norm, w_uq_p, w_uk, w_uv, tb=256):
    def body(zs_ref, c_ref, sa_ref, sb_ref, qg_ref, kg_ref, wq_ref, wk_ref, wv_ref,
             cqn_ref, ckvn_ref, q_ref, k_ref, v_ref):
        zs_ = zs_ref[...]
        c, sa, sb = c_ref[...], sa_ref[...], sb_ref[...]
        cqn = _rms_fwd(zs_[:, 0:Q_RANK], qg_ref[...]).astype(BF16)
        cqn_ref[...] = cqn
        for h in range(N_HEADS):
            qh = _dot(cqn, wq_ref[h]) * (QK_SCALE * LOG2E)
            q_ref[:, h * HEAD_PAD:h * HEAD_PAD + NOPE] = qh[:, :NOPE].astype(BF16)
            q_ref[:, h * HEAD_PAD + NOPE:(h + 1) * HEAD_PAD] = _rope(qh[:, NOPE:], c, sa, sb).astype(BF16)
        kr = _rope(zs_[:, Q_RANK + KV_RANK:ZS], c, sa, sb).astype(BF16)
        ckvn = _rms_fwd(zs_[:, Q_RANK:Q_RANK + KV_RANK], kg_ref[...]).astype(BF16)
        ckvn_ref[...] = ckvn
        kn = _dot(ckvn, wk_ref[...]).astype(BF16)
        v_ref[...] = _dot(ckvn, wv_ref[...]).astype(BF16)
        for h in range(N_HEADS):
            k_ref[:, h * HEAD_PAD:h * HEAD_PAD + NOPE] = kn[:, h * NOPE:(h + 1) * NOPE]
            k_ref[:, h * HEAD_PAD + NOPE:(h + 1) * HEAD_PAD] = kr

    hp = N_HEADS * HEAD_PAD
    return _row_call(body, "fwd_qkv", tb, [zs, tc, tsa, tsb], [q_norm, kv_norm, w_uq_p, w_uk, w_uv],
                     [(Q_RANK, BF16), (KV_RANK, BF16), (hp, BF16), (hp, BF16), (D_MODEL, BF16)], [])


def _attn_fwd(q, k, v, exchange, tq=512):
    t = q.shape[0]
    nq = t // tq

    def body(q_ref, k_ref, v_ref, o_ref, lse_ref, m_sc, l_sc, acc_sc, s_sc):
        i = pl.program_id(1)
        m_sc[...] = jnp.full_like(m_sc, -1e30)
        l_sc[...] = jnp.zeros_like(l_sc)
        acc_sc[...] = jnp.zeros_like(acc_sc)
        qb = q_ref[...]

        def rows(j):
            return pl.ds(pl.multiple_of(j * tq, tq), tq)

        def scores(j, slot):
            s_sc[slot] = _dot_nt(k_ref[rows(j), :], qb)

        def update(j, slot, masked):
            st = s_sc[slot]
            if masked:
                key = lax.broadcasted_iota(jnp.int32, (tq, tq), 0)
                qry = lax.broadcasted_iota(jnp.int32, (tq, tq), 1)
                st = jnp.where(key <= qry, st, -1e30)
            m_prev = m_sc[...]
            m_new = jnp.maximum(m_prev, jnp.max(st, axis=0, keepdims=True))
            alpha = jnp.exp2(m_prev - m_new)
            pt = jnp.exp2(st - m_new)
            l_sc[...] = alpha * l_sc[...] + jnp.sum(pt, axis=0, keepdims=True)
            acc_sc[...] = alpha * acc_sc[...] + _dot_tn(v_ref[rows(j), :], pt.astype(BF16))
            m_sc[...] = m_new

        def pair(p, carry):
            scores(2 * p + 1, 1)
            update(2 * p, 0, False)
            scores(2 * p + 2, 0)
            update(2 * p + 1, 1, False)
            return carry

        scores(0, 0)
        lax.fori_loop(0, i // 2, pair, 0)

        @pl.when(i % 2 == 1)
        def _():
            scores(i, 1)
            update(i - 1, 0, False)
            update(i, 1, True)

        @pl.when(i % 2 == 0)
        def _():
            update(i, 0, True)

        l = l_sc[...]
        o_ref[...] = (acc_sc[...] / l).T.astype(BF16)
        lse_ref[...] = m_sc[...] + jnp.log2(l)

    steps = _phase_steps(len(exchange.phases), N_HEADS * nq)
    fn = _hosted(body, 3, 2, exchange, lambda p: pl.program_id(0) * nq + pl.program_id(1) == steps[p])
    res = pl.pallas_call(
        fn, name="attn_fwd", grid=(N_HEADS, nq),
        in_specs=[pl.BlockSpec((tq, HEAD_PAD), lambda h, i: (i, h)),
                  pl.BlockSpec((t, HEAD_PAD), lambda h, i: (0, h)),
                  pl.BlockSpec((t, NOPE), lambda h, i: (0, h))] + [_ANY] * len(exchange.ins),
        out_specs=[pl.BlockSpec((tq, NOPE), lambda h, i: (i, h)),
                   pl.BlockSpec((None, None, 1, tq), lambda h, i: (h, i, 0, 0))] + [_ANY] * len(exchange.out_shapes),
        out_shape=[jax.ShapeDtypeStruct((t, D_MODEL), BF16),
                   jax.ShapeDtypeStruct((N_HEADS, nq, 1, tq), F32)] + exchange.out_shapes,
        scratch_shapes=[pltpu.VMEM((1, tq), F32), pltpu.VMEM((1, tq), F32), pltpu.VMEM((NOPE, tq), F32),
                        pltpu.VMEM((2, tq, tq), F32)] + exchange.scratch,
        compiler_params=_params(2),
    )(q, k, v, *exchange.ins)
    return res[:2], res[2:]


def _conv_fwd(u, conv_w, conv_b, ln_g, ln_b, tb=256):
    t, c = u.shape
    halo = 32

    def body(u_ref, up_ref, w_ref, b_ref, g_ref, be_ref, co_ref, act_ref, buf, shifted):
        i = pl.program_id(0)
        buf[0:halo, :] = jnp.where(i == 0, 0.0, up_ref[...])
        buf[halo:halo + tb, :] = u_ref[...]
        _shifted_copies(buf, shifted, tb)
        acc = jnp.zeros((tb, c), F32)
        for k in range(CONV_W):
            acc = acc + w_ref[k:k + 1, :] * _rows_at(buf, shifted, halo - (CONV_W - 1) + k, tb)
        co = acc + b_ref[...]
        co_ref[...] = co
        mu = jnp.mean(co, axis=-1, keepdims=True)
        xc = co - mu
        r = lax.rsqrt(jnp.mean(xc * xc, axis=-1, keepdims=True) + EPS)
        y = xc * r * g_ref[...] + be_ref[...]
        act_ref[...] = (y * _sigmoid(y)).astype(BF16)

    ratio = tb // halo
    return pl.pallas_call(
        functools.partial(body), name="conv_fwd", grid=(t // tb,),
        in_specs=[pl.BlockSpec((tb, c), lambda i: (i, 0)),
                  pl.BlockSpec((halo, c), lambda i: (jnp.maximum(i * ratio - 1, 0), 0)),
                  pl.BlockSpec(conv_w.shape, lambda i: (0, 0)),
                  pl.BlockSpec((1, c), lambda i: (0, 0)), pl.BlockSpec((1, c), lambda i: (0, 0)),
                  pl.BlockSpec((1, c), lambda i: (0, 0))],
        out_specs=[pl.BlockSpec((tb, c), lambda i: (i, 0)), pl.BlockSpec((tb, c), lambda i: (i, 0))],
        out_shape=[jax.ShapeDtypeStruct((t, c), F32), jax.ShapeDtypeStruct((t, c), BF16)],
        scratch_shapes=[pltpu.VMEM((tb + halo, c), F32), pltpu.VMEM((SUBLANES - 1, tb + halo - SUBLANES, c), F32)],
        compiler_params=_params(1),
    )(u, u, conv_w, conv_b, ln_g, ln_b)


def _fwd_merge(attn, uact, sa, sc, x, w_o, w_pw2, b_pw2, w_out, g_post, tb=512):
    def body(at_ref, ua_ref, sa_ref, sc_ref, x_ref, wo_ref, wp_ref, bp_ref, wout_ref, g_ref,
             ya_ref, yc_ref, mb_ref, m_ref, x1_ref):
        ya = _dot(at_ref[...], wo_ref[...])
        yc = _dot(ua_ref[...], wp_ref[...]) + bp_ref[...]
        ya_ref[...] = ya.astype(BF16)
        yc_ref[...] = yc.astype(BF16)
        mb = (sa_ref[...] * ya + sc_ref[...] * yc).astype(BF16)
        mb_ref[...] = mb
        m = _dot(mb, wout_ref[...])
        m_ref[...] = m
        x1_ref[...] = x_ref[...] + _rms_fwd(m, g_ref[...])

    d = D_MODEL
    return _row_call(body, "fwd_merge", tb, [attn, uact, sa, sc, x], [w_o, w_pw2, b_pw2, w_out, g_post],
                     [(d, BF16), (d, BF16), (d, BF16), (d, F32), (d, F32)], [])


def _fwd_ff1(x1, g, w_ff1, tb=512):
    def body(x1_ref, g_ref, w_ref, h2_ref, r1_ref, act_ref):
        h2 = _rms_fwd(x1_ref[...], g_ref[...]).astype(BF16)
        h2_ref[...] = h2
        for j in range(N_DEV):
            cols = slice(j * FF_SHARD, (j + 1) * FF_SHARD)
            r1 = jnp.maximum(_dot(h2, w_ref[j]), 0.0)
            r1_ref[:, cols] = r1.astype(BF16)
            act_ref[:, cols] = (r1 * r1).astype(BF16)

    return _row_call(body, "fwd_ff1", tb, [x1], [g, w_ff1], [(D_MODEL, BF16), (D_FF, BF16), (D_FF, BF16)], [])


def _fwd_ff2_loss(act, x1, target, w_ff2, g, tb=512):
    def body(act_ref, x1_ref, tg_ref, w_ref, g_ref, f_ref, dy_ref, loss_ref):
        f = _dot(act_ref[...], w_ref[...])
        f_ref[...] = f
        e = x1_ref[...] + _rms_fwd(f, g_ref[...]) - tg_ref[...]
        dy_ref[...] = e * (1.0 / D_MODEL)
        _acc(loss_ref, jnp.sum(e * e))

    return _row_call(body, "fwd_ff2_loss", tb, [act, x1, target], [w_ff2, g],
                     [(D_MODEL, F32), (D_MODEL, F32)], [((8, 128), F32)])


def _bwd_ff2(dy, f, r1, w_ff2, g, tb=512):
    def body(dy_ref, f_ref, r1_ref, w_ref, g_ref, df_ref, df1_ref, dg_ref):
        df, dg = _rms_bwd(dy_ref[...], f_ref[...], g_ref[...])
        _acc(dg_ref, dg)
        dfb = df.astype(BF16)
        df_ref[...] = dfb
        dact = _dot_nt(dfb, w_ref[...])
        df1_ref[...] = (dact * (2.0 * r1_ref[...].astype(F32))).astype(BF16)

    return _row_call(body, "bwd_ff2", tb, [dy, f, r1], [w_ff2, g], [(D_MODEL, BF16), (D_FF, BF16)],
                     [((1, D_MODEL), F32)])


def _bwd_ff1(df1, x1, dy, w_ff1, g, exchange, tb=512):
    def body(df1_ref, x1_ref, dy_ref, w_ref, g_ref, dx1_ref, dg_ref):
        dh2 = _dot_nt(df1_ref[:, 0:FF_SHARD], w_ref[0])
        for j in range(1, N_DEV):
            dh2 = dh2 + _dot_nt(df1_ref[:, j * FF_SHARD:(j + 1) * FF_SHARD], w_ref[j])
        dxn, dg = _rms_bwd(dh2, x1_ref[...], g_ref[...])
        _acc(dg_ref, dg)
        dx1_ref[...] = dy_ref[...] + dxn

    return _row_call(body, "bwd_ff1", tb, [df1, x1, dy], [w_ff1, g], [(D_MODEL, F32)], [((1, D_MODEL), F32)],
                     exchange=exchange)


def _bwd_merge(dx1, m, sa, sc, ya, yc, attn, w_out, w_o, w_pw2, g_post, exchange, tb=256):
    def body(dx1_ref, m_ref, sa_ref, sc_ref, ya_ref, yc_ref, at_ref, wout_ref, wo_ref, wp_ref, g_ref,
             dm_ref, dya_ref, dyc_ref, dga_ref, dgc_ref, dat_ref, dua_ref, dg_ref, dbp_ref, delta_ref):
        dm, dg = _rms_bwd(dx1_ref[...], m_ref[...], g_ref[...])
        _acc(dg_ref, dg)
        dmb = dm.astype(BF16)
        dm_ref[...] = dmb
        dmerged = _dot_nt(dmb, wout_ref[...])
        sa, sc = sa_ref[...], sc_ref[...]
        dya = dmerged * sa
        dyc = dmerged * sc
        _acc(dbp_ref, jnp.sum(dyc, axis=0, keepdims=True))
        dyab = dya.astype(BF16)
        dycb = dyc.astype(BF16)
        dya_ref[...] = dyab
        dyc_ref[...] = dycb
        dga_ref[...] = (dmerged * ya_ref[...].astype(F32) * sa * (1.0 - sa)).astype(BF16)
        dgc_ref[...] = (dmerged * yc_ref[...].astype(F32) * sc * (1.0 - sc)).astype(BF16)
        dat = _dot_nt(dyab, wo_ref[...])
        dat_ref[...] = dat.astype(BF16)
        prod = dat * at_ref[...].astype(F32)
        lane = lax.broadcasted_iota(jnp.int32, (tb, NOPE), 1)
        dl = jnp.zeros((tb, NOPE), F32)
        for h in range(N_HEADS):
            dl = dl + jnp.where(lane == h, jnp.sum(prod[:, h * NOPE:(h + 1) * NOPE], axis=1, keepdims=True), 0.0)
        dlt = dl.T
        for h in range(N_HEADS):
            delta_ref[h] = dlt[h:h + 1, :]
        dua_ref[...] = _dot_nt(dycb, wp_ref[...]).astype(BF16)

    d = D_MODEL
    return _row_call(body, "bwd_merge", tb, [dx1, m, sa, sc, ya, yc, attn], [w_out, w_o, w_pw2, g_post],
                     [(d, BF16), (d, BF16), (d, BF16), (d, BF16), (d, BF16), (d, BF16), (d, BF16)],
                     [((1, d), F32), ((1, d), F32)], lane_outs=(N_HEADS,), exchange=exchange)


def _bwd_ln(dua, co, ln_g, ln_b, tb=256):
    def body(dua_ref, co_ref, g_ref, be_ref, dco_ref, dg_ref, db_ref, dcb_ref):
        co = co_ref[...]
        g = g_ref[...]
        mu = jnp.mean(co, axis=-1, keepdims=True)
        xc = co - mu
        r = lax.rsqrt(jnp.mean(xc * xc, axis=-1, keepdims=True) + EPS)
        xh = xc * r
        y = xh * g + be_ref[...]
        s = _sigmoid(y)
        dy = dua_ref[...].astype(F32) * (s + y * s * (1.0 - s))
        _acc(db_ref, jnp.sum(dy, axis=0, keepdims=True))
        _acc(dg_ref, jnp.sum(dy * xh, axis=0, keepdims=True))
        gy = dy * g
        dco = r * (gy - jnp.mean(gy, axis=-1, keepdims=True) - xh * jnp.mean(gy * xh, axis=-1, keepdims=True))
        dco_ref[...] = dco
        _acc(dcb_ref, jnp.sum(dco, axis=0, keepdims=True))

    d = D_MODEL
    return _row_call(body, "bwd_ln", tb, [dua, co], [ln_g, ln_b], [(d, F32)],
                     [((1, d), F32), ((1, d), F32), ((1, d), F32)])


def _conv_bwd(dco, u, a, sb, conv_w, exchange, tb=256):
    t, c = u.shape
    halo = 32
    ratio = tb // halo
    nblk = t // tb

    def body(d_ref, dn_ref, u_ref, up_ref, a_ref, sb_ref, w_ref, da_ref, db_ref, dw_ref, bufd, bufu, shd, shu):
        i = pl.program_id(0)

        @pl.when(i == 0)
        def _():
            dw_ref[...] = jnp.zeros_like(dw_ref)

        dco = d_ref[...]
        bufd[0:tb, :] = dco
        bufd[tb:tb + halo, :] = jnp.where(i == nblk - 1, 0.0, dn_ref[...])
        bufu[0:halo, :] = jnp.where(i == 0, 0.0, up_ref[...])
        bufu[halo:halo + tb, :] = u_ref[...]
        _shifted_copies(bufd, shd, tb)
        _shifted_copies(bufu, shu, tb)
        du = jnp.zeros((tb, c), F32)
        for k in range(CONV_W):
            du = du + w_ref[k:k + 1, :] * _rows_at(bufd, shd, CONV_W - 1 - k, tb)
            dw_ref[k:k + 1, :] += jnp.sum(dco * _rows_at(bufu, shu, halo - (CONV_W - 1) + k, tb), axis=0,
                                          keepdims=True)
        sb_ = sb_ref[...]
        da_ref[...] = (du * sb_).astype(BF16)
        db_ref[...] = (du * a_ref[...] * sb_ * (1.0 - sb_)).astype(BF16)

    steps = _phase_steps(len(exchange.phases), nblk)
    fn = _hosted(body, 7, 3, exchange, lambda p: pl.program_id(0) == steps[p])
    res = pl.pallas_call(
        fn, name="conv_bwd", grid=(nblk,),
        in_specs=[pl.BlockSpec((tb, c), lambda i: (i, 0)),
                  pl.BlockSpec((halo, c), lambda i: (jnp.minimum((i + 1) * ratio, t // halo - 1), 0)),
                  pl.BlockSpec((tb, c), lambda i: (i, 0)),
                  pl.BlockSpec((halo, c), lambda i: (jnp.maximum(i * ratio - 1, 0), 0)),
                  pl.BlockSpec((tb, c), lambda i: (i, 0)), pl.BlockSpec((tb, c), lambda i: (i, 0)),
                  pl.BlockSpec(conv_w.shape, lambda i: (0, 0))] + [_ANY] * len(exchange.ins),
        out_specs=[pl.BlockSpec((tb, c), lambda i: (i, 0)), pl.BlockSpec((tb, c), lambda i: (i, 0)),
                   pl.BlockSpec((32, c), lambda i: (0, 0))] + [_ANY] * len(exchange.out_shapes),
        out_shape=[jax.ShapeDtypeStruct((t, c), BF16), jax.ShapeDtypeStruct((t, c), BF16),
                   jax.ShapeDtypeStruct((32, c), F32)] + exchange.out_shapes,
        scratch_shapes=[pltpu.VMEM((tb + halo, c), F32), pltpu.VMEM((tb + halo, c), F32),
                        pltpu.VMEM((SUBLANES - 1, tb + halo - SUBLANES, c), F32),
                        pltpu.VMEM((SUBLANES - 1, tb + halo - SUBLANES, c), F32)] + exchange.scratch,
        compiler_params=_params(1),
    )(dco, dco, u, u, a, sb, conv_w, *exchange.ins)
    return res[:3], res[3:]


def _attn_bwd(q, k, v, do, lse2, delta, exchange, tq=512):
    t = q.shape[0]
    nq = t // tq
    td = delta.shape[-1]
    per = tq // td

    def body(q_ref, k_ref, v_ref, do_ref, lse_ref, dl_ref, dq_ref, dk_ref, dv_ref, dk_sc, dv_sc, s_sc, dp_sc, dq_sc):
        j = pl.program_id(1)

        @pl.when(j == 0)
        def _():
            dq_sc[...] = jnp.zeros_like(dq_sc)

        dk_sc[...] = jnp.zeros_like(dk_sc)
        dv_sc[...] = jnp.zeros_like(dv_sc)
        kb, vb = k_ref[...], v_ref[...]

        def rows(i):
            return pl.ds(pl.multiple_of(i * tq, tq), tq)

        def ahead(i, slot):
            i = jnp.minimum(i, nq - 1)
            s_sc[slot] = _dot_nt(kb, q_ref[rows(i), :])
            dp_sc[slot] = _dot_nt(vb, do_ref[rows(i), :])

        def finish(i, slot, masked):
            qb, dob = q_ref[rows(i), :], do_ref[rows(i), :]
            pt = jnp.exp2(s_sc[slot] - lse_ref[i])
            if masked:
                key = lax.broadcasted_iota(jnp.int32, (tq, tq), 0)
                qry = lax.broadcasted_iota(jnp.int32, (tq, tq), 1)
                pt = jnp.where(key <= qry, pt, 0.0)
            dv_sc[...] += _dot(pt.astype(BF16), dob)
            dl = jnp.concatenate([dl_ref[per * i + r] for r in range(per)], axis=-1)
            dst = (pt * (dp_sc[slot] - dl)).astype(BF16)
            dk_sc[...] += _dot(dst, qb)
            dq_sc[rows(i), :] += _dot_tn(dst, kb)

        def pair(p, carry):
            i = j + 1 + 2 * p
            ahead(i + 1, 0)
            finish(i, 1, False)
            ahead(i + 2, 1)
            finish(i + 1, 0, False)
            return carry

        n_after = nq - 1 - j
        ahead(j, 0)
        ahead(j + 1, 1)
        finish(j, 0, True)
        lax.fori_loop(0, n_after // 2, pair, 0)

        @pl.when(n_after % 2 == 1)
        def _():
            finish(nq - 1, 1, False)

        @pl.when(j == nq - 1)
        def _():
            dq_ref[...] = dq_sc[...].astype(BF16)

        dk_ref[...] = (dk_sc[...] * (1.0 / LOG2E)).astype(BF16)
        dv_ref[...] = dv_sc[...].astype(BF16)

    hp = N_HEADS * HEAD_PAD
    steps = _phase_steps(len(exchange.phases), N_HEADS * nq)
    fn = _hosted(body, 6, 3, exchange, lambda p: pl.program_id(0) * nq + pl.program_id(1) == steps[p])
    res = pl.pallas_call(
        fn, name="attn_bwd", grid=(N_HEADS, nq),
        in_specs=[pl.BlockSpec((t, HEAD_PAD), lambda h, j: (0, h)),
                  pl.BlockSpec((tq, HEAD_PAD), lambda h, j: (j, h)),
                  pl.BlockSpec((tq, NOPE), lambda h, j: (j, h)),
                  pl.BlockSpec((t, NOPE), lambda h, j: (0, h)),
                  pl.BlockSpec((None, nq, 1, tq), lambda h, j: (h, 0, 0, 0)),
                  pl.BlockSpec((None, t // td, 1, td), lambda h, j: (h, 0, 0, 0))] + [_ANY] * len(exchange.ins),
        out_specs=[pl.BlockSpec((t, HEAD_PAD), lambda h, j: (0, h)),
                   pl.BlockSpec((tq, HEAD_PAD), lambda h, j: (j, h)),
                   pl.BlockSpec((tq, NOPE), lambda h, j: (j, h))] + [_ANY] * len(exchange.out_shapes),
        out_shape=[jax.ShapeDtypeStruct((t, hp), BF16), jax.ShapeDtypeStruct((t, hp), BF16),
                   jax.ShapeDtypeStruct((t, D_MODEL), BF16)] + exchange.out_shapes,
        scratch_shapes=[pltpu.VMEM((tq, HEAD_PAD), F32), pltpu.VMEM((tq, NOPE), F32), pltpu.VMEM((2, tq, tq), F32),
                        pltpu.VMEM((2, tq, tq), F32), pltpu.VMEM((t, HEAD_PAD), F32)] + exchange.scratch,
        compiler_params=_params(2),
    )(q, k, v, do, lse2, delta, *exchange.ins)
    return res[:3], res[3:]


def _bwd_qkv(dq, dk, dv, zs, tc, tsa, tsb, q_norm, kv_norm, w_uq_p, w_uk, w_uv, tb=256):
    def body(dq_ref, dk_ref, dv_ref, zs_ref, c_ref, sa_ref, sb_ref, qg_ref, kg_ref, wq_ref, wk_ref, wv_ref,
             dqp_ref, dkn_ref, dzs_ref, dqg_ref, dkg_ref):
        c, sa, sb = c_ref[...], sa_ref[...], sb_ref[...]
        zs_ = zs_ref[...]
        dcqn = jnp.zeros((tb, Q_RANK), F32)
        dkr = jnp.zeros((tb, NOPE), F32)
        for h in range(N_HEADS):
            nope = slice(h * HEAD_PAD, h * HEAD_PAD + NOPE)
            rope = slice(h * HEAD_PAD + NOPE, (h + 1) * HEAD_PAD)
            dqp_ref[:, nope] = (dq_ref[:, nope].astype(F32) * QK_SCALE).astype(BF16)
            dqp_ref[:, rope] = (_rope_bwd(dq_ref[:, rope].astype(F32), c, sa, sb) * QK_SCALE).astype(BF16)
            dcqn = dcqn + _dot_nt(dqp_ref[:, h * HEAD_PAD:(h + 1) * HEAD_PAD], wq_ref[h])
            dkn_ref[:, h * NOPE:(h + 1) * NOPE] = dk_ref[:, nope]
            dkr = dkr + dk_ref[:, rope].astype(F32)
        dcq, dqg = _rms_bwd(dcqn, zs_[:, 0:Q_RANK], qg_ref[...])
        _acc(dqg_ref, dqg)
        dzs_ref[:, 0:Q_RANK] = dcq.astype(BF16)
        dzs_ref[:, Q_RANK + KV_RANK:ZS] = _rope_bwd(dkr, c, sa, sb).astype(BF16)
        dckvn = _dot_nt(dkn_ref[...], wk_ref[...]) + _dot_nt(dv_ref[...], wv_ref[...])
        dckv, dkg = _rms_bwd(dckvn, zs_[:, Q_RANK:Q_RANK + KV_RANK], kg_ref[...])
        _acc(dkg_ref, dkg)
        dzs_ref[:, Q_RANK:Q_RANK + KV_RANK] = dckv.astype(BF16)

    hp = N_HEADS * HEAD_PAD
    return _row_call(body, "bwd_qkv", tb, [dq, dk, dv, zs, tc, tsa, tsb], [q_norm, kv_norm, w_uq_p, w_uk, w_uv],
                     [(hp, BF16), (D_MODEL, BF16), (ZS, BF16)], [((1, Q_RANK), F32), ((1, KV_RANK), F32)])


def _bwd_in_proj(dzs, da, db, dga, dgc, x, dx1, w_in_p, g_pre, exchange, tb=512):
    def body(dzs_ref, da_ref, db_ref, dga_ref, dgc_ref, x_ref, dx1_ref, w_ref, g_ref, gx_ref, dg_ref):
        dh = _dot(dzs_ref[...], w_ref[0:ZS, :])
        dh = dh + _dot(da_ref[...], w_ref[OFF_A:OFF_B, :])
        dh = dh + _dot(db_ref[...], w_ref[OFF_B:OFF_GA, :])
        dh = dh + _dot(dga_ref[...], w_ref[OFF_GA:OFF_GC, :])
        dh = dh + _dot(dgc_ref[...], w_ref[OFF_GC:D_IN_PAD, :])
        dxn, dg = _rms_bwd(dh, x_ref[...], g_ref[...])
        _acc(dg_ref, dg)
        gx_ref[...] = dx1_ref[...] + dxn

    return _row_call(body, "bwd_in_proj", tb, [dzs, da, db, dga, dgc, x, dx1], [w_in_p, g_pre],
                     [(D_MODEL, F32)], [((1, D_MODEL), F32)], exchange=exchange)


def _mm_tn(a, b, name, shard_cols=None, tt=2048):
    t, m = a.shape
    n = b.shape[1]
    tm = min(m, 1024)
    tn = min(n, 1024)
    tt = min(t, tt)
    nt = t // tt
    per = tn // shard_cols if shard_cols else 1

    def body(a_ref, b_ref, o_ref, acc):
        k = pl.program_id(2)

        @pl.when(k == 0)
        def _():
            acc[...] = jnp.zeros_like(acc)

        acc[...] += _dot_tn(a_ref[...], b_ref[...])

        @pl.when(k == nt - 1)
        def _():
            if shard_cols:
                for s in range(per):
                    o_ref[s] = acc[:, s * shard_cols:(s + 1) * shard_cols].astype(BF16)
            else:
                o_ref[...] = acc[...].astype(BF16)

    if shard_cols:
        out_spec = pl.BlockSpec((per, tm, shard_cols), lambda i, j, k: (j, i, 0))
        out_shape = jax.ShapeDtypeStruct((n // shard_cols, m, shard_cols), BF16)
    else:
        out_spec = pl.BlockSpec((tm, tn), lambda i, j, k: (i, j))
        out_shape = jax.ShapeDtypeStruct((m, n), BF16)
    return pl.pallas_call(
        functools.partial(body), name=name, grid=(m // tm, n // tn, nt),
        in_specs=[pl.BlockSpec((tt, tm), lambda i, j, k: (k, i)), pl.BlockSpec((tt, tn), lambda i, j, k: (k, j))],
        out_specs=out_spec, out_shape=out_shape, scratch_shapes=[pltpu.VMEM((tm, tn), F32)],
        compiler_params=_params(3),
    )(a, b)


_ANY = pl.BlockSpec(memory_space=pl.ANY)
_MESH = pl.DeviceIdType.MESH

_Exchange = collections.namedtuple("_Exchange", "ins out_shapes scratch phases")


def _ag_exchange(shards):
    n = len(shards)

    def parts(ins, outs, sems):
        send_sems, recv_sems, _ = sems
        x, y, c = lax.axis_index("x"), lax.axis_index("y"), lax.axis_index("c")
        chips = [(1 - x, y), (x, 1 - y), (1 - x, 1 - y)]

        def copy(w, k, block, to, src=None):
            dst = outs[w].at[4 * block[0] + 2 * block[1] + block[2]]
            return pltpu.make_async_remote_copy(
                src_ref=dst if src is None else src, dst_ref=dst, send_sem=send_sems.at[7 * w + k],
                recv_sem=recv_sems.at[7 * w + k], device_id=to, device_id_type=_MESH)

        def first(w):
            return [copy(w, 0, (x, y, c), (x, y, 1 - c), src=ins[w])] + [
                copy(w, 1 + j, (x, y, c), (*chip, c), src=ins[w]) for j, chip in enumerate(chips)]

        def mine(w):
            return pltpu.make_async_copy(ins[w], outs[w].at[4 * x + 2 * y + c], sems[2].at[w])

        return (x, y, c), chips, copy, first, mine

    def start(ins, outs, sems):
        _, _, _, first, mine = parts(ins, outs, sems)
        for w in range(n):
            mine(w).start()
            for cp in first(w):
                cp.start()

    def forward(ins, outs, sems):
        (x, y, c), chips, copy, _, _ = parts(ins, outs, sems)
        for j, chip in enumerate(chips):
            for w in range(n):
                copy(w, 1 + j, (*chip, c), (x, y, c)).wait_recv()
                copy(w, 4 + j, (*chip, c), (x, y, 1 - c)).start()

    def finish(ins, outs, sems):
        (x, y, c), chips, copy, first, mine = parts(ins, outs, sems)
        for w in range(n):
            copy(w, 0, (x, y, 1 - c), (x, y, c)).wait_recv()
        for j, chip in enumerate(chips):
            for w in range(n):
                copy(w, 4 + j, (*chip, 1 - c), (x, y, c)).wait_recv()
        for w in range(n):
            for cp in first(w) + [copy(w, 4 + j, (*chip, c), (x, y, 1 - c)) for j, chip in enumerate(chips)]:
                cp.wait_send()
            mine(w).wait()

    return _Exchange(
        ins=list(shards), out_shapes=[jax.ShapeDtypeStruct((N_DEV,) + s.shape, s.dtype) for s in shards],
        scratch=[pltpu.SemaphoreType.DMA((7 * n,)), pltpu.SemaphoreType.DMA((7 * n,)), pltpu.SemaphoreType.DMA((n,))],
        phases=[start, forward, finish])


def _pair_exchange(gs):
    n = len(gs)

    def copies(ins, outs, sems):
        x, y, c = lax.axis_index("x"), lax.axis_index("y"), lax.axis_index("c")
        return [pltpu.make_async_remote_copy(
            src_ref=ins[w].at[:, 1 - c], dst_ref=outs[w], send_sem=sems[0].at[w], recv_sem=sems[1].at[w],
            device_id=(x, y, 1 - c), device_id_type=_MESH) for w in range(n)]

    return _start_then_wait(gs, [jax.ShapeDtypeStruct((4,) + g.shape[2:], g.dtype) for g in gs], n, copies)


def _start_then_wait(ins, out_shapes, n_copies, copies):
    def start(ins_, outs, sems):
        for cp in copies(ins_, outs, sems):
            cp.start()

    def finish(ins_, outs, sems):
        for cp in copies(ins_, outs, sems):
            cp.wait()

    return _Exchange(ins=list(ins), out_shapes=out_shapes,
                     scratch=[pltpu.SemaphoreType.DMA((n_copies,)), pltpu.SemaphoreType.DMA((n_copies,))],
                     phases=[start, finish])


def _run_exchange(ex, name):
    ni, no = len(ex.ins), len(ex.out_shapes)

    def body(*refs):
        for phase in ex.phases:
            phase(refs[:ni], refs[ni:ni + no], refs[ni + no:])

    return pl.pallas_call(functools.partial(body), name=name, out_shape=ex.out_shapes, in_specs=[_ANY] * ni,
                          out_specs=[_ANY] * no, scratch_shapes=ex.scratch)(*ex.ins)


def _run_exchange_on_sequencer(ex, name, collective_id):
    src = [jax.new_ref(a, memory_space=pltpu.MemorySpace.HBM) for a in ex.ins]
    dst = [jax.empty_ref(s, memory_space=pltpu.MemorySpace.HBM) for s in ex.out_shapes]

    def body(*sems):
        x, y, c = lax.axis_index("x"), lax.axis_index("y"), lax.axis_index("c")
        barrier = pltpu.get_barrier_semaphore()
        for peer in [(x, y, 1 - c), (1 - x, y, c), (x, 1 - y, c), (1 - x, 1 - y, c)]:
            pl.semaphore_signal(barrier, inc=1, device_id=peer, device_id_type=_MESH)
        pl.semaphore_wait(barrier, 4)
        for phase in ex.phases:
            phase(src, dst, sems)

    pl.kernel(body, mesh=plsc.ScalarSubcoreMesh(axis_name="sequencer", num_cores=1), name=name,
              scratch_types=ex.scratch, compiler_params=pltpu.CompilerParams(collective_id=collective_id))()
    return [d[...] for d in dst]


def _hosted(body, n_in, n_out, ex, when):
    ni, no, ns = len(ex.ins), len(ex.out_shapes), len(ex.scratch)

    def fn(*refs):
        ins, ex_ins = refs[:n_in], refs[n_in:n_in + ni]
        outs = refs[n_in + ni:n_in + ni + n_out]
        ex_outs = refs[n_in + ni + n_out:n_in + ni + n_out + no]
        scratch, sems = refs[n_in + ni + n_out + no:len(refs) - ns], refs[len(refs) - ns:]
        last = len(ex.phases) - 1
        for k in range(last):
            pl.when(when(k))(functools.partial(ex.phases[k], ex_ins, ex_outs, sems))
        body(*ins, *outs, *scratch)
        pl.when(when(last))(functools.partial(ex.phases[last], ex_ins, ex_outs, sems))

    return fn


def _phase_steps(n_phases, n_steps):
    return [0, n_steps - 1] if n_phases == 2 else [0, 2 * n_steps // 3, n_steps - 1]


def _row_block(shape, steps, lead, pick):
    blk = (None,) * lead + (shape[0] // steps,) + tuple(shape[1:])
    return pl.BlockSpec(blk, lambda *a: tuple(pick(*a)) + (a[-2],) + (0,) * (len(shape) - 1))


def _pair_sum(gs, ls, c_idx, name, steps):
    n = len(gs)

    def body(c_ref, *refs):
        for w in range(n):
            refs[2 * n + w][...] = (refs[w][...].astype(F32) + refs[n + w][...].astype(F32)).astype(BF16)

    shapes = [g.shape[2:] for g in gs]
    return pl.pallas_call(
        functools.partial(body), name=name,
        grid_spec=pltpu.PrefetchScalarGridSpec(
            num_scalar_prefetch=1, grid=(4, steps),
            in_specs=[_row_block(s, steps, 2, lambda k, i, c: (k, c[0])) for s in shapes]
            + [_row_block(s, steps, 1, lambda k, i, c: (k,)) for s in shapes],
            out_specs=[_row_block(s, steps, 1, lambda k, i, c: (k,)) for s in shapes]),
        out_shape=[jax.ShapeDtypeStruct((4,) + tuple(s), BF16) for s in shapes], compiler_params=_params(2),
    )(c_idx, *gs, *ls)


def _chip_exchange(ps):
    n = len(ps)

    def copies(ins, outs, sems):
        x, y, c = lax.axis_index("x"), lax.axis_index("y"), lax.axis_index("c")
        chips = [(1 - x, y), (x, 1 - y), (1 - x, 1 - y)]
        return [pltpu.make_async_remote_copy(
            src_ref=ins[w].at[2 * px + py], dst_ref=outs[w].at[s], send_sem=sems[0].at[3 * w + s],
            recv_sem=sems[1].at[3 * w + s], device_id=(px, py, c), device_id_type=_MESH)
            for w in range(n) for s, (px, py) in enumerate(chips)]

    return _start_then_wait(ps, [jax.ShapeDtypeStruct((3,) + p.shape[1:], p.dtype) for p in ps], 3 * n, copies)


def _adamw(w, g, m, v):
    m2 = ADAM_B1 * m + (1.0 - ADAM_B1) * g
    v2 = ADAM_B2 * v + (1.0 - ADAM_B2) * (g * g)
    m_hat = m2 / (1.0 - ADAM_B1 ** ADAM_STEP)
    v_hat = v2 / (1.0 - ADAM_B2 ** ADAM_STEP)
    delta = -ADAM_LR * (m_hat / (jnp.sqrt(v_hat) + ADAM_EPS) + ADAM_WD * w)
    return delta, m2, v2


def _update(gs, ls, qs, ws, ms, vs, idx, name, steps):
    n = len(gs)

    def body(idx_ref, *refs):
        g, l, q, w, m, v = (refs[k * n:(k + 1) * n] for k in range(6))
        outs = refs[6 * n:]
        for i in range(n):
            gr = g[i][...].astype(F32) + l[i][...].astype(F32)
            gr = gr + q[i][0].astype(F32)
            gr = gr + q[i][1].astype(F32)
            gr = gr + q[i][2].astype(F32)
            outs[4 * i][...] = gr
            outs[4 * i + 1][...], outs[4 * i + 2][...], outs[4 * i + 3][...] = _adamw(w[i][...], gr, m[i][...], v[i][...])

    shapes = [w.shape for w in ws]
    own = [_row_block(s, steps, 0, lambda i, c: ()) for s in shapes]
    res = pl.pallas_call(
        functools.partial(body), name=name,
        grid_spec=pltpu.PrefetchScalarGridSpec(
            num_scalar_prefetch=1, grid=(steps,),
            in_specs=[_row_block(s, steps, 2, lambda i, c: (c[0], c[1])) for s in shapes]
            + [_row_block(s, steps, 1, lambda i, c: (c[0],)) for s in shapes]
            + [pl.BlockSpec((3, s[0] // steps) + tuple(s[1:]), lambda i, c, nd=len(s): (0, i) + (0,) * (nd - 1))
               for s in shapes] + own * 3,
            out_specs=[b for b in own for _ in range(4)]),
        out_shape=[jax.ShapeDtypeStruct(s, F32) for s in shapes for _ in range(4)], compiler_params=_params(1),
    )(idx, *gs, *ls, *qs, *ws, *ms, *vs)
    return [res[4 * i:4 * i + 4] for i in range(n)]


def _update_small(sv_all, ws, ms, vs):
    n = len(ws)

    def body(all_ref, *refs):
        w, m, v = refs[:n], refs[n:2 * n], refs[2 * n:3 * n]
        loss_ref, outs = refs[3 * n], refs[3 * n + 1:]
        total = all_ref[0]
        for dev in range(1, N_DEV):
            total = total + all_ref[dev]
        loss_ref[...] = total[LOSS_ROW:LOSS_ROW + 1, :]
        for i in range(n):
            gr = total[i:i + 1, 0:w[i].shape[1]]
            outs[4 * i][...] = gr
            outs[4 * i + 1][...], outs[4 * i + 2][...], outs[4 * i + 3][...] = _adamw(w[i][...], gr, m[i][...], v[i][...])

    res = pl.pallas_call(
        functools.partial(body), name="update_small",
        out_shape=[jax.ShapeDtypeStruct((1, 1024), F32)]
        + [jax.ShapeDtypeStruct(a.shape, F32) for a in ws for _ in range(4)],
    )(sv_all, *ws, *ms, *vs)
    return res[0], [res[1 + 4 * i:5 + 4 * i] for i in range(n)]


def _to_exchange(shards):
    def block(n):
        a = shards[n][0]
        if n == "w_uq":
            return jnp.pad(a, ((0, 0), (0, HEAD_PAD - NOPE - ROPE)))
        return a.T if n == "w_in" else a

    return [block(n) for n in _BIG]


def _from_exchange(arrs, like):
    def shard(n, a):
        if n == "w_uq":
            a = a[:, :NOPE + ROPE]
        return (a.T if n == "w_in" else a).reshape(like[n].shape)

    return {n: shard(n, a) for n, a in zip(_BIG, arrs)}


def _padded_w_in(g_in):
    w_in_t = g_in.reshape(N_DEV * IN_SHARD, D_MODEL)
    kr_end = Q_RANK + KV_RANK + ROPE
    return jnp.concatenate([w_in_t[:kr_end], jnp.zeros((128 - ROPE, D_MODEL), BF16), w_in_t[kr_end:]], axis=0)


def _full_conv_w(g_conv):
    taps = (g_conv[:, 0].astype(F32) + g_conv[:, 1].astype(F32)).reshape(N_DEV, CONV_W, D_MODEL // N_DEV)
    return jnp.pad(taps.transpose(1, 0, 2).reshape(CONV_W, D_MODEL), ((0, 1), (0, 0)))


def _pad_rows(a, rows):
    return jnp.pad(a, ((0, rows - a.shape[0]), (0, 0)))


def _small_pack(vals):
    rows = [jnp.pad(v.reshape(1, -1), ((0, 0), (0, 1024 - v.size))) for v in vals]
    return _pad_rows(jnp.concatenate(rows, axis=0), SMALL_ROWS)


def _rope_tables(positions):
    inv_freq = ROPE_THETA ** (-jnp.arange(0, ROPE, 2, dtype=F32) / ROPE)
    ang = positions.reshape(-1).astype(F32)[:, None] * inv_freq
    cos, sin = jnp.cos(ang), jnp.sin(ang)
    t = cos.shape[0]
    z32, z64 = jnp.zeros((t, HALF), F32), jnp.zeros((t, HEAD_PAD - NOPE - ROPE), F32)
    tc = jnp.concatenate([cos, cos, z64], axis=1)
    tsa = jnp.concatenate([-sin, z32, z64], axis=1)
    tsb = jnp.concatenate([z32, sin, z64], axis=1)
    return tc, tsa, tsb


def _blocks(dw):
    if dw.ndim == 2:
        dw = dw.reshape(N_DEV, dw.shape[0] // N_DEV, dw.shape[1])
    return dw.reshape((4, 2) + dw.shape[1:])


def _step(x, positions, target, small, send, c_idx):
    s_in, s_uq, s_uk, s_uv, s_o, s_conv, s_pw2, s_out, s_ff1, s_ff2 = send
    tc, tsa, tsb = _rope_tables(positions)

    w_in_p = _padded_w_in(_run_exchange_on_sequencer(_ag_exchange([s_in]), "ag_w_in", 0)[0])
    (h, zs, a, sb, u, sa, sc), (w_uq, w_uk, w_uv) = _fwd_in_proj(
        x, small["norm_mix_pre"], w_in_p, _ag_exchange([s_uq, s_uk, s_uv]))
    w_uk, w_uv = w_uk.reshape(KV_RANK, -1), w_uv.reshape(KV_RANK, -1)
    cqn, ckvn, q, k, v = _fwd_qkv(zs, tc, tsa, tsb, small["q_norm"], small["kv_norm"], w_uq, w_uk, w_uv)
    (attn, lse), (w_o, g_conv, w_pw2, w_out, w_ff1, w_ff2) = _attn_fwd(
        q, k, v, _ag_exchange([s_o, s_conv, s_pw2, s_out, s_ff1, s_ff2]))
    w_o, w_pw2, w_out = (w.reshape(D_MODEL, D_MODEL) for w in (w_o, w_pw2, w_out))
    w_ff2, conv_w = w_ff2.reshape(D_FF, D_MODEL), _full_conv_w(g_conv)
    co, uact = _conv_fwd(u, conv_w, small["conv_b"], small["conv_ln_g"], small["conv_ln_b"])
    ya, yc, mb, m, x1 = _fwd_merge(attn, uact, sa, sc, x, w_o, w_pw2, small["b_pw2"], w_out, small["norm_mix_post"])
    h2, r1, act = _fwd_ff1(x1, small["norm_mlp_pre"], w_ff1)
    f, dy, loss_blk = _fwd_ff2_loss(act, x1, target, w_ff2, small["norm_mlp_post"])

    df, df1, dg_mlp_post = _bwd_ff2(dy, f, r1, w_ff2, small["norm_mlp_post"])
    g_ff2 = _blocks(_mm_tn(act, df, "dw_ff2"))
    (dx1, dg_mlp_pre), l_ff2 = _bwd_ff1(df1, x1, dy, w_ff1, small["norm_mlp_pre"], _pair_exchange([g_ff2]))
    g_ff1 = _blocks(_mm_tn(h2, df1, "dw_ff1", shard_cols=FF_SHARD))
    (dmb, dya, dyc, dga, dgc, dat, dua, dg_mix_post, db_pw2, delta), l_ff1 = _bwd_merge(
        dx1, m, sa, sc, ya, yc, attn, w_out, w_o, w_pw2, small["norm_mix_post"], _pair_exchange([g_ff1]))
    g_ff, l_ff = [g_ff1, g_ff2], [l_ff1[0], l_ff2[0]]
    p_ff = _pair_sum(g_ff, l_ff, c_idx, "rs_pair_sum_ff", 2)
    g_mix = [_blocks(_mm_tn(attn, dya, "dw_o")), _blocks(_mm_tn(uact, dyc, "dw_pw2")),
             _blocks(_mm_tn(mb, dmb, "dw_out"))]
    dco, dln_g, dln_b, dconv_b = _bwd_ln(dua, co, small["conv_ln_g"], small["conv_ln_b"])
    (da, db, dconv), l_mix = _conv_bwd(dco, u, a, sb, conv_w, _pair_exchange(g_mix))
    p_mix = _pair_sum(g_mix, l_mix, c_idx, "rs_pair_sum_mix", 1)
    (dq, dk, dv), q_early = _attn_bwd(q, k, v, dat, lse, delta, _chip_exchange(p_ff + p_mix))
    dqp, dkn, dzs, dq_norm, dkv_norm = _bwd_qkv(dq, dk, dv, zs, tc, tsa, tsb, small["q_norm"], small["kv_norm"],
                                                w_uq, w_uk, w_uv)
    dw_in_t = jnp.concatenate([_mm_tn(dzs, h, "dw_in_zs")[:Q_RANK + KV_RANK + ROPE], _mm_tn(da, h, "dw_in_a"),
                               _mm_tn(db, h, "dw_in_b"), _mm_tn(dga, h, "dw_in_ga"), _mm_tn(dgc, h, "dw_in_gc")],
                              axis=0)
    g_late = [_blocks(dw_in_t),
              _blocks(_mm_tn(cqn, dqp, "dw_uq", shard_cols=HEAD_PAD)),
              _blocks(_mm_tn(ckvn, dkn, "dw_uk").reshape(N_DEV, -1, N_HEADS, NOPE)),
              _blocks(_mm_tn(ckvn, dv, "dw_uv").reshape(N_DEV, -1, N_HEADS, NOPE)),
              _blocks(dconv[:CONV_W].reshape(CONV_W, N_DEV, 1, -1).transpose(1, 0, 2, 3).astype(BF16))]
    l_late = _run_exchange(_pair_exchange(g_late), "rs_pair_exchange_late")
    p_late = _pair_sum(g_late, l_late, c_idx, "rs_pair_sum_late", 1)
    (grad_x, dg_pre), q_late = _bwd_in_proj(dzs, da, db, dga, dgc, x, dx1, w_in_p, small["norm_mix_pre"],
                                            _chip_exchange(p_late))

    order = lambda late, mix, ff: list(late[:4]) + [mix[0], late[4], mix[1], mix[2]] + list(ff)
    exchanged = [order(g_late, g_mix, g_ff), order(l_late, l_mix, l_ff),
                 order(q_late, q_early[2:], q_early[:2])]
    small_grads = (dg_pre, dq_norm, dkv_norm, dconv_b, dln_g, dln_b, db_pw2, dg_mix_post, dg_mlp_pre, dg_mlp_post)
    return loss_blk, grad_x, small_grads, exchanged


def kernel(x, positions, norm_mix_pre, w_in, q_norm, w_uq, kv_norm, w_uk, w_uv, w_o_attn, conv_w, conv_b, conv_ln_g, conv_ln_b, w_pw2, b_pw2, w_out, norm_mix_post, norm_mlp_pre, w_ff1, w_ff2, norm_mlp_post, loss_target, m_norm_mix_pre, m_w_in, m_q_norm, m_w_uq, m_kv_norm, m_w_uk, m_w_uv, m_w_o_attn, m_conv_w, m_conv_b, m_conv_ln_g, m_conv_ln_b, m_w_pw2, m_b_pw2, m_w_out, m_norm_mix_post, m_norm_mlp_pre, m_w_ff1, m_w_ff2, m_norm_mlp_post, v_norm_mix_pre, v_w_in, v_q_norm, v_w_uq, v_kv_norm, v_w_uk, v_w_uv, v_w_o_attn, v_conv_w, v_conv_b, v_conv_ln_g, v_conv_ln_b, v_w_pw2, v_b_pw2, v_w_out, v_norm_mix_post, v_norm_mlp_pre, v_w_ff1, v_w_ff2, v_norm_mlp_post):
    wts = dict(norm_mix_pre=norm_mix_pre, w_in=w_in, q_norm=q_norm, w_uq=w_uq, kv_norm=kv_norm, w_uk=w_uk, w_uv=w_uv,
               w_o_attn=w_o_attn, conv_w=conv_w, conv_b=conv_b, conv_ln_g=conv_ln_g, conv_ln_b=conv_ln_b,
               w_pw2=w_pw2, b_pw2=b_pw2, w_out=w_out, norm_mix_post=norm_mix_post, norm_mlp_pre=norm_mlp_pre,
               w_ff1=w_ff1, w_ff2=w_ff2, norm_mlp_post=norm_mlp_post)
    mom_m = dict(norm_mix_pre=m_norm_mix_pre, w_in=m_w_in, q_norm=m_q_norm, w_uq=m_w_uq, kv_norm=m_kv_norm,
                 w_uk=m_w_uk, w_uv=m_w_uv, w_o_attn=m_w_o_attn, conv_w=m_conv_w, conv_b=m_conv_b,
                 conv_ln_g=m_conv_ln_g, conv_ln_b=m_conv_ln_b, w_pw2=m_w_pw2, b_pw2=m_b_pw2, w_out=m_w_out,
                 norm_mix_post=m_norm_mix_post, norm_mlp_pre=m_norm_mlp_pre, w_ff1=m_w_ff1, w_ff2=m_w_ff2,
                 norm_mlp_post=m_norm_mlp_post)
    mom_v = dict(norm_mix_pre=v_norm_mix_pre, w_in=v_w_in, q_norm=v_q_norm, w_uq=v_w_uq, kv_norm=v_kv_norm,
                 w_uk=v_w_uk, w_uv=v_w_uv, w_o_attn=v_w_o_attn, conv_w=v_conv_w, conv_b=v_conv_b,
                 conv_ln_g=v_conv_ln_g, conv_ln_b=v_conv_ln_b, w_pw2=v_w_pw2, b_pw2=v_b_pw2, w_out=v_w_out,
                 norm_mix_post=v_norm_mix_post, norm_mlp_pre=v_norm_mlp_pre, w_ff1=v_w_ff1, w_ff2=v_w_ff2,
                 norm_mlp_post=v_norm_mlp_post)
    cx, cy, cc = lax.axis_index("x"), lax.axis_index("y"), lax.axis_index("c")

    big_local = {n: wts[n] for n in _BIG}
    w_ex = _to_exchange(big_local)
    send = [a.astype(BF16) for a in w_ex]
    conv_i = _BIG.index("conv_w")
    conv_lo = (w_ex[conv_i] - send[conv_i].astype(F32)).astype(BF16)
    send[conv_i] = jnp.stack([send[conv_i], conv_lo])

    small = {n: wts[n].reshape(1, -1) for n in _SMALL}
    c_idx = cc.reshape(1).astype(jnp.int32)
    loss_blk, grad_x, small_grads, (g4, l_sib, q_in) = _step(x[0], positions, loss_target[0], small, send, c_idx)

    idx = jnp.stack([2 * cx + cy, cc]).astype(jnp.int32)
    m_ex, v_ex = _to_exchange({n: mom_m[n] for n in _BIG}), _to_exchange({n: mom_v[n] for n in _BIG})
    upd = [None] * len(_BIG)
    for group, steps in ((("w_in",), 1), (("w_uq", "w_ff1", "w_ff2"), 4),
                         (("w_uk", "w_uv", "w_o_attn", "conv_w", "w_pw2", "w_out"), 1)):
        ids = [_BIG.index(n) for n in group]
        pick = lambda arrs: [arrs[i] for i in ids]
        res = _update(pick(g4), pick(l_sib), pick(q_in), pick(w_ex), pick(m_ex), pick(v_ex), idx,
                      "update_" + group[0], steps)
        for i, r in zip(ids, res):
            upd[i] = r
    out_g, out_d, out_m, out_v = (_from_exchange([u[j] for u in upd], big_local) for j in range(4))

    loss_row = jnp.broadcast_to(loss_blk[0:1, 0:1], (1, 1024))
    sv = _small_pack(list(small_grads) + [loss_row])
    sv_all = _run_exchange(_ag_exchange([sv]), "ag_small")[0]
    loss_sum, upd_small = _update_small(sv_all, [wts[n] for n in _SMALL], [mom_m[n] for n in _SMALL],
                                        [mom_v[n] for n in _SMALL])
    for n, r in zip(_SMALL, upd_small):
        out_g[n], out_d[n], out_m[n], out_v[n] = r
    loss = loss_sum[0, 0] * (0.5 / D_MODEL)

    return (loss, grad_x[None], *[out_g[n] for n in _WEIGHTS], *[out_d[n] for n in _WEIGHTS],
            *[out_m[n] for n in _WEIGHTS], *[out_v[n] for n in _WEIGHTS])
```
